```python
import jax, jax.numpy as jnp
from jax import lax
import numpy as np

D_MODEL = 1024
BATCH = 8
SEQ = 2048
DEPTH = 1
DEC_BATCH = 4
DEC_SEQ = 4096
PAST_LEN = 128

GRID_W = 64
N_HEADS = 8
N_KV_HEADS = 2
HEAD_DIM = 64
N_GROUPS_PER_KV = N_HEADS // N_KV_HEADS
ATTN_WIDTH = N_HEADS * HEAD_DIM
KV_WIDTH = N_KV_HEADS * HEAD_DIM
Q_BLOCK = 128
ROPE_THETA = 10000.0
ROPE_PAIRS_PER_AXIS = HEAD_DIM // 4
POOL_WINDOWS = (2, 4, 8, 16)
N_POOL_GROUPS = 4
POOL_WIDTH = 512
POOL_GROUP_DIM = POOL_WIDTH // N_POOL_GROUPS
N_BRANCHES = 2
IN_WIDTH = ATTN_WIDTH + 2 * KV_WIDTH + POOL_WIDTH + N_BRANCHES * D_MODEL
N_EXPERTS = 32
TOP_K = 4
D_FF = 1024
SWIGLU_ALPHA = 1.702
SWIGLU_LIMIT = 7.0
MOE_BLOCK = 128
NORM_EPS = 1e-6

kernel_name = 'hybrid_gqa_pool_moe_encoder'


def rmsnorm(x, g):
    xf = x.astype(jnp.float32)
    y = xf * lax.rsqrt(jnp.mean(xf * xf, axis=-1, keepdims=True) + NORM_EPS)
    return (y * g.astype(jnp.float32)).astype(x.dtype)


def axial_rope_tables(seq_len):
    rows = seq_len // GRID_W
    row_ids, col_ids = jnp.meshgrid(jnp.arange(rows), jnp.arange(GRID_W), indexing='ij')
    row_ids = row_ids.reshape(-1).astype(jnp.float32)
    col_ids = col_ids.reshape(-1).astype(jnp.float32)
    inv_freq = ROPE_THETA ** (-jnp.arange(ROPE_PAIRS_PER_AXIS, dtype=jnp.float32) / ROPE_PAIRS_PER_AXIS)
    ang = jnp.stack([row_ids[:, None] * inv_freq, col_ids[:, None] * inv_freq], axis=1)
    return jnp.cos(ang), jnp.sin(ang)


def apply_axial_rope(x, cos, sin):
    B, S, H, _ = x.shape
    xr = x.astype(jnp.float32).reshape(B, S, H, 2, 2, ROPE_PAIRS_PER_AXIS)
    x1, x2 = xr[..., 0, :], xr[..., 1, :]
    c = cos[None, :, None]
    s = sin[None, :, None]
    out = jnp.stack([x1 * c - x2 * s, x2 * c + x1 * s], axis=-2)
    return out.reshape(B, S, H, HEAD_DIM).astype(x.dtype)


def blocked_gqa(q, k, v):
    B, S = q.shape[0], q.shape[1]
    nq = S // Q_BLOCK
    qb = q.reshape(B, nq, Q_BLOCK, N_KV_HEADS, N_GROUPS_PER_KV, HEAD_DIM).transpose(1, 0, 2, 3, 4, 5)
    scale = HEAD_DIM ** -0.5

    def one_block(qblk):
        s = jnp.einsum('bqkgd,bskd->bkgqs', qblk, k, preferred_element_type=jnp.float32) * scale
        p = jax.nn.softmax(s, axis=-1).astype(v.dtype)
        return jnp.einsum('bkgqs,bskd->bqkgd', p, v)

    o = lax.map(one_block, qb)
    return o.transpose(1, 0, 2, 3, 4, 5).reshape(B, S, ATTN_WIDTH)


def multiscale_pool(u):
    B, S, C = u.shape
    uf = u.astype(jnp.float32)
    cs = jnp.concatenate([jnp.zeros((B, 1, C), jnp.float32), jnp.cumsum(uf, axis=1)], axis=1)
    pos = jnp.arange(S)
    outs = []
    for g, w in enumerate(POOL_WINDOWS):
        sl = slice(g * POOL_GROUP_DIM, (g + 1) * POOL_GROUP_DIM)
        lo = jnp.clip(pos - w // 2, 0, S)
        hi = jnp.clip(pos + w // 2, 0, S)
        csg = cs[..., sl]
        total = jnp.take(csg, hi, axis=1) - jnp.take(csg, lo, axis=1)
        mean = total / (hi - lo).astype(jnp.float32)[None, :, None]
        outs.append(mean - uf[..., sl])
    return jnp.concatenate(outs, axis=-1).astype(u.dtype)


def gated_mixer(x, norm_g, w_in, q_norm_g, k_norm_g, w_attn_proj, pool_group_w, pool_scale, w_pool_proj, w_out):
    B, S, _ = x.shape
    h = rmsnorm(x, norm_g)
    z = h @ w_in
    cuts = [ATTN_WIDTH, ATTN_WIDTH + KV_WIDTH, ATTN_WIDTH + 2 * KV_WIDTH,
            ATTN_WIDTH + 2 * KV_WIDTH + POOL_WIDTH, ATTN_WIDTH + 2 * KV_WIDTH + POOL_WIDTH + D_MODEL]
    q, k, v, u, g_a, g_b = jnp.split(z, cuts, axis=-1)
    cos, sin = axial_rope_tables(S)
    q = apply_axial_rope(rmsnorm(q.reshape(B, S, N_HEADS, HEAD_DIM), q_norm_g), cos, sin)
    k = apply_axial_rope(rmsnorm(k.reshape(B, S, N_KV_HEADS, HEAD_DIM), k_norm_g), cos, sin)
    v = v.reshape(B, S, N_KV_HEADS, HEAD_DIM)
    y_attn = blocked_gqa(q, k, v) @ w_attn_proj
    pooled = multiscale_pool(u).reshape(B, S, N_POOL_GROUPS, POOL_GROUP_DIM)
    pooled = jnp.einsum('bsgc,gcd->bsgd', pooled, pool_group_w).reshape(B, S, POOL_WIDTH) * pool_scale
    y_pool = pooled @ w_pool_proj
    merged = jax.nn.sigmoid(g_a) * y_attn + jax.nn.sigmoid(g_b) * y_pool
    return x + merged @ w_out


def clamped_swiglu(gu):
    gate, up = gu[..., :D_FF], gu[..., D_FF:]
    gate = jnp.minimum(gate, SWIGLU_LIMIT)
    up = jnp.clip(up, -SWIGLU_LIMIT, SWIGLU_LIMIT)
    return (up + 1.0) * (gate * jax.nn.sigmoid(SWIGLU_ALPHA * gate))


def moe_ffn(x, norm_g, w_router, b_router, w_gu, b_gu, w_down, b_down):
    B, S, D = x.shape
    h = rmsnorm(x, norm_g).reshape(-1, D)
    T = h.shape[0]
    TK = T * TOP_K
    logits = h.astype(jnp.float32) @ w_router.astype(jnp.float32) + b_router.astype(jnp.float32)
    top_vals, top_idx = lax.top_k(logits, TOP_K)
    gates = jax.nn.softmax(top_vals, axis=-1)
    flat_e = top_idx.reshape(-1).astype(jnp.int32)
    flat_tok = jnp.repeat(jnp.arange(T, dtype=jnp.int32), TOP_K)
    order = jnp.argsort(flat_e)
    sorted_e = flat_e[order]
    sorted_tok = flat_tok[order]
    sorted_gate = gates.reshape(-1)[order]
    counts = jnp.bincount(flat_e, length=N_EXPERTS)
    padded = ((counts + MOE_BLOCK - 1) // MOE_BLOCK) * MOE_BLOCK
    padded_end = jnp.cumsum(padded)
    padded_start = padded_end - padded
    group_start = jnp.cumsum(counts) - counts
    rank = jnp.arange(TK, dtype=jnp.int32) - group_start[sorted_e]
    dest = padded_start[sorted_e] + rank
    n_blocks = (TK + MOE_BLOCK - 1) // MOE_BLOCK + N_EXPERTS
    n_rows = n_blocks * MOE_BLOCK
    row_tok = jnp.full((n_rows,), T, jnp.int32).at[dest].set(sorted_tok)
    row_gate = jnp.zeros((n_rows,), jnp.float32).at[dest].set(sorted_gate)
    block_e = jnp.clip(jnp.searchsorted(padded_end, jnp.arange(n_blocks) * MOE_BLOCK, side='right'),
                       0, N_EXPERTS - 1)
    h_pad = jnp.concatenate([h, jnp.zeros((1, D), h.dtype)], axis=0)
    xb = h_pad[row_tok].reshape(n_blocks, MOE_BLOCK, D)

    def expert_block(args):
        xblk, e = args
        act = clamped_swiglu(xblk @ w_gu[e] + b_gu[e])
        return act @ w_down[e] + b_down[e]

    yb = lax.map(expert_block, (xb, block_e)).reshape(n_rows, D)
    contrib = yb.astype(jnp.float32) * row_gate[:, None]
    y = jax.ops.segment_sum(contrib, row_tok, num_segments=T + 1)[:T]
    return x + y.astype(x.dtype).reshape(B, S, D)


def setup_inputs(seed: int = 0) -> dict:
    key = jax.random.key(seed)
    ks = jax.random.split(key, 20)
    f32 = jnp.float32
    nrm = lambda k, shape, scale: jax.random.normal(k, shape, f32) * scale
    L = DEPTH
    return {
        'x_prompt': nrm(ks[0], (BATCH, SEQ, D_MODEL), 1.0),
        'x_sample': nrm(ks[1], (DEC_BATCH, DEC_SEQ, D_MODEL), 1.0),
        'mix_norm_g': 1.0 + nrm(ks[2], (L, D_MODEL), 0.02),
        'w_in': nrm(ks[3], (L, D_MODEL, IN_WIDTH), D_MODEL ** -0.5),
        'q_norm_g': 1.0 + nrm(ks[4], (L, HEAD_DIM), 0.02),
        'k_norm_g': 1.0 + nrm(ks[5], (L, HEAD_DIM), 0.02),
        'w_attn_proj': nrm(ks[6], (L, ATTN_WIDTH, D_MODEL), ATTN_WIDTH ** -0.5),
        'pool_group_w': nrm(ks[7], (L, N_POOL_GROUPS, POOL_GROUP_DIM, POOL_GROUP_DIM), POOL_GROUP_DIM ** -0.5),
        'pool_scale': 1.0 + nrm(ks[8], (L, POOL_WIDTH), 0.1),
        'w_pool_proj': nrm(ks[9], (L, POOL_WIDTH, D_MODEL), POOL_WIDTH ** -0.5),
        'w_out': nrm(ks[10], (L, D_MODEL, D_MODEL), D_MODEL ** -0.5),
        'ffn_norm_g': 1.0 + nrm(ks[11], (L, D_MODEL), 0.02),
        'w_router': nrm(ks[12], (L, D_MODEL, N_EXPERTS), D_MODEL ** -0.5),
        'b_router': nrm(ks[13], (L, N_EXPERTS), 0.01),
        'w_gu': nrm(ks[14], (L, N_EXPERTS, D_MODEL, 2 * D_FF), D_MODEL ** -0.5),
        'b_gu': nrm(ks[15], (L, N_EXPERTS, 2 * D_FF), 0.01),
        'w_down': nrm(ks[16], (L, N_EXPERTS, D_FF, D_MODEL), D_FF ** -0.5),
        'b_down': nrm(ks[17], (L, N_EXPERTS, D_MODEL), 0.01),
    }


def reference(x_prompt, x_sample, mix_norm_g, w_in, q_norm_g, k_norm_g, w_attn_proj, pool_group_w,
              pool_scale, w_pool_proj, w_out, ffn_norm_g, w_router, b_router, w_gu, b_gu, w_down, b_down):
    def trunk(x):
        for l in range(DEPTH):
            x = gated_mixer(x, mix_norm_g[l], w_in[l], q_norm_g[l], k_norm_g[l], w_attn_proj[l],
                            pool_group_w[l], pool_scale[l], w_pool_proj[l], w_out[l])
            x = moe_ffn(x, ffn_norm_g[l], w_router[l], b_router[l], w_gu[l], b_gu[l], w_down[l], b_down[l])
        return x

    y_prompt = trunk(x_prompt)
    y_sample = trunk(x_sample)
    return (y_prompt, y_sample)
```

```python
import functools

import jax
import jax.numpy as jnp
from jax import lax
from jax.experimental import pallas as pl
from jax.experimental.pallas import tpu as pltpu

D_MODEL = 1024
GRID_W = 64
N_HEADS = 8
N_KV_HEADS = 2
HEAD_DIM = 64
N_GROUPS_PER_KV = N_HEADS // N_KV_HEADS
ATTN_WIDTH = N_HEADS * HEAD_DIM
KV_WIDTH = N_KV_HEADS * HEAD_DIM
ROPE_THETA = 10000.0
ROPE_PAIRS_PER_AXIS = HEAD_DIM // 4
POOL_WINDOWS = (2, 4, 8, 16)
N_POOL_GROUPS = 4
POOL_WIDTH = 512
POOL_GROUP_DIM = POOL_WIDTH // N_POOL_GROUPS
N_EXPERTS = 32
TOP_K = 4
D_FF = 1024
SWIGLU_ALPHA = 1.702
SWIGLU_LIMIT = 7.0
NORM_EPS = 1e-6

LANES = 128
SUBLANES = 8
POOL_HALO = 8
LOG2_E = 1.4426950408889634

IN_PROJ_ROWS = 512
ATTN_Q_ROWS = 128
MIX_ROWS = 256
MOE_ROWS = 256
DISPATCH_ROWS = 256
COMBINE_ROWS = 128
VMEM_LIMIT = 56 * 1024 * 1024

_bf16 = jnp.bfloat16
_f32 = jnp.float32


def _dot(a, b):
    return jnp.dot(a, b, preferred_element_type=_f32)


def _split_bf16(x):
    hi = x.astype(_bf16)
    lo = (x - hi.astype(_f32)).astype(_bf16)
    return hi, lo


def _sigmoid(x):
    return 1.0 / (1.0 + jnp.exp(-x))


def _in_proj_body(x_ref, g_ref, w_ref, qg_ref, kg_ref, cos_ref, sin_ref, bd_ref,
                  q_ref, k_ref, v_ref, u_ref, sa_ref, sb_ref):
    x = x_ref[...]
    ms = jnp.mean(x * x, axis=-1, keepdims=True)
    h = (x * lax.rsqrt(ms + NORM_EPS) * g_ref[...]).astype(_bf16)
    cos = cos_ref[...]
    sin = sin_ref[...]
    lane = lax.broadcasted_iota(jnp.int32, cos.shape, 1)
    first_half = (lane % (2 * ROPE_PAIRS_PER_AXIS)) < ROPE_PAIRS_PER_AXIS

    def head_norm_rope(z, gain):
        width = z.shape[1]
        hi, lo = _split_bf16(z * z)
        bd = bd_ref[:width, :width]
        ss = _dot(hi, bd) + _dot(lo, bd)
        zn = z * lax.rsqrt(ss * (1.0 / HEAD_DIM) + NORM_EPS) * gain
        outs = []
        for j in range(width // LANES):
            c = zn[:, j * LANES:(j + 1) * LANES]
            partner = jnp.where(first_half,
                                pltpu.roll(c, LANES - ROPE_PAIRS_PER_AXIS, 1),
                                pltpu.roll(c, ROPE_PAIRS_PER_AXIS, 1))
            outs.append(c * cos + partner * sin)
        return outs

    c0 = 0
    zq = _dot(h, w_ref[:, c0:c0 + ATTN_WIDTH])
    for j, o in enumerate(head_norm_rope(zq, qg_ref[...])):
        q_ref[:, j * LANES:(j + 1) * LANES] = o.astype(_bf16)
    c0 += ATTN_WIDTH
    zk = _dot(h, w_ref[:, c0:c0 + KV_WIDTH])
    (kr,) = head_norm_rope(zk, kg_ref[...])
    for j in range(N_KV_HEADS):
        k_ref[j] = kr[:, j * HEAD_DIM:(j + 1) * HEAD_DIM].astype(_bf16)
    c0 += KV_WIDTH
    v_ref[...] = _dot(h, w_ref[:, c0:c0 + KV_WIDTH]).astype(_bf16)
    c0 += KV_WIDTH
    u_ref[...] = _dot(h, w_ref[:, c0:c0 + POOL_WIDTH]).astype(_bf16)
    c0 += POOL_WIDTH
    sa_ref[...] = _sigmoid(_dot(h, w_ref[:, c0:c0 + D_MODEL])).astype(_bf16)
    c0 += D_MODEL
    sb_ref[...] = _sigmoid(_dot(h, w_ref[:, c0:c0 + D_MODEL])).astype(_bf16)


def _in_proj(x2, norm_g, w_in, qg, kg, cos_t, sin_t, bd, seq):
    tokens = x2.shape[0]
    tm = IN_PROJ_ROWS
    in_width = w_in.shape[1]
    tiles_per_seq = seq // tm
    const = lambda i: (0, 0)
    row = lambda i: (i, 0)
    return pl.pallas_call(
        _in_proj_body,
        grid=(tokens // tm,),
        in_specs=[
            pl.BlockSpec((tm, D_MODEL), row),
            pl.BlockSpec((1, D_MODEL), const),
            pl.BlockSpec((D_MODEL, in_width), const),
            pl.BlockSpec((1, ATTN_WIDTH), const),
            pl.BlockSpec((1, KV_WIDTH), const),
            pl.BlockSpec((tm, LANES), lambda i: (i % tiles_per_seq, 0)),
            pl.BlockSpec((tm, LANES), lambda i: (i % tiles_per_seq, 0)),
            pl.BlockSpec((ATTN_WIDTH, ATTN_WIDTH), const),
        ],
        out_specs=[
            pl.BlockSpec((tm, ATTN_WIDTH), row),
            pl.BlockSpec((N_KV_HEADS, tm, HEAD_DIM), lambda i: (0, i, 0)),
            pl.BlockSpec((tm, KV_WIDTH), row),
            pl.BlockSpec((tm, POOL_WIDTH), row),
            pl.BlockSpec((tm, D_MODEL), row),
            pl.BlockSpec((tm, D_MODEL), row),
        ],
        out_shape=[
            jax.ShapeDtypeStruct((tokens, ATTN_WIDTH), _bf16),
            jax.ShapeDtypeStruct((N_KV_HEADS, tokens, HEAD_DIM), _bf16),
            jax.ShapeDtypeStruct((tokens, KV_WIDTH), _bf16),
            jax.ShapeDtypeStruct((tokens, POOL_WIDTH), _bf16),
            jax.ShapeDtypeStruct((tokens, D_MODEL), _bf16),
            jax.ShapeDtypeStruct((tokens, D_MODEL), _bf16),
        ],
        compiler_params=pltpu.CompilerParams(
            dimension_semantics=("parallel",), vmem_limit_bytes=VMEM_LIMIT),
        name="in_proj",
    )(x2, norm_g, w_in, qg, kg, cos_t, sin_t, bd)


def _attention_body(q_ref, k_ref, vt_ref, o_ref):
    tq = q_ref.shape[0]
    k = k_ref[...]
    vt = vt_ref[...]
    q = q_ref[...]
    qs = jnp.concatenate([q[:, g * HEAD_DIM:(g + 1) * HEAD_DIM] for g in range(N_GROUPS_PER_KV)], axis=0)
    st = lax.dot_general(k, qs, (((1,), (1,)), ((), ())), preferred_element_type=_f32)
    m = jnp.max(st, axis=0, keepdims=True)
    p = jnp.exp2(st - m)
    denom = jnp.sum(p, axis=0, keepdims=True)
    ot = _dot(vt, p.astype(_bf16)) * (1.0 / denom)
    stacked = jnp.concatenate([ot[:, g * tq:(g + 1) * tq] for g in range(N_GROUPS_PER_KV)], axis=0)
    o_ref[...] = stacked.T.astype(_bf16)


def _attention(q, k2, vt, batch, seq):
    tq = ATTN_Q_ROWS
    nq = seq // tq
    group_width = N_GROUPS_PER_KV * HEAD_DIM
    return pl.pallas_call(
        _attention_body,
        grid=(batch, N_KV_HEADS, nq),
        in_specs=[
            pl.BlockSpec((tq, group_width), lambda b, kh, i: (b * nq + i, kh)),
            pl.BlockSpec((None, seq, HEAD_DIM), lambda b, kh, i: (kh, b, 0)),
            pl.BlockSpec((None, None, HEAD_DIM, seq), lambda b, kh, i: (b, kh, 0, 0)),
        ],
        out_specs=pl.BlockSpec((tq, group_width), lambda b, kh, i: (b * nq + i, kh)),
        out_shape=jax.ShapeDtypeStruct((batch * seq, ATTN_WIDTH), _bf16),
        compiler_params=pltpu.CompilerParams(
            dimension_semantics=("parallel", "parallel", "parallel"), vmem_limit_bytes=VMEM_LIMIT),
        name="attention",
    )(q, k2, vt)


def _mix_out_body(seq, x_ref, a_ref, up_ref, u_ref, un_ref, sa_ref, sb_ref,
                  wap_ref, pgw_ref, ps_ref, wpp_ref, wo_ref, g2_ref, wrh_ref, wrl_ref, br_ref, tri_ref,
                  x1_ref, h2_ref, idx_ref, gate_ref, rank_ref, cnt_ref,
                  ext_ref, carry_ref):
    i = pl.program_id(0)
    tm = x_ref.shape[0]

    @pl.when(i == 0)
    def _():
        carry_ref[...] = jnp.zeros_like(carry_ref)

    start = (i * tm) % seq
    has_prev = (start != 0).astype(_f32)
    has_next = (start + tm != seq).astype(_f32)
    ext_ref[0:POOL_HALO, :] = up_ref[...].astype(_f32) * has_prev
    ext_ref[POOL_HALO:POOL_HALO + tm, :] = u_ref[...].astype(_f32)
    ext_ref[POOL_HALO + tm:POOL_HALO + tm + POOL_HALO, :] = un_ref[...].astype(_f32) * has_next
    pos = start + lax.broadcasted_iota(jnp.int32, (tm, 1), 0)
    pooled = []
    for g, w in enumerate(POOL_WINDOWS):
        half = w // 2
        cols = slice(g * POOL_GROUP_DIM, (g + 1) * POOL_GROUP_DIM)
        tot = ext_ref[pl.ds(POOL_HALO - half, tm), cols]
        for d in range(-half + 1, half):
            tot = tot + ext_ref[pl.ds(POOL_HALO + d, tm), cols]
        cnt = (jnp.minimum(pos + half, seq) - jnp.maximum(pos - half, 0)).astype(_f32)
        diff = tot / cnt - ext_ref[pl.ds(POOL_HALO, tm), cols]
        pooled.append(_dot(diff.astype(_bf16), pgw_ref[g]))
    pooled = (jnp.concatenate(pooled, axis=1) * ps_ref[...]).astype(_bf16)
    y_pool = _dot(pooled, wpp_ref[...])
    y_attn = _dot(a_ref[...], wap_ref[...])
    merged = sa_ref[...].astype(_f32) * y_attn + sb_ref[...].astype(_f32) * y_pool
    x1 = x_ref[...] + _dot(merged.astype(_bf16), wo_ref[...])
    x1_ref[...] = x1

    ms = jnp.mean(x1 * x1, axis=-1, keepdims=True)
    h2 = x1 * lax.rsqrt(ms + NORM_EPS) * g2_ref[...]
    h2_ref[...] = h2
    hi, lo = _split_bf16(h2)
    wrh = wrh_ref[...]
    logits = _dot(hi, wrh) + _dot(lo, wrh) + _dot(hi, wrl_ref[...])
    lt = logits.T[:N_EXPERTS, :] + br_ref[...]
    eid = lax.broadcasted_iota(jnp.int32, lt.shape, 0)
    vals, idxs = [], []
    multi_hot = jnp.zeros(lt.shape, _f32)
    for _ in range(TOP_K):
        m = jnp.max(lt, axis=0, keepdims=True)
        sel = jnp.min(jnp.where(lt == m, eid, N_EXPERTS), axis=0, keepdims=True)
        hit = eid == sel
        vals.append(m)
        idxs.append(sel)
        multi_hot = multi_hot + hit.astype(_f32)
        lt = jnp.where(hit, -jnp.inf, lt)
    es = [jnp.exp(v - vals[0]) for v in vals]
    inv = 1.0 / (es[0] + es[1] + es[2] + es[3])
    gate_ref[...] = jnp.concatenate([e * inv for e in es], axis=0)
    idx_ref[...] = jnp.concatenate(idxs, axis=0)
    before = _dot(multi_hot.astype(_bf16), tri_ref[...]) + carry_ref[:, 0:1]
    ranks = [jnp.sum(jnp.where(eid == sel, before, 0.0), axis=0, keepdims=True) for sel in idxs]
    rank_ref[...] = jnp.concatenate(ranks, axis=0).astype(jnp.int32)
    carry_ref[...] = carry_ref[...] + jnp.sum(multi_hot, axis=1, keepdims=True)
    cnt_ref[...] = carry_ref[...]


def _mix_out(x2, attn, u, sa, sb, wap, pgw, ps, wpp, wo, g2, wrh, wrl, br, tri, seq):
    tokens = x2.shape[0]
    tm = MIX_ROWS
    n = tokens // tm
    halo_blocks = tm // POOL_HALO
    last_halo = tokens // POOL_HALO - 1
    row = lambda i: (i, 0)
    const2 = lambda i: (0, 0)
    const3 = lambda i: (0, 0, 0)
    col = lambda i: (0, i)
    return pl.pallas_call(
        functools.partial(_mix_out_body, seq),
        grid=(n,),
        in_specs=[
            pl.BlockSpec((tm, D_MODEL), row),
            pl.BlockSpec((tm, ATTN_WIDTH), row),
            pl.BlockSpec((POOL_HALO, POOL_WIDTH), lambda i: (jnp.maximum(i * halo_blocks - 1, 0), 0)),
            pl.BlockSpec((tm, POOL_WIDTH), row),
            pl.BlockSpec((POOL_HALO, POOL_WIDTH), lambda i: (jnp.minimum((i + 1) * halo_blocks, last_halo), 0)),
            pl.BlockSpec((tm, D_MODEL), row),
            pl.BlockSpec((tm, D_MODEL), row),
            pl.BlockSpec((ATTN_WIDTH, D_MODEL), const2),
            pl.BlockSpec((N_POOL_GROUPS, POOL_GROUP_DIM, POOL_GROUP_DIM), const3),
            pl.BlockSpec((1, POOL_WIDTH), const2),
            pl.BlockSpec((POOL_WIDTH, D_MODEL), const2),
            pl.BlockSpec((D_MODEL, D_MODEL), const2),
            pl.BlockSpec((1, D_MODEL), const2),
            pl.BlockSpec((D_MODEL, LANES), const2),
            pl.BlockSpec((D_MODEL, LANES), const2),
            pl.BlockSpec((N_EXPERTS, 1), const2),
            pl.BlockSpec((tm, tm), const2),
        ],
        out_specs=[
            pl.BlockSpec((tm, D_MODEL), row),
            pl.BlockSpec((tm, D_MODEL), row),
            pl.BlockSpec((TOP_K, tm), col),
            pl.BlockSpec((TOP_K, tm), col),
            pl.BlockSpec((TOP_K, tm), col),
            pl.BlockSpec((N_EXPERTS, LANES), const2),
        ],
        out_shape=[
            jax.ShapeDtypeStruct((tokens, D_MODEL), _f32),
            jax.ShapeDtypeStruct((tokens, D_MODEL), _f32),
            jax.ShapeDtypeStruct((TOP_K, tokens), jnp.int32),
            jax.ShapeDtypeStruct((TOP_K, tokens), _f32),
            jax.ShapeDtypeStruct((TOP_K, tokens), jnp.int32),
            jax.ShapeDtypeStruct((N_EXPERTS, LANES), _f32),
        ],
        scratch_shapes=[
            pltpu.VMEM((tm + 2 * POOL_HALO, POOL_WIDTH), _f32),
            pltpu.VMEM((N_EXPERTS, LANES), _f32),
        ],
        compiler_params=pltpu.CompilerParams(
            dimension_semantics=("arbitrary",), vmem_limit_bytes=VMEM_LIMIT),
        name="mix_out",
    )(x2, attn, u, u, u, sa, sb, wap, pgw, ps, wpp, wo, g2, wrh, wrl, br, tri)


def _row_copy(src_ref, src_row, dst_ref, dst_row, sem):
    return pltpu.make_async_copy(src_ref.at[pl.ds(src_row, 1)], dst_ref.at[pl.ds(dst_row, 1)], sem)


def _scatter_rows(dest_ref, h_ref, xs_ref, sem):
    tm = h_ref.shape[0]

    def issue(j, c):
        for k in range(TOP_K):
            _row_copy(h_ref, j, xs_ref, dest_ref[0, k, j], sem).start()
        return c

    lax.fori_loop(0, tm, issue, 0)

    def drain(j, c):
        for k in range(TOP_K):
            _row_copy(h_ref, 0, xs_ref, 0, sem).wait()
        return c

    lax.fori_loop(0, tm, drain, 0)


def _dispatch_body(n_first, start_ref, count_ref, used_ref, dest_ref, ha_ref, hb_ref, xs_ref, zero_ref, sem):
    i = pl.program_id(0)

    @pl.when(i == 0)
    def _():
        zero_ref[...] = jnp.zeros_like(zero_ref)
        n_blocks = xs_ref.shape[0] // MOE_ROWS

        def block_copy(b):
            return pltpu.make_async_copy(zero_ref, xs_ref.at[pl.ds(b * MOE_ROWS, MOE_ROWS)], sem)

        def issue_block(b, c):
            block_copy(b).start()
            return c

        lax.fori_loop(used_ref[0], n_blocks, issue_block, 0)
        for e in range(N_EXPERTS):
            base = start_ref[e]

            def issue(j, c, base=base):
                _row_copy(zero_ref, 0, xs_ref, base + j, sem).start()
                return c

            lax.fori_loop(0, count_ref[e], issue, 0)

        def drain_block(b, c):
            block_copy(0).wait()
            return c

        lax.fori_loop(used_ref[0], n_blocks, drain_block, 0)
        for e in range(N_EXPERTS):
            def drain(j, c):
                _row_copy(zero_ref, 0, xs_ref, 0, sem).wait()
                return c

            lax.fori_loop(0, count_ref[e], drain, 0)

    @pl.when(i < n_first)
    def _():
        _scatter_rows(dest_ref, ha_ref, xs_ref, sem)

    @pl.when(i >= n_first)
    def _():
        _scatter_rows(dest_ref, hb_ref, xs_ref, sem)


def _dispatch(pad_start, pad_count, n_used, dest3, h_a, h_b, n_rows):
    tm = DISPATCH_ROWS
    n_a = h_a.shape[0] // tm
    n_b = h_b.shape[0] // tm
    return pl.pallas_call(
        functools.partial(_dispatch_body, n_a),
        grid_spec=pltpu.PrefetchScalarGridSpec(
            num_scalar_prefetch=3,
            grid=(n_a + n_b,),
            in_specs=[
                pl.BlockSpec((1, TOP_K, tm), lambda i, s, c, u: (i, 0, 0), memory_space=pltpu.SMEM),
                pl.BlockSpec((tm, D_MODEL), lambda i, s, c, u: (jnp.minimum(i, n_a - 1), 0)),
                pl.BlockSpec((tm, D_MODEL), lambda i, s, c, u: (jnp.maximum(i - n_a, 0), 0)),
            ],
            out_specs=pl.BlockSpec(memory_space=pl.ANY),
            scratch_shapes=[pltpu.VMEM((MOE_ROWS, D_MODEL), _f32), pltpu.SemaphoreType.DMA(())],
        ),
        out_shape=jax.ShapeDtypeStruct((n_rows, D_MODEL), _f32),
        compiler_params=pltpu.CompilerParams(
            dimension_semantics=("arbitrary",), vmem_limit_bytes=VMEM_LIMIT),
        name="dispatch",
    )(pad_start, pad_count, n_used, dest3, h_a, h_b)


def _experts_body(be_ref, nu_ref, xs_ref, wgu_ref, bgu_ref, wd_ref, bd_ref, y_ref):
    del be_ref

    used = pl.program_id(0) < nu_ref[0]

    @pl.when(jnp.logical_not(used))
    def _():
        y_ref[...] = jnp.zeros_like(y_ref)

    @pl.when(used)
    def _():
        x = xs_ref[...].astype(_bf16)
        gu = _dot(x, wgu_ref[...]) + bgu_ref[...]
        gate = jnp.minimum(gu[:, :D_FF], SWIGLU_LIMIT)
        up = jnp.clip(gu[:, D_FF:], -SWIGLU_LIMIT, SWIGLU_LIMIT)
        act = (up + 1.0) * (gate * _sigmoid(SWIGLU_ALPHA * gate))
        y_ref[...] = _dot(act.astype(_bf16), wd_ref[...]) + bd_ref[...]


def _experts(block_e, n_used, xs, wgu, bgu, wd, bd):
    n_rows = xs.shape[0]
    n_blocks = n_rows // MOE_ROWS
    rows = lambda i, be, nu: (jnp.minimum(i, nu[0] - 1), 0)
    per_e = lambda i, be, nu: (be[i], 0, 0)
    return pl.pallas_call(
        _experts_body,
        grid_spec=pltpu.PrefetchScalarGridSpec(
            num_scalar_prefetch=2,
            grid=(n_blocks,),
            in_specs=[
                pl.BlockSpec((MOE_ROWS, D_MODEL), rows),
                pl.BlockSpec((None, D_MODEL, 2 * D_FF), per_e),
                pl.BlockSpec((None, 1, 2 * D_FF), per_e),
                pl.BlockSpec((None, D_FF, D_MODEL), per_e),
                pl.BlockSpec((None, 1, D_MODEL), per_e),
            ],
            out_specs=pl.BlockSpec((MOE_ROWS, D_MODEL), lambda i, be, nu: (i, 0)),
        ),
        out_shape=jax.ShapeDtypeStruct((n_rows, D_MODEL), _f32),
        compiler_params=pltpu.CompilerParams(
            dimension_semantics=("arbitrary",), vmem_limit_bytes=VMEM_LIMIT),
        name="experts",
    )(block_e, n_used, xs, wgu, bgu, wd, bd)


def _combine_body(dest_ref, x1_ref, gate_ref, yb_ref, o_ref, buf_ref, sem):
    tm = x1_ref.shape[0]

    def issue(j, c):
        for k in range(TOP_K):
            _row_copy(yb_ref, dest_ref[0, k, j], buf_ref.at[k], j, sem).start()
        return c

    lax.fori_loop(0, tm, issue, 0)

    def drain(j, c):
        for k in range(TOP_K):
            _row_copy(yb_ref, 0, buf_ref.at[k], 0, sem).wait()
        return c

    lax.fori_loop(0, tm, drain, 0)
    gates = gate_ref[...]
    acc = gates[:, 0:1] * buf_ref[0]
    for k in range(1, TOP_K):
        acc = acc + gates[:, k:k + 1] * buf_ref[k]
    o_ref[...] = x1_ref[...] + acc


def _combine(dest3, x1, gates_t, yb):
    tokens = x1.shape[0]
    tm = COMBINE_ROWS
    return pl.pallas_call(
        _combine_body,
        grid=(tokens // tm,),
        in_specs=[
            pl.BlockSpec((1, TOP_K, tm), lambda i: (i, 0, 0), memory_space=pltpu.SMEM),
            pl.BlockSpec((tm, D_MODEL), lambda i: (i, 0)),
            pl.BlockSpec((tm, TOP_K), lambda i: (i, 0)),
            pl.BlockSpec(memory_space=pl.ANY),
        ],
        out_specs=pl.BlockSpec((tm, D_MODEL), lambda i: (i, 0)),
        out_shape=jax.ShapeDtypeStruct((tokens, D_MODEL), _f32),
        scratch_shapes=[pltpu.VMEM((TOP_K, tm, D_MODEL), _f32), pltpu.SemaphoreType.DMA(())],
        compiler_params=pltpu.CompilerParams(
            dimension_semantics=("arbitrary",), vmem_limit_bytes=VMEM_LIMIT),
        name="combine",
    )(dest3, x1, gates_t, yb)


def _rope_tables(seq):
    pos = jnp.arange(seq)
    row_ids = (pos // GRID_W).astype(_f32)
    col_ids = (pos % GRID_W).astype(_f32)
    inv_freq = ROPE_THETA ** (-jnp.arange(ROPE_PAIRS_PER_AXIS, dtype=_f32) / ROPE_PAIRS_PER_AXIS)
    ang_r = row_ids[:, None] * inv_freq
    ang_c = col_ids[:, None] * inv_freq
    cos = jnp.concatenate([jnp.cos(ang_r)] * 2 + [jnp.cos(ang_c)] * 2, axis=1)
    sin = jnp.concatenate([-jnp.sin(ang_r), jnp.sin(ang_r), -jnp.sin(ang_c), jnp.sin(ang_c)], axis=1)
    reps = LANES // HEAD_DIM
    return jnp.tile(cos, (1, reps)), jnp.tile(sin, (1, reps))


def _mixer(x, p):
    batch, seq, _ = x.shape
    x2 = x.reshape(batch * seq, D_MODEL)
    cos_t, sin_t = _rope_tables(seq)
    q, k2, v, u, sa, sb = _in_proj(x2, p["mix_g"], p["w_in"], p["qg"], p["kg"], cos_t, sin_t, p["bd"], seq)
    vt = v.reshape(batch, seq, N_KV_HEADS, HEAD_DIM).transpose(0, 2, 3, 1)
    attn = _attention(q, k2, vt, batch, seq)
    return _mix_out(x2, attn, u, sa, sb, p["wap"], p["pgw"], p["ps"], p["wpp"], p["wo"], p["ffn_g"],
                    p["wrh"], p["wrl"], p["br"], p["tri"], seq)


def kernel(x_prompt, x_sample, mix_norm_g, w_in, q_norm_g, k_norm_g, w_attn_proj, pool_group_w, pool_scale,
           w_pool_proj, w_out, ffn_norm_g, w_router, b_router, w_gu, b_gu, w_down, b_down):
    depth = w_in.shape[0]
    xs_all = [x_prompt, x_sample]
    head_id = jnp.arange(ATTN_WIDTH) // HEAD_DIM
    block_diag = (head_id[:, None] == head_id[None, :]).astype(_bf16)
    tri_id = jnp.arange(MIX_ROWS)
    tri = (tri_id[:, None] < tri_id[None, :]).astype(_bf16)
    for l in range(depth):
        wr = jnp.pad(w_router[l].astype(_f32), ((0, 0), (0, LANES - N_EXPERTS)))
        wrh = wr.astype(_bf16)
        p = dict(
            mix_g=mix_norm_g[l].reshape(1, D_MODEL),
            w_in=w_in[l].astype(_bf16),
            qg=jnp.tile(q_norm_g[l] * (HEAD_DIM ** -0.5 * LOG2_E), N_HEADS).reshape(1, ATTN_WIDTH),
            kg=jnp.tile(k_norm_g[l], N_KV_HEADS).reshape(1, KV_WIDTH),
            bd=block_diag,
            wap=w_attn_proj[l].astype(_bf16),
            pgw=pool_group_w[l].astype(_bf16),
            ps=pool_scale[l].reshape(1, POOL_WIDTH),
            wpp=w_pool_proj[l].astype(_bf16),
            wo=w_out[l].astype(_bf16),
            ffn_g=ffn_norm_g[l].reshape(1, D_MODEL),
            wrh=wrh,
            wrl=(wr - wrh.astype(_f32)).astype(_bf16),
            br=b_router[l].astype(_f32).reshape(N_EXPERTS, 1),
            tri=tri,
        )
        mixed = [_mixer(x, p) for x in xs_all]

        counts = [m[5][:, 0].astype(jnp.int32) for m in mixed]
        total = sum(counts)
        padded = ((total + MOE_ROWS - 1) // MOE_ROWS) * MOE_ROWS
        padded_end = jnp.cumsum(padded)
        padded_start = padded_end - padded
        n_tok = sum(m[0].shape[0] for m in mixed)
        n_blocks = (n_tok * TOP_K + MOE_ROWS - 1) // MOE_ROWS + N_EXPERTS
        n_used = (padded_end[-1] // MOE_ROWS).astype(jnp.int32)
        blk = jnp.minimum(jnp.arange(n_blocks, dtype=jnp.int32), n_used - 1)
        block_e = jnp.clip(jnp.searchsorted(padded_end, blk * MOE_ROWS, side="right"),
                           0, N_EXPERTS - 1).astype(jnp.int32)
        dests = []
        seen = jnp.zeros((N_EXPERTS,), jnp.int32)
        for m, c in zip(mixed, counts):
            dests.append((padded_start + seen)[m[2]] + m[4])
            seen = seen + c
        d3 = jnp.concatenate([d.reshape(TOP_K, -1, DISPATCH_ROWS).transpose(1, 0, 2) for d in dests], axis=0)
        xs_buf = _dispatch((padded_start + total).astype(jnp.int32), (padded - total).astype(jnp.int32),
                           n_used.reshape(1), d3, mixed[0][1], mixed[1][1], n_blocks * MOE_ROWS)
        yb = _experts(block_e, n_used.reshape(1), xs_buf, w_gu[l].astype(_bf16), b_gu[l].reshape(N_EXPERTS, 1, -1),
                      w_down[l].astype(_bf16), b_down[l].reshape(N_EXPERTS, 1, -1))
        outs = []
        for x, m, dest in zip(xs_all, mixed, dests):
            d3 = dest.reshape(TOP_K, -1, COMBINE_ROWS).transpose(1, 0, 2)
            outs.append(_combine(d3, m[0], m[3].T, yb).reshape(x.shape))
        xs_all = outs
    return tuple(xs_all)
```

```python
import functools

import jax
import jax.numpy as jnp
from jax import lax
from jax.experimental import pallas as pl
from jax.experimental.pallas import tpu as pltpu

D_MODEL = 1024
GRID_W = 64
N_HEADS = 8
N_KV_HEADS = 2
HEAD_DIM = 64
N_GROUPS_PER_KV = N_HEADS // N_KV_HEADS
ATTN_WIDTH = N_HEADS * HEAD_DIM
KV_WIDTH = N_KV_HEADS * HEAD_DIM
ROPE_THETA = 10000.0
ROPE_PAIRS_PER_AXIS = HEAD_DIM // 4
POOL_WINDOWS = (2, 4, 8, 16)
N_POOL_GROUPS = 4
POOL_WIDTH = 512
POOL_GROUP_DIM = POOL_WIDTH // N_POOL_GROUPS
N_EXPERTS = 32
TOP_K = 4
D_FF = 1024
SWIGLU_ALPHA = 1.702
SWIGLU_LIMIT = 7.0
NORM_EPS = 1e-6

LANES = 128
SUBLANES = 8
POOL_HALO = 8
LOG2_E = 1.4426950408889634

IN_PROJ_ROWS = 512
ATTN_Q_ROWS = 128
MIX_ROWS = 256
MOE_ROWS = 256
DISPATCH_ROWS = 256
COMBINE_ROWS = 128
VMEM_LIMIT = 56 * 1024 * 1024

_bf16 = jnp.bfloat16
_f32 = jnp.float32


def _dot(a, b):
    return jnp.dot(a, b, preferred_element_type=_f32)


def _split_bf16(x):
    hi = x.astype(_bf16)
    lo = (x - hi.astype(_f32)).astype(_bf16)
    return hi, lo


def _sigmoid(x):
    return 1.0 / (1.0 + jnp.exp(-x))


def _in_proj_body(x_ref, g_ref, w_ref, qg_ref, kg_ref, cos_ref, sin_ref, bd_ref,
                  q_ref, k_ref, v_ref, u_ref, sa_ref, sb_ref):
    x = x_ref[...]
    ms = jnp.mean(x * x, axis=-1, keepdims=True)
    h = (x * lax.rsqrt(ms + NORM_EPS) * g_ref[...]).astype(_bf16)
    cos = cos_ref[...]
    sin = sin_ref[...]
    lane = lax.broadcasted_iota(jnp.int32, cos.shape, 1)
    first_half = (lane % (2 * ROPE_PAIRS_PER_AXIS)) < ROPE_PAIRS_PER_AXIS

    def head_norm_rope(z, gain):
        width = z.shape[1]
        hi, lo = _split_bf16(z * z)
        bd = bd_ref[:width, :width]
        ss = _dot(hi, bd) + _dot(lo, bd)
        zn = z * lax.rsqrt(ss * (1.0 / HEAD_DIM) + NORM_EPS) * gain
        outs = []
        for j in range(width // LANES):
            c = zn[:, j * LANES:(j + 1) * LANES]
            partner = jnp.where(first_half,
                                pltpu.roll(c, LANES - ROPE_PAIRS_PER_AXIS, 1),
                                pltpu.roll(c, ROPE_PAIRS_PER_AXIS, 1))
            outs.append(c * cos + partner * sin)
        return outs

    c0 = 0
    zq = _dot(h, w_ref[:, c0:c0 + ATTN_WIDTH])
    for j, o in enumerate(head_norm_rope(zq, qg_ref[...])):
        q_ref[:, j * LANES:(j + 1) * LANES] = o.astype(_bf16)
    c0 += ATTN_WIDTH
    zk = _dot(h, w_ref[:, c0:c0 + KV_WIDTH])
    (kr,) = head_norm_rope(zk, kg_ref[...])
    for j in range(N_KV_HEADS):
        k_ref[j] = kr[:, j * HEAD_DIM:(j + 1) * HEAD_DIM].astype(_bf16)
    c0 += KV_WIDTH
    v_ref[...] = _dot(h, w_ref[:, c0:c0 + KV_WIDTH]).astype(_bf16)
    c0 += KV_WIDTH
    u_ref[...] = _dot(h, w_ref[:, c0:c0 + POOL_WIDTH]).astype(_bf16)
    c0 += POOL_WIDTH
    sa_ref[...] = _sigmoid(_dot(h, w_ref[:, c0:c0 + D_MODEL])).astype(_bf16)
    c0 += D_MODEL
    sb_ref[...] = _sigmoid(_dot(h, w_ref[:, c0:c0 + D_MODEL])).astype(_bf16)


def _in_proj(x2, norm_g, w_in, qg, kg, cos_t, sin_t, bd, seq):
    tokens = x2.shape[0]
    tm = IN_PROJ_ROWS
    in_width = w_in.shape[1]
    tiles_per_seq = seq // tm
    const = lambda i: (0, 0)
    row = lambda i: (i, 0)
    return pl.pallas_call(
        _in_proj_body,
        grid=(tokens // tm,),
        in_specs=[
            pl.BlockSpec((tm, D_MODEL), row),
            pl.BlockSpec((1, D_MODEL), const),
            pl.BlockSpec((D_MODEL, in_width), const),
            pl.BlockSpec((1, ATTN_WIDTH), const),
            pl.BlockSpec((1, KV_WIDTH), const),
            pl.BlockSpec((tm, LANES), lambda i: (i % tiles_per_seq, 0)),
            pl.BlockSpec((tm, LANES), lambda i: (i % tiles_per_seq, 0)),
            pl.BlockSpec((ATTN_WIDTH, ATTN_WIDTH), const),
        ],
        out_specs=[
            pl.BlockSpec((tm, ATTN_WIDTH), row),
            pl.BlockSpec((N_KV_HEADS, tm, HEAD_DIM), lambda i: (0, i, 0)),
            pl.BlockSpec((tm, KV_WIDTH), row),
            pl.BlockSpec((tm, POOL_WIDTH), row),
            pl.BlockSpec((tm, D_MODEL), row),
            pl.BlockSpec((tm, D_MODEL), row),
        ],
        out_shape=[
            jax.ShapeDtypeStruct((tokens, ATTN_WIDTH), _bf16),
            jax.ShapeDtypeStruct((N_KV_HEADS, tokens, HEAD_DIM), _bf16),
            jax.ShapeDtypeStruct((tokens, KV_WIDTH), _bf16),
            jax.ShapeDtypeStruct((tokens, POOL_WIDTH), _bf16),
            jax.ShapeDtypeStruct((tokens, D_MODEL), _bf16),
            jax.ShapeDtypeStruct((tokens, D_MODEL), _bf16),
        ],
        compiler_params=pltpu.CompilerParams(
            dimension_semantics=("parallel",), vmem_limit_bytes=VMEM_LIMIT),
        name="in_proj",
    )(x2, norm_g, w_in, qg, kg, cos_t, sin_t, bd)


def _attention_body(q_ref, k_ref, vt_ref, o_ref):
    tq = q_ref.shape[0]
    k = k_ref[...]
    vt = vt_ref[...]
    q = q_ref[...]
    qs = jnp.concatenate([q[:, g * HEAD_DIM:(g + 1) * HEAD_DIM] for g in range(N_GROUPS_PER_KV)], axis=0)
    st = lax.dot_general(k, qs, (((1,), (1,)), ((), ())), preferred_element_type=_f32)
    m = jnp.max(st, axis=0, keepdims=True)
    p = jnp.exp2(st - m)
    denom = jnp.sum(p, axis=0, keepdims=True)
    ot = _dot(vt, p.astype(_bf16)) * (1.0 / denom)
    stacked = jnp.concatenate([ot[:, g * tq:(g + 1) * tq] for g in range(N_GROUPS_PER_KV)], axis=0)
    o_ref[...] = stacked.T.astype(_bf16)


def _attention(q, k2, vt, batch, seq):
    tq = ATTN_Q_ROWS
    nq = seq // tq
    group_width = N_GROUPS_PER_KV * HEAD_DIM
    return pl.pallas_call(
        _attention_body,
        grid=(batch, N_KV_HEADS, nq),
        in_specs=[
            pl.BlockSpec((tq, group_width), lambda b, kh, i: (b * nq + i, kh)),
            pl.BlockSpec((None, seq, HEAD_DIM), lambda b, kh, i: (kh, b, 0)),
            pl.BlockSpec((None, None, HEAD_DIM, seq), lambda b, kh, i: (b, kh, 0, 0)),
        ],
        out_specs=pl.BlockSpec((tq, group_width), lambda b, kh, i: (b * nq + i, kh)),
        out_shape=jax.ShapeDtypeStruct((batch * seq, ATTN_WIDTH), _bf16),
        compiler_params=pltpu.CompilerParams(
            dimension_semantics=("parallel", "parallel", "parallel"), vmem_limit_bytes=VMEM_LIMIT),
        name="attention",
    )(q, k2, vt)


def _mix_out_body(seq, x_ref, a_ref, up_ref, u_ref, un_ref, sa_ref, sb_ref,
                  wap_ref, pgw_ref, ps_ref, wpp_ref, wo_ref, g2_ref, wrh_ref, wrl_ref, br_ref, tri_ref,
                  x1_ref, h2_ref, idx_ref, gate_ref, rank_ref, cnt_ref,
                  ext_ref, carry_ref):
    i = pl.program_id(0)
    tm = x_ref.shape[0]

    @pl.when(i == 0)
    def _():
        carry_ref[...] = jnp.zeros_like(carry_ref)

    start = (i * tm) % seq
    has_prev = (start != 0).astype(_f32)
    has_next = (start + tm != seq).astype(_f32)
    ext_ref[0:POOL_HALO, :] = up_ref[...].astype(_f32) * has_prev
    ext_ref[POOL_HALO:POOL_HALO + tm, :] = u_ref[...].astype(_f32)
    ext_ref[POOL_HALO + tm:POOL_HALO + tm + POOL_HALO, :] = un_ref[...].astype(_f32) * has_next
    pos = start + lax.broadcasted_iota(jnp.int32, (tm, 1), 0)
    pooled = []
    for g, w in enumerate(POOL_WINDOWS):
        half = w // 2
        cols = slice(g * POOL_GROUP_DIM, (g + 1) * POOL_GROUP_DIM)
        tot = ext_ref[pl.ds(POOL_HALO - half, tm), cols]
        for d in range(-half + 1, half):
            tot = tot + ext_ref[pl.ds(POOL_HALO + d, tm), cols]
        cnt = (jnp.minimum(pos + half, seq) - jnp.maximum(pos - half, 0)).astype(_f32)
        diff = tot / cnt - ext_ref[pl.ds(POOL_HALO, tm), cols]
        pooled.append(_dot(diff.astype(_bf16), pgw_ref[g]))
    pooled = (jnp.concatenate(pooled, axis=1) * ps_ref[...]).astype(_bf16)
    y_pool = _dot(pooled, wpp_ref[...])
    y_attn = _dot(a_ref[...], wap_ref[...])
    merged = sa_ref[...].astype(_f32) * y_attn + sb_ref[...].astype(_f32) * y_pool
    x1 = x_ref[...] + _dot(merged.astype(_bf16), wo_ref[...])
    x1_ref[...] = x1

    ms = jnp.mean(x1 * x1, axis=-1, keepdims=True)
    h2 = x1 * lax.rsqrt(ms + NORM_EPS) * g2_ref[...]
    h2_ref[...] = h2
    hi, lo = _split_bf16(h2)
    wrh = wrh_ref[...]
    logits = _dot(hi, wrh) + _dot(lo, wrh) + _dot(hi, wrl_ref[...])
    lt = logits.T[:N_EXPERTS, :] + br_ref[...]
    eid = lax.broadcasted_iota(jnp.int32, lt.shape, 0)
    vals, idxs = [], []
    multi_hot = jnp.zeros(lt.shape, _f32)
    for _ in range(TOP_K):
        m = jnp.max(lt, axis=0, keepdims=True)
        sel = jnp.min(jnp.where(lt == m, eid, N_EXPERTS), axis=0, keepdims=True)
        hit = eid == sel
        vals.append(m)
        idxs.append(sel)
        multi_hot = multi_hot + hit.astype(_f32)
        lt = jnp.where(hit, -jnp.inf, lt)
    es = [jnp.exp(v - vals[0]) for v in vals]
    inv = 1.0 / (es[0] + es[1] + es[2] + es[3])
    gate_ref[...] = jnp.concatenate([e * inv for e in es], axis=0)
    idx_ref[...] = jnp.concatenate(idxs, axis=0)
    before = _dot(multi_hot.astype(_bf16), tri_ref[...]) + carry_ref[:, 0:1]
    ranks = [jnp.sum(jnp.where(eid == sel, before, 0.0), axis=0, keepdims=True) for sel in idxs]
    rank_ref[...] = jnp.concatenate(ranks, axis=0).astype(jnp.int32)
    carry_ref[...] = carry_ref[...] + jnp.sum(multi_hot, axis=1, keepdims=True)
    cnt_ref[...] = carry_ref[...]


def _mix_out(x2, attn, u, sa, sb, wap, pgw, ps, wpp, wo, g2, wrh, wrl, br, tri, seq):
    tokens = x2.shape[0]
    tm = MIX_ROWS
    n = tokens // tm
    halo_blocks = tm // POOL_HALO
    last_halo = tokens // POOL_HALO - 1
    row = lambda i: (i, 0)
    const2 = lambda i: (0, 0)
    const3 = lambda i: (0, 0, 0)
    col = lambda i: (0, i)
    return pl.pallas_call(
        functools.partial(_mix_out_body, seq),
        grid=(n,),
        in_specs=[
            pl.BlockSpec((tm, D_MODEL), row),
            pl.BlockSpec((tm, ATTN_WIDTH), row),
            pl.BlockSpec((POOL_HALO, POOL_WIDTH), lambda i: (jnp.maximum(i * halo_blocks - 1, 0), 0)),
            pl.BlockSpec((tm, POOL_WIDTH), row),
            pl.BlockSpec((POOL_HALO, POOL_WIDTH), lambda i: (jnp.minimum((i + 1) * halo_blocks, last_halo), 0)),
            pl.BlockSpec((tm, D_MODEL), row),
            pl.BlockSpec((tm, D_MODEL), row),
            pl.BlockSpec((ATTN_WIDTH, D_MODEL), const2),
            pl.BlockSpec((N_POOL_GROUPS, POOL_GROUP_DIM, POOL_GROUP_DIM), const3),
            pl.BlockSpec((1, POOL_WIDTH), const2),
            pl.BlockSpec((POOL_WIDTH, D_MODEL), const2),
            pl.BlockSpec((D_MODEL, D_MODEL), const2),
            pl.BlockSpec((1, D_MODEL), const2),
            pl.BlockSpec((D_MODEL, LANES), const2),
            pl.BlockSpec((D_MODEL, LANES), const2),
            pl.BlockSpec((N_EXPERTS, 1), const2),
            pl.BlockSpec((tm, tm), const2),
        ],
        out_specs=[
            pl.BlockSpec((tm, D_MODEL), row),
            pl.BlockSpec((tm, D_MODEL), row),
            pl.BlockSpec((TOP_K, tm), col),
            pl.BlockSpec((TOP_K, tm), col),
            pl.BlockSpec((TOP_K, tm), col),
            pl.BlockSpec((N_EXPERTS, LANES), const2),
        ],
        out_shape=[
            jax.ShapeDtypeStruct((tokens, D_MODEL), _f32),
            jax.ShapeDtypeStruct((tokens, D_MODEL), _f32),
            jax.ShapeDtypeStruct((TOP_K, tokens), jnp.int32),
            jax.ShapeDtypeStruct((TOP_K, tokens), _f32),
            jax.ShapeDtypeStruct((TOP_K, tokens), jnp.int32),
            jax.ShapeDtypeStruct((N_EXPERTS, LANES), _f32),
        ],
        scratch_shapes=[
            pltpu.VMEM((tm + 2 * POOL_HALO, POOL_WIDTH), _f32),
            pltpu.VMEM((N_EXPERTS, LANES), _f32),
        ],
        compiler_params=pltpu.CompilerParams(
            dimension_semantics=("arbitrary",), vmem_limit_bytes=VMEM_LIMIT),
        name="mix_out",
    )(x2, attn, u, u, u, sa, sb, wap, pgw, ps, wpp, wo, g2, wrh, wrl, br, tri)


def _row_copy(src_ref, src_row, dst_ref, dst_row, sem):
    return pltpu.make_async_copy(src_ref.at[pl.ds(src_row, 1)], dst_ref.at[pl.ds(dst_row, 1)], sem)


def _scatter_rows(dest_ref, h_ref, xs_ref, sem):
    tm = h_ref.shape[0]

    def issue(j, c):
        for k in range(TOP_K):
            _row_copy(h_ref, j, xs_ref, dest_ref[0, k, j], sem).start()
        return c

    lax.fori_loop(0, tm, issue, 0)

    def drain(j, c):
        for k in range(TOP_K):
            _row_copy(h_ref, 0, xs_ref, 0, sem).wait()
        return c

    lax.fori_loop(0, tm, drain, 0)


def _dispatch_body(n_first, start_ref, count_ref, used_ref, dest_ref, ha_ref, hb_ref, xs_ref, zero_ref, sem):
    i = pl.program_id(0)

    @pl.when(i == 0)
    def _():
        zero_ref[...] = jnp.zeros_like(zero_ref)
        n_blocks = xs_ref.shape[0] // MOE_ROWS

        def block_copy(b):
            return pltpu.make_async_copy(zero_ref, xs_ref.at[pl.ds(b * MOE_ROWS, MOE_ROWS)], sem)

        def issue_block(b, c):
            block_copy(b).start()
            return c

        lax.fori_loop(used_ref[0], n_blocks, issue_block, 0)
        for e in range(N_EXPERTS):
            base = start_ref[e]

            def issue(j, c, base=base):
                _row_copy(zero_ref, 0, xs_ref, base + j, sem).start()
                return c

            lax.fori_loop(0, count_ref[e], issue, 0)

        def drain_block(b, c):
            block_copy(0).wait()
            return c

        lax.fori_loop(used_ref[0], n_blocks, drain_block, 0)
        for e in range(N_EXPERTS):
            def drain(j, c):
                _row_copy(zero_ref, 0, xs_ref, 0, sem).wait()
                return c

            lax.fori_loop(0, count_ref[e], drain, 0)

    @pl.when(i < n_first)
    def _():
        _scatter_rows(dest_ref, ha_ref, xs_ref, sem)

    @pl.when(i >= n_first)
    def _():
        _scatter_rows(dest_ref, hb_ref, xs_ref, sem)


def _dispatch(pad_start, pad_count, n_used, dest3, h_a, h_b, n_rows):
    tm = DISPATCH_ROWS
    n_a = h_a.shape[0] // tm
    n_b = h_b.shape[0] // tm
    return pl.pallas_call(
        functools.partial(_dispatch_body, n_a),
        grid_spec=pltpu.PrefetchScalarGridSpec(
            num_scalar_prefetch=3,
            grid=(n_a + n_b,),
            in_specs=[
                pl.BlockSpec((1, TOP_K, tm), lambda i, s, c, u: (i, 0, 0), memory_space=pltpu.SMEM),
                pl.BlockSpec((tm, D_MODEL), lambda i, s, c, u: (jnp.minimum(i, n_a - 1), 0)),
                pl.BlockSpec((tm, D_MODEL), lambda i, s, c, u: (jnp.maximum(i - n_a, 0), 0)),
            ],
            out_specs=pl.BlockSpec(memory_space=pl.ANY),
            scratch_shapes=[pltpu.VMEM((MOE_ROWS, D_MODEL), _f32), pltpu.SemaphoreType.DMA(())],
        ),
        out_shape=jax.ShapeDtypeStruct((n_rows, D_MODEL), _f32),
        compiler_params=pltpu.CompilerParams(
            dimension_semantics=("arbitrary",), vmem_limit_bytes=VMEM_LIMIT),
        name="dispatch",
    )(pad_start, pad_count, n_used, dest3, h_a, h_b)


def _experts_body(be_ref, nu_ref, xs_ref, wgu_ref, bgu_ref, wd_ref, bd_ref, y_ref):
    del be_ref

    used = pl.program_id(0) < nu_ref[0]

    @pl.when(jnp.logical_not(used))
    def _():
        y_ref[...] = jnp.zeros_like(y_ref)

    @pl.when(used)
    def _():
        x = xs_ref[...].astype(_bf16)
        gu = _dot(x, wgu_ref[...]) + bgu_ref[...]
        gate = jnp.minimum(gu[:, :D_FF], SWIGLU_LIMIT)
        up = jnp.clip(gu[:, D_FF:], -SWIGLU_LIMIT, SWIGLU_LIMIT)
        act = (up + 1.0) * (gate * _sigmoid(SWIGLU_ALPHA * gate))
        y_ref[...] = _dot(act.astype(_bf16), wd_ref[...]) + bd_ref[...]


def _experts(block_e, n_used, xs, wgu, bgu, wd, bd):
    n_rows = xs.shape[0]
    n_blocks = n_rows // MOE_ROWS
    rows = lambda i, be, nu: (jnp.minimum(i, nu[0] - 1), 0)
    per_e = lambda i, be, nu: (be[i], 0, 0)
    return pl.pallas_call(
        _experts_body,
        grid_spec=pltpu.PrefetchScalarGridSpec(
            num_scalar_prefetch=2,
            grid=(n_blocks,),
            in_specs=[
                pl.BlockSpec((MOE_ROWS, D_MODEL), rows),
                pl.BlockSpec((None, D_MODEL, 2 * D_FF), per_e),
                pl.BlockSpec((None, 1, 2 * D_FF), per_e),
                pl.BlockSpec((None, D_FF, D_MODEL), per_e),
                pl.BlockSpec((None, 1, D_MODEL), per_e),
            ],
            out_specs=pl.BlockSpec((MOE_ROWS, D_MODEL), lambda i, be, nu: (i, 0)),
        ),
        out_shape=jax.ShapeDtypeStruct((n_rows, D_MODEL), _f32),
        compiler_params=pltpu.CompilerParams(
            dimension_semantics=("arbitrary",), vmem_limit_bytes=VMEM_LIMIT),
        name="experts",
    )(block_e, n_used, xs, wgu, bgu, wd, bd)


def _combine_body(dest_ref, x1_ref, gate_ref, yb_ref, o_ref, buf_ref, sem):
    tm = x1_ref.shape[0]

    def issue(j, c):
        for k in range(TOP_K):
            _row_copy(yb_ref, dest_ref[0, k, j], buf_ref.at[k], j, sem).start()
        return c

    lax.fori_loop(0, tm, issue, 0)

    def drain(j, c):
        for k in range(TOP_K):
            _row_copy(yb_ref, 0, buf_ref.at[k], 0, sem).wait()
        return c

    lax.fori_loop(0, tm, drain, 0)
    gates = gate_ref[...]
    acc = gates[:, 0:1] * buf_ref[0]
    for k in range(1, TOP_K):
        acc = acc + gates[:, k:k + 1] * buf_ref[k]
    o_ref[...] = x1_ref[...] + acc


def _combine(dest3, x1, gates_t, yb):
    tokens = x1.shape[0]
    tm = COMBINE_ROWS
    return pl.pallas_call(
        _combine_body,
        grid=(tokens // tm,),
        in_specs=[
            pl.BlockSpec((1, TOP_K, tm), lambda i: (i, 0, 0), memory_space=pltpu.SMEM),
            pl.BlockSpec((tm, D_MODEL), lambda i: (i, 0)),
            pl.BlockSpec((tm, TOP_K), lambda i: (i, 0)),
            pl.BlockSpec(memory_space=pl.ANY),
        ],
        out_specs=pl.BlockSpec((tm, D_MODEL), lambda i: (i, 0)),
        out_shape=jax.ShapeDtypeStruct((tokens, D_MODEL), _f32),
        scratch_shapes=[pltpu.VMEM((TOP_K, tm, D_MODEL), _f32), pltpu.SemaphoreType.DMA(())],
        compiler_params=pltpu.CompilerParams(
            dimension_semantics=("arbitrary",), vmem_limit_bytes=VMEM_LIMIT),
        name="combine",
    )(dest3, x1, gates_t, yb)


def _rope_tables(seq):
    pos = jnp.arange(seq)
    row_ids = (pos // GRID_W).astype(_f32)
    col_ids = (pos % GRID_W).astype(_f32)
    inv_freq = ROPE_THETA ** (-jnp.arange(ROPE_PAIRS_PER_AXIS, dtype=_f32) / ROPE_PAIRS_PER_AXIS)
    ang_r = row_ids[:, None] * inv_freq
    ang_c = col_ids[:, None] * inv_freq
    cos = jnp.concatenate([jnp.cos(ang_r)] * 2 + [jnp.cos(ang_c)] * 2, axis=1)
    sin = jnp.concatenate([-jnp.sin(ang_r), jnp.sin(ang_r), -jnp.sin(ang_c), jnp.sin(ang_c)], axis=1)
    reps = LANES // HEAD_DIM
    return jnp.tile(cos, (1, reps)), jnp.tile(sin, (1, reps))


def _mixer(x, p):
    batch, seq, _ = x.shape
    x2 = x.reshape(batch * seq, D_MODEL)
    cos_t, sin_t = _rope_tables(seq)
    q, k2, v, u, sa, sb = _in_proj(x2, p["mix_g"], p["w_in"], p["qg"], p["kg"], cos_t, sin_t, p["bd"], seq)
    vt = v.reshape(batch, seq, N_KV_HEADS, HEAD_DIM).transpose(0, 2, 3, 1)
    attn = _attention(q, k2, vt, batch, seq)
    return _mix_out(x2, attn, u, sa, sb, p["wap"], p["pgw"], p["ps"], p["wpp"], p["wo"], p["ffn_g"],
                    p["wrh"], p["wrl"], p["br"], p["tri"], seq)


def kernel(x_prompt, x_sample, mix_norm_g, w_in, q_norm_g, k_norm_g, w_attn_proj, pool_group_w, pool_scale,
           w_pool_proj, w_out, ffn_norm_g, w_router, b_router, w_gu, b_gu, w_down, b_down):
    depth = w_in.shape[0]
    xs_all = [x_prompt, x_sample]
    head_id = jnp.arange(ATTN_WIDTH) // HEAD_DIM
    block_diag = (head_id[:, None] == head_id[None, :]).astype(_bf16)
    tri_id = jnp.arange(MIX_ROWS)
    tri = (tri_id[:, None] < tri_id[None, :]).astype(_bf16)
    for l in range(depth):
        wr = jnp.pad(w_router[l].astype(_f32), ((0, 0), (0, LANES - N_EXPERTS)))
        wrh = wr.astype(_bf16)
        p = dict(
            mix_g=mix_norm_g[l].reshape(1, D_MODEL),
            w_in=w_in[l].astype(_bf16),
            qg=jnp.tile(q_norm_g[l] * (HEAD_DIM ** -0.5 * LOG2_E), N_HEADS).reshape(1, ATTN_WIDTH),
            kg=jnp.tile(k_norm_g[l], N_KV_HEADS).reshape(1, KV_WIDTH),
            bd=block_diag,
            wap=w_attn_proj[l].astype(_bf16),
            pgw=pool_group_w[l].astype(_bf16),
            ps=pool_scale[l].reshape(1, POOL_WIDTH),
            wpp=w_pool_proj[l].astype(_bf16),
            wo=w_out[l].astype(_bf16),
            ffn_g=ffn_norm_g[l].reshape(1, D_MODEL),
            wrh=wrh,
            wrl=(wr - wrh.astype(_f32)).astype(_bf16),
            br=b_router[l].astype(_f32).reshape(N_EXPERTS, 1),
            tri=tri,
        )
        mixed = [_mixer(x, p) for x in xs_all]

        counts = [m[5][:, 0].astype(jnp.int32) for m in mixed]
        total = sum(counts)
        padded = ((total + MOE_ROWS - 1) // MOE_ROWS) * MOE_ROWS
        padded_end = jnp.cumsum(padded)
        padded_start = padded_end - padded
        n_tok = sum(m[0].shape[0] for m in mixed)
        n_blocks = (n_tok * TOP_K + MOE_ROWS - 1) // MOE_ROWS + N_EXPERTS
        n_used = (padded_end[-1] // MOE_ROWS).astype(jnp.int32)
        blk = jnp.minimum(jnp.arange(n_blocks, dtype=jnp.int32), n_used - 1)
        block_e = jnp.sum((padded_end[None, :] <= (blk * MOE_ROWS)[:, None]).astype(jnp.int32), axis=1)
        block_e = jnp.minimum(block_e, N_EXPERTS - 1)
        expert_ids = jnp.arange(N_EXPERTS, dtype=jnp.int32)[:, None, None]
        dests = []
        seen = jnp.zeros((N_EXPERTS,), jnp.int32)
        for m, c in zip(mixed, counts):
            base = (padded_start + seen)[:, None, None]
            dests.append(jnp.sum(jnp.where(m[2][None] == expert_ids, base, 0), axis=0) + m[4])
            seen = seen + c
        d3 = jnp.concatenate([d.reshape(TOP_K, -1, DISPATCH_ROWS).transpose(1, 0, 2) for d in dests], axis=0)
        xs_buf = _dispatch((padded_start + total).astype(jnp.int32), (padded - total).astype(jnp.int32),
                           n_used.reshape(1), d3, mixed[0][1], mixed[1][1], n_blocks * MOE_ROWS)
        yb = _experts(block_e, n_used.reshape(1), xs_buf, w_gu[l].astype(_bf16), b_gu[l].reshape(N_EXPERTS, 1, -1),
                      w_down[l].astype(_bf16), b_down[l].reshape(N_EXPERTS, 1, -1))
        outs = []
        for x, m, dest in zip(xs_all, mixed, dests):
            d3 = dest.reshape(TOP_K, -1, COMBINE_ROWS).transpose(1, 0, 2)
            outs.append(_combine(d3, m[0], m[3].T, yb).reshape(x.shape))
        xs_all = outs
    return tuple(xs_all)
```

```python
import functools

import jax
import jax.numpy as jnp
from jax import lax
from jax.experimental import pallas as pl
from jax.experimental.pallas import tpu as pltpu

D_MODEL = 1024
GRID_W = 64
N_HEADS = 8
N_KV_HEADS = 2
HEAD_DIM = 64
N_GROUPS_PER_KV = N_HEADS // N_KV_HEADS
ATTN_WIDTH = N_HEADS * HEAD_DIM
KV_WIDTH = N_KV_HEADS * HEAD_DIM
ROPE_THETA = 10000.0
ROPE_PAIRS_PER_AXIS = HEAD_DIM // 4
POOL_WINDOWS = (2, 4, 8, 16)
N_POOL_GROUPS = 4
POOL_WIDTH = 512
POOL_GROUP_DIM = POOL_WIDTH // N_POOL_GROUPS
N_EXPERTS = 32
TOP_K = 4
D_FF = 1024
SWIGLU_ALPHA = 1.702
SWIGLU_LIMIT = 7.0
NORM_EPS = 1e-6

LANES = 128
SUBLANES = 8
POOL_HALO = 8
LOG2_E = 1.4426950408889634

IN_PROJ_ROWS = 512
ATTN_Q_ROWS = 128
ATTN_KEY_CHUNK = 512
MIX_ROWS = 256
MOE_ROWS = 256
DISPATCH_ROWS = 256
COMBINE_ROWS = 128
VMEM_LIMIT = 56 * 1024 * 1024

_bf16 = jnp.bfloat16
_f32 = jnp.float32


def _dot(a, b):
    return jnp.dot(a, b, preferred_element_type=_f32)


def _split_bf16(x):
    hi = x.astype(_bf16)
    lo = (x - hi.astype(_f32)).astype(_bf16)
    return hi, lo


def _sigmoid(x):
    return 1.0 / (1.0 + jnp.exp(-x))


def _in_proj_body(x_ref, g_ref, w_ref, wvt_ref, qg_ref, kg_ref, cos_ref, sin_ref, bd_ref,
                  q_ref, k_ref, vt_ref, u_ref, sa_ref, sb_ref):
    x = x_ref[...]
    ms = jnp.mean(x * x, axis=-1, keepdims=True)
    h = (x * lax.rsqrt(ms + NORM_EPS) * g_ref[...]).astype(_bf16)
    cos = cos_ref[...]
    sin = sin_ref[...]
    lane = lax.broadcasted_iota(jnp.int32, cos.shape, 1)
    first_half = (lane % (2 * ROPE_PAIRS_PER_AXIS)) < ROPE_PAIRS_PER_AXIS

    def head_norm_rope(z, gain):
        width = z.shape[1]
        hi, lo = _split_bf16(z * z)
        bd = bd_ref[:width, :width]
        ss = _dot(hi, bd) + _dot(lo, bd)
        zn = z * lax.rsqrt(ss * (1.0 / HEAD_DIM) + NORM_EPS) * gain
        outs = []
        for j in range(width // LANES):
            c = zn[:, j * LANES:(j + 1) * LANES]
            partner = jnp.where(first_half,
                                pltpu.roll(c, LANES - ROPE_PAIRS_PER_AXIS, 1),
                                pltpu.roll(c, ROPE_PAIRS_PER_AXIS, 1))
            outs.append(c * cos + partner * sin)
        return outs

    c0 = 0
    zq = _dot(h, w_ref[:, c0:c0 + ATTN_WIDTH])
    for j, o in enumerate(head_norm_rope(zq, qg_ref[...])):
        q_ref[:, j * LANES:(j + 1) * LANES] = o.astype(_bf16)
    c0 += ATTN_WIDTH
    zk = _dot(h, w_ref[:, c0:c0 + KV_WIDTH])
    (kr,) = head_norm_rope(zk, kg_ref[...])
    for j in range(N_KV_HEADS):
        k_ref[j] = kr[:, j * HEAD_DIM:(j + 1) * HEAD_DIM].astype(_bf16)
    c0 += KV_WIDTH
    zvt = lax.dot_general(wvt_ref[...], h, (((1,), (1,)), ((), ())), preferred_element_type=_f32)
    for j in range(N_KV_HEADS):
        vt_ref[j] = zvt[j * HEAD_DIM:(j + 1) * HEAD_DIM, :].astype(_bf16)
    c0 += KV_WIDTH
    u_ref[...] = _dot(h, w_ref[:, c0:c0 + POOL_WIDTH]).astype(_bf16)
    c0 += POOL_WIDTH
    sa_ref[...] = _sigmoid(_dot(h, w_ref[:, c0:c0 + D_MODEL])).astype(_bf16)
    c0 += D_MODEL
    sb_ref[...] = _sigmoid(_dot(h, w_ref[:, c0:c0 + D_MODEL])).astype(_bf16)


def _in_proj(x2, norm_g, w_in, wvt, qg, kg, cos_t, sin_t, bd, seq):
    tokens = x2.shape[0]
    tm = IN_PROJ_ROWS
    in_width = w_in.shape[1]
    tiles_per_seq = seq // tm
    const = lambda i: (0, 0)
    row = lambda i: (i, 0)
    return pl.pallas_call(
        _in_proj_body,
        grid=(tokens // tm,),
        in_specs=[
            pl.BlockSpec((tm, D_MODEL), row),
            pl.BlockSpec((1, D_MODEL), const),
            pl.BlockSpec((D_MODEL, in_width), const),
            pl.BlockSpec((KV_WIDTH, D_MODEL), const),
            pl.BlockSpec((1, ATTN_WIDTH), const),
            pl.BlockSpec((1, KV_WIDTH), const),
            pl.BlockSpec((tm, LANES), lambda i: (i % tiles_per_seq, 0)),
            pl.BlockSpec((tm, LANES), lambda i: (i % tiles_per_seq, 0)),
            pl.BlockSpec((ATTN_WIDTH, ATTN_WIDTH), const),
        ],
        out_specs=[
            pl.BlockSpec((tm, ATTN_WIDTH), row),
            pl.BlockSpec((N_KV_HEADS, tm, HEAD_DIM), lambda i: (0, i, 0)),
            pl.BlockSpec((N_KV_HEADS, HEAD_DIM, tm), lambda i: (0, 0, i)),
            pl.BlockSpec((tm, POOL_WIDTH), row),
            pl.BlockSpec((tm, D_MODEL), row),
            pl.BlockSpec((tm, D_MODEL), row),
        ],
        out_shape=[
            jax.ShapeDtypeStruct((tokens, ATTN_WIDTH), _bf16),
            jax.ShapeDtypeStruct((N_KV_HEADS, tokens, HEAD_DIM), _bf16),
            jax.ShapeDtypeStruct((N_KV_HEADS, HEAD_DIM, tokens), _bf16),
            jax.ShapeDtypeStruct((tokens, POOL_WIDTH), _bf16),
            jax.ShapeDtypeStruct((tokens, D_MODEL), _bf16),
            jax.ShapeDtypeStruct((tokens, D_MODEL), _bf16),
        ],
        compiler_params=pltpu.CompilerParams(
            dimension_semantics=("parallel",), vmem_limit_bytes=VMEM_LIMIT),
        name="in_proj",
    )(x2, norm_g, w_in, wvt, qg, kg, cos_t, sin_t, bd)


def _attention_step(q_ref, k_ref, vt_ref, o_ref, st_new, m_new, st_old, m_old):
    tq = q_ref.shape[0]
    q = q_ref[...]
    qs = jnp.concatenate([q[:, g * HEAD_DIM:(g + 1) * HEAD_DIM] for g in range(N_GROUPS_PER_KV)], axis=0)
    seq = k_ref.shape[0]
    m_prev = m_old[...]
    m_run = denom = ot = None
    for c0 in range(0, seq, ATTN_KEY_CHUNK):
        rows = slice(c0, c0 + ATTN_KEY_CHUNK)
        st = lax.dot_general(k_ref[rows, :], qs, (((1,), (1,)), ((), ())), preferred_element_type=_f32)
        st_new[rows, :] = st
        m_c = jnp.max(st, axis=0, keepdims=True)
        m_run = m_c if m_run is None else jnp.maximum(m_run, m_c)
        p = jnp.exp2(st_old[rows, :] - m_prev)
        l_c = jnp.sum(p, axis=0, keepdims=True)
        o_c = _dot(vt_ref[:, rows], p.astype(_bf16))
        denom = l_c if denom is None else denom + l_c
        ot = o_c if ot is None else ot + o_c
    m_new[...] = m_run
    ot = ot * (1.0 / denom)
    stacked = jnp.concatenate([ot[:, g * tq:(g + 1) * tq] for g in range(N_GROUPS_PER_KV)], axis=0)
    o_ref[...] = stacked.T.astype(_bf16)


def _attention_body(q_ref, k_ref, vt_ref, o_ref, st_a, m_a, st_b, m_b):
    n = pl.program_id(0)

    @pl.when(n == 0)
    def _():
        st_b[...] = jnp.zeros_like(st_b)
        m_b[...] = jnp.zeros_like(m_b)

    @pl.when(n % 2 == 0)
    def _():
        _attention_step(q_ref, k_ref, vt_ref, o_ref, st_a, m_a, st_b, m_b)

    @pl.when(n % 2 == 1)
    def _():
        _attention_step(q_ref, k_ref, vt_ref, o_ref, st_b, m_b, st_a, m_a)


def _attention(q, k2, vt, batch, seq):
    tq = ATTN_Q_ROWS
    nq = seq // tq
    n_blocks = batch * N_KV_HEADS * nq
    group_width = N_GROUPS_PER_KV * HEAD_DIM
    lanes = N_GROUPS_PER_KV * tq

    def decode(n):
        return n // (N_KV_HEADS * nq), (n // nq) % N_KV_HEADS, n % nq

    def scores_of(n):
        return decode(jnp.minimum(n, n_blocks - 1))

    def finish_of(n):
        return decode(jnp.maximum(n - 1, 0))

    def q_map(n):
        b, kh, i = scores_of(n)
        return b * nq + i, kh

    def k_map(n):
        b, kh, _ = scores_of(n)
        return kh, b, 0

    def vt_map(n):
        b, kh, _ = finish_of(n)
        return kh, 0, b

    def o_map(n):
        b, kh, i = finish_of(n)
        return b * nq + i, kh

    return pl.pallas_call(
        _attention_body,
        grid=(n_blocks + 1,),
        in_specs=[
            pl.BlockSpec((tq, group_width), q_map),
            pl.BlockSpec((None, seq, HEAD_DIM), k_map),
            pl.BlockSpec((None, HEAD_DIM, seq), vt_map),
        ],
        out_specs=pl.BlockSpec((tq, group_width), o_map),
        out_shape=jax.ShapeDtypeStruct((batch * seq, ATTN_WIDTH), _bf16),
        scratch_shapes=[
            pltpu.VMEM((seq, lanes), _f32), pltpu.VMEM((1, lanes), _f32),
            pltpu.VMEM((seq, lanes), _f32), pltpu.VMEM((1, lanes), _f32),
        ],
        compiler_params=pltpu.CompilerParams(
            dimension_semantics=("arbitrary",), vmem_limit_bytes=VMEM_LIMIT),
        name="attention",
    )(q, k2, vt)


def _mix_out_body(seq, x_ref, a_ref, up_ref, u_ref, un_ref, sa_ref, sb_ref,
                  wap_ref, pgw_ref, ps_ref, wpp_ref, wo_ref, g2_ref, wrh_ref, wrl_ref, br_ref, tri_ref,
                  x1_ref, h2_ref, idx_ref, gate_ref, rank_ref, cnt_ref,
                  ext_ref, carry_ref):
    i = pl.program_id(0)
    tm = x_ref.shape[0]

    @pl.when(i == 0)
    def _():
        carry_ref[...] = jnp.zeros_like(carry_ref)

    start = (i * tm) % seq
    has_prev = (start != 0).astype(_f32)
    has_next = (start + tm != seq).astype(_f32)
    ext_ref[0:POOL_HALO, :] = up_ref[...].astype(_f32) * has_prev
    ext_ref[POOL_HALO:POOL_HALO + tm, :] = u_ref[...].astype(_f32)
    ext_ref[POOL_HALO + tm:POOL_HALO + tm + POOL_HALO, :] = un_ref[...].astype(_f32) * has_next
    pos = start + lax.broadcasted_iota(jnp.int32, (tm, 1), 0)
    pooled = []
    for g, w in enumerate(POOL_WINDOWS):
        half = w // 2
        cols = slice(g * POOL_GROUP_DIM, (g + 1) * POOL_GROUP_DIM)
        tot = ext_ref[pl.ds(POOL_HALO - half, tm), cols]
        for d in range(-half + 1, half):
            tot = tot + ext_ref[pl.ds(POOL_HALO + d, tm), cols]
        cnt = (jnp.minimum(pos + half, seq) - jnp.maximum(pos - half, 0)).astype(_f32)
        diff = tot / cnt - ext_ref[pl.ds(POOL_HALO, tm), cols]
        pooled.append(_dot(diff.astype(_bf16), pgw_ref[g]))
    pooled = (jnp.concatenate(pooled, axis=1) * ps_ref[...]).astype(_bf16)
    y_pool = _dot(pooled, wpp_ref[...])
    y_attn = _dot(a_ref[...], wap_ref[...])
    merged = sa_ref[...].astype(_f32) * y_attn + sb_ref[...].astype(_f32) * y_pool
    x1 = x_ref[...] + _dot(merged.astype(_bf16), wo_ref[...])
    x1_ref[...] = x1

    ms = jnp.mean(x1 * x1, axis=-1, keepdims=True)
    h2 = x1 * lax.rsqrt(ms + NORM_EPS) * g2_ref[...]
    h2_ref[...] = h2
    hi, lo = _split_bf16(h2)
    wrh = wrh_ref[...]
    logits = _dot(hi, wrh) + _dot(lo, wrh) + _dot(hi, wrl_ref[...])
    lt = logits.T[:N_EXPERTS, :] + br_ref[...]
    eid = lax.broadcasted_iota(jnp.int32, lt.shape, 0)
    vals, idxs = [], []
    multi_hot = jnp.zeros(lt.shape, _f32)
    for _ in range(TOP_K):
        m = jnp.max(lt, axis=0, keepdims=True)
        sel = jnp.min(jnp.where(lt == m, eid, N_EXPERTS), axis=0, keepdims=True)
        hit = eid == sel
        vals.append(m)
        idxs.append(sel)
        multi_hot = multi_hot + hit.astype(_f32)
        lt = jnp.where(hit, -jnp.inf, lt)
    es = [jnp.exp(v - vals[0]) for v in vals]
    inv = 1.0 / (es[0] + es[1] + es[2] + es[3])
    gate_ref[...] = jnp.concatenate([e * inv for e in es], axis=0)
    idx_ref[...] = jnp.concatenate(idxs, axis=0)
    before = _dot(multi_hot.astype(_bf16), tri_ref[...]) + carry_ref[:, 0:1]
    ranks = [jnp.sum(jnp.where(eid == sel, before, 0.0), axis=0, keepdims=True) for sel in idxs]
    rank_ref[...] = jnp.concatenate(ranks, axis=0).astype(jnp.int32)
    carry_ref[...] = carry_ref[...] + jnp.sum(multi_hot, axis=1, keepdims=True)
    cnt_ref[...] = carry_ref[...]


def _mix_out(x2, attn, u, sa, sb, wap, pgw, ps, wpp, wo, g2, wrh, wrl, br, tri, seq):
    tokens = x2.shape[0]
    tm = MIX_ROWS
    n = tokens // tm
    halo_blocks = tm // POOL_HALO
    last_halo = tokens // POOL_HALO - 1
    row = lambda i: (i, 0)
    const2 = lambda i: (0, 0)
    const3 = lambda i: (0, 0, 0)
    col = lambda i: (0, i)
    return pl.pallas_call(
        functools.partial(_mix_out_body, seq),
        grid=(n,),
        in_specs=[
            pl.BlockSpec((tm, D_MODEL), row),
            pl.BlockSpec((tm, ATTN_WIDTH), row),
            pl.BlockSpec((POOL_HALO, POOL_WIDTH), lambda i: (jnp.maximum(i * halo_blocks - 1, 0), 0)),
            pl.BlockSpec((tm, POOL_WIDTH), row),
            pl.BlockSpec((POOL_HALO, POOL_WIDTH), lambda i: (jnp.minimum((i + 1) * halo_blocks, last_halo), 0)),
            pl.BlockSpec((tm, D_MODEL), row),
            pl.BlockSpec((tm, D_MODEL), row),
            pl.BlockSpec((ATTN_WIDTH, D_MODEL), const2),
            pl.BlockSpec((N_POOL_GROUPS, POOL_GROUP_DIM, POOL_GROUP_DIM), const3),
            pl.BlockSpec((1, POOL_WIDTH), const2),
            pl.BlockSpec((POOL_WIDTH, D_MODEL), const2),
            pl.BlockSpec((D_MODEL, D_MODEL), const2),
            pl.BlockSpec((1, D_MODEL), const2),
            pl.BlockSpec((D_MODEL, LANES), const2),
            pl.BlockSpec((D_MODEL, LANES), const2),
            pl.BlockSpec((N_EXPERTS, 1), const2),
            pl.BlockSpec((tm, tm), const2),
        ],
        out_specs=[
            pl.BlockSpec((tm, D_MODEL), row),
            pl.BlockSpec((tm, D_MODEL), row),
            pl.BlockSpec((TOP_K, tm), col),
            pl.BlockSpec((TOP_K, tm), col),
            pl.BlockSpec((TOP_K, tm), col),
            pl.BlockSpec((N_EXPERTS, LANES), const2),
        ],
        out_shape=[
            jax.ShapeDtypeStruct((tokens, D_MODEL), _f32),
            jax.ShapeDtypeStruct((tokens, D_MODEL), _f32),
            jax.ShapeDtypeStruct((TOP_K, tokens), jnp.int32),
            jax.ShapeDtypeStruct((TOP_K, tokens), _f32),
            jax.ShapeDtypeStruct((TOP_K, tokens), jnp.int32),
            jax.ShapeDtypeStruct((N_EXPERTS, LANES), _f32),
        ],
        scratch_shapes=[
            pltpu.VMEM((tm + 2 * POOL_HALO, POOL_WIDTH), _f32),
            pltpu.VMEM((N_EXPERTS, LANES), _f32),
        ],
        compiler_params=pltpu.CompilerParams(
            dimension_semantics=("arbitrary",), vmem_limit_bytes=VMEM_LIMIT),
        name="mix_out",
    )(x2, attn, u, u, u, sa, sb, wap, pgw, ps, wpp, wo, g2, wrh, wrl, br, tri)


def _row_copy(src_ref, src_row, dst_ref, dst_row, sem):
    return pltpu.make_async_copy(src_ref.at[pl.ds(src_row, 1)], dst_ref.at[pl.ds(dst_row, 1)], sem)


def _scatter_rows(dest_ref, h_ref, xs_ref, sem):
    tm = h_ref.shape[0]

    def issue(j, c):
        for k in range(TOP_K):
            _row_copy(h_ref, j, xs_ref, dest_ref[0, k, j], sem).start()
        return c

    lax.fori_loop(0, tm, issue, 0)

    for k in range(TOP_K):
        pltpu.make_async_copy(h_ref, xs_ref.at[pl.ds(0, tm)], sem).wait()


def _dispatch_body(n_first, start_ref, count_ref, used_ref, dest_ref, ha_ref, hb_ref, xs_ref, zero_ref, sem):
    i = pl.program_id(0)

    @pl.when(i == 0)
    def _():
        zero_ref[...] = jnp.zeros_like(zero_ref)
        n_blocks = xs_ref.shape[0] // MOE_ROWS

        def block_copy(b):
            return pltpu.make_async_copy(zero_ref, xs_ref.at[pl.ds(b * MOE_ROWS, MOE_ROWS)], sem)

        def issue_block(b, c):
            block_copy(b).start()
            return c

        lax.fori_loop(used_ref[0], n_blocks, issue_block, 0)
        for e in range(N_EXPERTS):
            base = start_ref[e]

            def issue(j, c, base=base):
                _row_copy(zero_ref, 0, xs_ref, base + j, sem).start()
                return c

            lax.fori_loop(0, count_ref[e], issue, 0)

        def drain_block(b, c):
            block_copy(0).wait()
            return c

        lax.fori_loop(used_ref[0], n_blocks, drain_block, 0)
        for e in range(N_EXPERTS):
            def drain(j, c):
                _row_copy(zero_ref, 0, xs_ref, 0, sem).wait()
                return c

            lax.fori_loop(0, count_ref[e], drain, 0)

    @pl.when(i < n_first)
    def _():
        _scatter_rows(dest_ref, ha_ref, xs_ref, sem)

    @pl.when(i >= n_first)
    def _():
        _scatter_rows(dest_ref, hb_ref, xs_ref, sem)


def _dispatch(pad_start, pad_count, n_used, dest3, h_a, h_b, n_rows):
    tm = DISPATCH_ROWS
    n_a = h_a.shape[0] // tm
    n_b = h_b.shape[0] // tm
    return pl.pallas_call(
        functools.partial(_dispatch_body, n_a),
        grid_spec=pltpu.PrefetchScalarGridSpec(
            num_scalar_prefetch=3,
            grid=(n_a + n_b,),
            in_specs=[
                pl.BlockSpec((1, TOP_K, tm), lambda i, s, c, u: (i, 0, 0), memory_space=pltpu.SMEM),
                pl.BlockSpec((tm, D_MODEL), lambda i, s, c, u: (jnp.minimum(i, n_a - 1), 0)),
                pl.BlockSpec((tm, D_MODEL), lambda i, s, c, u: (jnp.maximum(i - n_a, 0), 0)),
            ],
            out_specs=pl.BlockSpec(memory_space=pl.ANY),
            scratch_shapes=[pltpu.VMEM((MOE_ROWS, D_MODEL), _f32), pltpu.SemaphoreType.DMA(())],
        ),
        out_shape=jax.ShapeDtypeStruct((n_rows, D_MODEL), _f32),
        compiler_params=pltpu.CompilerParams(
            dimension_semantics=("arbitrary",), vmem_limit_bytes=VMEM_LIMIT),
        name="dispatch",
    )(pad_start, pad_count, n_used, dest3, h_a, h_b)


def _experts_body(be_ref, nu_ref, xs_ref, wgu_ref, bgu_ref, wd_ref, bd_ref, y_ref):
    del be_ref

    used = pl.program_id(0) < nu_ref[0]

    @pl.when(jnp.logical_not(used))
    def _():
        y_ref[...] = jnp.zeros_like(y_ref)

    @pl.when(used)
    def _():
        x = xs_ref[...].astype(_bf16)
        gu = _dot(x, wgu_ref[...]) + bgu_ref[...]
        gate = jnp.minimum(gu[:, :D_FF], SWIGLU_LIMIT)
        up = jnp.clip(gu[:, D_FF:], -SWIGLU_LIMIT, SWIGLU_LIMIT)
        act = (up + 1.0) * (gate * _sigmoid(SWIGLU_ALPHA * gate))
        y_ref[...] = _dot(act.astype(_bf16), wd_ref[...]) + bd_ref[...]


def _experts(block_e, n_used, xs, wgu, bgu, wd, bd):
    n_rows = xs.shape[0]
    n_blocks = n_rows // MOE_ROWS
    rows = lambda i, be, nu: (jnp.minimum(i, nu[0] - 1), 0)
    per_e = lambda i, be, nu: (be[i], 0, 0)
    return pl.pallas_call(
        _experts_body,
        grid_spec=pltpu.PrefetchScalarGridSpec(
            num_scalar_prefetch=2,
            grid=(n_blocks,),
            in_specs=[
                pl.BlockSpec((MOE_ROWS, D_MODEL), rows),
                pl.BlockSpec((None, D_MODEL, 2 * D_FF), per_e),
                pl.BlockSpec((None, 1, 2 * D_FF), per_e),
                pl.BlockSpec((None, D_FF, D_MODEL), per_e),
                pl.BlockSpec((None, 1, D_MODEL), per_e),
            ],
            out_specs=pl.BlockSpec((MOE_ROWS, D_MODEL), lambda i, be, nu: (i, 0)),
        ),
        out_shape=jax.ShapeDtypeStruct((n_rows, D_MODEL), _f32),
        compiler_params=pltpu.CompilerParams(
            dimension_semantics=("arbitrary",), vmem_limit_bytes=VMEM_LIMIT),
        name="experts",
    )(block_e, n_used, xs, wgu, bgu, wd, bd)


def _combine_body(dest_ref, x1_ref, gate_ref, yb_ref, o_ref, buf_ref, sem):
    tm = x1_ref.shape[0]

    def issue(j, c):
        for k in range(TOP_K):
            _row_copy(yb_ref, dest_ref[0, k, j], buf_ref.at[k], j, sem).start()
        return c

    lax.fori_loop(0, tm, issue, 0)

    for k in range(TOP_K):
        pltpu.make_async_copy(yb_ref.at[pl.ds(0, tm)], buf_ref.at[k], sem).wait()
    gates = gate_ref[...]
    acc = gates[:, 0:1] * buf_ref[0]
    for k in range(1, TOP_K):
        acc = acc + gates[:, k:k + 1] * buf_ref[k]
    o_ref[...] = x1_ref[...] + acc


def _combine(dest3, x1, gates_t, yb):
    tokens = x1.shape[0]
    tm = COMBINE_ROWS
    return pl.pallas_call(
        _combine_body,
        grid=(tokens // tm,),
        in_specs=[
            pl.BlockSpec((1, TOP_K, tm), lambda i: (i, 0, 0), memory_space=pltpu.SMEM),
            pl.BlockSpec((tm, D_MODEL), lambda i: (i, 0)),
            pl.BlockSpec((tm, TOP_K), lambda i: (i, 0)),
            pl.BlockSpec(memory_space=pl.ANY),
        ],
        out_specs=pl.BlockSpec((tm, D_MODEL), lambda i: (i, 0)),
        out_shape=jax.ShapeDtypeStruct((tokens, D_MODEL), _f32),
        scratch_shapes=[pltpu.VMEM((TOP_K, tm, D_MODEL), _f32), pltpu.SemaphoreType.DMA(())],
        compiler_params=pltpu.CompilerParams(
            dimension_semantics=("arbitrary",), vmem_limit_bytes=VMEM_LIMIT),
        name="combine",
    )(dest3, x1, gates_t, yb)


def _rope_tables(seq):
    pos = jnp.arange(seq)
    row_ids = (pos // GRID_W).astype(_f32)
    col_ids = (pos % GRID_W).astype(_f32)
    inv_freq = ROPE_THETA ** (-jnp.arange(ROPE_PAIRS_PER_AXIS, dtype=_f32) / ROPE_PAIRS_PER_AXIS)
    ang_r = row_ids[:, None] * inv_freq
    ang_c = col_ids[:, None] * inv_freq
    cos = jnp.concatenate([jnp.cos(ang_r)] * 2 + [jnp.cos(ang_c)] * 2, axis=1)
    sin = jnp.concatenate([-jnp.sin(ang_r), jnp.sin(ang_r), -jnp.sin(ang_c), jnp.sin(ang_c)], axis=1)
    reps = LANES // HEAD_DIM
    return jnp.tile(cos, (1, reps)), jnp.tile(sin, (1, reps))


def _mixer(x, p):
    batch, seq, _ = x.shape
    x2 = x.reshape(batch * seq, D_MODEL)
    cos_t, sin_t = _rope_tables(seq)
    q, k2, vt, u, sa, sb = _in_proj(x2, p["mix_g"], p["w_in"], p["wvt"], p["qg"], p["kg"], cos_t, sin_t, p["bd"], seq)
    attn = _attention(q, k2, vt, batch, seq)
    return _mix_out(x2, attn, u, sa, sb, p["wap"], p["pgw"], p["ps"], p["wpp"], p["wo"], p["ffn_g"],
                    p["wrh"], p["wrl"], p["br"], p["tri"], seq)


def kernel(x_prompt, x_sample, mix_norm_g, w_in, q_norm_g, k_norm_g, w_attn_proj, pool_group_w, pool_scale,
           w_pool_proj, w_out, ffn_norm_g, w_router, b_router, w_gu, b_gu, w_down, b_down):
    depth = w_in.shape[0]
    xs_all = [x_prompt, x_sample]
    head_id = jnp.arange(ATTN_WIDTH) // HEAD_DIM
    block_diag = (head_id[:, None] == head_id[None, :]).astype(_bf16)
    tri_id = jnp.arange(MIX_ROWS)
    tri = (tri_id[:, None] < tri_id[None, :]).astype(_bf16)
    for l in range(depth):
        wr = jnp.pad(w_router[l].astype(_f32), ((0, 0), (0, LANES - N_EXPERTS)))
        wrh = wr.astype(_bf16)
        p = dict(
            mix_g=mix_norm_g[l].reshape(1, D_MODEL),
            w_in=w_in[l].astype(_bf16),
            wvt=w_in[l][:, ATTN_WIDTH + KV_WIDTH:ATTN_WIDTH + 2 * KV_WIDTH].T.astype(_bf16),
            qg=jnp.tile(q_norm_g[l] * (HEAD_DIM ** -0.5 * LOG2_E), N_HEADS).reshape(1, ATTN_WIDTH),
            kg=jnp.tile(k_norm_g[l], N_KV_HEADS).reshape(1, KV_WIDTH),
            bd=block_diag,
            wap=w_attn_proj[l].astype(_bf16),
            pgw=pool_group_w[l].astype(_bf16),
            ps=pool_scale[l].reshape(1, POOL_WIDTH),
            wpp=w_pool_proj[l].astype(_bf16),
            wo=w_out[l].astype(_bf16),
            ffn_g=ffn_norm_g[l].reshape(1, D_MODEL),
            wrh=wrh,
            wrl=(wr - wrh.astype(_f32)).astype(_bf16),
            br=b_router[l].astype(_f32).reshape(N_EXPERTS, 1),
            tri=tri,
        )
        mixed = [_mixer(x, p) for x in xs_all]

        counts = [m[5][:, 0].astype(jnp.int32) for m in mixed]
        total = sum(counts)
        padded = ((total + MOE_ROWS - 1) // MOE_ROWS) * MOE_ROWS
        padded_end = jnp.cumsum(padded)
        padded_start = padded_end - padded
        n_tok = sum(m[0].shape[0] for m in mixed)
        n_blocks = (n_tok * TOP_K + MOE_ROWS - 1) // MOE_ROWS + N_EXPERTS
        n_used = (padded_end[-1] // MOE_ROWS).astype(jnp.int32)
        blk = jnp.minimum(jnp.arange(n_blocks, dtype=jnp.int32), n_used - 1)
        block_e = jnp.sum((padded_end[None, :] <= (blk * MOE_ROWS)[:, None]).astype(jnp.int32), axis=1)
        block_e = jnp.minimum(block_e, N_EXPERTS - 1)
        expert_ids = jnp.arange(N_EXPERTS, dtype=jnp.int32)[:, None, None]
        dests = []
        seen = jnp.zeros((N_EXPERTS,), jnp.int32)
        for m, c in zip(mixed, counts):
            base = (padded_start + seen)[:, None, None]
            dests.append(jnp.sum(jnp.where(m[2][None] == expert_ids, base, 0), axis=0) + m[4])
            seen = seen + c
        d3 = jnp.concatenate([d.reshape(TOP_K, -1, DISPATCH_ROWS).transpose(1, 0, 2) for d in dests], axis=0)
        xs_buf = _dispatch((padded_start + total).astype(jnp.int32), (padded - total).astype(jnp.int32),
                           n_used.reshape(1), d3, mixed[0][1], mixed[1][1], n_blocks * MOE_ROWS)
        yb = _experts(block_e, n_used.reshape(1), xs_buf, w_gu[l], b_gu[l].reshape(N_EXPERTS, 1, -1),
                      w_down[l], b_down[l].reshape(N_EXPERTS, 1, -1))
        outs = []
        for x, m, dest in zip(xs_all, mixed, dests):
            d3 = dest.reshape(TOP_K, -1, COMBINE_ROWS).transpose(1, 0, 2)
            outs.append(_combine(d3, m[0], m[3].T, yb).reshape(x.shape))
        xs_all = outs
    return tuple(xs_all)
```

```python
import functools

import jax
import jax.numpy as jnp
from jax import lax
from jax.experimental import pallas as pl
from jax.experimental.pallas import tpu as pltpu

D_MODEL = 1024
GRID_W = 64
N_HEADS = 8
N_KV_HEADS = 2
HEAD_DIM = 64
N_GROUPS_PER_KV = N_HEADS // N_KV_HEADS
ATTN_WIDTH = N_HEADS * HEAD_DIM
KV_WIDTH = N_KV_HEADS * HEAD_DIM
ROPE_THETA = 10000.0
ROPE_PAIRS_PER_AXIS = HEAD_DIM // 4
POOL_WINDOWS = (2, 4, 8, 16)
N_POOL_GROUPS = 4
POOL_WIDTH = 512
POOL_GROUP_DIM = POOL_WIDTH // N_POOL_GROUPS
N_EXPERTS = 32
TOP_K = 4
D_FF = 1024
SWIGLU_ALPHA = 1.702
SWIGLU_LIMIT = 7.0
NORM_EPS = 1e-6

LANES = 128
SUBLANES = 8
POOL_HALO = 8
LOG2_E = 1.4426950408889634

IN_PROJ_ROWS = 512
ATTN_Q_ROWS = 128
ATTN_KEY_CHUNK = 512
MIX_ROWS = 256
MOE_ROWS = 512
DISPATCH_ROWS = 256
COMBINE_ROWS = 128
ISSUE_UNROLL = 4
VMEM_LIMIT = 56 * 1024 * 1024

_bf16 = jnp.bfloat16
_f32 = jnp.float32


def _dot(a, b):
    return jnp.dot(a, b, preferred_element_type=_f32)


def _split_bf16(x):
    hi = x.astype(_bf16)
    lo = (x - hi.astype(_f32)).astype(_bf16)
    return hi, lo


def _sigmoid(x):
    return 1.0 / (1.0 + jnp.exp(-x))


def _in_proj_body(x_ref, g_ref, w_ref, wvt_ref, qg_ref, kg_ref, cos_ref, sin_ref, bd_ref,
                  q_ref, k_ref, vt_ref, u_ref, sa_ref, sb_ref):
    x = x_ref[...]
    ms = jnp.mean(x * x, axis=-1, keepdims=True)
    h = (x * lax.rsqrt(ms + NORM_EPS) * g_ref[...]).astype(_bf16)
    cos = cos_ref[...]
    sin = sin_ref[...]
    lane = lax.broadcasted_iota(jnp.int32, cos.shape, 1)
    first_half = (lane % (2 * ROPE_PAIRS_PER_AXIS)) < ROPE_PAIRS_PER_AXIS

    def head_norm_rope(z, gain):
        width = z.shape[1]
        hi, lo = _split_bf16(z * z)
        bd = bd_ref[:width, :width]
        ss = _dot(hi, bd) + _dot(lo, bd)
        zn = z * lax.rsqrt(ss * (1.0 / HEAD_DIM) + NORM_EPS) * gain
        outs = []
        for j in range(width // LANES):
            c = zn[:, j * LANES:(j + 1) * LANES]
            partner = jnp.where(first_half,
                                pltpu.roll(c, LANES - ROPE_PAIRS_PER_AXIS, 1),
                                pltpu.roll(c, ROPE_PAIRS_PER_AXIS, 1))
            outs.append(c * cos + partner * sin)
        return outs

    c0 = 0
    zq = _dot(h, w_ref[:, c0:c0 + ATTN_WIDTH])
    for j, o in enumerate(head_norm_rope(zq, qg_ref[...])):
        q_ref[:, j * LANES:(j + 1) * LANES] = o.astype(_bf16)
    c0 += ATTN_WIDTH
    zk = _dot(h, w_ref[:, c0:c0 + KV_WIDTH])
    (kr,) = head_norm_rope(zk, kg_ref[...])
    for j in range(N_KV_HEADS):
        k_ref[j] = kr[:, j * HEAD_DIM:(j + 1) * HEAD_DIM].astype(_bf16)
    c0 += KV_WIDTH
    zvt = lax.dot_general(wvt_ref[...], h, (((1,), (1,)), ((), ())), preferred_element_type=_f32)
    for j in range(N_KV_HEADS):
        vt_ref[j] = zvt[j * HEAD_DIM:(j + 1) * HEAD_DIM, :].astype(_bf16)
    c0 += KV_WIDTH
    u_ref[...] = _dot(h, w_ref[:, c0:c0 + POOL_WIDTH]).astype(_bf16)
    c0 += POOL_WIDTH
    sa_ref[...] = _sigmoid(_dot(h, w_ref[:, c0:c0 + D_MODEL])).astype(_bf16)
    c0 += D_MODEL
    sb_ref[...] = _sigmoid(_dot(h, w_ref[:, c0:c0 + D_MODEL])).astype(_bf16)


def _in_proj(x2, norm_g, w_in, wvt, qg, kg, cos_t, sin_t, bd, seq):
    tokens = x2.shape[0]
    tm = IN_PROJ_ROWS
    in_width = w_in.shape[1]
    tiles_per_seq = seq // tm
    const = lambda i: (0, 0)
    row = lambda i: (i, 0)
    return pl.pallas_call(
        _in_proj_body,
        grid=(tokens // tm,),
        in_specs=[
            pl.BlockSpec((tm, D_MODEL), row),
            pl.BlockSpec((1, D_MODEL), const),
            pl.BlockSpec((D_MODEL, in_width), const),
            pl.BlockSpec((KV_WIDTH, D_MODEL), const),
            pl.BlockSpec((1, ATTN_WIDTH), const),
            pl.BlockSpec((1, KV_WIDTH), const),
            pl.BlockSpec((tm, LANES), lambda i: (i % tiles_per_seq, 0)),
            pl.BlockSpec((tm, LANES), lambda i: (i % tiles_per_seq, 0)),
            pl.BlockSpec((ATTN_WIDTH, ATTN_WIDTH), const),
        ],
        out_specs=[
            pl.BlockSpec((tm, ATTN_WIDTH), row),
            pl.BlockSpec((N_KV_HEADS, tm, HEAD_DIM), lambda i: (0, i, 0)),
            pl.BlockSpec((N_KV_HEADS, HEAD_DIM, tm), lambda i: (0, 0, i)),
            pl.BlockSpec((tm, POOL_WIDTH), row),
            pl.BlockSpec((tm, D_MODEL), row),
            pl.BlockSpec((tm, D_MODEL), row),
        ],
        out_shape=[
            jax.ShapeDtypeStruct((tokens, ATTN_WIDTH), _bf16),
            jax.ShapeDtypeStruct((N_KV_HEADS, tokens, HEAD_DIM), _bf16),
            jax.ShapeDtypeStruct((N_KV_HEADS, HEAD_DIM, tokens), _bf16),
            jax.ShapeDtypeStruct((tokens, POOL_WIDTH), _bf16),
            jax.ShapeDtypeStruct((tokens, D_MODEL), _bf16),
            jax.ShapeDtypeStruct((tokens, D_MODEL), _bf16),
        ],
        compiler_params=pltpu.CompilerParams(
            dimension_semantics=("parallel",), vmem_limit_bytes=VMEM_LIMIT),
        name="in_proj",
    )(x2, norm_g, w_in, wvt, qg, kg, cos_t, sin_t, bd)


def _attention_step(q_ref, k_ref, vt_ref, o_ref, st_new, m_new, st_old, m_old):
    tq = q_ref.shape[0]
    q = q_ref[...]
    qs = jnp.concatenate([q[:, g * HEAD_DIM:(g + 1) * HEAD_DIM] for g in range(N_GROUPS_PER_KV)], axis=0)
    seq = k_ref.shape[0]
    m_prev = m_old[...]
    m_run = denom = ot = None
    for c0 in range(0, seq, ATTN_KEY_CHUNK):
        rows = slice(c0, c0 + ATTN_KEY_CHUNK)
        st = lax.dot_general(k_ref[rows, :], qs, (((1,), (1,)), ((), ())), preferred_element_type=_f32)
        st_new[rows, :] = st
        m_c = jnp.max(st, axis=0, keepdims=True)
        m_run = m_c if m_run is None else jnp.maximum(m_run, m_c)
        p = jnp.exp2(st_old[rows, :] - m_prev)
        l_c = jnp.sum(p, axis=0, keepdims=True)
        o_c = _dot(vt_ref[:, rows], p.astype(_bf16))
        denom = l_c if denom is None else denom + l_c
        ot = o_c if ot is None else ot + o_c
    m_new[...] = m_run
    ot = ot * (1.0 / denom)
    stacked = jnp.concatenate([ot[:, g * tq:(g + 1) * tq] for g in range(N_GROUPS_PER_KV)], axis=0)
    o_ref[...] = stacked.T.astype(_bf16)


def _attention_body(q_ref, k_ref, vt_ref, o_ref, st_a, m_a, st_b, m_b):
    n = pl.program_id(0)

    @pl.when(n == 0)
    def _():
        st_b[...] = jnp.zeros_like(st_b)
        m_b[...] = jnp.zeros_like(m_b)

    @pl.when(n % 2 == 0)
    def _():
        _attention_step(q_ref, k_ref, vt_ref, o_ref, st_a, m_a, st_b, m_b)

    @pl.when(n % 2 == 1)
    def _():
        _attention_step(q_ref, k_ref, vt_ref, o_ref, st_b, m_b, st_a, m_a)


def _attention(q, k2, vt, batch, seq):
    tq = ATTN_Q_ROWS
    nq = seq // tq
    n_blocks = batch * N_KV_HEADS * nq
    group_width = N_GROUPS_PER_KV * HEAD_DIM
    lanes = N_GROUPS_PER_KV * tq

    def decode(n):
        return n // (N_KV_HEADS * nq), (n // nq) % N_KV_HEADS, n % nq

    def scores_of(n):
        return decode(jnp.minimum(n, n_blocks - 1))

    def finish_of(n):
        return decode(jnp.maximum(n - 1, 0))

    def q_map(n):
        b, kh, i = scores_of(n)
        return b * nq + i, kh

    def k_map(n):
        b, kh, _ = scores_of(n)
        return kh, b, 0

    def vt_map(n):
        b, kh, _ = finish_of(n)
        return kh, 0, b

    def o_map(n):
        b, kh, i = finish_of(n)
        return b * nq + i, kh

    return pl.pallas_call(
        _attention_body,
        grid=(n_blocks + 1,),
        in_specs=[
            pl.BlockSpec((tq, group_width), q_map),
            pl.BlockSpec((None, seq, HEAD_DIM), k_map),
            pl.BlockSpec((None, HEAD_DIM, seq), vt_map),
        ],
        out_specs=pl.BlockSpec((tq, group_width), o_map),
        out_shape=jax.ShapeDtypeStruct((batch * seq, ATTN_WIDTH), _bf16),
        scratch_shapes=[
            pltpu.VMEM((seq, lanes), _f32), pltpu.VMEM((1, lanes), _f32),
            pltpu.VMEM((seq, lanes), _f32), pltpu.VMEM((1, lanes), _f32),
        ],
        compiler_params=pltpu.CompilerParams(
            dimension_semantics=("arbitrary",), vmem_limit_bytes=VMEM_LIMIT),
        name="attention",
    )(q, k2, vt)


def _mix_out_body(seq, x_ref, a_ref, up_ref, u_ref, un_ref, sa_ref, sb_ref,
                  wap_ref, pgw_ref, ps_ref, wpp_ref, wo_ref, g2_ref, wrh_ref, wrl_ref, br_ref, tri_ref,
                  x1_ref, h2_ref, idx_ref, gate_ref, rank_ref, cnt_ref,
                  ext_ref, carry_ref):
    i = pl.program_id(0)
    tm = x_ref.shape[0]

    @pl.when(i == 0)
    def _():
        carry_ref[...] = jnp.zeros_like(carry_ref)

    start = (i * tm) % seq
    has_prev = (start != 0).astype(_f32)
    has_next = (start + tm != seq).astype(_f32)
    ext_ref[0:POOL_HALO, :] = up_ref[...].astype(_f32) * has_prev
    ext_ref[POOL_HALO:POOL_HALO + tm, :] = u_ref[...].astype(_f32)
    ext_ref[POOL_HALO + tm:POOL_HALO + tm + POOL_HALO, :] = un_ref[...].astype(_f32) * has_next
    pos = start + lax.broadcasted_iota(jnp.int32, (tm, 1), 0)
    pooled = []
    for g, w in enumerate(POOL_WINDOWS):
        half = w // 2
        cols = slice(g * POOL_GROUP_DIM, (g + 1) * POOL_GROUP_DIM)
        tot = ext_ref[pl.ds(POOL_HALO - half, tm), cols]
        for d in range(-half + 1, half):
            tot = tot + ext_ref[pl.ds(POOL_HALO + d, tm), cols]
        cnt = (jnp.minimum(pos + half, seq) - jnp.maximum(pos - half, 0)).astype(_f32)
        diff = tot / cnt - ext_ref[pl.ds(POOL_HALO, tm), cols]
        pooled.append(_dot(diff.astype(_bf16), pgw_ref[g]))
    pooled = (jnp.concatenate(pooled, axis=1) * ps_ref[...]).astype(_bf16)
    y_pool = _dot(pooled, wpp_ref[...])
    y_attn = _dot(a_ref[...], wap_ref[...])
    merged = sa_ref[...].astype(_f32) * y_attn + sb_ref[...].astype(_f32) * y_pool
    x1 = x_ref[...] + _dot(merged.astype(_bf16), wo_ref[...])
    x1_ref[...] = x1

    ms = jnp.mean(x1 * x1, axis=-1, keepdims=True)
    h2 = x1 * lax.rsqrt(ms + NORM_EPS) * g2_ref[...]
    h2_ref[...] = h2
    hi, lo = _split_bf16(h2)
    wrh = wrh_ref[...]
    logits = _dot(hi, wrh) + _dot(lo, wrh) + _dot(hi, wrl_ref[...])
    lt = logits.T[:N_EXPERTS, :] + br_ref[...]
    eid = lax.broadcasted_iota(jnp.int32, lt.shape, 0)
    vals, idxs = [], []
    multi_hot = jnp.zeros(lt.shape, _f32)
    for _ in range(TOP_K):
        m = jnp.max(lt, axis=0, keepdims=True)
        sel = jnp.min(jnp.where(lt == m, eid, N_EXPERTS), axis=0, keepdims=True)
        hit = eid == sel
        vals.append(m)
        idxs.append(sel)
        multi_hot = multi_hot + hit.astype(_f32)
        lt = jnp.where(hit, -jnp.inf, lt)
    es = [jnp.exp(v - vals[0]) for v in vals]
    inv = 1.0 / (es[0] + es[1] + es[2] + es[3])
    gate_ref[...] = jnp.concatenate([e * inv for e in es], axis=0)
    idx_ref[...] = jnp.concatenate(idxs, axis=0)
    before = _dot(multi_hot.astype(_bf16), tri_ref[...]) + carry_ref[:, 0:1]
    ranks = [jnp.sum(jnp.where(eid == sel, before, 0.0), axis=0, keepdims=True) for sel in idxs]
    rank_ref[...] = jnp.concatenate(ranks, axis=0).astype(jnp.int32)
    carry_ref[...] = carry_ref[...] + jnp.sum(multi_hot, axis=1, keepdims=True)
    cnt_ref[...] = carry_ref[...]


def _mix_out(x2, attn, u, sa, sb, wap, pgw, ps, wpp, wo, g2, wrh, wrl, br, tri, seq):
    tokens = x2.shape[0]
    tm = MIX_ROWS
    n = tokens // tm
    halo_blocks = tm // POOL_HALO
    last_halo = tokens // POOL_HALO - 1
    row = lambda i: (i, 0)
    const2 = lambda i: (0, 0)
    const3 = lambda i: (0, 0, 0)
    col = lambda i: (0, i)
    return pl.pallas_call(
        functools.partial(_mix_out_body, seq),
        grid=(n,),
        in_specs=[
            pl.BlockSpec((tm, D_MODEL), row),
            pl.BlockSpec((tm, ATTN_WIDTH), row),
            pl.BlockSpec((POOL_HALO, POOL_WIDTH), lambda i: (jnp.maximum(i * halo_blocks - 1, 0), 0)),
            pl.BlockSpec((tm, POOL_WIDTH), row),
            pl.BlockSpec((POOL_HALO, POOL_WIDTH), lambda i: (jnp.minimum((i + 1) * halo_blocks, last_halo), 0)),
            pl.BlockSpec((tm, D_MODEL), row),
            pl.BlockSpec((tm, D_MODEL), row),
            pl.BlockSpec((ATTN_WIDTH, D_MODEL), const2),
            pl.BlockSpec((N_POOL_GROUPS, POOL_GROUP_DIM, POOL_GROUP_DIM), const3),
            pl.BlockSpec((1, POOL_WIDTH), const2),
            pl.BlockSpec((POOL_WIDTH, D_MODEL), const2),
            pl.BlockSpec((D_MODEL, D_MODEL), const2),
            pl.BlockSpec((1, D_MODEL), const2),
            pl.BlockSpec((D_MODEL, LANES), const2),
            pl.BlockSpec((D_MODEL, LANES), const2),
            pl.BlockSpec((N_EXPERTS, 1), const2),
            pl.BlockSpec((tm, tm), const2),
        ],
        out_specs=[
            pl.BlockSpec((tm, D_MODEL), row),
            pl.BlockSpec((tm, D_MODEL), row),
            pl.BlockSpec((TOP_K, tm), col),
            pl.BlockSpec((TOP_K, tm), col),
            pl.BlockSpec((TOP_K, tm), col),
            pl.BlockSpec((N_EXPERTS, LANES), const2),
        ],
        out_shape=[
            jax.ShapeDtypeStruct((tokens, D_MODEL), _f32),
            jax.ShapeDtypeStruct((tokens, D_MODEL), _f32),
            jax.ShapeDtypeStruct((TOP_K, tokens), jnp.int32),
            jax.ShapeDtypeStruct((TOP_K, tokens), _f32),
            jax.ShapeDtypeStruct((TOP_K, tokens), jnp.int32),
            jax.ShapeDtypeStruct((N_EXPERTS, LANES), _f32),
        ],
        scratch_shapes=[
            pltpu.VMEM((tm + 2 * POOL_HALO, POOL_WIDTH), _f32),
            pltpu.VMEM((N_EXPERTS, LANES), _f32),
        ],
        compiler_params=pltpu.CompilerParams(
            dimension_semantics=("arbitrary",), vmem_limit_bytes=VMEM_LIMIT),
        name="mix_out",
    )(x2, attn, u, u, u, sa, sb, wap, pgw, ps, wpp, wo, g2, wrh, wrl, br, tri)


def _row_copy(src_ref, src_row, dst_ref, dst_row, sem):
    return pltpu.make_async_copy(src_ref.at[pl.ds(src_row, 1)], dst_ref.at[pl.ds(dst_row, 1)], sem)


def _scatter_rows(dest_ref, h_ref, xs_ref, sem):
    tm = h_ref.shape[0]

    def issue(g, c):
        for u in range(ISSUE_UNROLL):
            j = g * ISSUE_UNROLL + u
            for k in range(TOP_K):
                _row_copy(h_ref, j, xs_ref, dest_ref[0, k, j], sem).start()
        return c

    lax.fori_loop(0, tm // ISSUE_UNROLL, issue, 0)

    for k in range(TOP_K):
        pltpu.make_async_copy(h_ref, xs_ref.at[pl.ds(0, tm)], sem).wait()


def _dispatch_body(n_first, start_ref, count_ref, used_ref, dest_ref, ha_ref, hb_ref, xs_ref, zero_ref, sem):
    i = pl.program_id(0)

    @pl.when(i == 0)
    def _():
        zero_ref[...] = jnp.zeros_like(zero_ref)
        n_blocks = xs_ref.shape[0] // MOE_ROWS

        def block_copy(b):
            return pltpu.make_async_copy(zero_ref, xs_ref.at[pl.ds(b * MOE_ROWS, MOE_ROWS)], sem)

        def issue_block(b, c):
            block_copy(b).start()
            return c

        lax.fori_loop(used_ref[0], n_blocks, issue_block, 0)
        for e in range(N_EXPERTS):
            base = start_ref[e]

            def issue(j, c, base=base):
                _row_copy(zero_ref, 0, xs_ref, base + j, sem).start()
                return c

            lax.fori_loop(0, count_ref[e], issue, 0)

        def drain_block(b, c):
            block_copy(0).wait()
            return c

        lax.fori_loop(used_ref[0], n_blocks, drain_block, 0)
        for e in range(N_EXPERTS):
            def drain(j, c):
                _row_copy(zero_ref, 0, xs_ref, 0, sem).wait()
                return c

            lax.fori_loop(0, count_ref[e], drain, 0)

    @pl.when(i < n_first)
    def _():
        _scatter_rows(dest_ref, ha_ref, xs_ref, sem)

    @pl.when(i >= n_first)
    def _():
        _scatter_rows(dest_ref, hb_ref, xs_ref, sem)


def _dispatch(pad_start, pad_count, n_used, dest3, h_a, h_b, n_rows):
    tm = DISPATCH_ROWS
    n_a = h_a.shape[0] // tm
    n_b = h_b.shape[0] // tm
    return pl.pallas_call(
        functools.partial(_dispatch_body, n_a),
        grid_spec=pltpu.PrefetchScalarGridSpec(
            num_scalar_prefetch=3,
            grid=(n_a + n_b,),
            in_specs=[
                pl.BlockSpec((1, TOP_K, tm), lambda i, s, c, u: (i, 0, 0), memory_space=pltpu.SMEM),
                pl.BlockSpec((tm, D_MODEL), lambda i, s, c, u: (jnp.minimum(i, n_a - 1), 0)),
                pl.BlockSpec((tm, D_MODEL), lambda i, s, c, u: (jnp.maximum(i - n_a, 0), 0)),
            ],
            out_specs=pl.BlockSpec(memory_space=pl.ANY),
            scratch_shapes=[pltpu.VMEM((MOE_ROWS, D_MODEL), _f32), pltpu.SemaphoreType.DMA(())],
        ),
        out_shape=jax.ShapeDtypeStruct((n_rows, D_MODEL), _f32),
        compiler_params=pltpu.CompilerParams(
            dimension_semantics=("arbitrary",), vmem_limit_bytes=VMEM_LIMIT),
        name="dispatch",
    )(pad_start, pad_count, n_used, dest3, h_a, h_b)


def _experts_body(be_ref, nu_ref, xs_ref, wgu_ref, bgu_ref, wd_ref, bd_ref, y_ref, wgu_bf, wd_bf):
    i = pl.program_id(0)
    used = i < nu_ref[0]

    @pl.when(jnp.logical_not(used))
    def _():
        y_ref[...] = jnp.zeros_like(y_ref)

    @pl.when(jnp.logical_and(used, jnp.logical_or(i == 0, be_ref[i] != be_ref[jnp.maximum(i - 1, 0)])))
    def _():
        wgu_bf[...] = wgu_ref[...].astype(_bf16)
        wd_bf[...] = wd_ref[...].astype(_bf16)

    @pl.when(used)
    def _():
        x = xs_ref[...].astype(_bf16)
        gu = _dot(x, wgu_bf[...]) + bgu_ref[...]
        gate = jnp.minimum(gu[:, :D_FF], SWIGLU_LIMIT)
        up = jnp.clip(gu[:, D_FF:], -SWIGLU_LIMIT, SWIGLU_LIMIT)
        act = (up + 1.0) * (gate * _sigmoid(SWIGLU_ALPHA * gate))
        y_ref[...] = _dot(act.astype(_bf16), wd_bf[...]) + bd_ref[...]


def _experts(block_e, n_used, xs, wgu, bgu, wd, bd):
    n_rows = xs.shape[0]
    n_blocks = n_rows // MOE_ROWS
    rows = lambda i, be, nu: (jnp.minimum(i, nu[0] - 1), 0)
    per_e = lambda i, be, nu: (be[i], 0, 0)
    return pl.pallas_call(
        _experts_body,
        grid_spec=pltpu.PrefetchScalarGridSpec(
            num_scalar_prefetch=2,
            grid=(n_blocks,),
            in_specs=[
                pl.BlockSpec((MOE_ROWS, D_MODEL), rows),
                pl.BlockSpec((None, D_MODEL, 2 * D_FF), per_e),
                pl.BlockSpec((None, 1, 2 * D_FF), per_e),
                pl.BlockSpec((None, D_FF, D_MODEL), per_e),
                pl.BlockSpec((None, 1, D_MODEL), per_e),
            ],
            out_specs=pl.BlockSpec((MOE_ROWS, D_MODEL), lambda i, be, nu: (i, 0)),
            scratch_shapes=[pltpu.VMEM((D_MODEL, 2 * D_FF), _bf16), pltpu.VMEM((D_FF, D_MODEL), _bf16)],
        ),
        out_shape=jax.ShapeDtypeStruct((n_rows, D_MODEL), _f32),
        compiler_params=pltpu.CompilerParams(
            dimension_semantics=("arbitrary",), vmem_limit_bytes=VMEM_LIMIT),
        name="experts",
    )(block_e, n_used, xs, wgu, bgu, wd, bd)


def _combine_body(dest_ref, next_ref, x1_ref, gate_ref, yb_ref, o_ref, buf_ref, sems):
    tm = x1_ref.shape[0]
    i = pl.program_id(0)
    slot = i % 2

    def gather(idx_ref, s):
        def issue(g, c):
            for u in range(ISSUE_UNROLL):
                j = g * ISSUE_UNROLL + u
                for k in range(TOP_K):
                    _row_copy(yb_ref, idx_ref[0, k, j], buf_ref.at[s, k], j, sems.at[s]).start()
            return c

        lax.fori_loop(0, tm // ISSUE_UNROLL, issue, 0)

    @pl.when(i == 0)
    def _():
        gather(dest_ref, slot)

    @pl.when(i + 1 < pl.num_programs(0))
    def _():
        gather(next_ref, 1 - slot)

    for k in range(TOP_K):
        pltpu.make_async_copy(yb_ref.at[pl.ds(0, tm)], buf_ref.at[slot, k], sems.at[slot]).wait()
    gates = gate_ref[...]
    acc = gates[:, 0:1] * buf_ref[slot, 0]
    for k in range(1, TOP_K):
        acc = acc + gates[:, k:k + 1] * buf_ref[slot, k]
    o_ref[...] = x1_ref[...] + acc


def _combine(dest3, x1, gates_t, yb):
    tokens = x1.shape[0]
    tm = COMBINE_ROWS
    n = tokens // tm
    return pl.pallas_call(
        _combine_body,
        grid=(n,),
        in_specs=[
            pl.BlockSpec((1, TOP_K, tm), lambda i: (i, 0, 0), memory_space=pltpu.SMEM),
            pl.BlockSpec((1, TOP_K, tm), lambda i: (jnp.minimum(i + 1, n - 1), 0, 0), memory_space=pltpu.SMEM),
            pl.BlockSpec((tm, D_MODEL), lambda i: (i, 0)),
            pl.BlockSpec((tm, TOP_K), lambda i: (i, 0)),
            pl.BlockSpec(memory_space=pl.ANY),
        ],
        out_specs=pl.BlockSpec((tm, D_MODEL), lambda i: (i, 0)),
        out_shape=jax.ShapeDtypeStruct((tokens, D_MODEL), _f32),
        scratch_shapes=[pltpu.VMEM((2, TOP_K, tm, D_MODEL), _f32), pltpu.SemaphoreType.DMA((2,))],
        compiler_params=pltpu.CompilerParams(
            dimension_semantics=("arbitrary",), vmem_limit_bytes=VMEM_LIMIT),
        name="combine",
    )(dest3, dest3, x1, gates_t, yb)


def _rope_tables(seq):
    pos = jnp.arange(seq)
    row_ids = (pos // GRID_W).astype(_f32)
    col_ids = (pos % GRID_W).astype(_f32)
    inv_freq = ROPE_THETA ** (-jnp.arange(ROPE_PAIRS_PER_AXIS, dtype=_f32) / ROPE_PAIRS_PER_AXIS)
    ang_r = row_ids[:, None] * inv_freq
    ang_c = col_ids[:, None] * inv_freq
    cos = jnp.concatenate([jnp.cos(ang_r)] * 2 + [jnp.cos(ang_c)] * 2, axis=1)
    sin = jnp.concatenate([-jnp.sin(ang_r), jnp.sin(ang_r), -jnp.sin(ang_c), jnp.sin(ang_c)], axis=1)
    reps = LANES // HEAD_DIM
    return jnp.tile(cos, (1, reps)), jnp.tile(sin, (1, reps))


def _mixer(x, p):
    batch, seq, _ = x.shape
    x2 = x.reshape(batch * seq, D_MODEL)
    cos_t, sin_t = _rope_tables(seq)
    q, k2, vt, u, sa, sb = _in_proj(x2, p["mix_g"], p["w_in"], p["wvt"], p["qg"], p["kg"], cos_t, sin_t, p["bd"], seq)
    attn = _attention(q, k2, vt, batch, seq)
    return _mix_out(x2, attn, u, sa, sb, p["wap"], p["pgw"], p["ps"], p["wpp"], p["wo"], p["ffn_g"],
                    p["wrh"], p["wrl"], p["br"], p["tri"], seq)


def kernel(x_prompt, x_sample, mix_norm_g, w_in, q_norm_g, k_norm_g, w_attn_proj, pool_group_w, pool_scale,
           w_pool_proj, w_out, ffn_norm_g, w_router, b_router, w_gu, b_gu, w_down, b_down):
    depth = w_in.shape[0]
    xs_all = [x_prompt, x_sample]
    head_id = jnp.arange(ATTN_WIDTH) // HEAD_DIM
    block_diag = (head_id[:, None] == head_id[None, :]).astype(_bf16)
    tri_id = jnp.arange(MIX_ROWS)
    tri = (tri_id[:, None] < tri_id[None, :]).astype(_bf16)
    for l in range(depth):
        wr = jnp.pad(w_router[l].astype(_f32), ((0, 0), (0, LANES - N_EXPERTS)))
        wrh = wr.astype(_bf16)
        p = dict(
            mix_g=mix_norm_g[l].reshape(1, D_MODEL),
            w_in=w_in[l].astype(_bf16),
            wvt=w_in[l][:, ATTN_WIDTH + KV_WIDTH:ATTN_WIDTH + 2 * KV_WIDTH].T.astype(_bf16),
            qg=jnp.tile(q_norm_g[l] * (HEAD_DIM ** -0.5 * LOG2_E), N_HEADS).reshape(1, ATTN_WIDTH),
            kg=jnp.tile(k_norm_g[l], N_KV_HEADS).reshape(1, KV_WIDTH),
            bd=block_diag,
            wap=w_attn_proj[l].astype(_bf16),
            pgw=pool_group_w[l].astype(_bf16),
            ps=pool_scale[l].reshape(1, POOL_WIDTH),
            wpp=w_pool_proj[l].astype(_bf16),
            wo=w_out[l].astype(_bf16),
            ffn_g=ffn_norm_g[l].reshape(1, D_MODEL),
            wrh=wrh,
            wrl=(wr - wrh.astype(_f32)).astype(_bf16),
            br=b_router[l].astype(_f32).reshape(N_EXPERTS, 1),
            tri=tri,
        )
        mixed = [_mixer(x, p) for x in xs_all]

        counts = [m[5][:, 0].astype(jnp.int32) for m in mixed]
        total = sum(counts)
        padded = ((total + MOE_ROWS - 1) // MOE_ROWS) * MOE_ROWS
        padded_end = jnp.cumsum(padded)
        padded_start = padded_end - padded
        n_tok = sum(m[0].shape[0] for m in mixed)
        n_blocks = (n_tok * TOP_K + MOE_ROWS - 1) // MOE_ROWS + N_EXPERTS
        n_used = (padded_end[-1] // MOE_ROWS).astype(jnp.int32)
        blk = jnp.minimum(jnp.arange(n_blocks, dtype=jnp.int32), n_used - 1)
        block_e = jnp.sum((padded_end[None, :] <= (blk * MOE_ROWS)[:, None]).astype(jnp.int32), axis=1)
        block_e = jnp.minimum(block_e, N_EXPERTS - 1)
        expert_ids = jnp.arange(N_EXPERTS, dtype=jnp.int32)[:, None, None]
        dests = []
        seen = jnp.zeros((N_EXPERTS,), jnp.int32)
        for m, c in zip(mixed, counts):
            base = (padded_start + seen)[:, None, None]
            dests.append(jnp.sum(jnp.where(m[2][None] == expert_ids, base, 0), axis=0) + m[4])
            seen = seen + c
        d3 = jnp.concatenate([d.reshape(TOP_K, -1, DISPATCH_ROWS).transpose(1, 0, 2) for d in dests], axis=0)
        xs_buf = _dispatch((padded_start + total).astype(jnp.int32), (padded - total).astype(jnp.int32),
                           n_used.reshape(1), d3, mixed[0][1], mixed[1][1], n_blocks * MOE_ROWS)
        yb = _experts(block_e, n_used.reshape(1), xs_buf, w_gu[l], b_gu[l].reshape(N_EXPERTS, 1, -1),
                      w_down[l], b_down[l].reshape(N_EXPERTS, 1, -1))
        outs = []
        for x, m, dest in zip(xs_all, mixed, dests):
            d3 = dest.reshape(TOP_K, -1, COMBINE_ROWS).transpose(1, 0, 2)
            outs.append(_combine(d3, m[0], m[3].T, yb).reshape(x.shape))
        xs_all = outs
    return tuple(xs_all)
```

```python
import functools

import jax
import jax.numpy as jnp
from jax import lax
from jax.experimental import pallas as pl
from jax.experimental.pallas import tpu as pltpu

D_MODEL = 1024
GRID_W = 64
N_HEADS = 8
N_KV_HEADS = 2
HEAD_DIM = 64
N_GROUPS_PER_KV = N_HEADS // N_KV_HEADS
ATTN_WIDTH = N_HEADS * HEAD_DIM
KV_WIDTH = N_KV_HEADS * HEAD_DIM
ROPE_THETA = 10000.0
ROPE_PAIRS_PER_AXIS = HEAD_DIM // 4
POOL_WINDOWS = (2, 4, 8, 16)
N_POOL_GROUPS = 4
POOL_WIDTH = 512
POOL_GROUP_DIM = POOL_WIDTH // N_POOL_GROUPS
N_EXPERTS = 32
TOP_K = 4
D_FF = 1024
SWIGLU_ALPHA = 1.702
SWIGLU_LIMIT = 7.0
NORM_EPS = 1e-6

LANES = 128
SUBLANES = 8
POOL_HALO = 8
LOG2_E = 1.4426950408889634

IN_PROJ_ROWS = 512
ATTN_Q_ROWS = 128
ATTN_KEY_CHUNK = 512
MIX_ROWS = 256
MOE_ROWS = 512
DISPATCH_ROWS = 1024
COMBINE_ROWS = 256
ISSUE_UNROLL = 4
VMEM_LIMIT = 56 * 1024 * 1024

_bf16 = jnp.bfloat16
_f32 = jnp.float32


def _dot(a, b):
    return jnp.dot(a, b, preferred_element_type=_f32)


def _split_bf16(x):
    hi = x.astype(_bf16)
    lo = (x - hi.astype(_f32)).astype(_bf16)
    return hi, lo


def _sigmoid(x):
    return 1.0 / (1.0 + jnp.exp(-x))


def _in_proj_body(x_ref, g_ref, w_ref, wvt_ref, qg_ref, kg_ref, cos_ref, sin_ref, bd_ref,
                  q_ref, k_ref, vt_ref, u_ref, sa_ref, sb_ref):
    x = x_ref[...]
    ms = jnp.mean(x * x, axis=-1, keepdims=True)
    h = (x * lax.rsqrt(ms + NORM_EPS) * g_ref[...]).astype(_bf16)
    cos = cos_ref[...]
    sin = sin_ref[...]
    lane = lax.broadcasted_iota(jnp.int32, cos.shape, 1)
    first_half = (lane % (2 * ROPE_PAIRS_PER_AXIS)) < ROPE_PAIRS_PER_AXIS

    def head_norm_rope(z, gain):
        width = z.shape[1]
        ss = _dot((z * z).astype(_bf16), bd_ref[:width, :width])
        zn = z * lax.rsqrt(ss * (1.0 / HEAD_DIM) + NORM_EPS) * gain
        outs = []
        for j in range(width // LANES):
            c = zn[:, j * LANES:(j + 1) * LANES]
            partner = jnp.where(first_half,
                                pltpu.roll(c, LANES - ROPE_PAIRS_PER_AXIS, 1),
                                pltpu.roll(c, ROPE_PAIRS_PER_AXIS, 1))
            outs.append(c * cos + partner * sin)
        return outs

    c0 = 0
    zq = _dot(h, w_ref[:, c0:c0 + ATTN_WIDTH])
    for j, o in enumerate(head_norm_rope(zq, qg_ref[...])):
        q_ref[:, j * LANES:(j + 1) * LANES] = o.astype(_bf16)
    c0 += ATTN_WIDTH
    zk = _dot(h, w_ref[:, c0:c0 + KV_WIDTH])
    (kr,) = head_norm_rope(zk, kg_ref[...])
    for j in range(N_KV_HEADS):
        k_ref[j] = kr[:, j * HEAD_DIM:(j + 1) * HEAD_DIM].astype(_bf16)
    c0 += KV_WIDTH
    zvt = lax.dot_general(wvt_ref[...], h, (((1,), (1,)), ((), ())), preferred_element_type=_f32)
    for j in range(N_KV_HEADS):
        vt_ref[j] = zvt[j * HEAD_DIM:(j + 1) * HEAD_DIM, :].astype(_bf16)
    c0 += KV_WIDTH
    u_ref[...] = _dot(h, w_ref[:, c0:c0 + POOL_WIDTH]).astype(_bf16)
    c0 += POOL_WIDTH
    sa_ref[...] = _sigmoid(_dot(h, w_ref[:, c0:c0 + D_MODEL])).astype(_bf16)
    c0 += D_MODEL
    sb_ref[...] = _sigmoid(_dot(h, w_ref[:, c0:c0 + D_MODEL])).astype(_bf16)


def _in_proj(x2, norm_g, w_in, wvt, qg, kg, cos_t, sin_t, bd, seq):
    tokens = x2.shape[0]
    tm = IN_PROJ_ROWS
    in_width = w_in.shape[1]
    tiles_per_seq = seq // tm
    const = lambda i: (0, 0)
    row = lambda i: (i, 0)
    return pl.pallas_call(
        _in_proj_body,
        grid=(tokens // tm,),
        in_specs=[
            pl.BlockSpec((tm, D_MODEL), row),
            pl.BlockSpec((1, D_MODEL), const),
            pl.BlockSpec((D_MODEL, in_width), const),
            pl.BlockSpec((KV_WIDTH, D_MODEL), const),
            pl.BlockSpec((1, ATTN_WIDTH), const),
            pl.BlockSpec((1, KV_WIDTH), const),
            pl.BlockSpec((tm, LANES), lambda i: (i % tiles_per_seq, 0)),
            pl.BlockSpec((tm, LANES), lambda i: (i % tiles_per_seq, 0)),
            pl.BlockSpec((ATTN_WIDTH, ATTN_WIDTH), const),
        ],
        out_specs=[
            pl.BlockSpec((tm, ATTN_WIDTH), row),
            pl.BlockSpec((N_KV_HEADS, tm, HEAD_DIM), lambda i: (0, i, 0)),
            pl.BlockSpec((N_KV_HEADS, HEAD_DIM, tm), lambda i: (0, 0, i)),
            pl.BlockSpec((tm, POOL_WIDTH), row),
            pl.BlockSpec((tm, D_MODEL), row),
            pl.BlockSpec((tm, D_MODEL), row),
        ],
        out_shape=[
            jax.ShapeDtypeStruct((tokens, ATTN_WIDTH), _bf16),
            jax.ShapeDtypeStruct((N_KV_HEADS, tokens, HEAD_DIM), _bf16),
            jax.ShapeDtypeStruct((N_KV_HEADS, HEAD_DIM, tokens), _bf16),
            jax.ShapeDtypeStruct((tokens, POOL_WIDTH), _bf16),
            jax.ShapeDtypeStruct((tokens, D_MODEL), _bf16),
            jax.ShapeDtypeStruct((tokens, D_MODEL), _bf16),
        ],
        compiler_params=pltpu.CompilerParams(
            dimension_semantics=("parallel",), vmem_limit_bytes=VMEM_LIMIT),
        name="in_proj",
    )(x2, norm_g, w_in, wvt, qg, kg, cos_t, sin_t, bd)


def _attention_step(q_ref, k_ref, vt_ref, o_ref, st_new, m_new, st_old, m_old):
    tq = q_ref.shape[0]
    q = q_ref[...]
    qs = jnp.concatenate([q[:, g * HEAD_DIM:(g + 1) * HEAD_DIM] for g in range(N_GROUPS_PER_KV)], axis=0)
    seq = k_ref.shape[0]
    m_prev = m_old[...]
    m_run = denom = ot = None
    for c0 in range(0, seq, ATTN_KEY_CHUNK):
        rows = slice(c0, c0 + ATTN_KEY_CHUNK)
        st = lax.dot_general(k_ref[rows, :], qs, (((1,), (1,)), ((), ())), preferred_element_type=_f32)
        st_new[rows, :] = st
        m_c = jnp.max(st, axis=0, keepdims=True)
        m_run = m_c if m_run is None else jnp.maximum(m_run, m_c)
        p = jnp.exp2(st_old[rows, :] - m_prev)
        l_c = jnp.sum(p, axis=0, keepdims=True)
        o_c = _dot(vt_ref[:, rows], p.astype(_bf16))
        denom = l_c if denom is None else denom + l_c
        ot = o_c if ot is None else ot + o_c
    m_new[...] = m_run
    ot = ot * (1.0 / denom)
    stacked = jnp.concatenate([ot[:, g * tq:(g + 1) * tq] for g in range(N_GROUPS_PER_KV)], axis=0)
    o_ref[...] = stacked.T.astype(_bf16)


def _attention_body(q_ref, k_ref, vt_ref, o_ref, st_a, m_a, st_b, m_b):
    n = pl.program_id(0)

    @pl.when(n == 0)
    def _():
        st_b[...] = jnp.zeros_like(st_b)
        m_b[...] = jnp.zeros_like(m_b)

    @pl.when(n % 2 == 0)
    def _():
        _attention_step(q_ref, k_ref, vt_ref, o_ref, st_a, m_a, st_b, m_b)

    @pl.when(n % 2 == 1)
    def _():
        _attention_step(q_ref, k_ref, vt_ref, o_ref, st_b, m_b, st_a, m_a)


def _attention(q, k2, vt, batch, seq):
    tq = ATTN_Q_ROWS
    nq = seq // tq
    n_blocks = batch * N_KV_HEADS * nq
    group_width = N_GROUPS_PER_KV * HEAD_DIM
    lanes = N_GROUPS_PER_KV * tq

    def decode(n):
        return n // (N_KV_HEADS * nq), (n // nq) % N_KV_HEADS, n % nq

    def scores_of(n):
        return decode(jnp.minimum(n, n_blocks - 1))

    def finish_of(n):
        return decode(jnp.maximum(n - 1, 0))

    def q_map(n):
        b, kh, i = scores_of(n)
        return b * nq + i, kh

    def k_map(n):
        b, kh, _ = scores_of(n)
        return kh, b, 0

    def vt_map(n):
        b, kh, _ = finish_of(n)
        return kh, 0, b

    def o_map(n):
        b, kh, i = finish_of(n)
        return b * nq + i, kh

    return pl.pallas_call(
        _attention_body,
        grid=(n_blocks + 1,),
        in_specs=[
            pl.BlockSpec((tq, group_width), q_map),
            pl.BlockSpec((None, seq, HEAD_DIM), k_map),
            pl.BlockSpec((None, HEAD_DIM, seq), vt_map),
        ],
        out_specs=pl.BlockSpec((tq, group_width), o_map),
        out_shape=jax.ShapeDtypeStruct((batch * seq, ATTN_WIDTH), _bf16),
        scratch_shapes=[
            pltpu.VMEM((seq, lanes), _f32), pltpu.VMEM((1, lanes), _f32),
            pltpu.VMEM((seq, lanes), _f32), pltpu.VMEM((1, lanes), _f32),
        ],
        compiler_params=pltpu.CompilerParams(
            dimension_semantics=("arbitrary",), vmem_limit_bytes=VMEM_LIMIT),
        name="attention",
    )(q, k2, vt)


def _mix_out_body(seq, x_ref, a_ref, up_ref, u_ref, un_ref, sa_ref, sb_ref,
                  wap_ref, pgw_ref, ps_ref, wpp_ref, wo_ref, g2_ref, wr_ref, br_ref, tri_ref,
                  x1_ref, h2_ref, idx_ref, gate_ref, rank_ref, cnt_ref,
                  ext_ref, lv_ref, carry_ref):
    i = pl.program_id(0)
    tm = x_ref.shape[0]

    @pl.when(i == 0)
    def _():
        carry_ref[...] = jnp.zeros_like(carry_ref)
        ext_ref[...] = jnp.zeros_like(ext_ref)
        lv_ref[...] = jnp.zeros_like(lv_ref)

    start = (i * tm) % seq
    has_prev = (start != 0).astype(_f32)
    has_next = (start + tm != seq).astype(_f32)
    first = 2 * POOL_HALO
    ext_ref[POOL_HALO:first, :] = up_ref[...].astype(_f32) * has_prev
    ext_ref[first:first + tm, :] = u_ref[...].astype(_f32)
    ext_ref[first + tm:first + tm + POOL_HALO, :] = un_ref[...].astype(_f32) * has_next
    y_attn = _dot(a_ref[...], wap_ref[...])
    lo, n = POOL_HALO, tm + 2 * POOL_HALO
    pos = start + lax.broadcasted_iota(jnp.int32, (tm, 1), 0)
    pooled = []
    for g, w in enumerate(POOL_WINDOWS):
        half = w // 2
        assert half == 1 << g
        wide = slice(g * POOL_GROUP_DIM, POOL_WIDTH)
        cols = slice(g * POOL_GROUP_DIM, (g + 1) * POOL_GROUP_DIM)
        if g == 0:
            lv_ref[0, lo:lo + n, wide] = ext_ref[pl.ds(lo - 1, n), wide] + ext_ref[pl.ds(lo, n), wide]
        elif g < N_POOL_GROUPS - 1:
            lv_ref[g, lo:lo + n, wide] = (lv_ref[g - 1, pl.ds(lo - half // 2, n), wide]
                                          + lv_ref[g - 1, pl.ds(lo + half // 2, n), wide])
        if g < N_POOL_GROUPS - 1:
            tot = lv_ref[g, first:first + tm, cols]
        else:
            tot = (lv_ref[g - 1, pl.ds(first - half // 2, tm), cols]
                   + lv_ref[g - 1, pl.ds(first + half // 2, tm), cols])
        cnt = (jnp.minimum(pos + half, seq) - jnp.maximum(pos - half, 0)).astype(_f32)
        diff = tot / cnt - ext_ref[first:first + tm, cols]
        pooled.append(_dot(diff.astype(_bf16), pgw_ref[g]))
    pooled = (jnp.concatenate(pooled, axis=1) * ps_ref[...]).astype(_bf16)
    y_pool = _dot(pooled, wpp_ref[...])
    merged = sa_ref[...].astype(_f32) * y_attn + sb_ref[...].astype(_f32) * y_pool
    x1 = x_ref[...] + _dot(merged.astype(_bf16), wo_ref[...])
    x1_ref[...] = x1

    ms = jnp.mean(x1 * x1, axis=-1, keepdims=True)
    h2 = x1 * lax.rsqrt(ms + NORM_EPS) * g2_ref[...]
    h2_ref[...] = h2
    hi, lo = _split_bf16(h2)
    both = _dot(hi, wr_ref[...])
    logits = both[:, :LANES] + both[:, LANES:] + _dot(lo, wr_ref[:, :LANES])
    lt = logits.T[:N_EXPERTS, :] + br_ref[...]
    eid = lax.broadcasted_iota(jnp.int32, lt.shape, 0)
    vals, idxs = [], []
    multi_hot = jnp.zeros(lt.shape, _f32)
    for _ in range(TOP_K):
        m = jnp.max(lt, axis=0, keepdims=True)
        sel = jnp.min(jnp.where(lt == m, eid, N_EXPERTS), axis=0, keepdims=True)
        hit = eid == sel
        vals.append(m)
        idxs.append(sel)
        multi_hot = multi_hot + hit.astype(_f32)
        lt = jnp.where(hit, -jnp.inf, lt)
    es = [jnp.exp(v - vals[0]) for v in vals]
    inv = 1.0 / (es[0] + es[1] + es[2] + es[3])
    gate_ref[...] = jnp.concatenate([e * inv for e in es], axis=0)
    idx_ref[...] = jnp.concatenate(idxs, axis=0)
    before = _dot(multi_hot.astype(_bf16), tri_ref[...]) + carry_ref[:, 0:1]
    ranks = [jnp.sum(jnp.where(eid == sel, before, 0.0), axis=0, keepdims=True) for sel in idxs]
    rank_ref[...] = jnp.concatenate(ranks, axis=0).astype(jnp.int32)
    carry_ref[...] = carry_ref[...] + jnp.sum(multi_hot, axis=1, keepdims=True)
    cnt_ref[...] = carry_ref[...]


def _mix_out(x2, attn, u, sa, sb, wap, pgw, ps, wpp, wo, g2, wr, br, tri, seq):
    tokens = x2.shape[0]
    tm = MIX_ROWS
    n = tokens // tm
    halo_blocks = tm // POOL_HALO
    last_halo = tokens // POOL_HALO - 1
    row = lambda i: (i, 0)
    const2 = lambda i: (0, 0)
    const3 = lambda i: (0, 0, 0)
    col = lambda i: (0, i)
    return pl.pallas_call(
        functools.partial(_mix_out_body, seq),
        grid=(n,),
        in_specs=[
            pl.BlockSpec((tm, D_MODEL), row),
            pl.BlockSpec((tm, ATTN_WIDTH), row),
            pl.BlockSpec((POOL_HALO, POOL_WIDTH), lambda i: (jnp.maximum(i * halo_blocks - 1, 0), 0)),
            pl.BlockSpec((tm, POOL_WIDTH), row),
            pl.BlockSpec((POOL_HALO, POOL_WIDTH), lambda i: (jnp.minimum((i + 1) * halo_blocks, last_halo), 0)),
            pl.BlockSpec((tm, D_MODEL), row),
            pl.BlockSpec((tm, D_MODEL), row),
            pl.BlockSpec((ATTN_WIDTH, D_MODEL), const2),
            pl.BlockSpec((N_POOL_GROUPS, POOL_GROUP_DIM, POOL_GROUP_DIM), const3),
            pl.BlockSpec((1, POOL_WIDTH), const2),
            pl.BlockSpec((POOL_WIDTH, D_MODEL), const2),
            pl.BlockSpec((D_MODEL, D_MODEL), const2),
            pl.BlockSpec((1, D_MODEL), const2),
            pl.BlockSpec((D_MODEL, 2 * LANES), const2),
            pl.BlockSpec((N_EXPERTS, 1), const2),
            pl.BlockSpec((tm, tm), const2),
        ],
        out_specs=[
            pl.BlockSpec((tm, D_MODEL), row),
            pl.BlockSpec((tm, D_MODEL), row),
            pl.BlockSpec((TOP_K, tm), col),
            pl.BlockSpec((TOP_K, tm), col),
            pl.BlockSpec((TOP_K, tm), col),
            pl.BlockSpec((N_EXPERTS, LANES), const2),
        ],
        out_shape=[
            jax.ShapeDtypeStruct((tokens, D_MODEL), _f32),
            jax.ShapeDtypeStruct((tokens, D_MODEL), _f32),
            jax.ShapeDtypeStruct((TOP_K, tokens), jnp.int32),
            jax.ShapeDtypeStruct((TOP_K, tokens), _f32),
            jax.ShapeDtypeStruct((TOP_K, tokens), jnp.int32),
            jax.ShapeDtypeStruct((N_EXPERTS, LANES), _f32),
        ],
        scratch_shapes=[
            pltpu.VMEM((tm + 4 * POOL_HALO, POOL_WIDTH), _f32),
            pltpu.VMEM((N_POOL_GROUPS - 1, tm + 4 * POOL_HALO, POOL_WIDTH), _f32),
            pltpu.VMEM((N_EXPERTS, LANES), _f32),
        ],
        compiler_params=pltpu.CompilerParams(
            dimension_semantics=("arbitrary",), vmem_limit_bytes=VMEM_LIMIT),
        name="mix_out",
    )(x2, attn, u, u, u, sa, sb, wap, pgw, ps, wpp, wo, g2, wr, br, tri)


def _row_copy(src_ref, src_row, dst_ref, dst_row, sem):
    return pltpu.make_async_copy(src_ref.at[pl.ds(src_row, 1)], dst_ref.at[pl.ds(dst_row, 1)], sem)


def _scatter_rows(dest_ref, h_ref, xs_ref, sem):
    tm = h_ref.shape[0]

    def issue(g, c):
        for u in range(ISSUE_UNROLL):
            j = g * ISSUE_UNROLL + u
            for k in range(TOP_K):
                _row_copy(h_ref, j, xs_ref, dest_ref[0, k, j], sem).start()
        return c

    lax.fori_loop(0, tm // ISSUE_UNROLL, issue, 0)

    for k in range(TOP_K):
        pltpu.make_async_copy(h_ref, xs_ref.at[pl.ds(0, tm)], sem).wait()


def _dispatch_body(n_first, start_ref, count_ref, used_ref, dest_ref, ha_ref, hb_ref, xs_ref, zero_ref, sem):
    i = pl.program_id(0)

    @pl.when(i == 0)
    def _():
        zero_ref[...] = jnp.zeros_like(zero_ref)
        n_blocks = xs_ref.shape[0] // MOE_ROWS

        def block_copy(b):
            return pltpu.make_async_copy(zero_ref, xs_ref.at[pl.ds(b * MOE_ROWS, MOE_ROWS)], sem)

        def issue_block(b, c):
            block_copy(b).start()
            return c

        lax.fori_loop(used_ref[0], n_blocks, issue_block, 0)
        for e in range(N_EXPERTS):
            base = start_ref[e]

            def issue(j, c, base=base):
                _row_copy(zero_ref, 0, xs_ref, base + j, sem).start()
                return c

            lax.fori_loop(0, count_ref[e], issue, 0)

        def drain_block(b, c):
            block_copy(0).wait()
            return c

        lax.fori_loop(used_ref[0], n_blocks, drain_block, 0)
        for e in range(N_EXPERTS):
            def drain(j, c):
                _row_copy(zero_ref, 0, xs_ref, 0, sem).wait()
                return c

            lax.fori_loop(0, count_ref[e], drain, 0)

    @pl.when(i < n_first)
    def _():
        _scatter_rows(dest_ref, ha_ref, xs_ref, sem)

    @pl.when(i >= n_first)
    def _():
        _scatter_rows(dest_ref, hb_ref, xs_ref, sem)


def _dispatch(pad_start, pad_count, n_used, dest3, h_a, h_b, n_rows):
    tm = DISPATCH_ROWS
    n_a = h_a.shape[0] // tm
    n_b = h_b.shape[0] // tm
    return pl.pallas_call(
        functools.partial(_dispatch_body, n_a),
        grid_spec=pltpu.PrefetchScalarGridSpec(
            num_scalar_prefetch=3,
            grid=(n_a + n_b,),
            in_specs=[
                pl.BlockSpec((1, TOP_K, tm), lambda i, s, c, u: (i, 0, 0), memory_space=pltpu.SMEM),
                pl.BlockSpec((tm, D_MODEL), lambda i, s, c, u: (jnp.minimum(i, n_a - 1), 0)),
                pl.BlockSpec((tm, D_MODEL), lambda i, s, c, u: (jnp.maximum(i - n_a, 0), 0)),
            ],
            out_specs=pl.BlockSpec(memory_space=pl.ANY),
            scratch_shapes=[pltpu.VMEM((MOE_ROWS, D_MODEL), _f32), pltpu.SemaphoreType.DMA(())],
        ),
        out_shape=jax.ShapeDtypeStruct((n_rows, D_MODEL), _f32),
        compiler_params=pltpu.CompilerParams(
            dimension_semantics=("arbitrary",), vmem_limit_bytes=VMEM_LIMIT),
        name="dispatch",
    )(pad_start, pad_count, n_used, dest3, h_a, h_b)


def _experts_body(be_ref, nu_ref, xs_ref, wgu_ref, bgu_ref, wd_ref, bd_ref, y_ref, wgu_bf, wd_bf):
    i = pl.program_id(0)
    used = i < nu_ref[0]

    @pl.when(jnp.logical_not(used))
    def _():
        y_ref[...] = jnp.zeros_like(y_ref)

    @pl.when(jnp.logical_and(used, jnp.logical_or(i == 0, be_ref[i] != be_ref[jnp.maximum(i - 1, 0)])))
    def _():
        wgu_bf[...] = wgu_ref[...].astype(_bf16)
        wd_bf[...] = wd_ref[...].astype(_bf16)

    @pl.when(used)
    def _():
        x = xs_ref[...].astype(_bf16)
        gu = _dot(x, wgu_bf[...]) + bgu_ref[...]
        gate = jnp.minimum(gu[:, :D_FF], SWIGLU_LIMIT)
        up = jnp.clip(gu[:, D_FF:], -SWIGLU_LIMIT, SWIGLU_LIMIT)
        act = (up + 1.0) * (gate * _sigmoid(SWIGLU_ALPHA * gate))
        y_ref[...] = _dot(act.astype(_bf16), wd_bf[...]) + bd_ref[...]


def _experts(block_e, n_used, xs, wgu, bgu, wd, bd):
    n_rows = xs.shape[0]
    n_blocks = n_rows // MOE_ROWS
    rows = lambda i, be, nu: (jnp.minimum(i, nu[0] - 1), 0)
    per_e = lambda i, be, nu: (be[i], 0, 0)
    return pl.pallas_call(
        _experts_body,
        grid_spec=pltpu.PrefetchScalarGridSpec(
            num_scalar_prefetch=2,
            grid=(n_blocks,),
            in_specs=[
                pl.BlockSpec((MOE_ROWS, D_MODEL), rows),
                pl.BlockSpec((None, D_MODEL, 2 * D_FF), per_e),
                pl.BlockSpec((None, 1, 2 * D_FF), per_e),
                pl.BlockSpec((None, D_FF, D_MODEL), per_e),
                pl.BlockSpec((None, 1, D_MODEL), per_e),
            ],
            out_specs=pl.BlockSpec((MOE_ROWS, D_MODEL), lambda i, be, nu: (i, 0)),
            scratch_shapes=[pltpu.VMEM((D_MODEL, 2 * D_FF), _bf16), pltpu.VMEM((D_FF, D_MODEL), _bf16)],
        ),
        out_shape=jax.ShapeDtypeStruct((n_rows, D_MODEL), _f32),
        compiler_params=pltpu.CompilerParams(
            dimension_semantics=("arbitrary",), vmem_limit_bytes=VMEM_LIMIT),
        name="experts",
    )(block_e, n_used, xs, wgu, bgu, wd, bd)


def _combine_body(dest_ref, next_ref, x1_ref, gate_ref, yb_ref, o_ref, buf_ref, sems):
    tm = x1_ref.shape[0]
    i = pl.program_id(0)
    slot = i % 2

    def gather(idx_ref, s):
        def issue(g, c):
            for u in range(ISSUE_UNROLL):
                j = g * ISSUE_UNROLL + u
                for k in range(TOP_K):
                    _row_copy(yb_ref, idx_ref[0, k, j], buf_ref.at[s, k], j, sems.at[s]).start()
            return c

        lax.fori_loop(0, tm // ISSUE_UNROLL, issue, 0)

    @pl.when(i == 0)
    def _():
        gather(dest_ref, slot)

    @pl.when(i + 1 < pl.num_programs(0))
    def _():
        gather(next_ref, 1 - slot)

    for k in range(TOP_K):
        pltpu.make_async_copy(yb_ref.at[pl.ds(0, tm)], buf_ref.at[slot, k], sems.at[slot]).wait()
    gates = gate_ref[...]
    acc = gates[:, 0:1] * buf_ref[slot, 0]
    for k in range(1, TOP_K):
        acc = acc + gates[:, k:k + 1] * buf_ref[slot, k]
    o_ref[...] = x1_ref[...] + acc


def _combine(dest3, x1, gates_t, yb):
    tokens = x1.shape[0]
    tm = COMBINE_ROWS
    n = tokens // tm
    return pl.pallas_call(
        _combine_body,
        grid=(n,),
        in_specs=[
            pl.BlockSpec((1, TOP_K, tm), lambda i: (i, 0, 0), memory_space=pltpu.SMEM),
            pl.BlockSpec((1, TOP_K, tm), lambda i: (jnp.minimum(i + 1, n - 1), 0, 0), memory_space=pltpu.SMEM),
            pl.BlockSpec((tm, D_MODEL), lambda i: (i, 0)),
            pl.BlockSpec((tm, TOP_K), lambda i: (i, 0)),
            pl.BlockSpec(memory_space=pl.ANY),
        ],
        out_specs=pl.BlockSpec((tm, D_MODEL), lambda i: (i, 0)),
        out_shape=jax.ShapeDtypeStruct((tokens, D_MODEL), _f32),
        scratch_shapes=[pltpu.VMEM((2, TOP_K, tm, D_MODEL), _f32), pltpu.SemaphoreType.DMA((2,))],
        compiler_params=pltpu.CompilerParams(
            dimension_semantics=("arbitrary",), vmem_limit_bytes=VMEM_LIMIT),
        name="combine",
    )(dest3, dest3, x1, gates_t, yb)


def _rope_tables(seq):
    pos = jnp.arange(seq)
    row_ids = (pos // GRID_W).astype(_f32)
    col_ids = (pos % GRID_W).astype(_f32)
    inv_freq = ROPE_THETA ** (-jnp.arange(ROPE_PAIRS_PER_AXIS, dtype=_f32) / ROPE_PAIRS_PER_AXIS)
    ang_r = row_ids[:, None] * inv_freq
    ang_c = col_ids[:, None] * inv_freq
    cos = jnp.concatenate([jnp.cos(ang_r)] * 2 + [jnp.cos(ang_c)] * 2, axis=1)
    sin = jnp.concatenate([-jnp.sin(ang_r), jnp.sin(ang_r), -jnp.sin(ang_c), jnp.sin(ang_c)], axis=1)
    reps = LANES // HEAD_DIM
    return jnp.tile(cos, (1, reps)), jnp.tile(sin, (1, reps))


def _mixer(x, p):
    batch, seq, _ = x.shape
    x2 = x.reshape(batch * seq, D_MODEL)
    cos_t, sin_t = _rope_tables(seq)
    q, k2, vt, u, sa, sb = _in_proj(x2, p["mix_g"], p["w_in"], p["wvt"], p["qg"], p["kg"], cos_t, sin_t, p["bd"], seq)
    attn = _attention(q, k2, vt, batch, seq)
    return _mix_out(x2, attn, u, sa, sb, p["wap"], p["pgw"], p["ps"], p["wpp"], p["wo"], p["ffn_g"],
                    p["wr"], p["br"], p["tri"], seq)


def kernel(x_prompt, x_sample, mix_norm_g, w_in, q_norm_g, k_norm_g, w_attn_proj, pool_group_w, pool_scale,
           w_pool_proj, w_out, ffn_norm_g, w_router, b_router, w_gu, b_gu, w_down, b_down):
    depth = w_in.shape[0]
    xs_all = [x_prompt, x_sample]
    head_id = jnp.arange(ATTN_WIDTH) // HEAD_DIM
    block_diag = (head_id[:, None] == head_id[None, :]).astype(_bf16)
    tri_id = jnp.arange(MIX_ROWS)
    tri = (tri_id[:, None] < tri_id[None, :]).astype(_bf16)
    for l in range(depth):
        wr = jnp.pad(w_router[l].astype(_f32), ((0, 0), (0, LANES - N_EXPERTS)))
        wrh = wr.astype(_bf16)
        p = dict(
            mix_g=mix_norm_g[l].reshape(1, D_MODEL),
            w_in=w_in[l].astype(_bf16),
            wvt=w_in[l][:, ATTN_WIDTH + KV_WIDTH:ATTN_WIDTH + 2 * KV_WIDTH].T.astype(_bf16),
            qg=jnp.tile(q_norm_g[l] * (HEAD_DIM ** -0.5 * LOG2_E), N_HEADS).reshape(1, ATTN_WIDTH),
            kg=jnp.tile(k_norm_g[l], N_KV_HEADS).reshape(1, KV_WIDTH),
            bd=block_diag,
            wap=w_attn_proj[l].astype(_bf16),
            pgw=pool_group_w[l].astype(_bf16),
            ps=pool_scale[l].reshape(1, POOL_WIDTH),
            wpp=w_pool_proj[l].astype(_bf16),
            wo=w_out[l].astype(_bf16),
            ffn_g=ffn_norm_g[l].reshape(1, D_MODEL),
            wr=jnp.concatenate([wrh, (wr - wrh.astype(_f32)).astype(_bf16)], axis=1),
            br=b_router[l].astype(_f32).reshape(N_EXPERTS, 1),
            tri=tri,
        )
        mixed = [_mixer(x, p) for x in xs_all]

        counts = [m[5][:, 0].astype(jnp.int32) for m in mixed]
        total = sum(counts)
        padded = ((total + MOE_ROWS - 1) // MOE_ROWS) * MOE_ROWS
        padded_end = jnp.cumsum(padded)
        padded_start = padded_end - padded
        n_tok = sum(m[0].shape[0] for m in mixed)
        n_blocks = (n_tok * TOP_K + MOE_ROWS - 1) // MOE_ROWS + N_EXPERTS
        n_used = (padded_end[-1] // MOE_ROWS).astype(jnp.int32)
        blk = jnp.minimum(jnp.arange(n_blocks, dtype=jnp.int32), n_used - 1)
        block_e = jnp.sum((padded_end[None, :] <= (blk * MOE_ROWS)[:, None]).astype(jnp.int32), axis=1)
        block_e = jnp.minimum(block_e, N_EXPERTS - 1)
        expert_ids = jnp.arange(N_EXPERTS, dtype=jnp.int32)[:, None, None]
        dests = []
        seen = jnp.zeros((N_EXPERTS,), jnp.int32)
        for m, c in zip(mixed, counts):
            base = (padded_start + seen)[:, None, None]
            dests.append(jnp.sum(jnp.where(m[2][None] == expert_ids, base, 0), axis=0) + m[4])
            seen = seen + c
        d3 = jnp.concatenate([d.reshape(TOP_K, -1, DISPATCH_ROWS).transpose(1, 0, 2) for d in dests], axis=0)
        xs_buf = _dispatch((padded_start + total).astype(jnp.int32), (padded - total).astype(jnp.int32),
                           n_used.reshape(1), d3, mixed[0][1], mixed[1][1], n_blocks * MOE_ROWS)
        yb = _experts(block_e, n_used.reshape(1), xs_buf, w_gu[l], b_gu[l].reshape(N_EXPERTS, 1, -1),
                      w_down[l], b_down[l].reshape(N_EXPERTS, 1, -1))
        outs = []
        for x, m, dest in zip(xs_all, mixed, dests):
            d3 = dest.reshape(TOP_K, -1, COMBINE_ROWS).transpose(1, 0, 2)
            outs.append(_combine(d3, m[0], m[3].T, yb).reshape(x.shape))
        xs_all = outs
    return tuple(xs_all)
```

```python
import functools

import jax
import jax.numpy as jnp
from jax import lax
from jax.experimental import pallas as pl
from jax.experimental.pallas import tpu as pltpu
from jax.experimental.pallas import tpu_sc as plsc

D_MODEL = 1024
GRID_W = 64
N_HEADS = 8
N_KV_HEADS = 2
HEAD_DIM = 64
N_GROUPS_PER_KV = N_HEADS // N_KV_HEADS
ATTN_WIDTH = N_HEADS * HEAD_DIM
KV_WIDTH = N_KV_HEADS * HEAD_DIM
ROPE_THETA = 10000.0
ROPE_PAIRS_PER_AXIS = HEAD_DIM // 4
POOL_WINDOWS = (2, 4, 8, 16)
N_POOL_GROUPS = 4
POOL_WIDTH = 512
POOL_GROUP_DIM = POOL_WIDTH // N_POOL_GROUPS
N_EXPERTS = 32
TOP_K = 4
D_FF = 1024
SWIGLU_ALPHA = 1.702
SWIGLU_LIMIT = 7.0
NORM_EPS = 1e-6

LANES = 128
SUBLANES = 8
POOL_HALO = 8
LOG2_E = 1.4426950408889634

IN_PROJ_ROWS = 512
ATTN_Q_ROWS = 128
ATTN_KEY_CHUNK = 512
MIX_ROWS = 256
MOE_ROWS = 512
SC_CORES = 2
SC_SUBCORES = 16
SC_LANES = 16
SC_CHUNK = 32
COMBINE_ROWS = 256
ISSUE_UNROLL = 4
VMEM_LIMIT = 56 * 1024 * 1024

_bf16 = jnp.bfloat16
_f32 = jnp.float32


def _dot(a, b):
    return jnp.dot(a, b, preferred_element_type=_f32)


def _split_bf16(x):
    hi = x.astype(_bf16)
    lo = (x - hi.astype(_f32)).astype(_bf16)
    return hi, lo


def _sigmoid(x):
    return 1.0 / (1.0 + jnp.exp(-x))


def _in_proj_body(x_ref, g_ref, w_ref, wvt_ref, qg_ref, kg_ref, cos_ref, sin_ref, bd_ref,
                  q_ref, k_ref, vt_ref, u_ref, sa_ref, sb_ref):
    x = x_ref[...]
    ms = jnp.mean(x * x, axis=-1, keepdims=True)
    h = (x * lax.rsqrt(ms + NORM_EPS) * g_ref[...]).astype(_bf16)
    cos = cos_ref[...]
    sin = sin_ref[...]
    lane = lax.broadcasted_iota(jnp.int32, cos.shape, 1)
    first_half = (lane % (2 * ROPE_PAIRS_PER_AXIS)) < ROPE_PAIRS_PER_AXIS

    def head_norm_rope(z, gain):
        width = z.shape[1]
        ss = _dot((z * z).astype(_bf16), bd_ref[:width, :width])
        zn = z * lax.rsqrt(ss * (1.0 / HEAD_DIM) + NORM_EPS) * gain
        outs = []
        for j in range(width // LANES):
            c = zn[:, j * LANES:(j + 1) * LANES]
            partner = jnp.where(first_half,
                                pltpu.roll(c, LANES - ROPE_PAIRS_PER_AXIS, 1),
                                pltpu.roll(c, ROPE_PAIRS_PER_AXIS, 1))
            outs.append(c * cos + partner * sin)
        return outs

    c0 = 0
    zq = _dot(h, w_ref[:, c0:c0 + ATTN_WIDTH])
    for j, o in enumerate(head_norm_rope(zq, qg_ref[...])):
        q_ref[:, j * LANES:(j + 1) * LANES] = o.astype(_bf16)
    c0 += ATTN_WIDTH
    zk = _dot(h, w_ref[:, c0:c0 + KV_WIDTH])
    (kr,) = head_norm_rope(zk, kg_ref[...])
    for j in range(N_KV_HEADS):
        k_ref[j] = kr[:, j * HEAD_DIM:(j + 1) * HEAD_DIM].astype(_bf16)
    c0 += KV_WIDTH
    zvt = lax.dot_general(wvt_ref[...], h, (((1,), (1,)), ((), ())), preferred_element_type=_f32)
    for j in range(N_KV_HEADS):
        vt_ref[j] = zvt[j * HEAD_DIM:(j + 1) * HEAD_DIM, :].astype(_bf16)
    c0 += KV_WIDTH
    u_ref[...] = _dot(h, w_ref[:, c0:c0 + POOL_WIDTH]).astype(_bf16)
    c0 += POOL_WIDTH
    sa_ref[...] = _sigmoid(_dot(h, w_ref[:, c0:c0 + D_MODEL])).astype(_bf16)
    c0 += D_MODEL
    sb_ref[...] = _sigmoid(_dot(h, w_ref[:, c0:c0 + D_MODEL])).astype(_bf16)


def _in_proj(x2, norm_g, w_in, wvt, qg, kg, cos_t, sin_t, bd, seq):
    tokens = x2.shape[0]
    tm = IN_PROJ_ROWS
    in_width = w_in.shape[1]
    tiles_per_seq = seq // tm
    const = lambda i: (0, 0)
    row = lambda i: (i, 0)
    return pl.pallas_call(
        _in_proj_body,
        grid=(tokens // tm,),
        in_specs=[
            pl.BlockSpec((tm, D_MODEL), row),
            pl.BlockSpec((1, D_MODEL), const),
            pl.BlockSpec((D_MODEL, in_width), const),
            pl.BlockSpec((KV_WIDTH, D_MODEL), const),
            pl.BlockSpec((1, ATTN_WIDTH), const),
            pl.BlockSpec((1, KV_WIDTH), const),
            pl.BlockSpec((tm, LANES), lambda i: (i % tiles_per_seq, 0)),
            pl.BlockSpec((tm, LANES), lambda i: (i % tiles_per_seq, 0)),
            pl.BlockSpec((ATTN_WIDTH, ATTN_WIDTH), const),
        ],
        out_specs=[
            pl.BlockSpec((tm, ATTN_WIDTH), row),
            pl.BlockSpec((N_KV_HEADS, tm, HEAD_DIM), lambda i: (0, i, 0)),
            pl.BlockSpec((N_KV_HEADS, HEAD_DIM, tm), lambda i: (0, 0, i)),
            pl.BlockSpec((tm, POOL_WIDTH), row),
            pl.BlockSpec((tm, D_MODEL), row),
            pl.BlockSpec((tm, D_MODEL), row),
        ],
        out_shape=[
            jax.ShapeDtypeStruct((tokens, ATTN_WIDTH), _bf16),
            jax.ShapeDtypeStruct((N_KV_HEADS, tokens, HEAD_DIM), _bf16),
            jax.ShapeDtypeStruct((N_KV_HEADS, HEAD_DIM, tokens), _bf16),
            jax.ShapeDtypeStruct((tokens, POOL_WIDTH), _bf16),
            jax.ShapeDtypeStruct((tokens, D_MODEL), _bf16),
            jax.ShapeDtypeStruct((tokens, D_MODEL), _bf16),
        ],
        compiler_params=pltpu.CompilerParams(
            dimension_semantics=("parallel",), vmem_limit_bytes=VMEM_LIMIT),
        name="in_proj",
    )(x2, norm_g, w_in, wvt, qg, kg, cos_t, sin_t, bd)


def _attention_step(q_ref, k_ref, vt_ref, o_ref, st_new, m_new, st_old, m_old):
    tq = q_ref.shape[0]
    q = q_ref[...]
    qs = jnp.concatenate([q[:, g * HEAD_DIM:(g + 1) * HEAD_DIM] for g in range(N_GROUPS_PER_KV)], axis=0)
    seq = k_ref.shape[0]
    m_prev = m_old[...]
    m_run = denom = ot = None
    for c0 in range(0, seq, ATTN_KEY_CHUNK):
        rows = slice(c0, c0 + ATTN_KEY_CHUNK)
        st = lax.dot_general(k_ref[rows, :], qs, (((1,), (1,)), ((), ())), preferred_element_type=_f32)
        st_new[rows, :] = st
        m_c = jnp.max(st, axis=0, keepdims=True)
        m_run = m_c if m_run is None else jnp.maximum(m_run, m_c)
        p = jnp.exp2(st_old[rows, :] - m_prev)
        l_c = jnp.sum(p, axis=0, keepdims=True)
        o_c = _dot(vt_ref[:, rows], p.astype(_bf16))
        denom = l_c if denom is None else denom + l_c
        ot = o_c if ot is None else ot + o_c
    m_new[...] = m_run
    ot = ot * (1.0 / denom)
    stacked = jnp.concatenate([ot[:, g * tq:(g + 1) * tq] for g in range(N_GROUPS_PER_KV)], axis=0)
    o_ref[...] = stacked.T.astype(_bf16)


def _attention_body(q_ref, k_ref, vt_ref, o_ref, st_a, m_a, st_b, m_b):
    n = pl.program_id(0)

    @pl.when(n == 0)
    def _():
        st_b[...] = jnp.zeros_like(st_b)
        m_b[...] = jnp.zeros_like(m_b)

    @pl.when(n % 2 == 0)
    def _():
        _attention_step(q_ref, k_ref, vt_ref, o_ref, st_a, m_a, st_b, m_b)

    @pl.when(n % 2 == 1)
    def _():
        _attention_step(q_ref, k_ref, vt_ref, o_ref, st_b, m_b, st_a, m_a)


def _attention(q, k2, vt, batch, seq):
    tq = ATTN_Q_ROWS
    nq = seq // tq
    n_blocks = batch * N_KV_HEADS * nq
    group_width = N_GROUPS_PER_KV * HEAD_DIM
    lanes = N_GROUPS_PER_KV * tq

    def decode(n):
        return n // (N_KV_HEADS * nq), (n // nq) % N_KV_HEADS, n % nq

    def scores_of(n):
        return decode(jnp.minimum(n, n_blocks - 1))

    def finish_of(n):
        return decode(jnp.maximum(n - 1, 0))

    def q_map(n):
        b, kh, i = scores_of(n)
        return b * nq + i, kh

    def k_map(n):
        b, kh, _ = scores_of(n)
        return kh, b, 0

    def vt_map(n):
        b, kh, _ = finish_of(n)
        return kh, 0, b

    def o_map(n):
        b, kh, i = finish_of(n)
        return b * nq + i, kh

    return pl.pallas_call(
        _attention_body,
        grid=(n_blocks + 1,),
        in_specs=[
            pl.BlockSpec((tq, group_width), q_map),
            pl.BlockSpec((None, seq, HEAD_DIM), k_map),
            pl.BlockSpec((None, HEAD_DIM, seq), vt_map),
        ],
        out_specs=pl.BlockSpec((tq, group_width), o_map),
        out_shape=jax.ShapeDtypeStruct((batch * seq, ATTN_WIDTH), _bf16),
        scratch_shapes=[
            pltpu.VMEM((seq, lanes), _f32), pltpu.VMEM((1, lanes), _f32),
            pltpu.VMEM((seq, lanes), _f32), pltpu.VMEM((1, lanes), _f32),
        ],
        compiler_params=pltpu.CompilerParams(
            dimension_semantics=("arbitrary",), vmem_limit_bytes=VMEM_LIMIT),
        name="attention",
    )(q, k2, vt)


def _mix_out_body(seq, x_ref, a_ref, up_ref, u_ref, un_ref, sa_ref, sb_ref,
                  wap_ref, pgw_ref, ps_ref, wpp_ref, wo_ref, g2_ref, wr_ref, br_ref, tri_ref,
                  x1_ref, h2_ref, idx_ref, gate_ref, rank_ref, cnt_ref,
                  ext_ref, lv_ref, carry_ref):
    i = pl.program_id(0)
    tm = x_ref.shape[0]

    @pl.when(i == 0)
    def _():
        carry_ref[...] = jnp.zeros_like(carry_ref)
        ext_ref[...] = jnp.zeros_like(ext_ref)
        lv_ref[...] = jnp.zeros_like(lv_ref)

    start = (i * tm) % seq
    has_prev = (start != 0).astype(_f32)
    has_next = (start + tm != seq).astype(_f32)
    first = 2 * POOL_HALO
    ext_ref[POOL_HALO:first, :] = up_ref[...].astype(_f32) * has_prev
    ext_ref[first:first + tm, :] = u_ref[...].astype(_f32)
    ext_ref[first + tm:first + tm + POOL_HALO, :] = un_ref[...].astype(_f32) * has_next
    y_attn = _dot(a_ref[...], wap_ref[...])
    lo, n = POOL_HALO, tm + 2 * POOL_HALO
    pos = start + lax.broadcasted_iota(jnp.int32, (tm, 1), 0)
    pooled = []
    for g, w in enumerate(POOL_WINDOWS):
        half = w // 2
        assert half == 1 << g
        wide = slice(g * POOL_GROUP_DIM, POOL_WIDTH)
        cols = slice(g * POOL_GROUP_DIM, (g + 1) * POOL_GROUP_DIM)
        if g == 0:
            lv_ref[0, lo:lo + n, wide] = ext_ref[pl.ds(lo - 1, n), wide] + ext_ref[pl.ds(lo, n), wide]
        elif g < N_POOL_GROUPS - 1:
            lv_ref[g, lo:lo + n, wide] = (lv_ref[g - 1, pl.ds(lo - half // 2, n), wide]
                                          + lv_ref[g - 1, pl.ds(lo + half // 2, n), wide])
        if g < N_POOL_GROUPS - 1:
            tot = lv_ref[g, first:first + tm, cols]
        else:
            tot = (lv_ref[g - 1, pl.ds(first - half // 2, tm), cols]
                   + lv_ref[g - 1, pl.ds(first + half // 2, tm), cols])
        cnt = (jnp.minimum(pos + half, seq) - jnp.maximum(pos - half, 0)).astype(_f32)
        diff = tot / cnt - ext_ref[first:first + tm, cols]
        pooled.append(_dot(diff.astype(_bf16), pgw_ref[g]))
    pooled = (jnp.concatenate(pooled, axis=1) * ps_ref[...]).astype(_bf16)
    y_pool = _dot(pooled, wpp_ref[...])
    merged = sa_ref[...].astype(_f32) * y_attn + sb_ref[...].astype(_f32) * y_pool
    x1 = x_ref[...] + _dot(merged.astype(_bf16), wo_ref[...])
    x1_ref[...] = x1

    ms = jnp.mean(x1 * x1, axis=-1, keepdims=True)
    h2 = x1 * lax.rsqrt(ms + NORM_EPS) * g2_ref[...]
    h2_ref[...] = h2
    hi, lo = _split_bf16(h2)
    both = _dot(hi, wr_ref[...])
    logits = both[:, :LANES] + both[:, LANES:] + _dot(lo, wr_ref[:, :LANES])
    lt = logits.T[:N_EXPERTS, :] + br_ref[...]
    eid = lax.broadcasted_iota(jnp.int32, lt.shape, 0)
    vals, idxs = [], []
    multi_hot = jnp.zeros(lt.shape, _f32)
    for _ in range(TOP_K):
        m = jnp.max(lt, axis=0, keepdims=True)
        sel = jnp.min(jnp.where(lt == m, eid, N_EXPERTS), axis=0, keepdims=True)
        hit = eid == sel
        vals.append(m)
        idxs.append(sel)
        multi_hot = multi_hot + hit.astype(_f32)
        lt = jnp.where(hit, -jnp.inf, lt)
    es = [jnp.exp(v - vals[0]) for v in vals]
    inv = 1.0 / (es[0] + es[1] + es[2] + es[3])
    gate_ref[...] = jnp.concatenate([e * inv for e in es], axis=0)
    idx_ref[...] = jnp.concatenate(idxs, axis=0)
    before = _dot(multi_hot.astype(_bf16), tri_ref[...]) + carry_ref[:, 0:1]
    ranks = [jnp.sum(jnp.where(eid == sel, before, 0.0), axis=0, keepdims=True) for sel in idxs]
    rank_ref[...] = jnp.concatenate(ranks, axis=0).astype(jnp.int32)
    carry_ref[...] = carry_ref[...] + jnp.sum(multi_hot, axis=1, keepdims=True)
    cnt_ref[...] = carry_ref[...]


def _mix_out(x2, attn, u, sa, sb, wap, pgw, ps, wpp, wo, g2, wr, br, tri, seq):
    tokens = x2.shape[0]
    tm = MIX_ROWS
    n = tokens // tm
    halo_blocks = tm // POOL_HALO
    last_halo = tokens // POOL_HALO - 1
    row = lambda i: (i, 0)
    const2 = lambda i: (0, 0)
    const3 = lambda i: (0, 0, 0)
    col = lambda i: (0, i)
    return pl.pallas_call(
        functools.partial(_mix_out_body, seq),
        grid=(n,),
        in_specs=[
            pl.BlockSpec((tm, D_MODEL), row),
            pl.BlockSpec((tm, ATTN_WIDTH), row),
            pl.BlockSpec((POOL_HALO, POOL_WIDTH), lambda i: (jnp.maximum(i * halo_blocks - 1, 0), 0)),
            pl.BlockSpec((tm, POOL_WIDTH), row),
            pl.BlockSpec((POOL_HALO, POOL_WIDTH), lambda i: (jnp.minimum((i + 1) * halo_blocks, last_halo), 0)),
            pl.BlockSpec((tm, D_MODEL), row),
            pl.BlockSpec((tm, D_MODEL), row),
            pl.BlockSpec((ATTN_WIDTH, D_MODEL), const2),
            pl.BlockSpec((N_POOL_GROUPS, POOL_GROUP_DIM, POOL_GROUP_DIM), const3),
            pl.BlockSpec((1, POOL_WIDTH), const2),
            pl.BlockSpec((POOL_WIDTH, D_MODEL), const2),
            pl.BlockSpec((D_MODEL, D_MODEL), const2),
            pl.BlockSpec((1, D_MODEL), const2),
            pl.BlockSpec((D_MODEL, 2 * LANES), const2),
            pl.BlockSpec((N_EXPERTS, 1), const2),
            pl.BlockSpec((tm, tm), const2),
        ],
        out_specs=[
            pl.BlockSpec((tm, D_MODEL), row),
            pl.BlockSpec((tm, D_MODEL), row),
            pl.BlockSpec((TOP_K, tm), col),
            pl.BlockSpec((TOP_K, tm), col),
            pl.BlockSpec((TOP_K, tm), col),
            pl.BlockSpec((N_EXPERTS, LANES), const2),
        ],
        out_shape=[
            jax.ShapeDtypeStruct((tokens, D_MODEL), _f32),
            jax.ShapeDtypeStruct((tokens, D_MODEL), _f32),
            jax.ShapeDtypeStruct((TOP_K, tokens), jnp.int32),
            jax.ShapeDtypeStruct((TOP_K, tokens), _f32),
            jax.ShapeDtypeStruct((TOP_K, tokens), jnp.int32),
            jax.ShapeDtypeStruct((N_EXPERTS, LANES), _f32),
        ],
        scratch_shapes=[
            pltpu.VMEM((tm + 4 * POOL_HALO, POOL_WIDTH), _f32),
            pltpu.VMEM((N_POOL_GROUPS - 1, tm + 4 * POOL_HALO, POOL_WIDTH), _f32),
            pltpu.VMEM((N_EXPERTS, LANES), _f32),
        ],
        compiler_params=pltpu.CompilerParams(
            dimension_semantics=("arbitrary",), vmem_limit_bytes=VMEM_LIMIT),
        name="mix_out",
    )(x2, attn, u, u, u, sa, sb, wap, pgw, ps, wpp, wo, g2, wr, br, tri)


def _row_copy(src_ref, src_row, dst_ref, dst_row, sem):
    return pltpu.make_async_copy(src_ref.at[pl.ds(src_row, 1)], dst_ref.at[pl.ds(dst_row, 1)], sem)


def _dispatch(dest4, pad3, h_a, h_b, n_rows):
    workers, n_chunks = dest4.shape[0], dest4.shape[1]
    n_pad_chunks = pad3.shape[1]
    per_worker = n_chunks * SC_CHUNK
    workers_a = h_a.shape[0] // per_worker
    assert workers_a * per_worker == h_a.shape[0] and (workers - workers_a) * per_worker == h_b.shape[0]
    assert n_chunks % 2 == 0
    mesh = plsc.VectorSubcoreMesh(core_axis_name="c", subcore_axis_name="s")

    @functools.partial(
        pl.kernel, mesh=mesh,
        out_type=jax.ShapeDtypeStruct((n_rows, D_MODEL), _f32),
        scratch_types=[
            pltpu.VMEM((n_chunks, TOP_K, SC_CHUNK), jnp.int32),
            pltpu.VMEM((n_pad_chunks, SC_CHUNK), jnp.int32),
            pltpu.VMEM((SC_CHUNK, D_MODEL), _f32),
            pltpu.VMEM((SC_CHUNK, D_MODEL), _f32),
            pltpu.SemaphoreType.DMA,
            pltpu.SemaphoreType.DMA,
            pltpu.SemaphoreType.DMA,
        ],
        name="dispatch",
    )
    def body(dest_hbm, pad_hbm, ha_hbm, hb_hbm, out_hbm, idx_v, pad_v, rows0, rows1, load_sem0, load_sem1, store_sem):
        wid = lax.axis_index("s") * SC_CORES + lax.axis_index("c")
        pltpu.sync_copy(dest_hbm.at[wid], idx_v)
        pltpu.sync_copy(pad_hbm.at[wid], pad_v)

        zero = jnp.zeros((SC_LANES,), _f32)

        @pl.loop(0, SC_CHUNK)
        def _(r):
            @pl.loop(0, D_MODEL, step=SC_LANES)
            def _(c):
                rows0.at[r, pl.ds(c, SC_LANES)][...] = zero

        @pl.loop(0, n_pad_chunks)
        def _(c):
            pltpu.sync_copy(rows0, out_hbm.at[pad_v.at[c]])

        def scatter_tokens(src_hbm, base):
            bufs = (rows0, rows1)
            load_sems = (load_sem0, load_sem1)

            def load(c, b):
                return pltpu.make_async_copy(src_hbm.at[pl.ds(base + c * SC_CHUNK, SC_CHUNK)], bufs[b], load_sems[b])

            load(0, 0).start()

            @pl.loop(0, n_chunks, step=2)
            def _(c):
                for b in range(2):
                    cc = c + b
                    load(cc, b).wait()

                    @pl.when(cc + 1 < n_chunks)
                    def _():
                        load(cc + 1, 1 - b).start()

                    copies = [pltpu.make_async_copy(bufs[b], out_hbm.at[idx_v.at[cc, k]], store_sem)
                              for k in range(TOP_K)]
                    for cp in copies:
                        cp.start()
                    for cp in copies:
                        cp.wait()

        @pl.when(wid < workers_a)
        def _():
            scatter_tokens(ha_hbm, wid * per_worker)

        @pl.when(wid >= workers_a)
        def _():
            scatter_tokens(hb_hbm, (wid - workers_a) * per_worker)

    return body(dest4, pad3, h_a, h_b)


def _experts_body(be_ref, nu_ref, xs_ref, wgu_ref, bgu_ref, wd_ref, bd_ref, y_ref, wgu_bf, wd_bf):
    i = pl.program_id(0)
    used = i < nu_ref[0]

    @pl.when(jnp.logical_not(used))
    def _():
        y_ref[...] = jnp.zeros_like(y_ref)

    @pl.when(jnp.logical_and(used, jnp.logical_or(i == 0, be_ref[i] != be_ref[jnp.maximum(i - 1, 0)])))
    def _():
        wgu_bf[...] = wgu_ref[...].astype(_bf16)
        wd_bf[...] = wd_ref[...].astype(_bf16)

    @pl.when(used)
    def _():
        x = xs_ref[...].astype(_bf16)
        gu = _dot(x, wgu_bf[...]) + bgu_ref[...]
        gate = jnp.minimum(gu[:, :D_FF], SWIGLU_LIMIT)
        up = jnp.clip(gu[:, D_FF:], -SWIGLU_LIMIT, SWIGLU_LIMIT)
        act = (up + 1.0) * (gate * _sigmoid(SWIGLU_ALPHA * gate))
        y_ref[...] = _dot(act.astype(_bf16), wd_bf[...]) + bd_ref[...]


def _experts(block_e, n_used, xs, wgu, bgu, wd, bd):
    n_blocks = block_e.shape[0]
    n_rows = n_blocks * MOE_ROWS
    rows = lambda i, be, nu: (jnp.minimum(i, nu[0] - 1), 0)
    per_e = lambda i, be, nu: (be[i], 0, 0)
    return pl.pallas_call(
        _experts_body,
        grid_spec=pltpu.PrefetchScalarGridSpec(
            num_scalar_prefetch=2,
            grid=(n_blocks,),
            in_specs=[
                pl.BlockSpec((MOE_ROWS, D_MODEL), rows),
                pl.BlockSpec((None, D_MODEL, 2 * D_FF), per_e),
                pl.BlockSpec((None, 1, 2 * D_FF), per_e),
                pl.BlockSpec((None, D_FF, D_MODEL), per_e),
                pl.BlockSpec((None, 1, D_MODEL), per_e),
            ],
            out_specs=pl.BlockSpec((MOE_ROWS, D_MODEL), lambda i, be, nu: (i, 0)),
            scratch_shapes=[pltpu.VMEM((D_MODEL, 2 * D_FF), _bf16), pltpu.VMEM((D_FF, D_MODEL), _bf16)],
        ),
        out_shape=jax.ShapeDtypeStruct((n_rows, D_MODEL), _f32),
        compiler_params=pltpu.CompilerParams(
            dimension_semantics=("arbitrary",), vmem_limit_bytes=VMEM_LIMIT),
        name="experts",
    )(block_e, n_used, xs, wgu, bgu, wd, bd)


def _combine_body(dest_ref, next_ref, x1_ref, gate_ref, yb_ref, o_ref, buf_ref, sems):
    tm = x1_ref.shape[0]
    i = pl.program_id(0)
    slot = i % 2

    def gather(idx_ref, s):
        def issue(g, c):
            for u in range(ISSUE_UNROLL):
                j = g * ISSUE_UNROLL + u
                for k in range(TOP_K):
                    _row_copy(yb_ref, idx_ref[0, k, j], buf_ref.at[s, k], j, sems.at[s]).start()
            return c

        lax.fori_loop(0, tm // ISSUE_UNROLL, issue, 0)

    @pl.when(i == 0)
    def _():
        gather(dest_ref, slot)

    @pl.when(i + 1 < pl.num_programs(0))
    def _():
        gather(next_ref, 1 - slot)

    for k in range(TOP_K):
        pltpu.make_async_copy(yb_ref.at[pl.ds(0, tm)], buf_ref.at[slot, k], sems.at[slot]).wait()
    gates = gate_ref[...]
    acc = gates[:, 0:1] * buf_ref[slot, 0]
    for k in range(1, TOP_K):
        acc = acc + gates[:, k:k + 1] * buf_ref[slot, k]
    o_ref[...] = x1_ref[...] + acc


def _combine(dest3, x1, gates_t, yb):
    tokens = x1.shape[0]
    tm = COMBINE_ROWS
    n = tokens // tm
    return pl.pallas_call(
        _combine_body,
        grid=(n,),
        in_specs=[
            pl.BlockSpec((1, TOP_K, tm), lambda i: (i, 0, 0), memory_space=pltpu.SMEM),
            pl.BlockSpec((1, TOP_K, tm), lambda i: (jnp.minimum(i + 1, n - 1), 0, 0), memory_space=pltpu.SMEM),
            pl.BlockSpec((tm, D_MODEL), lambda i: (i, 0)),
            pl.BlockSpec((tm, TOP_K), lambda i: (i, 0)),
            pl.BlockSpec(memory_space=pl.ANY),
        ],
        out_specs=pl.BlockSpec((tm, D_MODEL), lambda i: (i, 0)),
        out_shape=jax.ShapeDtypeStruct((tokens, D_MODEL), _f32),
        scratch_shapes=[pltpu.VMEM((2, TOP_K, tm, D_MODEL), _f32), pltpu.SemaphoreType.DMA((2,))],
        compiler_params=pltpu.CompilerParams(
            dimension_semantics=("arbitrary",), vmem_limit_bytes=VMEM_LIMIT),
        name="combine",
    )(dest3, dest3, x1, gates_t, yb)


def _rope_tables(seq):
    pos = jnp.arange(seq)
    row_ids = (pos // GRID_W).astype(_f32)
    col_ids = (pos % GRID_W).astype(_f32)
    inv_freq = ROPE_THETA ** (-jnp.arange(ROPE_PAIRS_PER_AXIS, dtype=_f32) / ROPE_PAIRS_PER_AXIS)
    ang_r = row_ids[:, None] * inv_freq
    ang_c = col_ids[:, None] * inv_freq
    cos = jnp.concatenate([jnp.cos(ang_r)] * 2 + [jnp.cos(ang_c)] * 2, axis=1)
    sin = jnp.concatenate([-jnp.sin(ang_r), jnp.sin(ang_r), -jnp.sin(ang_c), jnp.sin(ang_c)], axis=1)
    reps = LANES // HEAD_DIM
    return jnp.tile(cos, (1, reps)), jnp.tile(sin, (1, reps))


def _mixer(x, p):
    batch, seq, _ = x.shape
    x2 = x.reshape(batch * seq, D_MODEL)
    cos_t, sin_t = _rope_tables(seq)
    q, k2, vt, u, sa, sb = _in_proj(x2, p["mix_g"], p["w_in"], p["wvt"], p["qg"], p["kg"], cos_t, sin_t, p["bd"], seq)
    attn = _attention(q, k2, vt, batch, seq)
    return _mix_out(x2, attn, u, sa, sb, p["wap"], p["pgw"], p["ps"], p["wpp"], p["wo"], p["ffn_g"],
                    p["wr"], p["br"], p["tri"], seq)


def kernel(x_prompt, x_sample, mix_norm_g, w_in, q_norm_g, k_norm_g, w_attn_proj, pool_group_w, pool_scale,
           w_pool_proj, w_out, ffn_norm_g, w_router, b_router, w_gu, b_gu, w_down, b_down):
    depth = w_in.shape[0]
    xs_all = [x_prompt, x_sample]
    head_id = jnp.arange(ATTN_WIDTH) // HEAD_DIM
    block_diag = (head_id[:, None] == head_id[None, :]).astype(_bf16)
    tri_id = jnp.arange(MIX_ROWS)
    tri = (tri_id[:, None] < tri_id[None, :]).astype(_bf16)
    for l in range(depth):
        wr = jnp.pad(w_router[l].astype(_f32), ((0, 0), (0, LANES - N_EXPERTS)))
        wrh = wr.astype(_bf16)
        p = dict(
            mix_g=mix_norm_g[l].reshape(1, D_MODEL),
            w_in=w_in[l].astype(_bf16),
            wvt=w_in[l][:, ATTN_WIDTH + KV_WIDTH:ATTN_WIDTH + 2 * KV_WIDTH].T.astype(_bf16),
            qg=jnp.tile(q_norm_g[l] * (HEAD_DIM ** -0.5 * LOG2_E), N_HEADS).reshape(1, ATTN_WIDTH),
            kg=jnp.tile(k_norm_g[l], N_KV_HEADS).reshape(1, KV_WIDTH),
            bd=block_diag,
            wap=w_attn_proj[l].astype(_bf16),
            pgw=pool_group_w[l].astype(_bf16),
            ps=pool_scale[l].reshape(1, POOL_WIDTH),
            wpp=w_pool_proj[l].astype(_bf16),
            wo=w_out[l].astype(_bf16),
            ffn_g=ffn_norm_g[l].reshape(1, D_MODEL),
            wr=jnp.concatenate([wrh, (wr - wrh.astype(_f32)).astype(_bf16)], axis=1),
            br=b_router[l].astype(_f32).reshape(N_EXPERTS, 1),
            tri=tri,
        )
        mixed = [_mixer(x, p) for x in xs_all]

        counts = [m[5][:, 0].astype(jnp.int32) for m in mixed]
        total = sum(counts)
        padded = ((total + MOE_ROWS - 1) // MOE_ROWS) * MOE_ROWS
        padded_end = jnp.cumsum(padded)
        padded_start = padded_end - padded
        n_tok = sum(m[0].shape[0] for m in mixed)
        n_blocks = (n_tok * TOP_K + MOE_ROWS - 1) // MOE_ROWS + N_EXPERTS
        n_used = (padded_end[-1] // MOE_ROWS).astype(jnp.int32)
        blk = jnp.minimum(jnp.arange(n_blocks, dtype=jnp.int32), n_used - 1)
        block_e = jnp.sum((padded_end[None, :] <= (blk * MOE_ROWS)[:, None]).astype(jnp.int32), axis=1)
        block_e = jnp.minimum(block_e, N_EXPERTS - 1)
        expert_ids = jnp.arange(N_EXPERTS, dtype=jnp.int32)[:, None, None]
        dests = []
        seen = jnp.zeros((N_EXPERTS,), jnp.int32)
        for m, c in zip(mixed, counts):
            base = (padded_start + seen)[:, None, None]
            dests.append(jnp.sum(jnp.where(m[2][None] == expert_ids, base, 0), axis=0) + m[4])
            seen = seen + c
        workers = SC_CORES * SC_SUBCORES
        per_worker = n_tok // workers
        dest4 = jnp.concatenate(
            [d.reshape(TOP_K, -1, per_worker // SC_CHUNK, SC_CHUNK).transpose(1, 2, 0, 3) for d in dests], axis=0)
        n_rows = n_blocks * MOE_ROWS
        slot = jnp.arange(MOE_ROWS, dtype=jnp.int32)[None, :]
        pad_rows = jnp.where(slot < (padded - total)[:, None], (padded_start + total)[:, None] + slot,
                             n_rows + jnp.arange(N_EXPERTS, dtype=jnp.int32)[:, None] * MOE_ROWS + slot)
        pad3 = pad_rows.astype(jnp.int32).reshape(workers, -1, SC_CHUNK)
        xs_buf = _dispatch(dest4, pad3, mixed[0][1], mixed[1][1], n_rows + N_EXPERTS * MOE_ROWS)
        yb = _experts(block_e, n_used.reshape(1), xs_buf, w_gu[l], b_gu[l].reshape(N_EXPERTS, 1, -1),
                      w_down[l], b_down[l].reshape(N_EXPERTS, 1, -1))
        outs = []
        for x, m, dest in zip(xs_all, mixed, dests):
            d3 = dest.reshape(TOP_K, -1, COMBINE_ROWS).transpose(1, 0, 2)
            outs.append(_combine(d3, m[0], m[3].T, yb).reshape(x.shape))
        xs_all = outs
    return tuple(xs_all)
```

```python
import functools

import jax
import jax.numpy as jnp
from jax import lax
from jax.experimental import pallas as pl
from jax.experimental.pallas import tpu as pltpu
from jax.experimental.pallas import tpu_sc as plsc

D_MODEL = 1024
GRID_W = 64
N_HEADS = 8
N_KV_HEADS = 2
HEAD_DIM = 64
N_GROUPS_PER_KV = N_HEADS // N_KV_HEADS
ATTN_WIDTH = N_HEADS * HEAD_DIM
KV_WIDTH = N_KV_HEADS * HEAD_DIM
ROPE_THETA = 10000.0
ROPE_PAIRS_PER_AXIS = HEAD_DIM // 4
POOL_WINDOWS = (2, 4, 8, 16)
N_POOL_GROUPS = 4
POOL_WIDTH = 512
POOL_GROUP_DIM = POOL_WIDTH // N_POOL_GROUPS
N_EXPERTS = 32
TOP_K = 4
D_FF = 1024
SWIGLU_ALPHA = 1.702
SWIGLU_LIMIT = 7.0
NORM_EPS = 1e-6

LANES = 128
SUBLANES = 8
POOL_HALO = 8
LOG2_E = 1.4426950408889634

IN_PROJ_ROWS = 512
ATTN_Q_ROWS = 128
ATTN_KEY_CHUNK = 512
MIX_ROWS = 256
MOE_ROWS = 512
SC_CORES = 2
SC_SUBCORES = 16
SC_LANES = 16
SC_CHUNK = 32
SC_COMBINE_CHUNK = 8
VMEM_LIMIT = 56 * 1024 * 1024

_bf16 = jnp.bfloat16
_f32 = jnp.float32


def _dot(a, b):
    return jnp.dot(a, b, preferred_element_type=_f32)


def _split_bf16(x):
    hi = x.astype(_bf16)
    lo = (x - hi.astype(_f32)).astype(_bf16)
    return hi, lo


def _sigmoid(x):
    return 1.0 / (1.0 + jnp.exp(-x))


def _in_proj_body(x_ref, g_ref, w_ref, wvt_ref, qg_ref, kg_ref, cos_ref, sin_ref, bd_ref,
                  q_ref, k_ref, vt_ref, u_ref, sa_ref, sb_ref):
    x = x_ref[...]
    ms = jnp.mean(x * x, axis=-1, keepdims=True)
    h = (x * lax.rsqrt(ms + NORM_EPS) * g_ref[...]).astype(_bf16)
    cos = cos_ref[...]
    sin = sin_ref[...]
    lane = lax.broadcasted_iota(jnp.int32, cos.shape, 1)
    first_half = (lane % (2 * ROPE_PAIRS_PER_AXIS)) < ROPE_PAIRS_PER_AXIS

    def head_norm_rope(z, gain):
        width = z.shape[1]
        ss = _dot((z * z).astype(_bf16), bd_ref[:width, :width])
        zn = z * lax.rsqrt(ss * (1.0 / HEAD_DIM) + NORM_EPS) * gain
        outs = []
        for j in range(width // LANES):
            c = zn[:, j * LANES:(j + 1) * LANES]
            partner = jnp.where(first_half,
                                pltpu.roll(c, LANES - ROPE_PAIRS_PER_AXIS, 1),
                                pltpu.roll(c, ROPE_PAIRS_PER_AXIS, 1))
            outs.append(c * cos + partner * sin)
        return outs

    c0 = 0
    zq = _dot(h, w_ref[:, c0:c0 + ATTN_WIDTH])
    for j, o in enumerate(head_norm_rope(zq, qg_ref[...])):
        q_ref[:, j * LANES:(j + 1) * LANES] = o.astype(_bf16)
    c0 += ATTN_WIDTH
    zk = _dot(h, w_ref[:, c0:c0 + KV_WIDTH])
    (kr,) = head_norm_rope(zk, kg_ref[...])
    for j in range(N_KV_HEADS):
        k_ref[j] = kr[:, j * HEAD_DIM:(j + 1) * HEAD_DIM].astype(_bf16)
    c0 += KV_WIDTH
    zvt = lax.dot_general(wvt_ref[...], h, (((1,), (1,)), ((), ())), preferred_element_type=_f32)
    for j in range(N_KV_HEADS):
        vt_ref[j] = zvt[j * HEAD_DIM:(j + 1) * HEAD_DIM, :].astype(_bf16)
    c0 += KV_WIDTH
    u_ref[...] = _dot(h, w_ref[:, c0:c0 + POOL_WIDTH]).astype(_bf16)
    c0 += POOL_WIDTH
    sa_ref[...] = _sigmoid(_dot(h, w_ref[:, c0:c0 + D_MODEL])).astype(_bf16)
    c0 += D_MODEL
    sb_ref[...] = _sigmoid(_dot(h, w_ref[:, c0:c0 + D_MODEL])).astype(_bf16)


def _in_proj(x2, norm_g, w_in, wvt, qg, kg, cos_t, sin_t, bd, seq):
    tokens = x2.shape[0]
    tm = IN_PROJ_ROWS
    in_width = w_in.shape[1]
    tiles_per_seq = seq // tm
    const = lambda i: (0, 0)
    row = lambda i: (i, 0)
    return pl.pallas_call(
        _in_proj_body,
        grid=(tokens // tm,),
        in_specs=[
            pl.BlockSpec((tm, D_MODEL), row),
            pl.BlockSpec((1, D_MODEL), const),
            pl.BlockSpec((D_MODEL, in_width), const),
            pl.BlockSpec((KV_WIDTH, D_MODEL), const),
            pl.BlockSpec((1, ATTN_WIDTH), const),
            pl.BlockSpec((1, KV_WIDTH), const),
            pl.BlockSpec((tm, LANES), lambda i: (i % tiles_per_seq, 0)),
            pl.BlockSpec((tm, LANES), lambda i: (i % tiles_per_seq, 0)),
            pl.BlockSpec((ATTN_WIDTH, ATTN_WIDTH), const),
        ],
        out_specs=[
            pl.BlockSpec((tm, ATTN_WIDTH), row),
            pl.BlockSpec((N_KV_HEADS, tm, HEAD_DIM), lambda i: (0, i, 0)),
            pl.BlockSpec((N_KV_HEADS, HEAD_DIM, tm), lambda i: (0, 0, i)),
            pl.BlockSpec((tm, POOL_WIDTH), row),
            pl.BlockSpec((tm, D_MODEL), row),
            pl.BlockSpec((tm, D_MODEL), row),
        ],
        out_shape=[
            jax.ShapeDtypeStruct((tokens, ATTN_WIDTH), _bf16),
            jax.ShapeDtypeStruct((N_KV_HEADS, tokens, HEAD_DIM), _bf16),
            jax.ShapeDtypeStruct((N_KV_HEADS, HEAD_DIM, tokens), _bf16),
            jax.ShapeDtypeStruct((tokens, POOL_WIDTH), _bf16),
            jax.ShapeDtypeStruct((tokens, D_MODEL), _bf16),
            jax.ShapeDtypeStruct((tokens, D_MODEL), _bf16),
        ],
        compiler_params=pltpu.CompilerParams(
            dimension_semantics=("parallel",), vmem_limit_bytes=VMEM_LIMIT),
        name="in_proj",
    )(x2, norm_g, w_in, wvt, qg, kg, cos_t, sin_t, bd)


def _attention_step(q_ref, k_ref, vt_ref, o_ref, st_new, m_new, st_old, m_old):
    tq = q_ref.shape[0]
    q = q_ref[...]
    qs = jnp.concatenate([q[:, g * HEAD_DIM:(g + 1) * HEAD_DIM] for g in range(N_GROUPS_PER_KV)], axis=0)
    seq = k_ref.shape[0]
    m_prev = m_old[...]
    m_run = denom = ot = None
    for c0 in range(0, seq, ATTN_KEY_CHUNK):
        rows = slice(c0, c0 + ATTN_KEY_CHUNK)
        st = lax.dot_general(k_ref[rows, :], qs, (((1,), (1,)), ((), ())), preferred_element_type=_f32)
        st_new[rows, :] = st
        m_c = jnp.max(st, axis=0, keepdims=True)
        m_run = m_c if m_run is None else jnp.maximum(m_run, m_c)
        p = jnp.exp2(st_old[rows, :] - m_prev)
        l_c = jnp.sum(p, axis=0, keepdims=True)
        o_c = _dot(vt_ref[:, rows], p.astype(_bf16))
        denom = l_c if denom is None else denom + l_c
        ot = o_c if ot is None else ot + o_c
    m_new[...] = m_run
    ot = ot * (1.0 / denom)
    stacked = jnp.concatenate([ot[:, g * tq:(g + 1) * tq] for g in range(N_GROUPS_PER_KV)], axis=0)
    o_ref[...] = stacked.T.astype(_bf16)


def _attention_body(q_ref, k_ref, vt_ref, o_ref, st_a, m_a, st_b, m_b):
    n = pl.program_id(0)

    @pl.when(n == 0)
    def _():
        st_b[...] = jnp.zeros_like(st_b)
        m_b[...] = jnp.zeros_like(m_b)

    @pl.when(n % 2 == 0)
    def _():
        _attention_step(q_ref, k_ref, vt_ref, o_ref, st_a, m_a, st_b, m_b)

    @pl.when(n % 2 == 1)
    def _():
        _attention_step(q_ref, k_ref, vt_ref, o_ref, st_b, m_b, st_a, m_a)


def _attention(q, k2, vt, batch, seq):
    tq = ATTN_Q_ROWS
    nq = seq // tq
    n_blocks = batch * N_KV_HEADS * nq
    group_width = N_GROUPS_PER_KV * HEAD_DIM
    lanes = N_GROUPS_PER_KV * tq

    def decode(n):
        return n // (N_KV_HEADS * nq), (n // nq) % N_KV_HEADS, n % nq

    def scores_of(n):
        return decode(jnp.minimum(n, n_blocks - 1))

    def finish_of(n):
        return decode(jnp.maximum(n - 1, 0))

    def q_map(n):
        b, kh, i = scores_of(n)
        return b * nq + i, kh

    def k_map(n):
        b, kh, _ = scores_of(n)
        return kh, b, 0

    def vt_map(n):
        b, kh, _ = finish_of(n)
        return kh, 0, b

    def o_map(n):
        b, kh, i = finish_of(n)
        return b * nq + i, kh

    return pl.pallas_call(
        _attention_body,
        grid=(n_blocks + 1,),
        in_specs=[
            pl.BlockSpec((tq, group_width), q_map),
            pl.BlockSpec((None, seq, HEAD_DIM), k_map),
            pl.BlockSpec((None, HEAD_DIM, seq), vt_map),
        ],
        out_specs=pl.BlockSpec((tq, group_width), o_map),
        out_shape=jax.ShapeDtypeStruct((batch * seq, ATTN_WIDTH), _bf16),
        scratch_shapes=[
            pltpu.VMEM((seq, lanes), _f32), pltpu.VMEM((1, lanes), _f32),
            pltpu.VMEM((seq, lanes), _f32), pltpu.VMEM((1, lanes), _f32),
        ],
        compiler_params=pltpu.CompilerParams(
            dimension_semantics=("arbitrary",), vmem_limit_bytes=VMEM_LIMIT),
        name="attention",
    )(q, k2, vt)


def _mix_out_body(seq, x_ref, a_ref, up_ref, u_ref, un_ref, sa_ref, sb_ref,
                  wap_ref, pgw_ref, ps_ref, wpp_ref, wo_ref, g2_ref, wr_ref, br_ref, tri_ref,
                  x1_ref, h2_ref, idx_ref, gate_ref, rank_ref, cnt_ref,
                  ext_ref, lv_ref, carry_ref):
    i = pl.program_id(0)
    tm = x_ref.shape[0]

    @pl.when(i == 0)
    def _():
        carry_ref[...] = jnp.zeros_like(carry_ref)
        ext_ref[...] = jnp.zeros_like(ext_ref)
        lv_ref[...] = jnp.zeros_like(lv_ref)

    start = (i * tm) % seq
    has_prev = (start != 0).astype(_f32)
    has_next = (start + tm != seq).astype(_f32)
    first = 2 * POOL_HALO
    ext_ref[POOL_HALO:first, :] = up_ref[...].astype(_f32) * has_prev
    ext_ref[first:first + tm, :] = u_ref[...].astype(_f32)
    ext_ref[first + tm:first + tm + POOL_HALO, :] = un_ref[...].astype(_f32) * has_next
    y_attn = _dot(a_ref[...], wap_ref[...])
    lo, n = POOL_HALO, tm + 2 * POOL_HALO
    pos = start + lax.broadcasted_iota(jnp.int32, (tm, 1), 0)
    pooled = []
    for g, w in enumerate(POOL_WINDOWS):
        half = w // 2
        assert half == 1 << g
        wide = slice(g * POOL_GROUP_DIM, POOL_WIDTH)
        cols = slice(g * POOL_GROUP_DIM, (g + 1) * POOL_GROUP_DIM)
        if g == 0:
            lv_ref[0, lo:lo + n, wide] = ext_ref[pl.ds(lo - 1, n), wide] + ext_ref[pl.ds(lo, n), wide]
        elif g < N_POOL_GROUPS - 1:
            lv_ref[g, lo:lo + n, wide] = (lv_ref[g - 1, pl.ds(lo - half // 2, n), wide]
                                          + lv_ref[g - 1, pl.ds(lo + half // 2, n), wide])
        if g < N_POOL_GROUPS - 1:
            tot = lv_ref[g, first:first + tm, cols]
        else:
            tot = (lv_ref[g - 1, pl.ds(first - half // 2, tm), cols]
                   + lv_ref[g - 1, pl.ds(first + half // 2, tm), cols])
        cnt = (jnp.minimum(pos + half, seq) - jnp.maximum(pos - half, 0)).astype(_f32)
        diff = tot / cnt - ext_ref[first:first + tm, cols]
        pooled.append(_dot(diff.astype(_bf16), pgw_ref[g]))
    pooled = (jnp.concatenate(pooled, axis=1) * ps_ref[...]).astype(_bf16)
    y_pool = _dot(pooled, wpp_ref[...])
    merged = sa_ref[...].astype(_f32) * y_attn + sb_ref[...].astype(_f32) * y_pool
    x1 = x_ref[...] + _dot(merged.astype(_bf16), wo_ref[...])
    x1_ref[...] = x1

    ms = jnp.mean(x1 * x1, axis=-1, keepdims=True)
    h2 = x1 * lax.rsqrt(ms + NORM_EPS) * g2_ref[...]
    h2_ref[...] = h2
    hi, lo = _split_bf16(h2)
    both = _dot(hi, wr_ref[...])
    logits = both[:, :LANES] + both[:, LANES:] + _dot(lo, wr_ref[:, :LANES])
    lt = logits.T[:N_EXPERTS, :] + br_ref[...]
    eid = lax.broadcasted_iota(jnp.int32, lt.shape, 0)
    vals, idxs = [], []
    multi_hot = jnp.zeros(lt.shape, _f32)
    for _ in range(TOP_K):
        m = jnp.max(lt, axis=0, keepdims=True)
        sel = jnp.min(jnp.where(lt == m, eid, N_EXPERTS), axis=0, keepdims=True)
        hit = eid == sel
        vals.append(m)
        idxs.append(sel)
        multi_hot = multi_hot + hit.astype(_f32)
        lt = jnp.where(hit, -jnp.inf, lt)
    es = [jnp.exp(v - vals[0]) for v in vals]
    inv = 1.0 / (es[0] + es[1] + es[2] + es[3])
    gate_ref[...] = jnp.concatenate([e * inv for e in es], axis=0)
    idx_ref[...] = jnp.concatenate(idxs, axis=0)
    before = _dot(multi_hot.astype(_bf16), tri_ref[...]) + carry_ref[:, 0:1]
    ranks = [jnp.sum(jnp.where(eid == sel, before, 0.0), axis=0, keepdims=True) for sel in idxs]
    rank_ref[...] = jnp.concatenate(ranks, axis=0).astype(jnp.int32)
    carry_ref[...] = carry_ref[...] + jnp.sum(multi_hot, axis=1, keepdims=True)
    cnt_ref[...] = carry_ref[...]


def _mix_out(x2, attn, u, sa, sb, wap, pgw, ps, wpp, wo, g2, wr, br, tri, seq):
    tokens = x2.shape[0]
    tm = MIX_ROWS
    n = tokens // tm
    halo_blocks = tm // POOL_HALO
    last_halo = tokens // POOL_HALO - 1
    row = lambda i: (i, 0)
    const2 = lambda i: (0, 0)
    const3 = lambda i: (0, 0, 0)
    col = lambda i: (0, i)
    return pl.pallas_call(
        functools.partial(_mix_out_body, seq),
        grid=(n,),
        in_specs=[
            pl.BlockSpec((tm, D_MODEL), row),
            pl.BlockSpec((tm, ATTN_WIDTH), row),
            pl.BlockSpec((POOL_HALO, POOL_WIDTH), lambda i: (jnp.maximum(i * halo_blocks - 1, 0), 0)),
            pl.BlockSpec((tm, POOL_WIDTH), row),
            pl.BlockSpec((POOL_HALO, POOL_WIDTH), lambda i: (jnp.minimum((i + 1) * halo_blocks, last_halo), 0)),
            pl.BlockSpec((tm, D_MODEL), row),
            pl.BlockSpec((tm, D_MODEL), row),
            pl.BlockSpec((ATTN_WIDTH, D_MODEL), const2),
            pl.BlockSpec((N_POOL_GROUPS, POOL_GROUP_DIM, POOL_GROUP_DIM), const3),
            pl.BlockSpec((1, POOL_WIDTH), const2),
            pl.BlockSpec((POOL_WIDTH, D_MODEL), const2),
            pl.BlockSpec((D_MODEL, D_MODEL), const2),
            pl.BlockSpec((1, D_MODEL), const2),
            pl.BlockSpec((D_MODEL, 2 * LANES), const2),
            pl.BlockSpec((N_EXPERTS, 1), const2),
            pl.BlockSpec((tm, tm), const2),
        ],
        out_specs=[
            pl.BlockSpec((tm, D_MODEL), row),
            pl.BlockSpec((tm, D_MODEL), row),
            pl.BlockSpec((TOP_K, tm), col),
            pl.BlockSpec((TOP_K, tm), col),
            pl.BlockSpec((TOP_K, tm), col),
            pl.BlockSpec((N_EXPERTS, LANES), const2),
        ],
        out_shape=[
            jax.ShapeDtypeStruct((tokens, D_MODEL), _f32),
            jax.ShapeDtypeStruct((tokens, D_MODEL), _f32),
            jax.ShapeDtypeStruct((TOP_K, tokens), jnp.int32),
            jax.ShapeDtypeStruct((TOP_K, tokens), _f32),
            jax.ShapeDtypeStruct((TOP_K, tokens), jnp.int32),
            jax.ShapeDtypeStruct((N_EXPERTS, LANES), _f32),
        ],
        scratch_shapes=[
            pltpu.VMEM((tm + 4 * POOL_HALO, POOL_WIDTH), _f32),
            pltpu.VMEM((N_POOL_GROUPS - 1, tm + 4 * POOL_HALO, POOL_WIDTH), _f32),
            pltpu.VMEM((N_EXPERTS, LANES), _f32),
        ],
        compiler_params=pltpu.CompilerParams(
            dimension_semantics=("arbitrary",), vmem_limit_bytes=VMEM_LIMIT),
        name="mix_out",
    )(x2, attn, u, u, u, sa, sb, wap, pgw, ps, wpp, wo, g2, wr, br, tri)


def _worker_ranges(group_tokens, per_worker, workers):
    first = [0]
    for t in group_tokens:
        assert t % per_worker == 0
        first.append(first[-1] + t // per_worker)
    assert first[-1] == workers
    return first


def _dispatch(dest4, pad3, hs, n_rows):
    workers, n_chunks = dest4.shape[0], dest4.shape[1]
    n_pad_chunks = pad3.shape[1]
    per_worker = n_chunks * SC_CHUNK
    first_worker = _worker_ranges([h.shape[0] for h in hs], per_worker, workers)
    assert n_chunks % 2 == 0
    mesh = plsc.VectorSubcoreMesh(core_axis_name="c", subcore_axis_name="s")

    @functools.partial(
        pl.kernel, mesh=mesh,
        out_type=jax.ShapeDtypeStruct((n_rows, D_MODEL), _f32),
        scratch_types=[
            pltpu.VMEM((n_chunks, TOP_K, SC_CHUNK), jnp.int32),
            pltpu.VMEM((n_pad_chunks, SC_CHUNK), jnp.int32),
            pltpu.VMEM((SC_CHUNK, D_MODEL), _f32),
            pltpu.VMEM((SC_CHUNK, D_MODEL), _f32),
            pltpu.SemaphoreType.DMA,
            pltpu.SemaphoreType.DMA,
            pltpu.SemaphoreType.DMA,
        ],
        name="dispatch",
    )
    def body(dest_hbm, pad_hbm, *refs):
        h_hbms = refs[:len(hs)]
        out_hbm, idx_v, pad_v, rows0, rows1, load_sem0, load_sem1, store_sem = refs[len(hs):]
        wid = lax.axis_index("s") * SC_CORES + lax.axis_index("c")
        pltpu.sync_copy(dest_hbm.at[wid], idx_v)
        pltpu.sync_copy(pad_hbm.at[wid], pad_v)

        zero = jnp.zeros((SC_LANES,), _f32)

        @pl.loop(0, SC_CHUNK)
        def _(r):
            @pl.loop(0, D_MODEL, step=SC_LANES)
            def _(c):
                rows0.at[r, pl.ds(c, SC_LANES)][...] = zero

        @pl.loop(0, n_pad_chunks)
        def _(c):
            pltpu.sync_copy(rows0, out_hbm.at[pad_v.at[c]])

        def scatter_tokens(src_hbm, base):
            bufs = (rows0, rows1)
            load_sems = (load_sem0, load_sem1)

            def load(c, b):
                return pltpu.make_async_copy(src_hbm.at[pl.ds(base + c * SC_CHUNK, SC_CHUNK)], bufs[b], load_sems[b])

            load(0, 0).start()

            @pl.loop(0, n_chunks, step=2)
            def _(c):
                for b in range(2):
                    cc = c + b
                    load(cc, b).wait()

                    @pl.when(cc + 1 < n_chunks)
                    def _():
                        load(cc + 1, 1 - b).start()

                    copies = [pltpu.make_async_copy(bufs[b], out_hbm.at[idx_v.at[cc, k]], store_sem)
                              for k in range(TOP_K)]
                    for cp in copies:
                        cp.start()
                    for cp in copies:
                        cp.wait()

        for g, h_hbm in enumerate(h_hbms):
            @pl.when(jnp.logical_and(wid >= first_worker[g], wid < first_worker[g + 1]))
            def _(g=g, h_hbm=h_hbm):
                scatter_tokens(h_hbm, (wid - first_worker[g]) * per_worker)

    return body(dest4, pad3, *hs)


def _experts_body(be_ref, nu_ref, xs_ref, wgu_ref, bgu_ref, wd_ref, bd_ref, y_ref, wgu_bf, wd_bf):
    i = pl.program_id(0)
    used = i < nu_ref[0]

    @pl.when(jnp.logical_not(used))
    def _():
        y_ref[...] = jnp.zeros_like(y_ref)

    @pl.when(jnp.logical_and(used, jnp.logical_or(i == 0, be_ref[i] != be_ref[jnp.maximum(i - 1, 0)])))
    def _():
        wgu_bf[...] = wgu_ref[...].astype(_bf16)
        wd_bf[...] = wd_ref[...].astype(_bf16)

    @pl.when(used)
    def _():
        x = xs_ref[...].astype(_bf16)
        gu = _dot(x, wgu_bf[...]) + bgu_ref[...]
        gate = jnp.minimum(gu[:, :D_FF], SWIGLU_LIMIT)
        up = jnp.clip(gu[:, D_FF:], -SWIGLU_LIMIT, SWIGLU_LIMIT)
        act = (up + 1.0) * (gate * _sigmoid(SWIGLU_ALPHA * gate))
        y_ref[...] = _dot(act.astype(_bf16), wd_bf[...]) + bd_ref[...]


def _experts(block_e, n_used, xs, wgu, bgu, wd, bd):
    n_blocks = block_e.shape[0]
    n_rows = n_blocks * MOE_ROWS
    rows = lambda i, be, nu: (jnp.minimum(i, nu[0] - 1), 0)
    per_e = lambda i, be, nu: (be[i], 0, 0)
    return pl.pallas_call(
        _experts_body,
        grid_spec=pltpu.PrefetchScalarGridSpec(
            num_scalar_prefetch=2,
            grid=(n_blocks,),
            in_specs=[
                pl.BlockSpec((MOE_ROWS, D_MODEL), rows),
                pl.BlockSpec((None, D_MODEL, 2 * D_FF), per_e),
                pl.BlockSpec((None, 1, 2 * D_FF), per_e),
                pl.BlockSpec((None, D_FF, D_MODEL), per_e),
                pl.BlockSpec((None, 1, D_MODEL), per_e),
            ],
            out_specs=pl.BlockSpec((MOE_ROWS, D_MODEL), lambda i, be, nu: (i, 0)),
            scratch_shapes=[pltpu.VMEM((D_MODEL, 2 * D_FF), _bf16), pltpu.VMEM((D_FF, D_MODEL), _bf16)],
        ),
        out_shape=jax.ShapeDtypeStruct((n_rows, D_MODEL), _f32),
        compiler_params=pltpu.CompilerParams(
            dimension_semantics=("arbitrary",), vmem_limit_bytes=VMEM_LIMIT),
        name="experts",
    )(block_e, n_used, xs, wgu, bgu, wd, bd)


def _combine(dest_rows, gates, x1s, yb):
    workers = dest_rows.shape[0]
    per_worker = dest_rows.shape[1] * LANES // TOP_K
    n_chunks = per_worker // SC_COMBINE_CHUNK
    first_worker = _worker_ranges([x.shape[0] for x in x1s], per_worker, workers)
    n_groups = len(x1s)
    assert n_chunks % 2 == 0 and LANES % SC_COMBINE_CHUNK == 0
    mesh = plsc.VectorSubcoreMesh(core_axis_name="c", subcore_axis_name="s")
    row_buf = pltpu.VMEM((SC_COMBINE_CHUNK, D_MODEL), _f32)

    @functools.partial(
        pl.kernel, mesh=mesh,
        out_type=[jax.ShapeDtypeStruct(x.shape, _f32) for x in x1s],
        scratch_types=[
            pltpu.VMEM(dest_rows.shape[1:], jnp.int32),
            [[row_buf] * (TOP_K + 1), [row_buf] * (TOP_K + 1)],
            [pltpu.VMEM((SC_COMBINE_CHUNK, TOP_K * SC_LANES), _f32)] * 2,
            [pltpu.SemaphoreType.DMA] * 2, [pltpu.SemaphoreType.DMA] * 2, [pltpu.SemaphoreType.DMA] * 2,
        ],
        name="combine",
    )
    def body(dest_hbm, *refs):
        g_hbms, x_hbms = refs[:n_groups], refs[n_groups:2 * n_groups]
        yb_hbm = refs[2 * n_groups]
        o_hbms = refs[2 * n_groups + 1:3 * n_groups + 1]
        idx_v, row_bufs, gate_bufs, gather_sems, load_sems, store_sems = refs[3 * n_groups + 1:]
        wid = lax.axis_index("s") * SC_CORES + lax.axis_index("c")
        pltpu.sync_copy(dest_hbm.at[wid], idx_v)

        def run(g_hbm, x_hbm, o_hbm, base):
            def loads(c, b):
                rows = pl.ds(base + c * SC_COMBINE_CHUNK, SC_COMBINE_CHUNK)
                cps = []
                for k in range(TOP_K):
                    pos = (c * TOP_K + k) * SC_COMBINE_CHUNK
                    ids = idx_v.at[pos // LANES, pl.ds(pos % LANES, SC_COMBINE_CHUNK)]
                    cps.append(pltpu.make_async_copy(yb_hbm.at[ids], row_bufs[b][k], gather_sems[b]))
                cps.append(pltpu.make_async_copy(x_hbm.at[rows], row_bufs[b][TOP_K], load_sems[b]))
                cps.append(pltpu.make_async_copy(g_hbm.at[rows], gate_bufs[b], load_sems[b]))
                return cps

            def store(c, b):
                rows = pl.ds(base + c * SC_COMBINE_CHUNK, SC_COMBINE_CHUNK)
                return pltpu.make_async_copy(row_bufs[b][TOP_K], o_hbm.at[rows], store_sems[b])

            for cp in loads(0, 0):
                cp.start()

            @pl.loop(0, n_chunks, step=2)
            def _(c):
                for b in range(2):
                    cc = c + b

                    @pl.when(cc + 1 < n_chunks)
                    def _():
                        @pl.when(cc >= 1)
                        def _():
                            store(cc - 1, 1 - b).wait()
                        for cp in loads(cc + 1, 1 - b):
                            cp.start()

                    for cp in loads(cc, b):
                        cp.wait()
                    acc_buf, gate_buf = row_bufs[b][TOP_K], gate_bufs[b]

                    @pl.loop(0, SC_COMBINE_CHUNK)
                    def _(r):
                        g = [gate_buf.at[r, pl.ds(k * SC_LANES, SC_LANES)][...] for k in range(TOP_K)]

                        @plsc.parallel_loop(0, D_MODEL, step=SC_LANES, unroll=4)
                        def _(col):
                            acc = acc_buf.at[r, pl.ds(col, SC_LANES)][...]
                            for k in range(TOP_K):
                                acc = acc + g[k] * row_bufs[b][k].at[r, pl.ds(col, SC_LANES)][...]
                            acc_buf.at[r, pl.ds(col, SC_LANES)][...] = acc

                    store(cc, b).start()

            store(n_chunks - 2, 0).wait()
            store(n_chunks - 1, 1).wait()

        for g in range(n_groups):
            @pl.when(jnp.logical_and(wid >= first_worker[g], wid < first_worker[g + 1]))
            def _(g=g):
                run(g_hbms[g], x_hbms[g], o_hbms[g], (wid - first_worker[g]) * per_worker)

    return body(dest_rows, *gates, *x1s, yb)


def _rope_tables(seq):
    pos = jnp.arange(seq)
    row_ids = (pos // GRID_W).astype(_f32)
    col_ids = (pos % GRID_W).astype(_f32)
    inv_freq = ROPE_THETA ** (-jnp.arange(ROPE_PAIRS_PER_AXIS, dtype=_f32) / ROPE_PAIRS_PER_AXIS)
    ang_r = row_ids[:, None] * inv_freq
    ang_c = col_ids[:, None] * inv_freq
    cos = jnp.concatenate([jnp.cos(ang_r)] * 2 + [jnp.cos(ang_c)] * 2, axis=1)
    sin = jnp.concatenate([-jnp.sin(ang_r), jnp.sin(ang_r), -jnp.sin(ang_c), jnp.sin(ang_c)], axis=1)
    reps = LANES // HEAD_DIM
    return jnp.tile(cos, (1, reps)), jnp.tile(sin, (1, reps))


def _mixer(x, p):
    batch, seq, _ = x.shape
    x2 = x.reshape(batch * seq, D_MODEL)
    cos_t, sin_t = _rope_tables(seq)
    q, k2, vt, u, sa, sb = _in_proj(x2, p["mix_g"], p["w_in"], p["wvt"], p["qg"], p["kg"], cos_t, sin_t, p["bd"], seq)
    attn = _attention(q, k2, vt, batch, seq)
    return _mix_out(x2, attn, u, sa, sb, p["wap"], p["pgw"], p["ps"], p["wpp"], p["wo"], p["ffn_g"],
                    p["wr"], p["br"], p["tri"], seq)


def _moe(mixed, w_gu, b_gu, w_down, b_down):
    counts = [m[5][:, 0].astype(jnp.int32) for m in mixed]
    total = sum(counts)
    padded = ((total + MOE_ROWS - 1) // MOE_ROWS) * MOE_ROWS
    padded_end = jnp.cumsum(padded)
    padded_start = padded_end - padded
    n_tok = sum(m[0].shape[0] for m in mixed)
    n_blocks = (n_tok * TOP_K + MOE_ROWS - 1) // MOE_ROWS + N_EXPERTS
    n_used = (padded_end[-1] // MOE_ROWS).astype(jnp.int32)
    blk = jnp.minimum(jnp.arange(n_blocks, dtype=jnp.int32), n_used - 1)
    block_e = jnp.sum((padded_end[None, :] <= (blk * MOE_ROWS)[:, None]).astype(jnp.int32), axis=1)
    block_e = jnp.minimum(block_e, N_EXPERTS - 1)
    expert_ids = jnp.arange(N_EXPERTS, dtype=jnp.int32)[:, None, None]
    dests = []
    seen = jnp.zeros((N_EXPERTS,), jnp.int32)
    for m, c in zip(mixed, counts):
        base = (padded_start + seen)[:, None, None]
        dests.append(jnp.sum(jnp.where(m[2][None] == expert_ids, base, 0), axis=0) + m[4])
        seen = seen + c
    workers = SC_CORES * SC_SUBCORES
    per_worker = n_tok // workers
    dest4 = jnp.concatenate(
        [d.reshape(TOP_K, -1, per_worker // SC_CHUNK, SC_CHUNK).transpose(1, 2, 0, 3) for d in dests], axis=0)
    n_rows = n_blocks * MOE_ROWS
    slot = jnp.arange(MOE_ROWS, dtype=jnp.int32)[None, :]
    pad_rows = jnp.where(slot < (padded - total)[:, None], (padded_start + total)[:, None] + slot,
                         n_rows + jnp.arange(N_EXPERTS, dtype=jnp.int32)[:, None] * MOE_ROWS + slot)
    pad3 = pad_rows.astype(jnp.int32).reshape(workers, -1, SC_CHUNK)
    xs_buf = _dispatch(dest4, pad3, [m[1] for m in mixed], n_rows + N_EXPERTS * MOE_ROWS)
    yb = _experts(block_e, n_used.reshape(1), xs_buf, w_gu, b_gu.reshape(N_EXPERTS, 1, -1),
                  w_down, b_down.reshape(N_EXPERTS, 1, -1))
    dest_rows = jnp.concatenate(
        [d.reshape(TOP_K, -1, per_worker // SC_COMBINE_CHUNK, SC_COMBINE_CHUNK).transpose(1, 2, 0, 3)
         .reshape(-1, per_worker * TOP_K // LANES, LANES) for d in dests], axis=0)
    gate_vecs = [jnp.repeat(m[3].T, SC_LANES, axis=1) for m in mixed]
    return _combine(dest_rows, gate_vecs, [m[0] for m in mixed], yb)


def kernel(x_prompt, x_sample, mix_norm_g, w_in, q_norm_g, k_norm_g, w_attn_proj, pool_group_w, pool_scale,
           w_pool_proj, w_out, ffn_norm_g, w_router, b_router, w_gu, b_gu, w_down, b_down):
    depth = w_in.shape[0]
    xs_all = [x_prompt, x_sample]
    head_id = jnp.arange(ATTN_WIDTH) // HEAD_DIM
    block_diag = (head_id[:, None] == head_id[None, :]).astype(_bf16)
    tri_id = jnp.arange(MIX_ROWS)
    tri = (tri_id[:, None] < tri_id[None, :]).astype(_bf16)
    for l in range(depth):
        wr = jnp.pad(w_router[l].astype(_f32), ((0, 0), (0, LANES - N_EXPERTS)))
        wrh = wr.astype(_bf16)
        p = dict(
            mix_g=mix_norm_g[l].reshape(1, D_MODEL),
            w_in=w_in[l].astype(_bf16),
            wvt=w_in[l][:, ATTN_WIDTH + KV_WIDTH:ATTN_WIDTH + 2 * KV_WIDTH].T.astype(_bf16),
            qg=jnp.tile(q_norm_g[l] * (HEAD_DIM ** -0.5 * LOG2_E), N_HEADS).reshape(1, ATTN_WIDTH),
            kg=jnp.tile(k_norm_g[l], N_KV_HEADS).reshape(1, KV_WIDTH),
            bd=block_diag,
            wap=w_attn_proj[l].astype(_bf16),
            pgw=pool_group_w[l].astype(_bf16),
            ps=pool_scale[l].reshape(1, POOL_WIDTH),
            wpp=w_pool_proj[l].astype(_bf16),
            wo=w_out[l].astype(_bf16),
            ffn_g=ffn_norm_g[l].reshape(1, D_MODEL),
            wr=jnp.concatenate([wrh, (wr - wrh.astype(_f32)).astype(_bf16)], axis=1),
            br=b_router[l].astype(_f32).reshape(N_EXPERTS, 1),
            tri=tri,
        )
        mixed = [_mixer(x, p) for x in xs_all]
        outs = [_moe([m], w_gu[l], b_gu[l], w_down[l], b_down[l])[0] for m in mixed]
        xs_all = [o.reshape(x.shape) for o, x in zip(outs, xs_all)]
    return tuple(xs_all)
```

```python
import functools

import jax
import jax.numpy as jnp
from jax import lax
from jax.experimental import pallas as pl
from jax.experimental.pallas import tpu as pltpu
from jax.experimental.pallas import tpu_sc as plsc

D_MODEL = 1024
GRID_W = 64
N_HEADS = 8
N_KV_HEADS = 2
HEAD_DIM = 64
N_GROUPS_PER_KV = N_HEADS // N_KV_HEADS
ATTN_WIDTH = N_HEADS * HEAD_DIM
KV_WIDTH = N_KV_HEADS * HEAD_DIM
ROPE_THETA = 10000.0
ROPE_PAIRS_PER_AXIS = HEAD_DIM // 4
POOL_WINDOWS = (2, 4, 8, 16)
N_POOL_GROUPS = 4
POOL_WIDTH = 512
POOL_GROUP_DIM = POOL_WIDTH // N_POOL_GROUPS
N_EXPERTS = 32
TOP_K = 4
D_FF = 1024
SWIGLU_ALPHA = 1.702
SWIGLU_LIMIT = 7.0
NORM_EPS = 1e-6

LANES = 128
SUBLANES = 8
POOL_HALO = 8
LOG2_E = 1.4426950408889634

IN_PROJ_ROWS = 512
ATTN_Q_ROWS = 256
ATTN_KEY_CHUNK = 256
MIX_ROWS = 512
MOE_ROWS = 512
SC_CORES = 2
SC_SUBCORES = 16
SC_LANES = 16
SC_CHUNK = 32
SC_COMBINE_CHUNK = 8
VMEM_LIMIT = 56 * 1024 * 1024

_bf16 = jnp.bfloat16
_f32 = jnp.float32


def _dot(a, b):
    return jnp.dot(a, b, preferred_element_type=_f32)


def _split_bf16(x):
    hi = x.astype(_bf16)
    lo = (x - hi.astype(_f32)).astype(_bf16)
    return hi, lo


def _sigmoid(x):
    return 1.0 / (1.0 + jnp.exp(-x))


def _in_proj_body(x_ref, g_ref, w_ref, wvt_ref, qg_ref, kg_ref, cos_ref, sin_ref, bd_ref,
                  q_ref, k_ref, vt_ref, u_ref, sa_ref, sb_ref):
    x = x_ref[...]
    ms = jnp.mean(x * x, axis=-1, keepdims=True)
    h = (x * lax.rsqrt(ms + NORM_EPS) * g_ref[...]).astype(_bf16)
    cos = cos_ref[...]
    sin = sin_ref[...]
    lane = lax.broadcasted_iota(jnp.int32, cos.shape, 1)
    first_half = (lane % (2 * ROPE_PAIRS_PER_AXIS)) < ROPE_PAIRS_PER_AXIS

    def head_norm_rope(z, gain):
        width = z.shape[1]
        ss = _dot((z * z).astype(_bf16), bd_ref[:width, :width])
        zn = z * lax.rsqrt(ss * (1.0 / HEAD_DIM) + NORM_EPS) * gain
        outs = []
        for j in range(width // LANES):
            c = zn[:, j * LANES:(j + 1) * LANES]
            partner = jnp.where(first_half,
                                pltpu.roll(c, LANES - ROPE_PAIRS_PER_AXIS, 1),
                                pltpu.roll(c, ROPE_PAIRS_PER_AXIS, 1))
            outs.append(c * cos + partner * sin)
        return outs

    c0 = 0
    zq = _dot(h, w_ref[:, c0:c0 + ATTN_WIDTH])
    for j, o in enumerate(head_norm_rope(zq, qg_ref[...])):
        q_ref[:, j * LANES:(j + 1) * LANES] = o.astype(_bf16)
    c0 += ATTN_WIDTH
    zk = _dot(h, w_ref[:, c0:c0 + KV_WIDTH])
    (kr,) = head_norm_rope(zk, kg_ref[...])
    for j in range(N_KV_HEADS):
        k_ref[j] = kr[:, j * HEAD_DIM:(j + 1) * HEAD_DIM].astype(_bf16)
    c0 += KV_WIDTH
    zvt = lax.dot_general(wvt_ref[...], h, (((1,), (1,)), ((), ())), preferred_element_type=_f32)
    for j in range(N_KV_HEADS):
        vt_ref[j] = zvt[j * HEAD_DIM:(j + 1) * HEAD_DIM, :].astype(_bf16)
    c0 += KV_WIDTH
    u_ref[...] = _dot(h, w_ref[:, c0:c0 + POOL_WIDTH]).astype(_bf16)
    c0 += POOL_WIDTH
    sa_ref[...] = _sigmoid(_dot(h, w_ref[:, c0:c0 + D_MODEL])).astype(_bf16)
    c0 += D_MODEL
    sb_ref[...] = _sigmoid(_dot(h, w_ref[:, c0:c0 + D_MODEL])).astype(_bf16)


def _in_proj(x2, norm_g, w_in, wvt, qg, kg, cos_t, sin_t, bd, seq):
    tokens = x2.shape[0]
    tm = IN_PROJ_ROWS
    in_width = w_in.shape[1]
    tiles_per_seq = seq // tm
    const = lambda i: (0, 0)
    row = lambda i: (i, 0)
    return pl.pallas_call(
        _in_proj_body,
        grid=(tokens // tm,),
        in_specs=[
            pl.BlockSpec((tm, D_MODEL), row),
            pl.BlockSpec((1, D_MODEL), const),
            pl.BlockSpec((D_MODEL, in_width), const),
            pl.BlockSpec((KV_WIDTH, D_MODEL), const),
            pl.BlockSpec((1, ATTN_WIDTH), const),
            pl.BlockSpec((1, KV_WIDTH), const),
            pl.BlockSpec((tm, LANES), lambda i: (i % tiles_per_seq, 0)),
            pl.BlockSpec((tm, LANES), lambda i: (i % tiles_per_seq, 0)),
            pl.BlockSpec((ATTN_WIDTH, ATTN_WIDTH), const),
        ],
        out_specs=[
            pl.BlockSpec((tm, ATTN_WIDTH), row),
            pl.BlockSpec((N_KV_HEADS, tm, HEAD_DIM), lambda i: (0, i, 0)),
            pl.BlockSpec((N_KV_HEADS, HEAD_DIM, tm), lambda i: (0, 0, i)),
            pl.BlockSpec((tm, POOL_WIDTH), row),
            pl.BlockSpec((tm, D_MODEL), row),
            pl.BlockSpec((tm, D_MODEL), row),
        ],
        out_shape=[
            jax.ShapeDtypeStruct((tokens, ATTN_WIDTH), _bf16),
            jax.ShapeDtypeStruct((N_KV_HEADS, tokens, HEAD_DIM), _bf16),
            jax.ShapeDtypeStruct((N_KV_HEADS, HEAD_DIM, tokens), _bf16),
            jax.ShapeDtypeStruct((tokens, POOL_WIDTH), _bf16),
            jax.ShapeDtypeStruct((tokens, D_MODEL), _bf16),
            jax.ShapeDtypeStruct((tokens, D_MODEL), _bf16),
        ],
        compiler_params=pltpu.CompilerParams(
            dimension_semantics=("parallel",), vmem_limit_bytes=VMEM_LIMIT),
        name="in_proj",
    )(x2, norm_g, w_in, wvt, qg, kg, cos_t, sin_t, bd)


def _attention_step(q_ref, k_ref, vt_ref, o_ref, st_new, m_new, st_old, m_old):
    tq = q_ref.shape[0]
    q = q_ref[...]
    qs = jnp.concatenate([q[:, g * HEAD_DIM:(g + 1) * HEAD_DIM] for g in range(N_GROUPS_PER_KV)], axis=0)
    seq = k_ref.shape[0]
    m_prev = m_old[...]
    m_run = denom = ot = None
    for c0 in range(0, seq, ATTN_KEY_CHUNK):
        rows = slice(c0, c0 + ATTN_KEY_CHUNK)
        st = lax.dot_general(k_ref[rows, :], qs, (((1,), (1,)), ((), ())), preferred_element_type=_f32)
        st_new[rows, :] = st
        m_c = jnp.max(st, axis=0, keepdims=True)
        m_run = m_c if m_run is None else jnp.maximum(m_run, m_c)
        p = jnp.exp2(st_old[rows, :] - m_prev)
        l_c = jnp.sum(p, axis=0, keepdims=True)
        o_c = _dot(vt_ref[:, rows], p.astype(_bf16))
        denom = l_c if denom is None else denom + l_c
        ot = o_c if ot is None else ot + o_c
    m_new[...] = m_run
    ot = ot * (1.0 / denom)
    stacked = jnp.concatenate([ot[:, g * tq:(g + 1) * tq] for g in range(N_GROUPS_PER_KV)], axis=0)
    o_ref[...] = stacked.T.astype(_bf16)


def _attention_body(q_ref, k_ref, vt_ref, o_ref, st_a, m_a, st_b, m_b):
    n = pl.program_id(0)

    @pl.when(n == 0)
    def _():
        st_b[...] = jnp.zeros_like(st_b)
        m_b[...] = jnp.zeros_like(m_b)

    @pl.when(n % 2 == 0)
    def _():
        _attention_step(q_ref, k_ref, vt_ref, o_ref, st_a, m_a, st_b, m_b)

    @pl.when(n % 2 == 1)
    def _():
        _attention_step(q_ref, k_ref, vt_ref, o_ref, st_b, m_b, st_a, m_a)


def _attention(q, k2, vt, batch, seq):
    tq = ATTN_Q_ROWS
    nq = seq // tq
    n_blocks = batch * N_KV_HEADS * nq
    group_width = N_GROUPS_PER_KV * HEAD_DIM
    lanes = N_GROUPS_PER_KV * tq

    def decode(n):
        return n // (N_KV_HEADS * nq), (n // nq) % N_KV_HEADS, n % nq

    def scores_of(n):
        return decode(jnp.minimum(n, n_blocks - 1))

    def finish_of(n):
        return decode(jnp.maximum(n - 1, 0))

    def q_map(n):
        b, kh, i = scores_of(n)
        return b * nq + i, kh

    def k_map(n):
        b, kh, _ = scores_of(n)
        return kh, b, 0

    def vt_map(n):
        b, kh, _ = finish_of(n)
        return kh, 0, b

    def o_map(n):
        b, kh, i = finish_of(n)
        return b * nq + i, kh

    return pl.pallas_call(
        _attention_body,
        grid=(n_blocks + 1,),
        in_specs=[
            pl.BlockSpec((tq, group_width), q_map),
            pl.BlockSpec((None, seq, HEAD_DIM), k_map),
            pl.BlockSpec((None, HEAD_DIM, seq), vt_map),
        ],
        out_specs=pl.BlockSpec((tq, group_width), o_map),
        out_shape=jax.ShapeDtypeStruct((batch * seq, ATTN_WIDTH), _bf16),
        scratch_shapes=[
            pltpu.VMEM((seq, lanes), _f32), pltpu.VMEM((1, lanes), _f32),
            pltpu.VMEM((seq, lanes), _f32), pltpu.VMEM((1, lanes), _f32),
        ],
        compiler_params=pltpu.CompilerParams(
            dimension_semantics=("arbitrary",), vmem_limit_bytes=VMEM_LIMIT),
        name="attention",
    )(q, k2, vt)


def _mix_out_body(seq, x_ref, a_ref, up_ref, u_ref, un_ref, sa_ref, sb_ref,
                  wap_ref, pgw_ref, ps_ref, wpp_ref, wo_ref, g2_ref, wr_ref, br_ref, tri_ref,
                  x1_ref, h2_ref, idx_ref, gate_ref, rank_ref, cnt_ref,
                  ext_ref, lv_ref, carry_ref):
    i = pl.program_id(0)
    tm = x_ref.shape[0]

    @pl.when(i == 0)
    def _():
        carry_ref[...] = jnp.zeros_like(carry_ref)
        ext_ref[...] = jnp.zeros_like(ext_ref)
        lv_ref[...] = jnp.zeros_like(lv_ref)

    start = (i * tm) % seq
    has_prev = (start != 0).astype(_f32)
    has_next = (start + tm != seq).astype(_f32)
    first = 2 * POOL_HALO
    ext_ref[POOL_HALO:first, :] = up_ref[...].astype(_f32) * has_prev
    ext_ref[first:first + tm, :] = u_ref[...].astype(_f32)
    ext_ref[first + tm:first + tm + POOL_HALO, :] = un_ref[...].astype(_f32) * has_next
    y_attn = _dot(a_ref[...], wap_ref[...])
    lo, n = POOL_HALO, tm + 2 * POOL_HALO
    pos = start + lax.broadcasted_iota(jnp.int32, (tm, 1), 0)
    pooled = []
    for g, w in enumerate(POOL_WINDOWS):
        half = w // 2
        assert half == 1 << g
        wide = slice(g * POOL_GROUP_DIM, POOL_WIDTH)
        cols = slice(g * POOL_GROUP_DIM, (g + 1) * POOL_GROUP_DIM)
        if g == 0:
            lv_ref[0, lo:lo + n, wide] = ext_ref[pl.ds(lo - 1, n), wide] + ext_ref[pl.ds(lo, n), wide]
        elif g < N_POOL_GROUPS - 1:
            lv_ref[g, lo:lo + n, wide] = (lv_ref[g - 1, pl.ds(lo - half // 2, n), wide]
                                          + lv_ref[g - 1, pl.ds(lo + half // 2, n), wide])
        if g < N_POOL_GROUPS - 1:
            tot = lv_ref[g, first:first + tm, cols]
        else:
            tot = (lv_ref[g - 1, pl.ds(first - half // 2, tm), cols]
                   + lv_ref[g - 1, pl.ds(first + half // 2, tm), cols])
        cnt = (jnp.minimum(pos + half, seq) - jnp.maximum(pos - half, 0)).astype(_f32)
        diff = tot / cnt - ext_ref[first:first + tm, cols]
        pooled.append(_dot(diff.astype(_bf16), pgw_ref[g]))
    pooled = (jnp.concatenate(pooled, axis=1) * ps_ref[...]).astype(_bf16)
    y_pool = _dot(pooled, wpp_ref[...])
    merged = sa_ref[...].astype(_f32) * y_attn + sb_ref[...].astype(_f32) * y_pool
    x1 = x_ref[...] + _dot(merged.astype(_bf16), wo_ref[...])
    x1_ref[...] = x1

    ms = jnp.mean(x1 * x1, axis=-1, keepdims=True)
    h2 = x1 * lax.rsqrt(ms + NORM_EPS) * g2_ref[...]
    h2_ref[...] = h2
    hi, lo = _split_bf16(h2)
    both = _dot(hi, wr_ref[...])
    logits = both[:, :LANES] + both[:, LANES:] + _dot(lo, wr_ref[:, :LANES])
    lt = logits.T[:N_EXPERTS, :] + br_ref[...]
    eid = lax.broadcasted_iota(jnp.int32, lt.shape, 0)
    vals, idxs = [], []
    multi_hot = jnp.zeros(lt.shape, _f32)
    for _ in range(TOP_K):
        m = jnp.max(lt, axis=0, keepdims=True)
        sel = jnp.min(jnp.where(lt == m, eid, N_EXPERTS), axis=0, keepdims=True)
        hit = eid == sel
        vals.append(m)
        idxs.append(sel)
        multi_hot = multi_hot + hit.astype(_f32)
        lt = jnp.where(hit, -jnp.inf, lt)
    es = [jnp.exp(v - vals[0]) for v in vals]
    inv = 1.0 / (es[0] + es[1] + es[2] + es[3])
    gate_ref[...] = jnp.concatenate([e * inv for e in es], axis=0)
    idx_ref[...] = jnp.concatenate(idxs, axis=0)
    before = _dot(multi_hot.astype(_bf16), tri_ref[...]) + carry_ref[:, 0:1]
    ranks = [jnp.sum(jnp.where(eid == sel, before, 0.0), axis=0, keepdims=True) for sel in idxs]
    rank_ref[...] = jnp.concatenate(ranks, axis=0).astype(jnp.int32)
    carry_ref[...] = carry_ref[...] + jnp.sum(multi_hot, axis=1, keepdims=True)
    cnt_ref[...] = carry_ref[...]


def _mix_out(x2, attn, u, sa, sb, wap, pgw, ps, wpp, wo, g2, wr, br, tri, seq):
    tokens = x2.shape[0]
    tm = MIX_ROWS
    n = tokens // tm
    halo_blocks = tm // POOL_HALO
    last_halo = tokens // POOL_HALO - 1
    row = lambda i: (i, 0)
    const2 = lambda i: (0, 0)
    const3 = lambda i: (0, 0, 0)
    col = lambda i: (0, i)
    return pl.pallas_call(
        functools.partial(_mix_out_body, seq),
        grid=(n,),
        in_specs=[
            pl.BlockSpec((tm, D_MODEL), row),
            pl.BlockSpec((tm, ATTN_WIDTH), row),
            pl.BlockSpec((POOL_HALO, POOL_WIDTH), lambda i: (jnp.maximum(i * halo_blocks - 1, 0), 0)),
            pl.BlockSpec((tm, POOL_WIDTH), row),
            pl.BlockSpec((POOL_HALO, POOL_WIDTH), lambda i: (jnp.minimum((i + 1) * halo_blocks, last_halo), 0)),
            pl.BlockSpec((tm, D_MODEL), row),
            pl.BlockSpec((tm, D_MODEL), row),
            pl.BlockSpec((ATTN_WIDTH, D_MODEL), const2),
            pl.BlockSpec((N_POOL_GROUPS, POOL_GROUP_DIM, POOL_GROUP_DIM), const3),
            pl.BlockSpec((1, POOL_WIDTH), const2),
            pl.BlockSpec((POOL_WIDTH, D_MODEL), const2),
            pl.BlockSpec((D_MODEL, D_MODEL), const2),
            pl.BlockSpec((1, D_MODEL), const2),
            pl.BlockSpec((D_MODEL, 2 * LANES), const2),
            pl.BlockSpec((N_EXPERTS, 1), const2),
            pl.BlockSpec((tm, tm), const2),
        ],
        out_specs=[
            pl.BlockSpec((tm, D_MODEL), row),
            pl.BlockSpec((tm, D_MODEL), row),
            pl.BlockSpec((TOP_K, tm), col),
            pl.BlockSpec((TOP_K, tm), col),
            pl.BlockSpec((TOP_K, tm), col),
            pl.BlockSpec((N_EXPERTS, LANES), const2),
        ],
        out_shape=[
            jax.ShapeDtypeStruct((tokens, D_MODEL), _f32),
            jax.ShapeDtypeStruct((tokens, D_MODEL), _f32),
            jax.ShapeDtypeStruct((TOP_K, tokens), jnp.int32),
            jax.ShapeDtypeStruct((TOP_K, tokens), _f32),
            jax.ShapeDtypeStruct((TOP_K, tokens), jnp.int32),
            jax.ShapeDtypeStruct((N_EXPERTS, LANES), _f32),
        ],
        scratch_shapes=[
            pltpu.VMEM((tm + 4 * POOL_HALO, POOL_WIDTH), _f32),
            pltpu.VMEM((N_POOL_GROUPS - 1, tm + 4 * POOL_HALO, POOL_WIDTH), _f32),
            pltpu.VMEM((N_EXPERTS, LANES), _f32),
        ],
        compiler_params=pltpu.CompilerParams(
            dimension_semantics=("arbitrary",), vmem_limit_bytes=VMEM_LIMIT),
        name="mix_out",
    )(x2, attn, u, u, u, sa, sb, wap, pgw, ps, wpp, wo, g2, wr, br, tri)


def _worker_ranges(group_tokens, per_worker, workers):
    first = [0]
    for t in group_tokens:
        assert t % per_worker == 0
        first.append(first[-1] + t // per_worker)
    assert first[-1] == workers
    return first


def _dispatch(dest4, pad3, hs, n_rows):
    workers, n_chunks = dest4.shape[0], dest4.shape[1]
    n_pad_chunks = pad3.shape[1]
    per_worker = n_chunks * SC_CHUNK
    first_worker = _worker_ranges([h.shape[0] for h in hs], per_worker, workers)
    assert n_chunks % 2 == 0
    mesh = plsc.VectorSubcoreMesh(core_axis_name="c", subcore_axis_name="s")

    @functools.partial(
        pl.kernel, mesh=mesh,
        out_type=jax.ShapeDtypeStruct((n_rows, D_MODEL), _f32),
        scratch_types=[
            pltpu.VMEM((n_chunks, TOP_K, SC_CHUNK), jnp.int32),
            pltpu.VMEM((n_pad_chunks, SC_CHUNK), jnp.int32),
            pltpu.VMEM((SC_CHUNK, D_MODEL), _f32),
            pltpu.VMEM((SC_CHUNK, D_MODEL), _f32),
            pltpu.SemaphoreType.DMA,
            pltpu.SemaphoreType.DMA,
            pltpu.SemaphoreType.DMA,
        ],
        name="dispatch",
    )
    def body(dest_hbm, pad_hbm, *refs):
        h_hbms = refs[:len(hs)]
        out_hbm, idx_v, pad_v, rows0, rows1, load_sem0, load_sem1, store_sem = refs[len(hs):]
        wid = lax.axis_index("s") * SC_CORES + lax.axis_index("c")
        pltpu.sync_copy(dest_hbm.at[wid], idx_v)
        pltpu.sync_copy(pad_hbm.at[wid], pad_v)

        zero = jnp.zeros((SC_LANES,), _f32)

        @pl.loop(0, SC_CHUNK)
        def _(r):
            @pl.loop(0, D_MODEL, step=SC_LANES)
            def _(c):
                rows0.at[r, pl.ds(c, SC_LANES)][...] = zero

        @pl.loop(0, n_pad_chunks)
        def _(c):
            pltpu.sync_copy(rows0, out_hbm.at[pad_v.at[c]])

        def scatter_tokens(src_hbm, base):
            bufs = (rows0, rows1)
            load_sems = (load_sem0, load_sem1)

            def load(c, b):
                return pltpu.make_async_copy(src_hbm.at[pl.ds(base + c * SC_CHUNK, SC_CHUNK)], bufs[b], load_sems[b])

            load(0, 0).start()

            @pl.loop(0, n_chunks, step=2)
            def _(c):
                for b in range(2):
                    cc = c + b
                    load(cc, b).wait()

                    @pl.when(cc + 1 < n_chunks)
                    def _():
                        load(cc + 1, 1 - b).start()

                    copies = [pltpu.make_async_copy(bufs[b], out_hbm.at[idx_v.at[cc, k]], store_sem)
                              for k in range(TOP_K)]
                    for cp in copies:
                        cp.start()
                    for cp in copies:
                        cp.wait()

        for g, h_hbm in enumerate(h_hbms):
            @pl.when(jnp.logical_and(wid >= first_worker[g], wid < first_worker[g + 1]))
            def _(g=g, h_hbm=h_hbm):
                scatter_tokens(h_hbm, (wid - first_worker[g]) * per_worker)

    return body(dest4, pad3, *hs)


def _experts_body(be_ref, nu_ref, xs_ref, wgu_ref, bgu_ref, wd_ref, bd_ref, y_ref):
    del be_ref
    used = pl.program_id(0) < nu_ref[0]

    @pl.when(jnp.logical_not(used))
    def _():
        y_ref[...] = jnp.zeros_like(y_ref)

    @pl.when(used)
    def _():
        x = xs_ref[...].astype(_bf16)
        gu = _dot(x, wgu_ref[...].astype(_bf16)) + bgu_ref[...]
        gate = jnp.minimum(gu[:, :D_FF], SWIGLU_LIMIT)
        up = jnp.clip(gu[:, D_FF:], -SWIGLU_LIMIT, SWIGLU_LIMIT)
        act = (up + 1.0) * (gate * _sigmoid(SWIGLU_ALPHA * gate))
        y_ref[...] = _dot(act.astype(_bf16), wd_ref[...].astype(_bf16)) + bd_ref[...]


def _experts(block_e, n_used, xs, wgu, bgu, wd, bd):
    n_blocks = block_e.shape[0]
    n_rows = n_blocks * MOE_ROWS
    rows = lambda i, be, nu: (jnp.minimum(i, nu[0] - 1), 0)
    per_e = lambda i, be, nu: (be[i], 0, 0)
    return pl.pallas_call(
        _experts_body,
        grid_spec=pltpu.PrefetchScalarGridSpec(
            num_scalar_prefetch=2,
            grid=(n_blocks,),
            in_specs=[
                pl.BlockSpec((MOE_ROWS, D_MODEL), rows),
                pl.BlockSpec((None, D_MODEL, 2 * D_FF), per_e),
                pl.BlockSpec((None, 1, 2 * D_FF), per_e),
                pl.BlockSpec((None, D_FF, D_MODEL), per_e),
                pl.BlockSpec((None, 1, D_MODEL), per_e),
            ],
            out_specs=pl.BlockSpec((MOE_ROWS, D_MODEL), lambda i, be, nu: (i, 0)),
        ),
        out_shape=jax.ShapeDtypeStruct((n_rows, D_MODEL), _f32),
        compiler_params=pltpu.CompilerParams(
            dimension_semantics=("arbitrary",), vmem_limit_bytes=VMEM_LIMIT),
        name="experts",
    )(block_e, n_used, xs, wgu, bgu, wd, bd)


def _combine(dest_rows, gates, x1s, yb):
    workers = dest_rows.shape[0]
    per_worker = dest_rows.shape[1] * LANES // TOP_K
    n_chunks = per_worker // SC_COMBINE_CHUNK
    first_worker = _worker_ranges([x.shape[0] for x in x1s], per_worker, workers)
    n_groups = len(x1s)
    assert n_chunks % 2 == 0 and LANES % SC_COMBINE_CHUNK == 0
    mesh = plsc.VectorSubcoreMesh(core_axis_name="c", subcore_axis_name="s")
    row_buf = pltpu.VMEM((SC_COMBINE_CHUNK, D_MODEL), _f32)

    @functools.partial(
        pl.kernel, mesh=mesh,
        out_type=[jax.ShapeDtypeStruct(x.shape, _f32) for x in x1s],
        scratch_types=[
            pltpu.VMEM(dest_rows.shape[1:], jnp.int32),
            [[row_buf] * (TOP_K + 1), [row_buf] * (TOP_K + 1)],
            [pltpu.VMEM((SC_COMBINE_CHUNK, TOP_K * SC_LANES), _f32)] * 2,
            [pltpu.SemaphoreType.DMA] * 2, [pltpu.SemaphoreType.DMA] * 2, [pltpu.SemaphoreType.DMA] * 2,
        ],
        name="combine",
    )
    def body(dest_hbm, *refs):
        g_hbms, x_hbms = refs[:n_groups], refs[n_groups:2 * n_groups]
        yb_hbm = refs[2 * n_groups]
        o_hbms = refs[2 * n_groups + 1:3 * n_groups + 1]
        idx_v, row_bufs, gate_bufs, gather_sems, load_sems, store_sems = refs[3 * n_groups + 1:]
        wid = lax.axis_index("s") * SC_CORES + lax.axis_index("c")
        pltpu.sync_copy(dest_hbm.at[wid], idx_v)

        def run(g_hbm, x_hbm, o_hbm, base):
            def loads(c, b):
                rows = pl.ds(base + c * SC_COMBINE_CHUNK, SC_COMBINE_CHUNK)
                cps = []
                for k in range(TOP_K):
                    pos = (c * TOP_K + k) * SC_COMBINE_CHUNK
                    ids = idx_v.at[pos // LANES, pl.ds(pos % LANES, SC_COMBINE_CHUNK)]
                    cps.append(pltpu.make_async_copy(yb_hbm.at[ids], row_bufs[b][k], gather_sems[b]))
                cps.append(pltpu.make_async_copy(x_hbm.at[rows], row_bufs[b][TOP_K], load_sems[b]))
                cps.append(pltpu.make_async_copy(g_hbm.at[rows], gate_bufs[b], load_sems[b]))
                return cps

            def store(c, b):
                rows = pl.ds(base + c * SC_COMBINE_CHUNK, SC_COMBINE_CHUNK)
                return pltpu.make_async_copy(row_bufs[b][TOP_K], o_hbm.at[rows], store_sems[b])

            for cp in loads(0, 0):
                cp.start()

            @pl.loop(0, n_chunks, step=2)
            def _(c):
                for b in range(2):
                    cc = c + b

                    @pl.when(cc + 1 < n_chunks)
                    def _():
                        @pl.when(cc >= 1)
                        def _():
                            store(cc - 1, 1 - b).wait()
                        for cp in loads(cc + 1, 1 - b):
                            cp.start()

                    for cp in loads(cc, b):
                        cp.wait()
                    acc_buf, gate_buf = row_bufs[b][TOP_K], gate_bufs[b]

                    @pl.loop(0, SC_COMBINE_CHUNK)
                    def _(r):
                        g = [gate_buf.at[r, pl.ds(k * SC_LANES, SC_LANES)][...] for k in range(TOP_K)]

                        @plsc.parallel_loop(0, D_MODEL, step=SC_LANES, unroll=4)
                        def _(col):
                            acc = acc_buf.at[r, pl.ds(col, SC_LANES)][...]
                            for k in range(TOP_K):
                                acc = acc + g[k] * row_bufs[b][k].at[r, pl.ds(col, SC_LANES)][...]
                            acc_buf.at[r, pl.ds(col, SC_LANES)][...] = acc

                    store(cc, b).start()

            store(n_chunks - 2, 0).wait()
            store(n_chunks - 1, 1).wait()

        for g in range(n_groups):
            @pl.when(jnp.logical_and(wid >= first_worker[g], wid < first_worker[g + 1]))
            def _(g=g):
                run(g_hbms[g], x_hbms[g], o_hbms[g], (wid - first_worker[g]) * per_worker)

    return body(dest_rows, *gates, *x1s, yb)


def _rope_tables(seq):
    pos = jnp.arange(seq)
    row_ids = (pos // GRID_W).astype(_f32)
    col_ids = (pos % GRID_W).astype(_f32)
    inv_freq = ROPE_THETA ** (-jnp.arange(ROPE_PAIRS_PER_AXIS, dtype=_f32) / ROPE_PAIRS_PER_AXIS)
    ang_r = row_ids[:, None] * inv_freq
    ang_c = col_ids[:, None] * inv_freq
    cos = jnp.concatenate([jnp.cos(ang_r)] * 2 + [jnp.cos(ang_c)] * 2, axis=1)
    sin = jnp.concatenate([-jnp.sin(ang_r), jnp.sin(ang_r), -jnp.sin(ang_c), jnp.sin(ang_c)], axis=1)
    reps = LANES // HEAD_DIM
    return jnp.tile(cos, (1, reps)), jnp.tile(sin, (1, reps))


def _mixer(x, p):
    batch, seq, _ = x.shape
    x2 = x.reshape(batch * seq, D_MODEL)
    cos_t, sin_t = _rope_tables(seq)
    q, k2, vt, u, sa, sb = _in_proj(x2, p["mix_g"], p["w_in"], p["wvt"], p["qg"], p["kg"], cos_t, sin_t, p["bd"], seq)
    attn = _attention(q, k2, vt, batch, seq)
    return _mix_out(x2, attn, u, sa, sb, p["wap"], p["pgw"], p["ps"], p["wpp"], p["wo"], p["ffn_g"],
                    p["wr"], p["br"], p["tri"], seq)


def _moe(mixed, w_gu, b_gu, w_down, b_down):
    counts = [m[5][:, 0].astype(jnp.int32) for m in mixed]
    total = sum(counts)
    padded = ((total + MOE_ROWS - 1) // MOE_ROWS) * MOE_ROWS
    padded_end = jnp.cumsum(padded)
    padded_start = padded_end - padded
    n_tok = sum(m[0].shape[0] for m in mixed)
    n_blocks = (n_tok * TOP_K + MOE_ROWS - 1) // MOE_ROWS + N_EXPERTS
    n_used = (padded_end[-1] // MOE_ROWS).astype(jnp.int32)
    blk = jnp.minimum(jnp.arange(n_blocks, dtype=jnp.int32), n_used - 1)
    block_e = jnp.sum((padded_end[None, :] <= (blk * MOE_ROWS)[:, None]).astype(jnp.int32), axis=1)
    block_e = jnp.minimum(block_e, N_EXPERTS - 1)
    expert_ids = jnp.arange(N_EXPERTS, dtype=jnp.int32)[:, None, None]
    dests = []
    seen = jnp.zeros((N_EXPERTS,), jnp.int32)
    for m, c in zip(mixed, counts):
        base = (padded_start + seen)[:, None, None]
        dests.append(jnp.sum(jnp.where(m[2][None] == expert_ids, base, 0), axis=0) + m[4])
        seen = seen + c
    workers = SC_CORES * SC_SUBCORES
    per_worker = n_tok // workers
    dest4 = jnp.concatenate(
        [d.reshape(TOP_K, -1, per_worker // SC_CHUNK, SC_CHUNK).transpose(1, 2, 0, 3) for d in dests], axis=0)
    n_rows = n_blocks * MOE_ROWS
    slot = jnp.arange(MOE_ROWS, dtype=jnp.int32)[None, :]
    pad_rows = jnp.where(slot < (padded - total)[:, None], (padded_start + total)[:, None] + slot,
                         n_rows + jnp.arange(N_EXPERTS, dtype=jnp.int32)[:, None] * MOE_ROWS + slot)
    pad3 = pad_rows.astype(jnp.int32).reshape(workers, -1, SC_CHUNK)
    xs_buf = _dispatch(dest4, pad3, [m[1] for m in mixed], n_rows + N_EXPERTS * MOE_ROWS)
    yb = _experts(block_e, n_used.reshape(1), xs_buf, w_gu, b_gu.reshape(N_EXPERTS, 1, -1),
                  w_down, b_down.reshape(N_EXPERTS, 1, -1))
    dest_rows = jnp.concatenate(
        [d.reshape(TOP_K, -1, per_worker // SC_COMBINE_CHUNK, SC_COMBINE_CHUNK).transpose(1, 2, 0, 3)
         .reshape(-1, per_worker * TOP_K // LANES, LANES) for d in dests], axis=0)
    gate_vecs = [jnp.repeat(m[3].T, SC_LANES, axis=1) for m in mixed]
    return _combine(dest_rows, gate_vecs, [m[0] for m in mixed], yb)


def kernel(x_prompt, x_sample, mix_norm_g, w_in, q_norm_g, k_norm_g, w_attn_proj, pool_group_w, pool_scale,
           w_pool_proj, w_out, ffn_norm_g, w_router, b_router, w_gu, b_gu, w_down, b_down):
    depth = w_in.shape[0]
    xs_all = [x_prompt, x_sample]
    head_id = jnp.arange(ATTN_WIDTH) // HEAD_DIM
    block_diag = (head_id[:, None] == head_id[None, :]).astype(_bf16)
    tri_id = jnp.arange(MIX_ROWS)
    tri = (tri_id[:, None] < tri_id[None, :]).astype(_bf16)
    for l in range(depth):
        wr = jnp.pad(w_router[l].astype(_f32), ((0, 0), (0, LANES - N_EXPERTS)))
        wrh = wr.astype(_bf16)
        p = dict(
            mix_g=mix_norm_g[l].reshape(1, D_MODEL),
            w_in=w_in[l].astype(_bf16),
            wvt=w_in[l][:, ATTN_WIDTH + KV_WIDTH:ATTN_WIDTH + 2 * KV_WIDTH].T.astype(_bf16),
            qg=jnp.tile(q_norm_g[l] * (HEAD_DIM ** -0.5 * LOG2_E), N_HEADS).reshape(1, ATTN_WIDTH),
            kg=jnp.tile(k_norm_g[l], N_KV_HEADS).reshape(1, KV_WIDTH),
            bd=block_diag,
            wap=w_attn_proj[l].astype(_bf16),
            pgw=pool_group_w[l].astype(_bf16),
            ps=pool_scale[l].reshape(1, POOL_WIDTH),
            wpp=w_pool_proj[l].astype(_bf16),
            wo=w_out[l].astype(_bf16),
            ffn_g=ffn_norm_g[l].reshape(1, D_MODEL),
            wr=jnp.concatenate([wrh, (wr - wrh.astype(_f32)).astype(_bf16)], axis=1),
            br=b_router[l].astype(_f32).reshape(N_EXPERTS, 1),
            tri=tri,
        )
        mixed = [_mixer(x, p) for x in xs_all]
        outs = [_moe([m], w_gu[l], b_gu[l], w_down[l], b_down[l])[0] for m in mixed]
        xs_all = [o.reshape(x.shape) for o, x in zip(outs, xs_all)]
    return tuple(xs_all)
```

```python
import functools

import jax
import jax.numpy as jnp
from jax import lax
from jax.experimental import pallas as pl
from jax.experimental.pallas import tpu as pltpu
from jax.experimental.pallas import tpu_sc as plsc

D_MODEL = 1024
HALF_MODEL = D_MODEL // 2
GRID_W = 64
N_HEADS = 8
N_KV_HEADS = 2
HEAD_DIM = 64
N_GROUPS_PER_KV = N_HEADS // N_KV_HEADS
ATTN_WIDTH = N_HEADS * HEAD_DIM
KV_WIDTH = N_KV_HEADS * HEAD_DIM
ROPE_THETA = 10000.0
ROPE_PAIRS_PER_AXIS = HEAD_DIM // 4
POOL_WINDOWS = (2, 4, 8, 16)
N_POOL_GROUPS = 4
POOL_WIDTH = 512
POOL_GROUP_DIM = POOL_WIDTH // N_POOL_GROUPS
N_EXPERTS = 32
TOP_K = 4
D_FF = 1024
SWIGLU_ALPHA = 1.702
SWIGLU_LIMIT = 7.0
NORM_EPS = 1e-6

LANES = 128
SUBLANES = 8
POOL_HALO = 8
LOG2_E = 1.4426950408889634

IN_PROJ_ROWS = 512
ATTN_Q_ROWS = 256
ATTN_KEY_CHUNK = 256
MIX_ROWS = 512
MOE_ROWS = 512
SC_CORES = 2
SC_SUBCORES = 16
SC_LANES = 16
SC_CHUNK = 32
SC_COMBINE_CHUNK = 16
VMEM_LIMIT = 56 * 1024 * 1024

_bf16 = jnp.bfloat16
_f32 = jnp.float32


def _dot(a, b):
    return jnp.dot(a, b, preferred_element_type=_f32)


def _split_bf16(x):
    hi = x.astype(_bf16)
    lo = (x - hi.astype(_f32)).astype(_bf16)
    return hi, lo


def _sigmoid(x):
    return 1.0 / (1.0 + jnp.exp(-x))


_HIGH_HALF = 0xFFFF0000


def _pack_bf16_pair(lo, hi):
    lo_bits = lax.bitcast_convert_type(lo.astype(_bf16).astype(_f32), jnp.uint32)
    hi_bits = lax.bitcast_convert_type(hi.astype(_bf16).astype(_f32), jnp.uint32)
    return lax.shift_right_logical(lo_bits, jnp.uint32(16)) | (hi_bits & jnp.uint32(_HIGH_HALF))


def _unpack_bf16_pair(words):
    lo = lax.bitcast_convert_type(lax.shift_left(words, jnp.uint32(16)), _f32)
    hi = lax.bitcast_convert_type(words & jnp.uint32(_HIGH_HALF), _f32)
    return lo, hi


def _in_proj_body(x_ref, g_ref, w_ref, wvt_ref, qg_ref, kg_ref, cos_ref, sin_ref, bd_ref,
                  q_ref, k_ref, vt_ref, u_ref, sa_ref, sb_ref):
    x = x_ref[...]
    ms = jnp.mean(x * x, axis=-1, keepdims=True)
    h = (x * lax.rsqrt(ms + NORM_EPS) * g_ref[...]).astype(_bf16)
    cos = cos_ref[...]
    sin = sin_ref[...]
    lane = lax.broadcasted_iota(jnp.int32, cos.shape, 1)
    first_half = (lane % (2 * ROPE_PAIRS_PER_AXIS)) < ROPE_PAIRS_PER_AXIS

    def head_norm_rope(z, gain):
        width = z.shape[1]
        ss = _dot((z * z).astype(_bf16), bd_ref[:width, :width])
        zn = z * lax.rsqrt(ss * (1.0 / HEAD_DIM) + NORM_EPS) * gain
        outs = []
        for j in range(width // LANES):
            c = zn[:, j * LANES:(j + 1) * LANES]
            partner = jnp.where(first_half,
                                pltpu.roll(c, LANES - ROPE_PAIRS_PER_AXIS, 1),
                                pltpu.roll(c, ROPE_PAIRS_PER_AXIS, 1))
            outs.append(c * cos + partner * sin)
        return outs

    c0 = 0
    zq = _dot(h, w_ref[:, c0:c0 + ATTN_WIDTH])
    for j, o in enumerate(head_norm_rope(zq, qg_ref[...])):
        q_ref[:, j * LANES:(j + 1) * LANES] = o.astype(_bf16)
    c0 += ATTN_WIDTH
    zk = _dot(h, w_ref[:, c0:c0 + KV_WIDTH])
    (kr,) = head_norm_rope(zk, kg_ref[...])
    for j in range(N_KV_HEADS):
        k_ref[j] = kr[:, j * HEAD_DIM:(j + 1) * HEAD_DIM].astype(_bf16)
    c0 += KV_WIDTH
    zvt = lax.dot_general(wvt_ref[...], h, (((1,), (1,)), ((), ())), preferred_element_type=_f32)
    for j in range(N_KV_HEADS):
        vt_ref[j] = zvt[j * HEAD_DIM:(j + 1) * HEAD_DIM, :].astype(_bf16)
    c0 += KV_WIDTH
    u_ref[...] = _dot(h, w_ref[:, c0:c0 + POOL_WIDTH]).astype(_bf16)
    c0 += POOL_WIDTH
    sa_ref[...] = _sigmoid(_dot(h, w_ref[:, c0:c0 + D_MODEL])).astype(_bf16)
    c0 += D_MODEL
    sb_ref[...] = _sigmoid(_dot(h, w_ref[:, c0:c0 + D_MODEL])).astype(_bf16)


def _in_proj(x2, norm_g, w_in, wvt, qg, kg, cos_t, sin_t, bd, seq):
    tokens = x2.shape[0]
    tm = IN_PROJ_ROWS
    in_width = w_in.shape[1]
    tiles_per_seq = seq // tm
    const = lambda i: (0, 0)
    row = lambda i: (i, 0)
    return pl.pallas_call(
        _in_proj_body,
        grid=(tokens // tm,),
        in_specs=[
            pl.BlockSpec((tm, D_MODEL), row),
            pl.BlockSpec((1, D_MODEL), const),
            pl.BlockSpec((D_MODEL, in_width), const),
            pl.BlockSpec((KV_WIDTH, D_MODEL), const),
            pl.BlockSpec((1, ATTN_WIDTH), const),
            pl.BlockSpec((1, KV_WIDTH), const),
            pl.BlockSpec((tm, LANES), lambda i: (i % tiles_per_seq, 0)),
            pl.BlockSpec((tm, LANES), lambda i: (i % tiles_per_seq, 0)),
            pl.BlockSpec((ATTN_WIDTH, ATTN_WIDTH), const),
        ],
        out_specs=[
            pl.BlockSpec((tm, ATTN_WIDTH), row),
            pl.BlockSpec((N_KV_HEADS, tm, HEAD_DIM), lambda i: (0, i, 0)),
            pl.BlockSpec((N_KV_HEADS, HEAD_DIM, tm), lambda i: (0, 0, i)),
            pl.BlockSpec((tm, POOL_WIDTH), row),
            pl.BlockSpec((tm, D_MODEL), row),
            pl.BlockSpec((tm, D_MODEL), row),
        ],
        out_shape=[
            jax.ShapeDtypeStruct((tokens, ATTN_WIDTH), _bf16),
            jax.ShapeDtypeStruct((N_KV_HEADS, tokens, HEAD_DIM), _bf16),
            jax.ShapeDtypeStruct((N_KV_HEADS, HEAD_DIM, tokens), _bf16),
            jax.ShapeDtypeStruct((tokens, POOL_WIDTH), _bf16),
            jax.ShapeDtypeStruct((tokens, D_MODEL), _bf16),
            jax.ShapeDtypeStruct((tokens, D_MODEL), _bf16),
        ],
        compiler_params=pltpu.CompilerParams(
            dimension_semantics=("parallel",), vmem_limit_bytes=VMEM_LIMIT),
        name="in_proj",
    )(x2, norm_g, w_in, wvt, qg, kg, cos_t, sin_t, bd)


def _attention_step(q_ref, k_ref, vt_ref, o_ref, st_new, m_new, st_old, m_old):
    tq = q_ref.shape[0]
    q = q_ref[...]
    qs = jnp.concatenate([q[:, g * HEAD_DIM:(g + 1) * HEAD_DIM] for g in range(N_GROUPS_PER_KV)], axis=0)
    seq = k_ref.shape[0]
    m_prev = m_old[...]
    m_run = denom = ot = None
    for c0 in range(0, seq, ATTN_KEY_CHUNK):
        rows = slice(c0, c0 + ATTN_KEY_CHUNK)
        st = lax.dot_general(k_ref[rows, :], qs, (((1,), (1,)), ((), ())), preferred_element_type=_f32)
        st_new[rows, :] = st
        m_c = jnp.max(st, axis=0, keepdims=True)
        m_run = m_c if m_run is None else jnp.maximum(m_run, m_c)
        p = jnp.exp2(st_old[rows, :] - m_prev)
        l_c = jnp.sum(p, axis=0, keepdims=True)
        o_c = _dot(vt_ref[:, rows], p.astype(_bf16))
        denom = l_c if denom is None else denom + l_c
        ot = o_c if ot is None else ot + o_c
    m_new[...] = m_run
    ot = ot * (1.0 / denom)
    stacked = jnp.concatenate([ot[:, g * tq:(g + 1) * tq] for g in range(N_GROUPS_PER_KV)], axis=0)
    o_ref[...] = stacked.T.astype(_bf16)


def _attention_body(q_ref, k_ref, vt_ref, o_ref, st_a, m_a, st_b, m_b):
    n = pl.program_id(0)

    @pl.when(n == 0)
    def _():
        st_b[...] = jnp.zeros_like(st_b)
        m_b[...] = jnp.zeros_like(m_b)

    @pl.when(n % 2 == 0)
    def _():
        _attention_step(q_ref, k_ref, vt_ref, o_ref, st_a, m_a, st_b, m_b)

    @pl.when(n % 2 == 1)
    def _():
        _attention_step(q_ref, k_ref, vt_ref, o_ref, st_b, m_b, st_a, m_a)


def _attention(q, k2, vt, batch, seq):
    tq = ATTN_Q_ROWS
    nq = seq // tq
    n_blocks = batch * N_KV_HEADS * nq
    group_width = N_GROUPS_PER_KV * HEAD_DIM
    lanes = N_GROUPS_PER_KV * tq

    def decode(n):
        return n // (N_KV_HEADS * nq), (n // nq) % N_KV_HEADS, n % nq

    def scores_of(n):
        return decode(jnp.minimum(n, n_blocks - 1))

    def finish_of(n):
        return decode(jnp.maximum(n - 1, 0))

    def q_map(n):
        b, kh, i = scores_of(n)
        return b * nq + i, kh

    def k_map(n):
        b, kh, _ = scores_of(n)
        return kh, b, 0

    def vt_map(n):
        b, kh, _ = finish_of(n)
        return kh, 0, b

    def o_map(n):
        b, kh, i = finish_of(n)
        return b * nq + i, kh

    return pl.pallas_call(
        _attention_body,
        grid=(n_blocks + 1,),
        in_specs=[
            pl.BlockSpec((tq, group_width), q_map),
            pl.BlockSpec((None, seq, HEAD_DIM), k_map),
            pl.BlockSpec((None, HEAD_DIM, seq), vt_map),
        ],
        out_specs=pl.BlockSpec((tq, group_width), o_map),
        out_shape=jax.ShapeDtypeStruct((batch * seq, ATTN_WIDTH), _bf16),
        scratch_shapes=[
            pltpu.VMEM((seq, lanes), _f32), pltpu.VMEM((1, lanes), _f32),
            pltpu.VMEM((seq, lanes), _f32), pltpu.VMEM((1, lanes), _f32),
        ],
        compiler_params=pltpu.CompilerParams(
            dimension_semantics=("arbitrary",), vmem_limit_bytes=VMEM_LIMIT),
        name="attention",
    )(q, k2, vt)


def _mix_out_body(seq, x_ref, a_ref, up_ref, u_ref, un_ref, sa_ref, sb_ref,
                  wap_ref, pgw_ref, ps_ref, wpp_ref, wo_ref, g2_ref, wr_ref, br_ref, tri_ref,
                  x1_ref, h2_ref, idx_ref, gate_ref, rank_ref, cnt_ref,
                  ext_ref, lv_ref, carry_ref):
    i = pl.program_id(0)
    tm = x_ref.shape[0]

    @pl.when(i == 0)
    def _():
        carry_ref[...] = jnp.zeros_like(carry_ref)
        ext_ref[...] = jnp.zeros_like(ext_ref)
        lv_ref[...] = jnp.zeros_like(lv_ref)

    start = (i * tm) % seq
    has_prev = (start != 0).astype(_f32)
    has_next = (start + tm != seq).astype(_f32)
    first = 2 * POOL_HALO
    ext_ref[POOL_HALO:first, :] = up_ref[...].astype(_f32) * has_prev
    ext_ref[first:first + tm, :] = u_ref[...].astype(_f32)
    ext_ref[first + tm:first + tm + POOL_HALO, :] = un_ref[...].astype(_f32) * has_next
    y_attn = _dot(a_ref[...], wap_ref[...])
    lo, n = POOL_HALO, tm + 2 * POOL_HALO
    pos = start + lax.broadcasted_iota(jnp.int32, (tm, 1), 0)
    pooled = []
    for g, w in enumerate(POOL_WINDOWS):
        half = w // 2
        assert half == 1 << g
        wide = slice(g * POOL_GROUP_DIM, POOL_WIDTH)
        cols = slice(g * POOL_GROUP_DIM, (g + 1) * POOL_GROUP_DIM)
        if g == 0:
            lv_ref[0, lo:lo + n, wide] = ext_ref[pl.ds(lo - 1, n), wide] + ext_ref[pl.ds(lo, n), wide]
        elif g < N_POOL_GROUPS - 1:
            lv_ref[g, lo:lo + n, wide] = (lv_ref[g - 1, pl.ds(lo - half // 2, n), wide]
                                          + lv_ref[g - 1, pl.ds(lo + half // 2, n), wide])
        if g < N_POOL_GROUPS - 1:
            tot = lv_ref[g, first:first + tm, cols]
        else:
            tot = (lv_ref[g - 1, pl.ds(first - half // 2, tm), cols]
                   + lv_ref[g - 1, pl.ds(first + half // 2, tm), cols])
        cnt = (jnp.minimum(pos + half, seq) - jnp.maximum(pos - half, 0)).astype(_f32)
        diff = tot / cnt - ext_ref[first:first + tm, cols]
        pooled.append(_dot(diff.astype(_bf16), pgw_ref[g]))
    pooled = (jnp.concatenate(pooled, axis=1) * ps_ref[...]).astype(_bf16)
    y_pool = _dot(pooled, wpp_ref[...])
    merged = sa_ref[...].astype(_f32) * y_attn + sb_ref[...].astype(_f32) * y_pool
    x1 = x_ref[...] + _dot(merged.astype(_bf16), wo_ref[...])
    x1_ref[...] = x1

    ms = jnp.mean(x1 * x1, axis=-1, keepdims=True)
    h2 = x1 * lax.rsqrt(ms + NORM_EPS) * g2_ref[...]
    h2_ref[...] = _pack_bf16_pair(h2[:, :HALF_MODEL], h2[:, HALF_MODEL:])
    hi, lo = _split_bf16(h2)
    both = _dot(hi, wr_ref[...])
    logits = both[:, :LANES] + both[:, LANES:] + _dot(lo, wr_ref[:, :LANES])
    lt = logits.T[:N_EXPERTS, :] + br_ref[...]
    eid = lax.broadcasted_iota(jnp.int32, lt.shape, 0)
    vals, idxs = [], []
    multi_hot = jnp.zeros(lt.shape, _f32)
    for _ in range(TOP_K):
        m = jnp.max(lt, axis=0, keepdims=True)
        sel = jnp.min(jnp.where(lt == m, eid, N_EXPERTS), axis=0, keepdims=True)
        hit = eid == sel
        vals.append(m)
        idxs.append(sel)
        multi_hot = multi_hot + hit.astype(_f32)
        lt = jnp.where(hit, -jnp.inf, lt)
    es = [jnp.exp(v - vals[0]) for v in vals]
    inv = 1.0 / (es[0] + es[1] + es[2] + es[3])
    gate_ref[...] = jnp.concatenate([e * inv for e in es], axis=0)
    idx_ref[...] = jnp.concatenate(idxs, axis=0)
    before = _dot(multi_hot.astype(_bf16), tri_ref[...]) + carry_ref[:, 0:1]
    ranks = [jnp.sum(jnp.where(eid == sel, before, 0.0), axis=0, keepdims=True) for sel in idxs]
    rank_ref[...] = jnp.concatenate(ranks, axis=0).astype(jnp.int32)
    carry_ref[...] = carry_ref[...] + jnp.sum(multi_hot, axis=1, keepdims=True)
    cnt_ref[...] = carry_ref[...]


def _mix_out(x2, attn, u, sa, sb, wap, pgw, ps, wpp, wo, g2, wr, br, tri, seq):
    tokens = x2.shape[0]
    tm = MIX_ROWS
    n = tokens // tm
    halo_blocks = tm // POOL_HALO
    last_halo = tokens // POOL_HALO - 1
    row = lambda i: (i, 0)
    const2 = lambda i: (0, 0)
    const3 = lambda i: (0, 0, 0)
    col = lambda i: (0, i)
    return pl.pallas_call(
        functools.partial(_mix_out_body, seq),
        grid=(n,),
        in_specs=[
            pl.BlockSpec((tm, D_MODEL), row),
            pl.BlockSpec((tm, ATTN_WIDTH), row),
            pl.BlockSpec((POOL_HALO, POOL_WIDTH), lambda i: (jnp.maximum(i * halo_blocks - 1, 0), 0)),
            pl.BlockSpec((tm, POOL_WIDTH), row),
            pl.BlockSpec((POOL_HALO, POOL_WIDTH), lambda i: (jnp.minimum((i + 1) * halo_blocks, last_halo), 0)),
            pl.BlockSpec((tm, D_MODEL), row),
            pl.BlockSpec((tm, D_MODEL), row),
            pl.BlockSpec((ATTN_WIDTH, D_MODEL), const2),
            pl.BlockSpec((N_POOL_GROUPS, POOL_GROUP_DIM, POOL_GROUP_DIM), const3),
            pl.BlockSpec((1, POOL_WIDTH), const2),
            pl.BlockSpec((POOL_WIDTH, D_MODEL), const2),
            pl.BlockSpec((D_MODEL, D_MODEL), const2),
            pl.BlockSpec((1, D_MODEL), const2),
            pl.BlockSpec((D_MODEL, 2 * LANES), const2),
            pl.BlockSpec((N_EXPERTS, 1), const2),
            pl.BlockSpec((tm, tm), const2),
        ],
        out_specs=[
            pl.BlockSpec((tm, D_MODEL), row),
            pl.BlockSpec((tm, HALF_MODEL), row),
            pl.BlockSpec((TOP_K, tm), col),
            pl.BlockSpec((TOP_K, tm), col),
            pl.BlockSpec((TOP_K, tm), col),
            pl.BlockSpec((N_EXPERTS, LANES), const2),
        ],
        out_shape=[
            jax.ShapeDtypeStruct((tokens, D_MODEL), _f32),
            jax.ShapeDtypeStruct((tokens, HALF_MODEL), jnp.uint32),
            jax.ShapeDtypeStruct((TOP_K, tokens), jnp.int32),
            jax.ShapeDtypeStruct((TOP_K, tokens), _f32),
            jax.ShapeDtypeStruct((TOP_K, tokens), jnp.int32),
            jax.ShapeDtypeStruct((N_EXPERTS, LANES), _f32),
        ],
        scratch_shapes=[
            pltpu.VMEM((tm + 4 * POOL_HALO, POOL_WIDTH), _f32),
            pltpu.VMEM((N_POOL_GROUPS - 1, tm + 4 * POOL_HALO, POOL_WIDTH), _f32),
            pltpu.VMEM((N_EXPERTS, LANES), _f32),
        ],
        compiler_params=pltpu.CompilerParams(
            dimension_semantics=("arbitrary",), vmem_limit_bytes=VMEM_LIMIT),
        name="mix_out",
    )(x2, attn, u, u, u, sa, sb, wap, pgw, ps, wpp, wo, g2, wr, br, tri)


def _worker_ranges(group_tokens, per_worker, workers):
    first = [0]
    for t in group_tokens:
        assert t % per_worker == 0
        first.append(first[-1] + t // per_worker)
    assert first[-1] == workers
    return first


def _dispatch(dest4, pad3, hs, n_rows):
    workers, n_chunks = dest4.shape[0], dest4.shape[1]
    n_pad_chunks = pad3.shape[1]
    per_worker = n_chunks * SC_CHUNK
    first_worker = _worker_ranges([h.shape[0] for h in hs], per_worker, workers)
    width, dtype = hs[0].shape[1], hs[0].dtype
    assert n_chunks % 2 == 0
    mesh = plsc.VectorSubcoreMesh(core_axis_name="c", subcore_axis_name="s")

    @functools.partial(
        pl.kernel, mesh=mesh,
        out_type=jax.ShapeDtypeStruct((n_rows, width), dtype),
        scratch_types=[
            pltpu.VMEM((n_chunks, TOP_K, SC_CHUNK), jnp.int32),
            pltpu.VMEM((n_pad_chunks, SC_CHUNK), jnp.int32),
            pltpu.VMEM((SC_CHUNK, width), dtype),
            pltpu.VMEM((SC_CHUNK, width), dtype),
            pltpu.SemaphoreType.DMA,
            pltpu.SemaphoreType.DMA,
            pltpu.SemaphoreType.DMA,
        ],
        name="dispatch",
    )
    def body(dest_hbm, pad_hbm, *refs):
        h_hbms = refs[:len(hs)]
        out_hbm, idx_v, pad_v, rows0, rows1, load_sem0, load_sem1, store_sem = refs[len(hs):]
        wid = lax.axis_index("s") * SC_CORES + lax.axis_index("c")
        pltpu.sync_copy(dest_hbm.at[wid], idx_v)
        pltpu.sync_copy(pad_hbm.at[wid], pad_v)

        zero = jnp.zeros((SC_LANES,), dtype)

        @pl.loop(0, SC_CHUNK)
        def _(r):
            @pl.loop(0, width, step=SC_LANES)
            def _(c):
                rows0.at[r, pl.ds(c, SC_LANES)][...] = zero

        @pl.loop(0, n_pad_chunks)
        def _(c):
            pltpu.sync_copy(rows0, out_hbm.at[pad_v.at[c]])

        def scatter_tokens(src_hbm, base):
            bufs = (rows0, rows1)
            load_sems = (load_sem0, load_sem1)

            def load(c, b):
                return pltpu.make_async_copy(src_hbm.at[pl.ds(base + c * SC_CHUNK, SC_CHUNK)], bufs[b], load_sems[b])

            load(0, 0).start()

            @pl.loop(0, n_chunks, step=2)
            def _(c):
                for b in range(2):
                    cc = c + b
                    load(cc, b).wait()

                    @pl.when(cc + 1 < n_chunks)
                    def _():
                        load(cc + 1, 1 - b).start()

                    copies = [pltpu.make_async_copy(bufs[b], out_hbm.at[idx_v.at[cc, k]], store_sem)
                              for k in range(TOP_K)]
                    for cp in copies:
                        cp.start()
                    for cp in copies:
                        cp.wait()

        for g, h_hbm in enumerate(h_hbms):
            @pl.when(jnp.logical_and(wid >= first_worker[g], wid < first_worker[g + 1]))
            def _(g=g, h_hbm=h_hbm):
                scatter_tokens(h_hbm, (wid - first_worker[g]) * per_worker)

    return body(dest4, pad3, *hs)


def _experts_body(be_ref, nu_ref, xs_ref, wgu_ref, bgu_ref, wd_ref, bd_ref, y_ref):
    del be_ref
    used = pl.program_id(0) < nu_ref[0]

    @pl.when(jnp.logical_not(used))
    def _():
        y_ref[...] = jnp.zeros_like(y_ref)

    @pl.when(used)
    def _():
        x = jnp.concatenate(_unpack_bf16_pair(xs_ref[...]), axis=1).astype(_bf16)
        gu = _dot(x, wgu_ref[...].astype(_bf16)) + bgu_ref[...]
        gate = jnp.minimum(gu[:, :D_FF], SWIGLU_LIMIT)
        up = jnp.clip(gu[:, D_FF:], -SWIGLU_LIMIT, SWIGLU_LIMIT)
        act = (up + 1.0) * (gate * _sigmoid(SWIGLU_ALPHA * gate))
        y = _dot(act.astype(_bf16), wd_ref[...].astype(_bf16)) + bd_ref[...]
        y_ref[...] = _pack_bf16_pair(y[:, :HALF_MODEL], y[:, HALF_MODEL:])


def _experts(block_e, n_used, xs, wgu, bgu, wd, bd):
    n_blocks = block_e.shape[0]
    n_rows = n_blocks * MOE_ROWS
    rows = lambda i, be, nu: (jnp.minimum(i, nu[0] - 1), 0)
    per_e = lambda i, be, nu: (be[i], 0, 0)
    return pl.pallas_call(
        _experts_body,
        grid_spec=pltpu.PrefetchScalarGridSpec(
            num_scalar_prefetch=2,
            grid=(n_blocks,),
            in_specs=[
                pl.BlockSpec((MOE_ROWS, HALF_MODEL), rows),
                pl.BlockSpec((None, D_MODEL, 2 * D_FF), per_e),
                pl.BlockSpec((None, 1, 2 * D_FF), per_e),
                pl.BlockSpec((None, D_FF, D_MODEL), per_e),
                pl.BlockSpec((None, 1, D_MODEL), per_e),
            ],
            out_specs=pl.BlockSpec((MOE_ROWS, HALF_MODEL), lambda i, be, nu: (i, 0)),
        ),
        out_shape=jax.ShapeDtypeStruct((n_rows, HALF_MODEL), jnp.uint32),
        compiler_params=pltpu.CompilerParams(
            dimension_semantics=("arbitrary",), vmem_limit_bytes=VMEM_LIMIT),
        name="experts",
    )(block_e, n_used, xs, wgu, bgu, wd, bd)


def _combine(dest_rows, gates, x1s, yb):
    workers = dest_rows.shape[0]
    per_worker = dest_rows.shape[1] * LANES // TOP_K
    n_chunks = per_worker // SC_COMBINE_CHUNK
    first_worker = _worker_ranges([x.shape[0] for x in x1s], per_worker, workers)
    n_groups = len(x1s)
    assert n_chunks % 2 == 0 and LANES % SC_COMBINE_CHUNK == 0
    mesh = plsc.VectorSubcoreMesh(core_axis_name="c", subcore_axis_name="s")
    row_buf = pltpu.VMEM((SC_COMBINE_CHUNK, D_MODEL), _f32)
    packed_buf = pltpu.VMEM((SC_COMBINE_CHUNK, HALF_MODEL), jnp.uint32)

    @functools.partial(
        pl.kernel, mesh=mesh,
        out_type=[jax.ShapeDtypeStruct(x.shape, _f32) for x in x1s],
        scratch_types=[
            pltpu.VMEM(dest_rows.shape[1:], jnp.int32),
            [[packed_buf] * TOP_K + [row_buf]] * 2,
            [pltpu.VMEM((SC_COMBINE_CHUNK, TOP_K * SC_LANES), _f32)] * 2,
            [pltpu.SemaphoreType.DMA] * 2, [pltpu.SemaphoreType.DMA] * 2, [pltpu.SemaphoreType.DMA] * 2,
        ],
        compiler_params=pltpu.CompilerParams(needs_layout_passes=False),
        name="combine",
    )
    def body(dest_hbm, *refs):
        g_hbms, x_hbms = refs[:n_groups], refs[n_groups:2 * n_groups]
        yb_hbm = refs[2 * n_groups]
        o_hbms = refs[2 * n_groups + 1:3 * n_groups + 1]
        idx_v, row_bufs, gate_bufs, gather_sems, load_sems, store_sems = refs[3 * n_groups + 1:]
        wid = lax.axis_index("s") * SC_CORES + lax.axis_index("c")
        pltpu.sync_copy(dest_hbm.at[wid], idx_v)

        def run(g_hbm, x_hbm, o_hbm, base):
            def loads(c, b):
                rows = pl.ds(base + c * SC_COMBINE_CHUNK, SC_COMBINE_CHUNK)
                cps = []
                for k in range(TOP_K):
                    pos = (c * TOP_K + k) * SC_COMBINE_CHUNK
                    ids = idx_v.at[pos // LANES, pl.ds(pos % LANES, SC_COMBINE_CHUNK)]
                    cps.append(pltpu.make_async_copy(yb_hbm.at[ids], row_bufs[b][k], gather_sems[b]))
                cps.append(pltpu.make_async_copy(x_hbm.at[rows], row_bufs[b][TOP_K], load_sems[b]))
                cps.append(pltpu.make_async_copy(g_hbm.at[rows], gate_bufs[b], load_sems[b]))
                return cps

            def store(c, b):
                rows = pl.ds(base + c * SC_COMBINE_CHUNK, SC_COMBINE_CHUNK)
                return pltpu.make_async_copy(row_bufs[b][TOP_K], o_hbm.at[rows], store_sems[b])

            for cp in loads(0, 0):
                cp.start()

            @pl.loop(0, n_chunks, step=2)
            def _(c):
                for b in range(2):
                    cc = c + b

                    @pl.when(cc + 1 < n_chunks)
                    def _():
                        @pl.when(cc >= 1)
                        def _():
                            store(cc - 1, 1 - b).wait()
                        for cp in loads(cc + 1, 1 - b):
                            cp.start()

                    for cp in loads(cc, b):
                        cp.wait()
                    acc_buf, gate_buf = row_bufs[b][TOP_K], gate_bufs[b]

                    @pl.loop(0, SC_COMBINE_CHUNK)
                    def _(r):
                        g = [gate_buf.at[r, pl.ds(k * SC_LANES, SC_LANES)][...] for k in range(TOP_K)]

                        @plsc.parallel_loop(0, HALF_MODEL, step=SC_LANES, unroll=4)
                        def _(col):
                            lo = acc_buf.at[r, pl.ds(col, SC_LANES)][...]
                            hi = acc_buf.at[r, pl.ds(HALF_MODEL + col, SC_LANES)][...]
                            for k in range(TOP_K):
                                words = row_bufs[b][k].at[r, pl.ds(col, SC_LANES)][...]
                                lo = lo + g[k] * plsc.bitcast(lax.shift_left(words, jnp.uint32(16)), _f32)
                                hi = hi + g[k] * plsc.bitcast(words & jnp.uint32(_HIGH_HALF), _f32)
                            acc_buf.at[r, pl.ds(col, SC_LANES)][...] = lo
                            acc_buf.at[r, pl.ds(HALF_MODEL + col, SC_LANES)][...] = hi

                    store(cc, b).start()

            store(n_chunks - 2, 0).wait()
            store(n_chunks - 1, 1).wait()

        for g in range(n_groups):
            @pl.when(jnp.logical_and(wid >= first_worker[g], wid < first_worker[g + 1]))
            def _(g=g):
                run(g_hbms[g], x_hbms[g], o_hbms[g], (wid - first_worker[g]) * per_worker)

    return body(dest_rows, *gates, *x1s, yb)


def _rope_tables(seq):
    pos = jnp.arange(seq)
    row_ids = (pos // GRID_W).astype(_f32)
    col_ids = (pos % GRID_W).astype(_f32)
    inv_freq = ROPE_THETA ** (-jnp.arange(ROPE_PAIRS_PER_AXIS, dtype=_f32) / ROPE_PAIRS_PER_AXIS)
    ang_r = row_ids[:, None] * inv_freq
    ang_c = col_ids[:, None] * inv_freq
    cos = jnp.concatenate([jnp.cos(ang_r)] * 2 + [jnp.cos(ang_c)] * 2, axis=1)
    sin = jnp.concatenate([-jnp.sin(ang_r), jnp.sin(ang_r), -jnp.sin(ang_c), jnp.sin(ang_c)], axis=1)
    reps = LANES // HEAD_DIM
    return jnp.tile(cos, (1, reps)), jnp.tile(sin, (1, reps))


def _mixer(x, p):
    batch, seq, _ = x.shape
    x2 = x.reshape(batch * seq, D_MODEL)
    cos_t, sin_t = _rope_tables(seq)
    q, k2, vt, u, sa, sb = _in_proj(x2, p["mix_g"], p["w_in"], p["wvt"], p["qg"], p["kg"], cos_t, sin_t, p["bd"], seq)
    attn = _attention(q, k2, vt, batch, seq)
    return _mix_out(x2, attn, u, sa, sb, p["wap"], p["pgw"], p["ps"], p["wpp"], p["wo"], p["ffn_g"],
                    p["wr"], p["br"], p["tri"], seq)


def _moe(mixed, w_gu, b_gu, w_down, b_down):
    counts = [m[5][:, 0].astype(jnp.int32) for m in mixed]
    total = sum(counts)
    padded = ((total + MOE_ROWS - 1) // MOE_ROWS) * MOE_ROWS
    padded_end = jnp.cumsum(padded)
    padded_start = padded_end - padded
    n_tok = sum(m[0].shape[0] for m in mixed)
    n_blocks = (n_tok * TOP_K + MOE_ROWS - 1) // MOE_ROWS + N_EXPERTS
    n_used = (padded_end[-1] // MOE_ROWS).astype(jnp.int32)
    blk = jnp.minimum(jnp.arange(n_blocks, dtype=jnp.int32), n_used - 1)
    block_e = jnp.sum((padded_end[None, :] <= (blk * MOE_ROWS)[:, None]).astype(jnp.int32), axis=1)
    block_e = jnp.minimum(block_e, N_EXPERTS - 1)
    expert_ids = jnp.arange(N_EXPERTS, dtype=jnp.int32)[:, None, None]
    dests = []
    seen = jnp.zeros((N_EXPERTS,), jnp.int32)
    for m, c in zip(mixed, counts):
        base = (padded_start + seen)[:, None, None]
        dests.append(jnp.sum(jnp.where(m[2][None] == expert_ids, base, 0), axis=0) + m[4])
        seen = seen + c
    workers = SC_CORES * SC_SUBCORES
    per_worker = n_tok // workers
    dest4 = jnp.concatenate(
        [d.reshape(TOP_K, -1, per_worker // SC_CHUNK, SC_CHUNK).transpose(1, 2, 0, 3) for d in dests], axis=0)
    n_rows = n_blocks * MOE_ROWS
    slot = jnp.arange(MOE_ROWS, dtype=jnp.int32)[None, :]
    pad_rows = jnp.where(slot < (padded - total)[:, None], (padded_start + total)[:, None] + slot,
                         n_rows + jnp.arange(N_EXPERTS, dtype=jnp.int32)[:, None] * MOE_ROWS + slot)
    pad3 = pad_rows.astype(jnp.int32).reshape(workers, -1, SC_CHUNK)
    xs_buf = _dispatch(dest4, pad3, [m[1] for m in mixed], n_rows + N_EXPERTS * MOE_ROWS)
    yb = _experts(block_e, n_used.reshape(1), xs_buf, w_gu, b_gu.reshape(N_EXPERTS, 1, -1),
                  w_down, b_down.reshape(N_EXPERTS, 1, -1))
    dest_rows = jnp.concatenate(
        [d.reshape(TOP_K, -1, per_worker // SC_COMBINE_CHUNK, SC_COMBINE_CHUNK).transpose(1, 2, 0, 3)
         .reshape(-1, per_worker * TOP_K // LANES, LANES) for d in dests], axis=0)
    gate_vecs = [jnp.repeat(m[3].T, SC_LANES, axis=1) for m in mixed]
    return _combine(dest_rows, gate_vecs, [m[0] for m in mixed], yb)


def kernel(x_prompt, x_sample, mix_norm_g, w_in, q_norm_g, k_norm_g, w_attn_proj, pool_group_w, pool_scale,
           w_pool_proj, w_out, ffn_norm_g, w_router, b_router, w_gu, b_gu, w_down, b_down):
    depth = w_in.shape[0]
    xs_all = [x_prompt, x_sample]
    head_id = jnp.arange(ATTN_WIDTH) // HEAD_DIM
    block_diag = (head_id[:, None] == head_id[None, :]).astype(_bf16)
    tri_id = jnp.arange(MIX_ROWS)
    tri = (tri_id[:, None] < tri_id[None, :]).astype(_bf16)
    for l in range(depth):
        wr = jnp.pad(w_router[l].astype(_f32), ((0, 0), (0, LANES - N_EXPERTS)))
        wrh = wr.astype(_bf16)
        p = dict(
            mix_g=mix_norm_g[l].reshape(1, D_MODEL),
            w_in=w_in[l].astype(_bf16),
            wvt=w_in[l][:, ATTN_WIDTH + KV_WIDTH:ATTN_WIDTH + 2 * KV_WIDTH].T.astype(_bf16),
            qg=jnp.tile(q_norm_g[l] * (HEAD_DIM ** -0.5 * LOG2_E), N_HEADS).reshape(1, ATTN_WIDTH),
            kg=jnp.tile(k_norm_g[l], N_KV_HEADS).reshape(1, KV_WIDTH),
            bd=block_diag,
            wap=w_attn_proj[l].astype(_bf16),
            pgw=pool_group_w[l].astype(_bf16),
            ps=pool_scale[l].reshape(1, POOL_WIDTH),
            wpp=w_pool_proj[l].astype(_bf16),
            wo=w_out[l].astype(_bf16),
            ffn_g=ffn_norm_g[l].reshape(1, D_MODEL),
            wr=jnp.concatenate([wrh, (wr - wrh.astype(_f32)).astype(_bf16)], axis=1),
            br=b_router[l].astype(_f32).reshape(N_EXPERTS, 1),
            tri=tri,
        )
        mixed = [_mixer(x, p) for x in xs_all]
        outs = [_moe([m], w_gu[l], b_gu[l], w_down[l], b_down[l])[0] for m in mixed]
        xs_all = [o.reshape(x.shape) for o, x in zip(outs, xs_all)]
    return tuple(xs_all)
```

```python
import functools

import jax
import jax.numpy as jnp
from jax import lax
from jax.experimental import pallas as pl
from jax.experimental.pallas import tpu as pltpu
from jax.experimental.pallas import tpu_sc as plsc

D_MODEL = 1024
HALF_MODEL = D_MODEL // 2
GRID_W = 64
N_HEADS = 8
N_KV_HEADS = 2
HEAD_DIM = 64
N_GROUPS_PER_KV = N_HEADS // N_KV_HEADS
ATTN_WIDTH = N_HEADS * HEAD_DIM
KV_WIDTH = N_KV_HEADS * HEAD_DIM
ROPE_THETA = 10000.0
ROPE_PAIRS_PER_AXIS = HEAD_DIM // 4
POOL_WINDOWS = (2, 4, 8, 16)
N_POOL_GROUPS = 4
POOL_WIDTH = 512
POOL_GROUP_DIM = POOL_WIDTH // N_POOL_GROUPS
N_EXPERTS = 32
TOP_K = 4
D_FF = 1024
SWIGLU_ALPHA = 1.702
SWIGLU_LIMIT = 7.0
NORM_EPS = 1e-6

LANES = 128
SUBLANES = 8
POOL_HALO = 8
LOG2_E = 1.4426950408889634

IN_PROJ_ROWS = 512
ATTN_Q_ROWS = 256
ATTN_KEY_CHUNK = 256
MIX_ROWS = 512
MOE_ROWS = 512
SC_CORES = 2
SC_SUBCORES = 16
SC_LANES = 16
SC_CHUNK = 32
SC_COMBINE_CHUNK = 16
VMEM_LIMIT = 56 * 1024 * 1024

_bf16 = jnp.bfloat16
_f32 = jnp.float32


def _dot(a, b):
    return jnp.dot(a, b, preferred_element_type=_f32)


def _split_bf16(x):
    hi = x.astype(_bf16)
    lo = (x - hi.astype(_f32)).astype(_bf16)
    return hi, lo


def _sigmoid(x):
    return 1.0 / (1.0 + jnp.exp(-x))


_HIGH_HALF = 0xFFFF0000


def _pack_bf16_pair(lo, hi):
    lo_bits = lax.bitcast_convert_type(lo.astype(_bf16).astype(_f32), jnp.uint32)
    hi_bits = lax.bitcast_convert_type(hi.astype(_bf16).astype(_f32), jnp.uint32)
    return lax.shift_right_logical(lo_bits, jnp.uint32(16)) | (hi_bits & jnp.uint32(_HIGH_HALF))


def _unpack_bf16_pair(words):
    lo = lax.bitcast_convert_type(lax.shift_left(words, jnp.uint32(16)), _f32)
    hi = lax.bitcast_convert_type(words & jnp.uint32(_HIGH_HALF), _f32)
    return lo, hi


def _in_proj_body(x_ref, g_ref, w_ref, wvt_ref, qg_ref, kg_ref, cos_ref, sin_ref, bd_ref,
                  q_ref, k_ref, vt_ref, u_ref, sa_ref, sb_ref):
    x = x_ref[...]
    ms = jnp.mean(x * x, axis=-1, keepdims=True)
    h = (x * lax.rsqrt(ms + NORM_EPS) * g_ref[...]).astype(_bf16)
    cos = cos_ref[...]
    sin = sin_ref[...]
    lane = lax.broadcasted_iota(jnp.int32, cos.shape, 1)
    first_half = (lane % (2 * ROPE_PAIRS_PER_AXIS)) < ROPE_PAIRS_PER_AXIS

    def head_norm_rope(z, gain):
        width = z.shape[1]
        ss = _dot((z * z).astype(_bf16), bd_ref[:width, :width])
        zn = z * lax.rsqrt(ss * (1.0 / HEAD_DIM) + NORM_EPS) * gain
        outs = []
        for j in range(width // LANES):
            c = zn[:, j * LANES:(j + 1) * LANES]
            partner = jnp.where(first_half,
                                pltpu.roll(c, LANES - ROPE_PAIRS_PER_AXIS, 1),
                                pltpu.roll(c, ROPE_PAIRS_PER_AXIS, 1))
            outs.append(c * cos + partner * sin)
        return outs

    c0 = 0
    zq = _dot(h, w_ref[:, c0:c0 + ATTN_WIDTH])
    for j, o in enumerate(head_norm_rope(zq, qg_ref[...])):
        q_ref[:, j * LANES:(j + 1) * LANES] = o.astype(_bf16)
    c0 += ATTN_WIDTH
    zk = _dot(h, w_ref[:, c0:c0 + KV_WIDTH])
    (kr,) = head_norm_rope(zk, kg_ref[...])
    for j in range(N_KV_HEADS):
        k_ref[j] = kr[:, j * HEAD_DIM:(j + 1) * HEAD_DIM].astype(_bf16)
    c0 += KV_WIDTH
    zvt = lax.dot_general(wvt_ref[...], h, (((1,), (1,)), ((), ())), preferred_element_type=_f32)
    for j in range(N_KV_HEADS):
        vt_ref[j] = zvt[j * HEAD_DIM:(j + 1) * HEAD_DIM, :].astype(_bf16)
    c0 += KV_WIDTH
    u_ref[...] = _dot(h, w_ref[:, c0:c0 + POOL_WIDTH]).astype(_bf16)
    c0 += POOL_WIDTH
    sa_ref[...] = _sigmoid(_dot(h, w_ref[:, c0:c0 + D_MODEL])).astype(_bf16)
    c0 += D_MODEL
    sb_ref[...] = _sigmoid(_dot(h, w_ref[:, c0:c0 + D_MODEL])).astype(_bf16)


def _in_proj(x2, norm_g, w_in, wvt, qg, kg, cos_t, sin_t, bd, seq):
    tokens = x2.shape[0]
    tm = IN_PROJ_ROWS
    in_width = w_in.shape[1]
    tiles_per_seq = seq // tm
    const = lambda i: (0, 0)
    row = lambda i: (i, 0)
    return pl.pallas_call(
        _in_proj_body,
        grid=(tokens // tm,),
        in_specs=[
            pl.BlockSpec((tm, D_MODEL), row),
            pl.BlockSpec((1, D_MODEL), const),
            pl.BlockSpec((D_MODEL, in_width), const),
            pl.BlockSpec((KV_WIDTH, D_MODEL), const),
            pl.BlockSpec((1, ATTN_WIDTH), const),
            pl.BlockSpec((1, KV_WIDTH), const),
            pl.BlockSpec((tm, LANES), lambda i: (i % tiles_per_seq, 0)),
            pl.BlockSpec((tm, LANES), lambda i: (i % tiles_per_seq, 0)),
            pl.BlockSpec((ATTN_WIDTH, ATTN_WIDTH), const),
        ],
        out_specs=[
            pl.BlockSpec((tm, ATTN_WIDTH), row),
            pl.BlockSpec((N_KV_HEADS, tm, HEAD_DIM), lambda i: (0, i, 0)),
            pl.BlockSpec((N_KV_HEADS, HEAD_DIM, tm), lambda i: (0, 0, i)),
            pl.BlockSpec((tm, POOL_WIDTH), row),
            pl.BlockSpec((tm, D_MODEL), row),
            pl.BlockSpec((tm, D_MODEL), row),
        ],
        out_shape=[
            jax.ShapeDtypeStruct((tokens, ATTN_WIDTH), _bf16),
            jax.ShapeDtypeStruct((N_KV_HEADS, tokens, HEAD_DIM), _bf16),
            jax.ShapeDtypeStruct((N_KV_HEADS, HEAD_DIM, tokens), _bf16),
            jax.ShapeDtypeStruct((tokens, POOL_WIDTH), _bf16),
            jax.ShapeDtypeStruct((tokens, D_MODEL), _bf16),
            jax.ShapeDtypeStruct((tokens, D_MODEL), _bf16),
        ],
        compiler_params=pltpu.CompilerParams(
            dimension_semantics=("parallel",), vmem_limit_bytes=VMEM_LIMIT),
        name="in_proj",
    )(x2, norm_g, w_in, wvt, qg, kg, cos_t, sin_t, bd)


def _attention_step(q_ref, k_ref, vt_ref, o_ref, st_new, m_new, st_old, m_old):
    tq = q_ref.shape[0]
    q = q_ref[...]
    qs = jnp.concatenate([q[:, g * HEAD_DIM:(g + 1) * HEAD_DIM] for g in range(N_GROUPS_PER_KV)], axis=0)
    seq = k_ref.shape[0]
    m_prev = m_old[...]
    m_run = denom = ot = None
    for c0 in range(0, seq, ATTN_KEY_CHUNK):
        rows = slice(c0, c0 + ATTN_KEY_CHUNK)
        st = lax.dot_general(k_ref[rows, :], qs, (((1,), (1,)), ((), ())), preferred_element_type=_f32)
        st_new[rows, :] = st
        m_c = jnp.max(st, axis=0, keepdims=True)
        m_run = m_c if m_run is None else jnp.maximum(m_run, m_c)
        p = jnp.exp2(st_old[rows, :] - m_prev)
        l_c = jnp.sum(p, axis=0, keepdims=True)
        o_c = _dot(vt_ref[:, rows], p.astype(_bf16))
        denom = l_c if denom is None else denom + l_c
        ot = o_c if ot is None else ot + o_c
    m_new[...] = m_run
    ot = ot * (1.0 / denom)
    stacked = jnp.concatenate([ot[:, g * tq:(g + 1) * tq] for g in range(N_GROUPS_PER_KV)], axis=0)
    o_ref[...] = stacked.T.astype(_bf16)


def _attention_body(q_ref, k_ref, vt_ref, o_ref, st_a, m_a, st_b, m_b):
    n = pl.program_id(0)

    @pl.when(n == 0)
    def _():
        st_b[...] = jnp.zeros_like(st_b)
        m_b[...] = jnp.zeros_like(m_b)

    @pl.when(n % 2 == 0)
    def _():
        _attention_step(q_ref, k_ref, vt_ref, o_ref, st_a, m_a, st_b, m_b)

    @pl.when(n % 2 == 1)
    def _():
        _attention_step(q_ref, k_ref, vt_ref, o_ref, st_b, m_b, st_a, m_a)


def _attention(q, k2, vt, batch, seq):
    tq = ATTN_Q_ROWS
    nq = seq // tq
    n_blocks = batch * N_KV_HEADS * nq
    group_width = N_GROUPS_PER_KV * HEAD_DIM
    lanes = N_GROUPS_PER_KV * tq

    def decode(n):
        return n // (N_KV_HEADS * nq), (n // nq) % N_KV_HEADS, n % nq

    def scores_of(n):
        return decode(jnp.minimum(n, n_blocks - 1))

    def finish_of(n):
        return decode(jnp.maximum(n - 1, 0))

    def q_map(n):
        b, kh, i = scores_of(n)
        return b * nq + i, kh

    def k_map(n):
        b, kh, _ = scores_of(n)
        return kh, b, 0

    def vt_map(n):
        b, kh, _ = finish_of(n)
        return kh, 0, b

    def o_map(n):
        b, kh, i = finish_of(n)
        return b * nq + i, kh

    return pl.pallas_call(
        _attention_body,
        grid=(n_blocks + 1,),
        in_specs=[
            pl.BlockSpec((tq, group_width), q_map),
            pl.BlockSpec((None, seq, HEAD_DIM), k_map),
            pl.BlockSpec((None, HEAD_DIM, seq), vt_map),
        ],
        out_specs=pl.BlockSpec((tq, group_width), o_map),
        out_shape=jax.ShapeDtypeStruct((batch * seq, ATTN_WIDTH), _bf16),
        scratch_shapes=[
            pltpu.VMEM((seq, lanes), _f32), pltpu.VMEM((1, lanes), _f32),
            pltpu.VMEM((seq, lanes), _f32), pltpu.VMEM((1, lanes), _f32),
        ],
        compiler_params=pltpu.CompilerParams(
            dimension_semantics=("arbitrary",), vmem_limit_bytes=VMEM_LIMIT),
        name="attention",
    )(q, k2, vt)


def _mix_out_body(seq, x_ref, a_ref, up_ref, u_ref, un_ref, sa_ref, sb_ref,
                  wap_ref, pgw_ref, ps_ref, wpp_ref, wo_ref, g2_ref, wr_ref, br_ref, tri_ref,
                  x1_ref, h2_ref, idx_ref, gate_ref, rank_ref, cnt_ref,
                  ext_ref, lv_ref, carry_ref):
    i = pl.program_id(0)
    tm = x_ref.shape[0]

    @pl.when(i == 0)
    def _():
        carry_ref[...] = jnp.zeros_like(carry_ref)
        ext_ref[...] = jnp.zeros_like(ext_ref)
        lv_ref[...] = jnp.zeros_like(lv_ref)

    start = (i * tm) % seq
    has_prev = (start != 0).astype(_f32)
    has_next = (start + tm != seq).astype(_f32)
    first = 2 * POOL_HALO
    ext_ref[POOL_HALO:first, :] = up_ref[...].astype(_f32) * has_prev
    ext_ref[first:first + tm, :] = u_ref[...].astype(_f32)
    ext_ref[first + tm:first + tm + POOL_HALO, :] = un_ref[...].astype(_f32) * has_next
    y_attn = _dot(a_ref[...], wap_ref[...])
    lo, n = POOL_HALO, tm + 2 * POOL_HALO
    pos = start + lax.broadcasted_iota(jnp.int32, (tm, 1), 0)
    pooled = []
    for g, w in enumerate(POOL_WINDOWS):
        half = w // 2
        assert half == 1 << g
        wide = slice(g * POOL_GROUP_DIM, POOL_WIDTH)
        cols = slice(g * POOL_GROUP_DIM, (g + 1) * POOL_GROUP_DIM)
        if g == 0:
            lv_ref[0, lo:lo + n, wide] = ext_ref[pl.ds(lo - 1, n), wide] + ext_ref[pl.ds(lo, n), wide]
        elif g < N_POOL_GROUPS - 1:
            lv_ref[g, lo:lo + n, wide] = (lv_ref[g - 1, pl.ds(lo - half // 2, n), wide]
                                          + lv_ref[g - 1, pl.ds(lo + half // 2, n), wide])
        if g < N_POOL_GROUPS - 1:
            tot = lv_ref[g, first:first + tm, cols]
        else:
            tot = (lv_ref[g - 1, pl.ds(first - half // 2, tm), cols]
                   + lv_ref[g - 1, pl.ds(first + half // 2, tm), cols])
        cnt = (jnp.minimum(pos + half, seq) - jnp.maximum(pos - half, 0)).astype(_f32)
        diff = tot / cnt - ext_ref[first:first + tm, cols]
        pooled.append(_dot(diff.astype(_bf16), pgw_ref[g]))
    pooled = (jnp.concatenate(pooled, axis=1) * ps_ref[...]).astype(_bf16)
    y_pool = _dot(pooled, wpp_ref[...])
    merged = sa_ref[...].astype(_f32) * y_attn + sb_ref[...].astype(_f32) * y_pool
    x1 = x_ref[...] + _dot(merged.astype(_bf16), wo_ref[...])
    x1_ref[...] = x1

    ms = jnp.mean(x1 * x1, axis=-1, keepdims=True)
    h2 = x1 * lax.rsqrt(ms + NORM_EPS) * g2_ref[...]
    h2_ref[...] = _pack_bf16_pair(h2[:, :HALF_MODEL], h2[:, HALF_MODEL:])
    hi, lo = _split_bf16(h2)
    both = _dot(hi, wr_ref[...])
    logits = both[:, :LANES] + both[:, LANES:] + _dot(lo, wr_ref[:, :LANES])
    lt = logits.T[:N_EXPERTS, :] + br_ref[...]
    eid = lax.broadcasted_iota(jnp.int32, lt.shape, 0)
    vals, idxs = [], []
    multi_hot = jnp.zeros(lt.shape, _f32)
    for _ in range(TOP_K):
        m = jnp.max(lt, axis=0, keepdims=True)
        sel = jnp.min(jnp.where(lt == m, eid, N_EXPERTS), axis=0, keepdims=True)
        hit = eid == sel
        vals.append(m)
        idxs.append(sel)
        multi_hot = multi_hot + hit.astype(_f32)
        lt = jnp.where(hit, -jnp.inf, lt)
    es = [jnp.exp(v - vals[0]) for v in vals]
    inv = 1.0 / (es[0] + es[1] + es[2] + es[3])
    gate_ref[...] = jnp.concatenate([e * inv for e in es], axis=0)
    idx_ref[...] = jnp.concatenate(idxs, axis=0)
    before = _dot(multi_hot.astype(_bf16), tri_ref[...]) + carry_ref[:, 0:1]
    ranks = [jnp.sum(jnp.where(eid == sel, before, 0.0), axis=0, keepdims=True) for sel in idxs]
    rank_ref[...] = jnp.concatenate(ranks, axis=0).astype(jnp.int32)
    carry_ref[...] = carry_ref[...] + jnp.sum(multi_hot, axis=1, keepdims=True)
    cnt_ref[...] = carry_ref[...]


def _mix_out(x2, attn, u, sa, sb, wap, pgw, ps, wpp, wo, g2, wr, br, tri, seq):
    tokens = x2.shape[0]
    tm = MIX_ROWS
    n = tokens // tm
    halo_blocks = tm // POOL_HALO
    last_halo = tokens // POOL_HALO - 1
    row = lambda i: (i, 0)
    const2 = lambda i: (0, 0)
    const3 = lambda i: (0, 0, 0)
    col = lambda i: (0, i)
    return pl.pallas_call(
        functools.partial(_mix_out_body, seq),
        grid=(n,),
        in_specs=[
            pl.BlockSpec((tm, D_MODEL), row),
            pl.BlockSpec((tm, ATTN_WIDTH), row),
            pl.BlockSpec((POOL_HALO, POOL_WIDTH), lambda i: (jnp.maximum(i * halo_blocks - 1, 0), 0)),
            pl.BlockSpec((tm, POOL_WIDTH), row),
            pl.BlockSpec((POOL_HALO, POOL_WIDTH), lambda i: (jnp.minimum((i + 1) * halo_blocks, last_halo), 0)),
            pl.BlockSpec((tm, D_MODEL), row),
            pl.BlockSpec((tm, D_MODEL), row),
            pl.BlockSpec((ATTN_WIDTH, D_MODEL), const2),
            pl.BlockSpec((N_POOL_GROUPS, POOL_GROUP_DIM, POOL_GROUP_DIM), const3),
            pl.BlockSpec((1, POOL_WIDTH), const2),
            pl.BlockSpec((POOL_WIDTH, D_MODEL), const2),
            pl.BlockSpec((D_MODEL, D_MODEL), const2),
            pl.BlockSpec((1, D_MODEL), const2),
            pl.BlockSpec((D_MODEL, 2 * LANES), const2),
            pl.BlockSpec((N_EXPERTS, 1), const2),
            pl.BlockSpec((tm, tm), const2),
        ],
        out_specs=[
            pl.BlockSpec((tm, D_MODEL), row),
            pl.BlockSpec((tm, HALF_MODEL), row),
            pl.BlockSpec((TOP_K, tm), col),
            pl.BlockSpec((TOP_K, tm), col),
            pl.BlockSpec((TOP_K, tm), col),
            pl.BlockSpec((N_EXPERTS, LANES), const2),
        ],
        out_shape=[
            jax.ShapeDtypeStruct((tokens, D_MODEL), _f32),
            jax.ShapeDtypeStruct((tokens, HALF_MODEL), jnp.uint32),
            jax.ShapeDtypeStruct((TOP_K, tokens), jnp.int32),
            jax.ShapeDtypeStruct((TOP_K, tokens), _f32),
            jax.ShapeDtypeStruct((TOP_K, tokens), jnp.int32),
            jax.ShapeDtypeStruct((N_EXPERTS, LANES), _f32),
        ],
        scratch_shapes=[
            pltpu.VMEM((tm + 4 * POOL_HALO, POOL_WIDTH), _f32),
            pltpu.VMEM((N_POOL_GROUPS - 1, tm + 4 * POOL_HALO, POOL_WIDTH), _f32),
            pltpu.VMEM((N_EXPERTS, LANES), _f32),
        ],
        compiler_params=pltpu.CompilerParams(
            dimension_semantics=("arbitrary",), vmem_limit_bytes=VMEM_LIMIT),
        name="mix_out",
    )(x2, attn, u, u, u, sa, sb, wap, pgw, ps, wpp, wo, g2, wr, br, tri)


def _worker_ranges(group_tokens, per_worker, workers):
    first = [0]
    for t in group_tokens:
        assert t % per_worker == 0
        first.append(first[-1] + t // per_worker)
    assert first[-1] == workers
    return first


def _dispatch(dest4, pad3, hs, n_rows):
    workers, n_chunks = dest4.shape[0], dest4.shape[1]
    n_pad_chunks = pad3.shape[1]
    per_worker = n_chunks * SC_CHUNK
    first_worker = _worker_ranges([h.shape[0] for h in hs], per_worker, workers)
    width, dtype = hs[0].shape[1], hs[0].dtype
    assert n_chunks % 2 == 0
    mesh = plsc.VectorSubcoreMesh(core_axis_name="c", subcore_axis_name="s")

    @functools.partial(
        pl.kernel, mesh=mesh,
        out_type=jax.ShapeDtypeStruct((n_rows, width), dtype),
        scratch_types=[
            pltpu.VMEM((n_chunks, TOP_K, SC_CHUNK), jnp.int32),
            pltpu.VMEM((n_pad_chunks, SC_CHUNK), jnp.int32),
            pltpu.VMEM((SC_CHUNK, width), dtype),
            pltpu.VMEM((SC_CHUNK, width), dtype),
            pltpu.SemaphoreType.DMA,
            pltpu.SemaphoreType.DMA,
            pltpu.SemaphoreType.DMA,
        ],
        name="dispatch",
    )
    def body(dest_hbm, pad_hbm, *refs):
        h_hbms = refs[:len(hs)]
        out_hbm, idx_v, pad_v, rows0, rows1, load_sem0, load_sem1, store_sem = refs[len(hs):]
        wid = lax.axis_index("s") * SC_CORES + lax.axis_index("c")
        pltpu.sync_copy(dest_hbm.at[wid], idx_v)
        pltpu.sync_copy(pad_hbm.at[wid], pad_v)

        zero = jnp.zeros((SC_LANES,), dtype)

        @pl.loop(0, SC_CHUNK)
        def _(r):
            @pl.loop(0, width, step=SC_LANES)
            def _(c):
                rows0.at[r, pl.ds(c, SC_LANES)][...] = zero

        @pl.loop(0, n_pad_chunks)
        def _(c):
            pltpu.sync_copy(rows0, out_hbm.at[pad_v.at[c]])

        def scatter_tokens(src_hbm, base):
            bufs = (rows0, rows1)
            load_sems = (load_sem0, load_sem1)

            def load(c, b):
                return pltpu.make_async_copy(src_hbm.at[pl.ds(base + c * SC_CHUNK, SC_CHUNK)], bufs[b], load_sems[b])

            load(0, 0).start()

            @pl.loop(0, n_chunks, step=2)
            def _(c):
                for b in range(2):
                    cc = c + b
                    load(cc, b).wait()

                    @pl.when(cc + 1 < n_chunks)
                    def _():
                        load(cc + 1, 1 - b).start()

                    copies = [pltpu.make_async_copy(bufs[b], out_hbm.at[idx_v.at[cc, k]], store_sem)
                              for k in range(TOP_K)]
                    for cp in copies:
                        cp.start()
                    for cp in copies:
                        cp.wait()

        for g, h_hbm in enumerate(h_hbms):
            @pl.when(jnp.logical_and(wid >= first_worker[g], wid < first_worker[g + 1]))
            def _(g=g, h_hbm=h_hbm):
                scatter_tokens(h_hbm, (wid - first_worker[g]) * per_worker)

    return body(dest4, pad3, *hs)


def _experts_body(be_ref, nu_ref, first_ref, slot_ref, next_ref, xs_ref, wgu_hbm, bgu_ref, wd_hbm, bd_ref, y_ref,
                  wgu_buf, wd_buf, sems):
    i = pl.program_id(0)
    used = i < nu_ref[0]
    slot = slot_ref[i]

    def weight_copies(e, s):
        return (pltpu.make_async_copy(wgu_hbm.at[e], wgu_buf.at[s], sems.at[s]),
                pltpu.make_async_copy(wd_hbm.at[e], wd_buf.at[s], sems.at[s]))

    @pl.when(i == 0)
    def _():
        for cp in weight_copies(be_ref[0], 0):
            cp.start()

    @pl.when(first_ref[i] == 1)
    def _():
        for cp in weight_copies(be_ref[i], slot):
            cp.wait()

        @pl.when(next_ref[i] >= 0)
        def _():
            for cp in weight_copies(next_ref[i], 1 - slot):
                cp.start()

    @pl.when(jnp.logical_not(used))
    def _():
        y_ref[...] = jnp.zeros_like(y_ref)

    @pl.when(used)
    def _():
        x = jnp.concatenate(_unpack_bf16_pair(xs_ref[...]), axis=1).astype(_bf16)
        gu = _dot(x, wgu_buf[slot].astype(_bf16)) + bgu_ref[...]
        gate = jnp.minimum(gu[:, :D_FF], SWIGLU_LIMIT)
        up = jnp.clip(gu[:, D_FF:], -SWIGLU_LIMIT, SWIGLU_LIMIT)
        act = (up + 1.0) * (gate * _sigmoid(SWIGLU_ALPHA * gate))
        y = _dot(act.astype(_bf16), wd_buf[slot].astype(_bf16)) + bd_ref[...]
        y_ref[...] = _pack_bf16_pair(y[:, :HALF_MODEL], y[:, HALF_MODEL:])


def _experts(block_e, n_used, first, slot, next_e, xs, wgu, bgu, wd, bd):
    n_blocks = block_e.shape[0]
    n_rows = n_blocks * MOE_ROWS
    rows = lambda i, be, nu, *_: (jnp.minimum(i, nu[0] - 1), 0)
    per_e = lambda i, be, *_: (be[i], 0, 0)
    return pl.pallas_call(
        _experts_body,
        grid_spec=pltpu.PrefetchScalarGridSpec(
            num_scalar_prefetch=5,
            grid=(n_blocks,),
            in_specs=[
                pl.BlockSpec((MOE_ROWS, HALF_MODEL), rows),
                pl.BlockSpec(memory_space=pl.ANY),
                pl.BlockSpec((None, 1, 2 * D_FF), per_e),
                pl.BlockSpec(memory_space=pl.ANY),
                pl.BlockSpec((None, 1, D_MODEL), per_e),
            ],
            out_specs=pl.BlockSpec((MOE_ROWS, HALF_MODEL), lambda i, *_: (i, 0)),
            scratch_shapes=[
                pltpu.VMEM((2, D_MODEL, 2 * D_FF), _f32),
                pltpu.VMEM((2, D_FF, D_MODEL), _f32),
                pltpu.SemaphoreType.DMA((2,)),
            ],
        ),
        out_shape=jax.ShapeDtypeStruct((n_rows, HALF_MODEL), jnp.uint32),
        compiler_params=pltpu.CompilerParams(
            dimension_semantics=("arbitrary",), vmem_limit_bytes=VMEM_LIMIT),
        name="experts",
    )(block_e, n_used, first, slot, next_e, xs, wgu, bgu, wd, bd)


def _combine(dest_rows, gates, x1s, yb):
    workers = dest_rows.shape[0]
    per_worker = dest_rows.shape[1] * LANES // TOP_K
    n_chunks = per_worker // SC_COMBINE_CHUNK
    first_worker = _worker_ranges([x.shape[0] for x in x1s], per_worker, workers)
    n_groups = len(x1s)
    assert n_chunks % 2 == 0 and LANES % SC_COMBINE_CHUNK == 0
    mesh = plsc.VectorSubcoreMesh(core_axis_name="c", subcore_axis_name="s")
    row_buf = pltpu.VMEM((SC_COMBINE_CHUNK, D_MODEL), _f32)
    packed_buf = pltpu.VMEM((SC_COMBINE_CHUNK, HALF_MODEL), jnp.uint32)

    @functools.partial(
        pl.kernel, mesh=mesh,
        out_type=[jax.ShapeDtypeStruct(x.shape, _f32) for x in x1s],
        scratch_types=[
            pltpu.VMEM(dest_rows.shape[1:], jnp.int32),
            [[packed_buf] * TOP_K + [row_buf]] * 2,
            [pltpu.VMEM((SC_COMBINE_CHUNK, TOP_K * SC_LANES), _f32)] * 2,
            [pltpu.SemaphoreType.DMA] * 2, [pltpu.SemaphoreType.DMA] * 2, [pltpu.SemaphoreType.DMA] * 2,
        ],
        compiler_params=pltpu.CompilerParams(needs_layout_passes=False),
        name="combine",
    )
    def body(dest_hbm, *refs):
        g_hbms, x_hbms = refs[:n_groups], refs[n_groups:2 * n_groups]
        yb_hbm = refs[2 * n_groups]
        o_hbms = refs[2 * n_groups + 1:3 * n_groups + 1]
        idx_v, row_bufs, gate_bufs, gather_sems, load_sems, store_sems = refs[3 * n_groups + 1:]
        wid = lax.axis_index("s") * SC_CORES + lax.axis_index("c")
        pltpu.sync_copy(dest_hbm.at[wid], idx_v)

        def run(g_hbm, x_hbm, o_hbm, base):
            def loads(c, b):
                rows = pl.ds(base + c * SC_COMBINE_CHUNK, SC_COMBINE_CHUNK)
                cps = []
                for k in range(TOP_K):
                    pos = (c * TOP_K + k) * SC_COMBINE_CHUNK
                    ids = idx_v.at[pos // LANES, pl.ds(pos % LANES, SC_COMBINE_CHUNK)]
                    cps.append(pltpu.make_async_copy(yb_hbm.at[ids], row_bufs[b][k], gather_sems[b]))
                cps.append(pltpu.make_async_copy(x_hbm.at[rows], row_bufs[b][TOP_K], load_sems[b]))
                cps.append(pltpu.make_async_copy(g_hbm.at[rows], gate_bufs[b], load_sems[b]))
                return cps

            def store(c, b):
                rows = pl.ds(base + c * SC_COMBINE_CHUNK, SC_COMBINE_CHUNK)
                return pltpu.make_async_copy(row_bufs[b][TOP_K], o_hbm.at[rows], store_sems[b])

            for cp in loads(0, 0):
                cp.start()

            @pl.loop(0, n_chunks, step=2)
            def _(c):
                for b in range(2):
                    cc = c + b

                    @pl.when(cc + 1 < n_chunks)
                    def _():
                        @pl.when(cc >= 1)
                        def _():
                            store(cc - 1, 1 - b).wait()
                        for cp in loads(cc + 1, 1 - b):
                            cp.start()

                    for cp in loads(cc, b):
                        cp.wait()
                    acc_buf, gate_buf = row_bufs[b][TOP_K], gate_bufs[b]

                    @pl.loop(0, SC_COMBINE_CHUNK)
                    def _(r):
                        g = [gate_buf.at[r, pl.ds(k * SC_LANES, SC_LANES)][...] for k in range(TOP_K)]

                        @plsc.parallel_loop(0, HALF_MODEL, step=SC_LANES, unroll=4)
                        def _(col):
                            lo = acc_buf.at[r, pl.ds(col, SC_LANES)][...]
                            hi = acc_buf.at[r, pl.ds(HALF_MODEL + col, SC_LANES)][...]
                            for k in range(TOP_K):
                                words = row_bufs[b][k].at[r, pl.ds(col, SC_LANES)][...]
                                lo = lo + g[k] * plsc.bitcast(lax.shift_left(words, jnp.uint32(16)), _f32)
                                hi = hi + g[k] * plsc.bitcast(words & jnp.uint32(_HIGH_HALF), _f32)
                            acc_buf.at[r, pl.ds(col, SC_LANES)][...] = lo
                            acc_buf.at[r, pl.ds(HALF_MODEL + col, SC_LANES)][...] = hi

                    store(cc, b).start()

            store(n_chunks - 2, 0).wait()
            store(n_chunks - 1, 1).wait()

        for g in range(n_groups):
            @pl.when(jnp.logical_and(wid >= first_worker[g], wid < first_worker[g + 1]))
            def _(g=g):
                run(g_hbms[g], x_hbms[g], o_hbms[g], (wid - first_worker[g]) * per_worker)

    return body(dest_rows, *gates, *x1s, yb)


def _rope_tables(seq):
    pos = jnp.arange(seq)
    row_ids = (pos // GRID_W).astype(_f32)
    col_ids = (pos % GRID_W).astype(_f32)
    inv_freq = ROPE_THETA ** (-jnp.arange(ROPE_PAIRS_PER_AXIS, dtype=_f32) / ROPE_PAIRS_PER_AXIS)
    ang_r = row_ids[:, None] * inv_freq
    ang_c = col_ids[:, None] * inv_freq
    cos = jnp.concatenate([jnp.cos(ang_r)] * 2 + [jnp.cos(ang_c)] * 2, axis=1)
    sin = jnp.concatenate([-jnp.sin(ang_r), jnp.sin(ang_r), -jnp.sin(ang_c), jnp.sin(ang_c)], axis=1)
    reps = LANES // HEAD_DIM
    return jnp.tile(cos, (1, reps)), jnp.tile(sin, (1, reps))


def _mixer(x, p):
    batch, seq, _ = x.shape
    x2 = x.reshape(batch * seq, D_MODEL)
    cos_t, sin_t = _rope_tables(seq)
    q, k2, vt, u, sa, sb = _in_proj(x2, p["mix_g"], p["w_in"], p["wvt"], p["qg"], p["kg"], cos_t, sin_t, p["bd"], seq)
    attn = _attention(q, k2, vt, batch, seq)
    return _mix_out(x2, attn, u, sa, sb, p["wap"], p["pgw"], p["ps"], p["wpp"], p["wo"], p["ffn_g"],
                    p["wr"], p["br"], p["tri"], seq)


def _moe(mixed, w_gu, b_gu, w_down, b_down):
    counts = [m[5][:, 0].astype(jnp.int32) for m in mixed]
    total = sum(counts)
    padded = ((total + MOE_ROWS - 1) // MOE_ROWS) * MOE_ROWS
    padded_end = jnp.cumsum(padded)
    padded_start = padded_end - padded
    n_tok = sum(m[0].shape[0] for m in mixed)
    n_blocks = (n_tok * TOP_K + MOE_ROWS - 1) // MOE_ROWS + N_EXPERTS
    n_used = (padded_end[-1] // MOE_ROWS).astype(jnp.int32)
    blk = jnp.minimum(jnp.arange(n_blocks, dtype=jnp.int32), n_used - 1)
    block_e = jnp.sum((padded_end[None, :] <= (blk * MOE_ROWS)[:, None]).astype(jnp.int32), axis=1)
    block_e = jnp.minimum(block_e, N_EXPERTS - 1)
    expert_ids = jnp.arange(N_EXPERTS, dtype=jnp.int32)[:, None, None]
    dests = []
    seen = jnp.zeros((N_EXPERTS,), jnp.int32)
    for m, c in zip(mixed, counts):
        base = (padded_start + seen)[:, None, None]
        dests.append(jnp.sum(jnp.where(m[2][None] == expert_ids, base, 0), axis=0) + m[4])
        seen = seen + c
    workers = SC_CORES * SC_SUBCORES
    per_worker = n_tok // workers
    dest4 = jnp.concatenate(
        [d.reshape(TOP_K, -1, per_worker // SC_CHUNK, SC_CHUNK).transpose(1, 2, 0, 3) for d in dests], axis=0)
    n_rows = n_blocks * MOE_ROWS
    slot = jnp.arange(MOE_ROWS, dtype=jnp.int32)[None, :]
    pad_rows = jnp.where(slot < (padded - total)[:, None], (padded_start + total)[:, None] + slot,
                         n_rows + jnp.arange(N_EXPERTS, dtype=jnp.int32)[:, None] * MOE_ROWS + slot)
    pad3 = pad_rows.astype(jnp.int32).reshape(workers, -1, SC_CHUNK)
    xs_buf = _dispatch(dest4, pad3, [m[1] for m in mixed], n_rows + N_EXPERTS * MOE_ROWS)
    blk_id = jnp.arange(n_blocks, dtype=jnp.int32)
    first = jnp.logical_and(block_e != jnp.concatenate([jnp.full((1,), -1, jnp.int32), block_e[:-1]]),
                            blk_id < n_used).astype(jnp.int32)
    slot = (jnp.cumsum(first) - 1) % 2
    group_end = jnp.sum(jnp.where(block_e[:, None] == jnp.arange(N_EXPERTS, dtype=jnp.int32)[None, :],
                                  (padded_end // MOE_ROWS)[None, :], 0), axis=1)
    next_e = jnp.sum(jnp.where(group_end[:, None] == blk_id[None, :], block_e[None, :], 0), axis=1)
    next_e = jnp.where(group_end < n_used, next_e, -1)
    yb = _experts(block_e, n_used.reshape(1), first, slot.astype(jnp.int32), next_e.astype(jnp.int32), xs_buf,
                  w_gu, b_gu.reshape(N_EXPERTS, 1, -1), w_down, b_down.reshape(N_EXPERTS, 1, -1))
    dest_rows = jnp.concatenate(
        [d.reshape(TOP_K, -1, per_worker // SC_COMBINE_CHUNK, SC_COMBINE_CHUNK).transpose(1, 2, 0, 3)
         .reshape(-1, per_worker * TOP_K // LANES, LANES) for d in dests], axis=0)
    gate_vecs = [jnp.repeat(m[3].T, SC_LANES, axis=1) for m in mixed]
    return _combine(dest_rows, gate_vecs, [m[0] for m in mixed], yb)


def kernel(x_prompt, x_sample, mix_norm_g, w_in, q_norm_g, k_norm_g, w_attn_proj, pool_group_w, pool_scale,
           w_pool_proj, w_out, ffn_norm_g, w_router, b_router, w_gu, b_gu, w_down, b_down):
    depth = w_in.shape[0]
    xs_all = [x_prompt, x_sample]
    head_id = jnp.arange(ATTN_WIDTH) // HEAD_DIM
    block_diag = (head_id[:, None] == head_id[None, :]).astype(_bf16)
    tri_id = jnp.arange(MIX_ROWS)
    tri = (tri_id[:, None] < tri_id[None, :]).astype(_bf16)
    for l in range(depth):
        wr = jnp.pad(w_router[l].astype(_f32), ((0, 0), (0, LANES - N_EXPERTS)))
        wrh = wr.astype(_bf16)
        p = dict(
            mix_g=mix_norm_g[l].reshape(1, D_MODEL),
            w_in=w_in[l].astype(_bf16),
            wvt=w_in[l][:, ATTN_WIDTH + KV_WIDTH:ATTN_WIDTH + 2 * KV_WIDTH].T.astype(_bf16),
            qg=jnp.tile(q_norm_g[l] * (HEAD_DIM ** -0.5 * LOG2_E), N_HEADS).reshape(1, ATTN_WIDTH),
            kg=jnp.tile(k_norm_g[l], N_KV_HEADS).reshape(1, KV_WIDTH),
            bd=block_diag,
            wap=w_attn_proj[l].astype(_bf16),
            pgw=pool_group_w[l].astype(_bf16),
            ps=pool_scale[l].reshape(1, POOL_WIDTH),
            wpp=w_pool_proj[l].astype(_bf16),
            wo=w_out[l].astype(_bf16),
            ffn_g=ffn_norm_g[l].reshape(1, D_MODEL),
            wr=jnp.concatenate([wrh, (wr - wrh.astype(_f32)).astype(_bf16)], axis=1),
            br=b_router[l].astype(_f32).reshape(N_EXPERTS, 1),
            tri=tri,
        )
        mixed = [_mixer(x, p) for x in xs_all]
        outs = [_moe([m], w_gu[l], b_gu[l], w_down[l], b_down[l])[0] for m in mixed]
        xs_all = [o.reshape(x.shape) for o, x in zip(outs, xs_all)]
    return tuple(xs_all)
```

```python
import functools

import jax
import jax.numpy as jnp
from jax import lax
from jax.experimental import pallas as pl
from jax.experimental.pallas import tpu as pltpu
from jax.experimental.pallas import tpu_sc as plsc

D_MODEL = 1024
HALF_MODEL = D_MODEL // 2
GRID_W = 64
N_HEADS = 8
N_KV_HEADS = 2
HEAD_DIM = 64
N_GROUPS_PER_KV = N_HEADS // N_KV_HEADS
ATTN_WIDTH = N_HEADS * HEAD_DIM
KV_WIDTH = N_KV_HEADS * HEAD_DIM
ROPE_THETA = 10000.0
ROPE_PAIRS_PER_AXIS = HEAD_DIM // 4
POOL_WINDOWS = (2, 4, 8, 16)
N_POOL_GROUPS = 4
POOL_WIDTH = 512
POOL_GROUP_DIM = POOL_WIDTH // N_POOL_GROUPS
N_EXPERTS = 32
TOP_K = 4
D_FF = 1024
SWIGLU_ALPHA = 1.702
SWIGLU_LIMIT = 7.0
NORM_EPS = 1e-6

LANES = 128
SUBLANES = 8
POOL_HALO = 8
LOG2_E = 1.4426950408889634

IN_PROJ_ROWS = 512
ATTN_Q_ROWS = 256
ATTN_KEY_CHUNK = 256
MIX_ROWS = 512
MOE_ROWS = 512
SC_CORES = 2
SC_SUBCORES = 16
SC_LANES = 16
SC_CHUNK = 32
SC_COMBINE_CHUNK = 16
VMEM_LIMIT = 56 * 1024 * 1024

_bf16 = jnp.bfloat16
_f32 = jnp.float32


def _dot(a, b):
    return jnp.dot(a, b, preferred_element_type=_f32)


def _split_bf16(x):
    hi = x.astype(_bf16)
    lo = (x - hi.astype(_f32)).astype(_bf16)
    return hi, lo


def _sigmoid(x):
    return 1.0 / (1.0 + jnp.exp(-x))


_HIGH_HALF = 0xFFFF0000


def _pack_bf16_pair(lo, hi):
    lo_bits = lax.bitcast_convert_type(lo.astype(_bf16).astype(_f32), jnp.uint32)
    hi_bits = lax.bitcast_convert_type(hi.astype(_bf16).astype(_f32), jnp.uint32)
    return lax.shift_right_logical(lo_bits, jnp.uint32(16)) | (hi_bits & jnp.uint32(_HIGH_HALF))


def _unpack_bf16_pair(words):
    lo = lax.bitcast_convert_type(lax.shift_left(words, jnp.uint32(16)), _f32)
    hi = lax.bitcast_convert_type(words & jnp.uint32(_HIGH_HALF), _f32)
    return lo, hi


def _in_proj_body(x_ref, g_ref, w_ref, wvt_ref, qg_ref, kg_ref, cos_ref, sin_ref, bd_ref,
                  q_ref, k_ref, vt_ref, u_ref, sa_ref, sb_ref):
    x = x_ref[...]
    ms = jnp.mean(x * x, axis=-1, keepdims=True)
    h = (x * lax.rsqrt(ms + NORM_EPS) * g_ref[...]).astype(_bf16)
    cos = cos_ref[...]
    sin = sin_ref[...]
    lane = lax.broadcasted_iota(jnp.int32, cos.shape, 1)
    first_half = (lane % (2 * ROPE_PAIRS_PER_AXIS)) < ROPE_PAIRS_PER_AXIS

    def head_norm_rope(z, gain):
        width = z.shape[1]
        ss = _dot((z * z).astype(_bf16), bd_ref[:width, :width])
        zn = z * lax.rsqrt(ss * (1.0 / HEAD_DIM) + NORM_EPS) * gain
        outs = []
        for j in range(width // LANES):
            c = zn[:, j * LANES:(j + 1) * LANES]
            partner = jnp.where(first_half,
                                pltpu.roll(c, LANES - ROPE_PAIRS_PER_AXIS, 1),
                                pltpu.roll(c, ROPE_PAIRS_PER_AXIS, 1))
            outs.append(c * cos + partner * sin)
        return outs

    c0 = 0
    zq = _dot(h, w_ref[:, c0:c0 + ATTN_WIDTH])
    for j, o in enumerate(head_norm_rope(zq, qg_ref[...])):
        q_ref[:, j * LANES:(j + 1) * LANES] = o.astype(_bf16)
    c0 += ATTN_WIDTH
    zk = _dot(h, w_ref[:, c0:c0 + KV_WIDTH])
    (kr,) = head_norm_rope(zk, kg_ref[...])
    for j in range(N_KV_HEADS):
        k_ref[j] = kr[:, j * HEAD_DIM:(j + 1) * HEAD_DIM].astype(_bf16)
    c0 += KV_WIDTH
    zvt = lax.dot_general(wvt_ref[...], h, (((1,), (1,)), ((), ())), preferred_element_type=_f32)
    for j in range(N_KV_HEADS):
        vt_ref[j] = zvt[j * HEAD_DIM:(j + 1) * HEAD_DIM, :].astype(_bf16)
    c0 += KV_WIDTH
    u_ref[...] = _dot(h, w_ref[:, c0:c0 + POOL_WIDTH]).astype(_bf16)
    c0 += POOL_WIDTH
    sa_ref[...] = _sigmoid(_dot(h, w_ref[:, c0:c0 + D_MODEL])).astype(_bf16)
    c0 += D_MODEL
    sb_ref[...] = _sigmoid(_dot(h, w_ref[:, c0:c0 + D_MODEL])).astype(_bf16)


def _in_proj(x2, norm_g, w_in, wvt, qg, kg, cos_t, sin_t, bd, seq):
    tokens = x2.shape[0]
    tm = IN_PROJ_ROWS
    in_width = w_in.shape[1]
    tiles_per_seq = seq // tm
    const = lambda i: (0, 0)
    row = lambda i: (i, 0)
    return pl.pallas_call(
        _in_proj_body,
        grid=(tokens // tm,),
        in_specs=[
            pl.BlockSpec((tm, D_MODEL), row),
            pl.BlockSpec((1, D_MODEL), const),
            pl.BlockSpec((D_MODEL, in_width), const),
            pl.BlockSpec((KV_WIDTH, D_MODEL), const),
            pl.BlockSpec((1, ATTN_WIDTH), const),
            pl.BlockSpec((1, KV_WIDTH), const),
            pl.BlockSpec((tm, LANES), lambda i: (i % tiles_per_seq, 0)),
            pl.BlockSpec((tm, LANES), lambda i: (i % tiles_per_seq, 0)),
            pl.BlockSpec((ATTN_WIDTH, ATTN_WIDTH), const),
        ],
        out_specs=[
            pl.BlockSpec((tm, ATTN_WIDTH), row),
            pl.BlockSpec((N_KV_HEADS, tm, HEAD_DIM), lambda i: (0, i, 0)),
            pl.BlockSpec((N_KV_HEADS, HEAD_DIM, tm), lambda i: (0, 0, i)),
            pl.BlockSpec((tm, POOL_WIDTH), row),
            pl.BlockSpec((tm, D_MODEL), row),
            pl.BlockSpec((tm, D_MODEL), row),
        ],
        out_shape=[
            jax.ShapeDtypeStruct((tokens, ATTN_WIDTH), _bf16),
            jax.ShapeDtypeStruct((N_KV_HEADS, tokens, HEAD_DIM), _bf16),
            jax.ShapeDtypeStruct((N_KV_HEADS, HEAD_DIM, tokens), _bf16),
            jax.ShapeDtypeStruct((tokens, POOL_WIDTH), _bf16),
            jax.ShapeDtypeStruct((tokens, D_MODEL), _bf16),
            jax.ShapeDtypeStruct((tokens, D_MODEL), _bf16),
        ],
        compiler_params=pltpu.CompilerParams(
            dimension_semantics=("parallel",), vmem_limit_bytes=VMEM_LIMIT),
        name="in_proj",
    )(x2, norm_g, w_in, wvt, qg, kg, cos_t, sin_t, bd)


def _attention_step(q_ref, k_ref, vt_ref, o_ref, st_new, m_new, st_old, m_old):
    tq = q_ref.shape[0]
    q = q_ref[...]
    qs = jnp.concatenate([q[:, g * HEAD_DIM:(g + 1) * HEAD_DIM] for g in range(N_GROUPS_PER_KV)], axis=0)
    seq = k_ref.shape[0]
    m_prev = m_old[...]
    m_run = denom = ot = None
    for c0 in range(0, seq, ATTN_KEY_CHUNK):
        rows = slice(c0, c0 + ATTN_KEY_CHUNK)
        st = lax.dot_general(k_ref[rows, :], qs, (((1,), (1,)), ((), ())), preferred_element_type=_f32)
        st_new[rows, :] = st
        m_c = jnp.max(st, axis=0, keepdims=True)
        m_run = m_c if m_run is None else jnp.maximum(m_run, m_c)
        p = jnp.exp2(st_old[rows, :] - m_prev)
        l_c = jnp.sum(p, axis=0, keepdims=True)
        o_c = _dot(vt_ref[:, rows], p.astype(_bf16))
        denom = l_c if denom is None else denom + l_c
        ot = o_c if ot is None else ot + o_c
    m_new[...] = m_run
    ot = ot * (1.0 / denom)
    stacked = jnp.concatenate([ot[:, g * tq:(g + 1) * tq] for g in range(N_GROUPS_PER_KV)], axis=0)
    o_ref[...] = stacked.T.astype(_bf16)


def _attention_body(q_ref, k_ref, vt_ref, o_ref, st_a, m_a, st_b, m_b):
    n = pl.program_id(0)

    @pl.when(n == 0)
    def _():
        st_b[...] = jnp.zeros_like(st_b)
        m_b[...] = jnp.zeros_like(m_b)

    @pl.when(n % 2 == 0)
    def _():
        _attention_step(q_ref, k_ref, vt_ref, o_ref, st_a, m_a, st_b, m_b)

    @pl.when(n % 2 == 1)
    def _():
        _attention_step(q_ref, k_ref, vt_ref, o_ref, st_b, m_b, st_a, m_a)


def _attention(q, k2, vt, batch, seq):
    tq = ATTN_Q_ROWS
    nq = seq // tq
    n_blocks = batch * N_KV_HEADS * nq
    group_width = N_GROUPS_PER_KV * HEAD_DIM
    lanes = N_GROUPS_PER_KV * tq

    def decode(n):
        return n // (N_KV_HEADS * nq), (n // nq) % N_KV_HEADS, n % nq

    def scores_of(n):
        return decode(jnp.minimum(n, n_blocks - 1))

    def finish_of(n):
        return decode(jnp.maximum(n - 1, 0))

    def q_map(n):
        b, kh, i = scores_of(n)
        return b * nq + i, kh

    def k_map(n):
        b, kh, _ = scores_of(n)
        return kh, b, 0

    def vt_map(n):
        b, kh, _ = finish_of(n)
        return kh, 0, b

    def o_map(n):
        b, kh, i = finish_of(n)
        return b * nq + i, kh

    return pl.pallas_call(
        _attention_body,
        grid=(n_blocks + 1,),
        in_specs=[
            pl.BlockSpec((tq, group_width), q_map),
            pl.BlockSpec((None, seq, HEAD_DIM), k_map),
            pl.BlockSpec((None, HEAD_DIM, seq), vt_map),
        ],
        out_specs=pl.BlockSpec((tq, group_width), o_map),
        out_shape=jax.ShapeDtypeStruct((batch * seq, ATTN_WIDTH), _bf16),
        scratch_shapes=[
            pltpu.VMEM((seq, lanes), _f32), pltpu.VMEM((1, lanes), _f32),
            pltpu.VMEM((seq, lanes), _f32), pltpu.VMEM((1, lanes), _f32),
        ],
        compiler_params=pltpu.CompilerParams(
            dimension_semantics=("arbitrary",), vmem_limit_bytes=VMEM_LIMIT),
        name="attention",
    )(q, k2, vt)


def _mix_out_body(seq, x_ref, a_ref, up_ref, u_ref, un_ref, sa_ref, sb_ref,
                  wap_ref, pgw_ref, ps_ref, wpp_ref, wo_ref, g2_ref, wr_ref, br_ref, tri_ref,
                  x1_ref, h2_ref, idx_ref, gate_ref, rank_ref, cnt_ref,
                  ext_ref, lv_ref, carry_ref):
    i = pl.program_id(0)
    tm = x_ref.shape[0]

    @pl.when(i == 0)
    def _():
        carry_ref[...] = jnp.zeros_like(carry_ref)
        ext_ref[...] = jnp.zeros_like(ext_ref)
        lv_ref[...] = jnp.zeros_like(lv_ref)

    start = (i * tm) % seq
    has_prev = (start != 0).astype(_f32)
    has_next = (start + tm != seq).astype(_f32)
    first = 2 * POOL_HALO
    ext_ref[POOL_HALO:first, :] = up_ref[...].astype(_f32) * has_prev
    ext_ref[first:first + tm, :] = u_ref[...].astype(_f32)
    ext_ref[first + tm:first + tm + POOL_HALO, :] = un_ref[...].astype(_f32) * has_next
    y_attn = _dot(a_ref[...], wap_ref[...])
    lo, n = POOL_HALO, tm + 2 * POOL_HALO
    pos = start + lax.broadcasted_iota(jnp.int32, (tm, 1), 0)
    pooled = []
    for g, w in enumerate(POOL_WINDOWS):
        half = w // 2
        assert half == 1 << g
        wide = slice(g * POOL_GROUP_DIM, POOL_WIDTH)
        cols = slice(g * POOL_GROUP_DIM, (g + 1) * POOL_GROUP_DIM)
        if g == 0:
            lv_ref[0, lo:lo + n, wide] = ext_ref[pl.ds(lo - 1, n), wide] + ext_ref[pl.ds(lo, n), wide]
        elif g < N_POOL_GROUPS - 1:
            lv_ref[g, lo:lo + n, wide] = (lv_ref[g - 1, pl.ds(lo - half // 2, n), wide]
                                          + lv_ref[g - 1, pl.ds(lo + half // 2, n), wide])
        if g < N_POOL_GROUPS - 1:
            tot = lv_ref[g, first:first + tm, cols]
        else:
            tot = (lv_ref[g - 1, pl.ds(first - half // 2, tm), cols]
                   + lv_ref[g - 1, pl.ds(first + half // 2, tm), cols])
        cnt = (jnp.minimum(pos + half, seq) - jnp.maximum(pos - half, 0)).astype(_f32)
        diff = tot / cnt - ext_ref[first:first + tm, cols]
        pooled.append(_dot(diff.astype(_bf16), pgw_ref[g]))
    pooled = (jnp.concatenate(pooled, axis=1) * ps_ref[...]).astype(_bf16)
    y_pool = _dot(pooled, wpp_ref[...])
    merged = sa_ref[...].astype(_f32) * y_attn + sb_ref[...].astype(_f32) * y_pool
    x1 = x_ref[...] + _dot(merged.astype(_bf16), wo_ref[...])
    x1_ref[...] = x1

    ms = jnp.mean(x1 * x1, axis=-1, keepdims=True)
    h2 = x1 * lax.rsqrt(ms + NORM_EPS) * g2_ref[...]
    h2_ref[...] = _pack_bf16_pair(h2[:, :HALF_MODEL], h2[:, HALF_MODEL:])
    hi, lo = _split_bf16(h2)
    both = _dot(hi, wr_ref[...])
    logits = both[:, :LANES] + both[:, LANES:] + _dot(lo, wr_ref[:, :LANES])
    lt = logits.T[:N_EXPERTS, :] + br_ref[...]
    eid = lax.broadcasted_iota(jnp.int32, lt.shape, 0)
    vals, idxs = [], []
    multi_hot = jnp.zeros(lt.shape, _f32)
    for _ in range(TOP_K):
        m = jnp.max(lt, axis=0, keepdims=True)
        sel = jnp.min(jnp.where(lt == m, eid, N_EXPERTS), axis=0, keepdims=True)
        hit = eid == sel
        vals.append(m)
        idxs.append(sel)
        multi_hot = multi_hot + hit.astype(_f32)
        lt = jnp.where(hit, -jnp.inf, lt)
    es = [jnp.exp(v - vals[0]) for v in vals]
    inv = 1.0 / (es[0] + es[1] + es[2] + es[3])
    gate_ref[...] = jnp.concatenate([e * inv for e in es], axis=0)
    idx_ref[...] = jnp.concatenate(idxs, axis=0)
    before = _dot(multi_hot.astype(_bf16), tri_ref[...]) + carry_ref[:, 0:1]
    ranks = [jnp.sum(jnp.where(eid == sel, before, 0.0), axis=0, keepdims=True) for sel in idxs]
    rank_ref[...] = jnp.concatenate(ranks, axis=0).astype(jnp.int32)
    carry_ref[...] = carry_ref[...] + jnp.sum(multi_hot, axis=1, keepdims=True)
    cnt_ref[...] = carry_ref[...]


def _mix_out(x2, attn, u, sa, sb, wap, pgw, ps, wpp, wo, g2, wr, br, tri, seq):
    tokens = x2.shape[0]
    tm = MIX_ROWS
    n = tokens // tm
    halo_blocks = tm // POOL_HALO
    last_halo = tokens // POOL_HALO - 1
    row = lambda i: (i, 0)
    const2 = lambda i: (0, 0)
    const3 = lambda i: (0, 0, 0)
    col = lambda i: (0, i)
    return pl.pallas_call(
        functools.partial(_mix_out_body, seq),
        grid=(n,),
        in_specs=[
            pl.BlockSpec((tm, D_MODEL), row),
            pl.BlockSpec((tm, ATTN_WIDTH), row),
            pl.BlockSpec((POOL_HALO, POOL_WIDTH), lambda i: (jnp.maximum(i * halo_blocks - 1, 0), 0)),
            pl.BlockSpec((tm, POOL_WIDTH), row),
            pl.BlockSpec((POOL_HALO, POOL_WIDTH), lambda i: (jnp.minimum((i + 1) * halo_blocks, last_halo), 0)),
            pl.BlockSpec((tm, D_MODEL), row),
            pl.BlockSpec((tm, D_MODEL), row),
            pl.BlockSpec((ATTN_WIDTH, D_MODEL), const2),
            pl.BlockSpec((N_POOL_GROUPS, POOL_GROUP_DIM, POOL_GROUP_DIM), const3),
            pl.BlockSpec((1, POOL_WIDTH), const2),
            pl.BlockSpec((POOL_WIDTH, D_MODEL), const2),
            pl.BlockSpec((D_MODEL, D_MODEL), const2),
            pl.BlockSpec((1, D_MODEL), const2),
            pl.BlockSpec((D_MODEL, 2 * LANES), const2),
            pl.BlockSpec((N_EXPERTS, 1), const2),
            pl.BlockSpec((tm, tm), const2),
        ],
        out_specs=[
            pl.BlockSpec((tm, D_MODEL), row),
            pl.BlockSpec((tm, HALF_MODEL), row),
            pl.BlockSpec((TOP_K, tm), col),
            pl.BlockSpec((TOP_K, tm), col),
            pl.BlockSpec((TOP_K, tm), col),
            pl.BlockSpec((N_EXPERTS, LANES), const2),
        ],
        out_shape=[
            jax.ShapeDtypeStruct((tokens, D_MODEL), _f32),
            jax.ShapeDtypeStruct((tokens, HALF_MODEL), jnp.uint32),
            jax.ShapeDtypeStruct((TOP_K, tokens), jnp.int32),
            jax.ShapeDtypeStruct((TOP_K, tokens), _f32),
            jax.ShapeDtypeStruct((TOP_K, tokens), jnp.int32),
            jax.ShapeDtypeStruct((N_EXPERTS, LANES), _f32),
        ],
        scratch_shapes=[
            pltpu.VMEM((tm + 4 * POOL_HALO, POOL_WIDTH), _f32),
            pltpu.VMEM((N_POOL_GROUPS - 1, tm + 4 * POOL_HALO, POOL_WIDTH), _f32),
            pltpu.VMEM((N_EXPERTS, LANES), _f32),
        ],
        compiler_params=pltpu.CompilerParams(
            dimension_semantics=("arbitrary",), vmem_limit_bytes=VMEM_LIMIT),
        name="mix_out",
    )(x2, attn, u, u, u, sa, sb, wap, pgw, ps, wpp, wo, g2, wr, br, tri)


def _worker_ranges(group_tokens, per_worker, workers):
    first = [0]
    for t in group_tokens:
        assert t % per_worker == 0
        first.append(first[-1] + t // per_worker)
    assert first[-1] == workers
    return first


def _dispatch(dest4, pad3, hs, n_rows):
    workers, n_chunks = dest4.shape[0], dest4.shape[1]
    n_pad_chunks = pad3.shape[1]
    per_worker = n_chunks * SC_CHUNK
    first_worker = _worker_ranges([h.shape[0] for h in hs], per_worker, workers)
    width, dtype = hs[0].shape[1], hs[0].dtype
    assert n_chunks % 2 == 0
    mesh = plsc.VectorSubcoreMesh(core_axis_name="c", subcore_axis_name="s")

    @functools.partial(
        pl.kernel, mesh=mesh,
        out_type=jax.ShapeDtypeStruct((n_rows, width), dtype),
        scratch_types=[
            pltpu.VMEM((n_chunks, TOP_K, SC_CHUNK), jnp.int32),
            pltpu.VMEM((n_pad_chunks, SC_CHUNK), jnp.int32),
            pltpu.VMEM((SC_CHUNK, width), dtype),
            pltpu.VMEM((SC_CHUNK, width), dtype),
            pltpu.SemaphoreType.DMA,
            pltpu.SemaphoreType.DMA,
            pltpu.SemaphoreType.DMA,
        ],
        name="dispatch",
    )
    def body(dest_hbm, pad_hbm, *refs):
        h_hbms = refs[:len(hs)]
        out_hbm, idx_v, pad_v, rows0, rows1, load_sem0, load_sem1, store_sem = refs[len(hs):]
        wid = lax.axis_index("s") * SC_CORES + lax.axis_index("c")
        pltpu.sync_copy(dest_hbm.at[wid], idx_v)
        pltpu.sync_copy(pad_hbm.at[wid], pad_v)

        zero = jnp.zeros((SC_LANES,), dtype)

        @pl.loop(0, SC_CHUNK)
        def _(r):
            @pl.loop(0, width, step=SC_LANES)
            def _(c):
                rows0.at[r, pl.ds(c, SC_LANES)][...] = zero

        @pl.loop(0, n_pad_chunks)
        def _(c):
            pltpu.sync_copy(rows0, out_hbm.at[pad_v.at[c]])

        def scatter_tokens(src_hbm, base):
            bufs = (rows0, rows1)
            load_sems = (load_sem0, load_sem1)

            def load(c, b):
                return pltpu.make_async_copy(src_hbm.at[pl.ds(base + c * SC_CHUNK, SC_CHUNK)], bufs[b], load_sems[b])

            load(0, 0).start()

            @pl.loop(0, n_chunks, step=2)
            def _(c):
                for b in range(2):
                    cc = c + b
                    load(cc, b).wait()

                    @pl.when(cc + 1 < n_chunks)
                    def _():
                        load(cc + 1, 1 - b).start()

                    copies = [pltpu.make_async_copy(bufs[b], out_hbm.at[idx_v.at[cc, k]], store_sem)
                              for k in range(TOP_K)]
                    for cp in copies:
                        cp.start()
                    for cp in copies:
                        cp.wait()

        for g, h_hbm in enumerate(h_hbms):
            @pl.when(jnp.logical_and(wid >= first_worker[g], wid < first_worker[g + 1]))
            def _(g=g, h_hbm=h_hbm):
                scatter_tokens(h_hbm, (wid - first_worker[g]) * per_worker)

    return body(dest4, pad3, *hs)


def _experts_body(be_ref, nu_ref, first_ref, slot_ref, next_ref, half_ref, xs_ref, wgu_hbm, bgu_ref, wd_hbm, bd_ref,
                  y_ref, wgu_buf, wd_buf, sems):
    i = pl.program_id(0)
    used = i < nu_ref[0]
    slot = slot_ref[i]

    def weight_copies(e, s):
        return (pltpu.make_async_copy(wgu_hbm.at[e], wgu_buf.at[s], sems.at[s]),
                pltpu.make_async_copy(wd_hbm.at[e], wd_buf.at[s], sems.at[s]))

    @pl.when(i == 0)
    def _():
        for cp in weight_copies(be_ref[0], 0):
            cp.start()

    @pl.when(first_ref[i] == 1)
    def _():
        for cp in weight_copies(be_ref[i], slot):
            cp.wait()

        @pl.when(next_ref[i] >= 0)
        def _():
            for cp in weight_copies(next_ref[i], 1 - slot):
                cp.start()

    @pl.when(jnp.logical_not(used))
    def _():
        y_ref[...] = jnp.zeros_like(y_ref)

    def run_rows(n):
        x = jnp.concatenate(_unpack_bf16_pair(xs_ref[:n, :]), axis=1).astype(_bf16)
        gu = _dot(x, wgu_buf[slot].astype(_bf16)) + bgu_ref[...]
        gate = jnp.minimum(gu[:, :D_FF], SWIGLU_LIMIT)
        up = jnp.clip(gu[:, D_FF:], -SWIGLU_LIMIT, SWIGLU_LIMIT)
        act = (up + 1.0) * (gate * _sigmoid(SWIGLU_ALPHA * gate))
        y = _dot(act.astype(_bf16), wd_buf[slot].astype(_bf16)) + bd_ref[...]
        y_ref[:n, :] = _pack_bf16_pair(y[:, :HALF_MODEL], y[:, HALF_MODEL:])

    lower_only = half_ref[i] == 1

    @pl.when(jnp.logical_and(used, jnp.logical_not(lower_only)))
    def _():
        run_rows(MOE_ROWS)

    @pl.when(jnp.logical_and(used, lower_only))
    def _():
        run_rows(MOE_ROWS // 2)
        y_ref[MOE_ROWS // 2:, :] = jnp.zeros((MOE_ROWS // 2, HALF_MODEL), jnp.uint32)


def _experts(block_e, n_used, first, slot, next_e, half, xs, wgu, bgu, wd, bd):
    n_blocks = block_e.shape[0]
    n_rows = n_blocks * MOE_ROWS
    rows = lambda i, be, nu, *_: (jnp.minimum(i, nu[0] - 1), 0)
    per_e = lambda i, be, *_: (be[i], 0, 0)
    return pl.pallas_call(
        _experts_body,
        grid_spec=pltpu.PrefetchScalarGridSpec(
            num_scalar_prefetch=6,
            grid=(n_blocks,),
            in_specs=[
                pl.BlockSpec((MOE_ROWS, HALF_MODEL), rows),
                pl.BlockSpec(memory_space=pl.ANY),
                pl.BlockSpec((None, 1, 2 * D_FF), per_e),
                pl.BlockSpec(memory_space=pl.ANY),
                pl.BlockSpec((None, 1, D_MODEL), per_e),
            ],
            out_specs=pl.BlockSpec((MOE_ROWS, HALF_MODEL), lambda i, *_: (i, 0)),
            scratch_shapes=[
                pltpu.VMEM((2, D_MODEL, 2 * D_FF), _f32),
                pltpu.VMEM((2, D_FF, D_MODEL), _f32),
                pltpu.SemaphoreType.DMA((2,)),
            ],
        ),
        out_shape=jax.ShapeDtypeStruct((n_rows, HALF_MODEL), jnp.uint32),
        compiler_params=pltpu.CompilerParams(
            dimension_semantics=("arbitrary",), vmem_limit_bytes=VMEM_LIMIT),
        name="experts",
    )(block_e, n_used, first, slot, next_e, half, xs, wgu, bgu, wd, bd)


def _combine(dest_rows, gates, x1s, yb):
    workers = dest_rows.shape[0]
    per_worker = dest_rows.shape[1] * LANES // TOP_K
    n_chunks = per_worker // SC_COMBINE_CHUNK
    first_worker = _worker_ranges([x.shape[0] for x in x1s], per_worker, workers)
    n_groups = len(x1s)
    assert n_chunks % 2 == 0 and LANES % SC_COMBINE_CHUNK == 0
    mesh = plsc.VectorSubcoreMesh(core_axis_name="c", subcore_axis_name="s")
    row_buf = pltpu.VMEM((SC_COMBINE_CHUNK, D_MODEL), _f32)
    packed_buf = pltpu.VMEM((SC_COMBINE_CHUNK, HALF_MODEL), jnp.uint32)

    @functools.partial(
        pl.kernel, mesh=mesh,
        out_type=[jax.ShapeDtypeStruct(x.shape, _f32) for x in x1s],
        scratch_types=[
            pltpu.VMEM(dest_rows.shape[1:], jnp.int32),
            [[packed_buf] * TOP_K + [row_buf]] * 2,
            [pltpu.VMEM((SC_COMBINE_CHUNK, TOP_K * SC_LANES), _f32)] * 2,
            [pltpu.SemaphoreType.DMA] * 2, [pltpu.SemaphoreType.DMA] * 2, [pltpu.SemaphoreType.DMA] * 2,
        ],
        compiler_params=pltpu.CompilerParams(needs_layout_passes=False),
        name="combine",
    )
    def body(dest_hbm, *refs):
        g_hbms, x_hbms = refs[:n_groups], refs[n_groups:2 * n_groups]
        yb_hbm = refs[2 * n_groups]
        o_hbms = refs[2 * n_groups + 1:3 * n_groups + 1]
        idx_v, row_bufs, gate_bufs, gather_sems, load_sems, store_sems = refs[3 * n_groups + 1:]
        wid = lax.axis_index("s") * SC_CORES + lax.axis_index("c")
        pltpu.sync_copy(dest_hbm.at[wid], idx_v)

        def run(g_hbm, x_hbm, o_hbm, base):
            def loads(c, b):
                rows = pl.ds(base + c * SC_COMBINE_CHUNK, SC_COMBINE_CHUNK)
                cps = []
                for k in range(TOP_K):
                    pos = (c * TOP_K + k) * SC_COMBINE_CHUNK
                    ids = idx_v.at[pos // LANES, pl.ds(pos % LANES, SC_COMBINE_CHUNK)]
                    cps.append(pltpu.make_async_copy(yb_hbm.at[ids], row_bufs[b][k], gather_sems[b]))
                cps.append(pltpu.make_async_copy(x_hbm.at[rows], row_bufs[b][TOP_K], load_sems[b]))
                cps.append(pltpu.make_async_copy(g_hbm.at[rows], gate_bufs[b], load_sems[b]))
                return cps

            def store(c, b):
                rows = pl.ds(base + c * SC_COMBINE_CHUNK, SC_COMBINE_CHUNK)
                return pltpu.make_async_copy(row_bufs[b][TOP_K], o_hbm.at[rows], store_sems[b])

            for cp in loads(0, 0):
                cp.start()

            @pl.loop(0, n_chunks, step=2)
            def _(c):
                for b in range(2):
                    cc = c + b

                    @pl.when(cc + 1 < n_chunks)
                    def _():
                        @pl.when(cc >= 1)
                        def _():
                            store(cc - 1, 1 - b).wait()
                        for cp in loads(cc + 1, 1 - b):
                            cp.start()

                    for cp in loads(cc, b):
                        cp.wait()
                    acc_buf, gate_buf = row_bufs[b][TOP_K], gate_bufs[b]

                    @pl.loop(0, SC_COMBINE_CHUNK)
                    def _(r):
                        g = [gate_buf.at[r, pl.ds(k * SC_LANES, SC_LANES)][...] for k in range(TOP_K)]

                        @plsc.parallel_loop(0, HALF_MODEL, step=SC_LANES, unroll=4)
                        def _(col):
                            lo = acc_buf.at[r, pl.ds(col, SC_LANES)][...]
                            hi = acc_buf.at[r, pl.ds(HALF_MODEL + col, SC_LANES)][...]
                            for k in range(TOP_K):
                                words = row_bufs[b][k].at[r, pl.ds(col, SC_LANES)][...]
                                lo = lo + g[k] * plsc.bitcast(lax.shift_left(words, jnp.uint32(16)), _f32)
                                hi = hi + g[k] * plsc.bitcast(words & jnp.uint32(_HIGH_HALF), _f32)
                            acc_buf.at[r, pl.ds(col, SC_LANES)][...] = lo
                            acc_buf.at[r, pl.ds(HALF_MODEL + col, SC_LANES)][...] = hi

                    store(cc, b).start()

            store(n_chunks - 2, 0).wait()
            store(n_chunks - 1, 1).wait()

        for g in range(n_groups):
            @pl.when(jnp.logical_and(wid >= first_worker[g], wid < first_worker[g + 1]))
            def _(g=g):
                run(g_hbms[g], x_hbms[g], o_hbms[g], (wid - first_worker[g]) * per_worker)

    return body(dest_rows, *gates, *x1s, yb)


def _rope_tables(seq):
    pos = jnp.arange(seq)
    row_ids = (pos // GRID_W).astype(_f32)
    col_ids = (pos % GRID_W).astype(_f32)
    inv_freq = ROPE_THETA ** (-jnp.arange(ROPE_PAIRS_PER_AXIS, dtype=_f32) / ROPE_PAIRS_PER_AXIS)
    ang_r = row_ids[:, None] * inv_freq
    ang_c = col_ids[:, None] * inv_freq
    cos = jnp.concatenate([jnp.cos(ang_r)] * 2 + [jnp.cos(ang_c)] * 2, axis=1)
    sin = jnp.concatenate([-jnp.sin(ang_r), jnp.sin(ang_r), -jnp.sin(ang_c), jnp.sin(ang_c)], axis=1)
    reps = LANES // HEAD_DIM
    return jnp.tile(cos, (1, reps)), jnp.tile(sin, (1, reps))


def _mixer(x, p):
    batch, seq, _ = x.shape
    x2 = x.reshape(batch * seq, D_MODEL)
    cos_t, sin_t = _rope_tables(seq)
    q, k2, vt, u, sa, sb = _in_proj(x2, p["mix_g"], p["w_in"], p["wvt"], p["qg"], p["kg"], cos_t, sin_t, p["bd"], seq)
    attn = _attention(q, k2, vt, batch, seq)
    return _mix_out(x2, attn, u, sa, sb, p["wap"], p["pgw"], p["ps"], p["wpp"], p["wo"], p["ffn_g"],
                    p["wr"], p["br"], p["tri"], seq)


def _moe(mixed, w_gu, b_gu, w_down, b_down):
    counts = [m[5][:, 0].astype(jnp.int32) for m in mixed]
    total = sum(counts)
    padded = ((total + MOE_ROWS - 1) // MOE_ROWS) * MOE_ROWS
    padded_end = jnp.cumsum(padded)
    padded_start = padded_end - padded
    n_tok = sum(m[0].shape[0] for m in mixed)
    n_blocks = (n_tok * TOP_K + MOE_ROWS - 1) // MOE_ROWS + N_EXPERTS
    n_used = (padded_end[-1] // MOE_ROWS).astype(jnp.int32)
    blk = jnp.minimum(jnp.arange(n_blocks, dtype=jnp.int32), n_used - 1)
    block_e = jnp.sum((padded_end[None, :] <= (blk * MOE_ROWS)[:, None]).astype(jnp.int32), axis=1)
    block_e = jnp.minimum(block_e, N_EXPERTS - 1)
    expert_ids = jnp.arange(N_EXPERTS, dtype=jnp.int32)[:, None, None]
    dests = []
    seen = jnp.zeros((N_EXPERTS,), jnp.int32)
    for m, c in zip(mixed, counts):
        base = (padded_start + seen)[:, None, None]
        dests.append(jnp.sum(jnp.where(m[2][None] == expert_ids, base, 0), axis=0) + m[4])
        seen = seen + c
    workers = SC_CORES * SC_SUBCORES
    per_worker = n_tok // workers
    dest4 = jnp.concatenate(
        [d.reshape(TOP_K, -1, per_worker // SC_CHUNK, SC_CHUNK).transpose(1, 2, 0, 3) for d in dests], axis=0)
    n_rows = n_blocks * MOE_ROWS
    slot = jnp.arange(MOE_ROWS, dtype=jnp.int32)[None, :]
    pad_rows = jnp.where(slot < (padded - total)[:, None], (padded_start + total)[:, None] + slot,
                         n_rows + jnp.arange(N_EXPERTS, dtype=jnp.int32)[:, None] * MOE_ROWS + slot)
    pad3 = pad_rows.astype(jnp.int32).reshape(workers, -1, SC_CHUNK)
    xs_buf = _dispatch(dest4, pad3, [m[1] for m in mixed], n_rows + N_EXPERTS * MOE_ROWS)
    blk_id = jnp.arange(n_blocks, dtype=jnp.int32)
    first = jnp.logical_and(block_e != jnp.concatenate([jnp.full((1,), -1, jnp.int32), block_e[:-1]]),
                            blk_id < n_used).astype(jnp.int32)
    slot = (jnp.cumsum(first) - 1) % 2
    group_end = jnp.sum(jnp.where(block_e[:, None] == jnp.arange(N_EXPERTS, dtype=jnp.int32)[None, :],
                                  (padded_end // MOE_ROWS)[None, :], 0), axis=1)
    next_e = jnp.sum(jnp.where(group_end[:, None] == blk_id[None, :], block_e[None, :], 0), axis=1)
    next_e = jnp.where(group_end < n_used, next_e, -1)
    token_end = jnp.sum(jnp.where(block_e[:, None] == jnp.arange(N_EXPERTS, dtype=jnp.int32)[None, :],
                                  (padded_start + total)[None, :], 0), axis=1)
    half = (token_end - blk_id * MOE_ROWS <= MOE_ROWS // 2).astype(jnp.int32)
    yb = _experts(block_e, n_used.reshape(1), first, slot.astype(jnp.int32), next_e.astype(jnp.int32), half, xs_buf,
                  w_gu, b_gu.reshape(N_EXPERTS, 1, -1), w_down, b_down.reshape(N_EXPERTS, 1, -1))
    dest_rows = jnp.concatenate(
        [d.reshape(TOP_K, -1, per_worker // SC_COMBINE_CHUNK, SC_COMBINE_CHUNK).transpose(1, 2, 0, 3)
         .reshape(-1, per_worker * TOP_K // LANES, LANES) for d in dests], axis=0)
    gate_vecs = [jnp.repeat(m[3].T, SC_LANES, axis=1) for m in mixed]
    return _combine(dest_rows, gate_vecs, [m[0] for m in mixed], yb)


def kernel(x_prompt, x_sample, mix_norm_g, w_in, q_norm_g, k_norm_g, w_attn_proj, pool_group_w, pool_scale,
           w_pool_proj, w_out, ffn_norm_g, w_router, b_router, w_gu, b_gu, w_down, b_down):
    depth = w_in.shape[0]
    xs_all = [x_prompt, x_sample]
    head_id = jnp.arange(ATTN_WIDTH) // HEAD_DIM
    block_diag = (head_id[:, None] == head_id[None, :]).astype(_bf16)
    tri_id = jnp.arange(MIX_ROWS)
    tri = (tri_id[:, None] < tri_id[None, :]).astype(_bf16)
    for l in range(depth):
        wr = jnp.pad(w_router[l].astype(_f32), ((0, 0), (0, LANES - N_EXPERTS)))
        wrh = wr.astype(_bf16)
        p = dict(
            mix_g=mix_norm_g[l].reshape(1, D_MODEL),
            w_in=w_in[l].astype(_bf16),
            wvt=w_in[l][:, ATTN_WIDTH + KV_WIDTH:ATTN_WIDTH + 2 * KV_WIDTH].T.astype(_bf16),
            qg=jnp.tile(q_norm_g[l] * (HEAD_DIM ** -0.5 * LOG2_E), N_HEADS).reshape(1, ATTN_WIDTH),
            kg=jnp.tile(k_norm_g[l], N_KV_HEADS).reshape(1, KV_WIDTH),
            bd=block_diag,
            wap=w_attn_proj[l].astype(_bf16),
            pgw=pool_group_w[l].astype(_bf16),
            ps=pool_scale[l].reshape(1, POOL_WIDTH),
            wpp=w_pool_proj[l].astype(_bf16),
            wo=w_out[l].astype(_bf16),
            ffn_g=ffn_norm_g[l].reshape(1, D_MODEL),
            wr=jnp.concatenate([wrh, (wr - wrh.astype(_f32)).astype(_bf16)], axis=1),
            br=b_router[l].astype(_f32).reshape(N_EXPERTS, 1),
            tri=tri,
        )
        mixed = [_mixer(x, p) for x in xs_all]
        outs = [_moe([m], w_gu[l], b_gu[l], w_down[l], b_down[l])[0] for m in mixed]
        xs_all = [o.reshape(x.shape) for o, x in zip(outs, xs_all)]
    return tuple(xs_all)
```

```python
import functools

import jax
import jax.numpy as jnp
from jax import lax
from jax.experimental import pallas as pl
from jax.experimental.pallas import tpu as pltpu
from jax.experimental.pallas import tpu_sc as plsc

D_MODEL = 1024
HALF_MODEL = D_MODEL // 2
GRID_W = 64
N_HEADS = 8
N_KV_HEADS = 2
HEAD_DIM = 64
N_GROUPS_PER_KV = N_HEADS // N_KV_HEADS
ATTN_WIDTH = N_HEADS * HEAD_DIM
KV_WIDTH = N_KV_HEADS * HEAD_DIM
ROPE_THETA = 10000.0
ROPE_PAIRS_PER_AXIS = HEAD_DIM // 4
POOL_WINDOWS = (2, 4, 8, 16)
N_POOL_GROUPS = 4
POOL_WIDTH = 512
POOL_GROUP_DIM = POOL_WIDTH // N_POOL_GROUPS
N_EXPERTS = 32
TOP_K = 4
D_FF = 1024
SWIGLU_ALPHA = 1.702
SWIGLU_LIMIT = 7.0
NORM_EPS = 1e-6

LANES = 128
SUBLANES = 8
POOL_HALO = 8
LOG2_E = 1.4426950408889634

IN_PROJ_ROWS = 1024
ATTN_Q_ROWS = 256
ATTN_KEY_CHUNK = 256
MIX_ROWS = 512
MOE_ROWS = 512
SC_CORES = 2
SC_SUBCORES = 16
SC_LANES = 16
SC_CHUNK = 32
SC_COMBINE_CHUNK = 16
V7X_VMEM_BYTES = 64 * 1024 * 1024
VMEM_LIMIT = V7X_VMEM_BYTES * 7 // 8

_bf16 = jnp.bfloat16
_f32 = jnp.float32


def _dot(a, b):
    return jnp.dot(a, b, preferred_element_type=_f32)


def _split_bf16(x):
    hi = x.astype(_bf16)
    lo = (x - hi.astype(_f32)).astype(_bf16)
    return hi, lo


def _sigmoid(x):
    return 1.0 / (1.0 + jnp.exp(-x))


_HIGH_HALF = 0xFFFF0000


def _pack_bf16_pair(lo, hi):
    lo_bits = lax.bitcast_convert_type(lo.astype(_bf16).astype(_f32), jnp.uint32)
    hi_bits = lax.bitcast_convert_type(hi.astype(_bf16).astype(_f32), jnp.uint32)
    return lax.shift_right_logical(lo_bits, jnp.uint32(16)) | (hi_bits & jnp.uint32(_HIGH_HALF))


def _unpack_bf16_pair(words):
    lo = lax.bitcast_convert_type(lax.shift_left(words, jnp.uint32(16)), _f32)
    hi = lax.bitcast_convert_type(words & jnp.uint32(_HIGH_HALF), _f32)
    return lo, hi


def _in_proj_body(x_ref, g_ref, w_ref, wvt_ref, qg_ref, kg_ref, cos_ref, sin_ref, bd_ref,
                  q_ref, k_ref, vt_ref, u_ref, sa_ref, sb_ref):
    x = x_ref[...]
    ms = jnp.mean(x * x, axis=-1, keepdims=True)
    h = (x * lax.rsqrt(ms + NORM_EPS) * g_ref[...]).astype(_bf16)
    cos = cos_ref[...]
    sin = sin_ref[...]
    lane = lax.broadcasted_iota(jnp.int32, cos.shape, 1)
    first_half = (lane % (2 * ROPE_PAIRS_PER_AXIS)) < ROPE_PAIRS_PER_AXIS

    def head_norm_rope(z, gain):
        width = z.shape[1]
        ss = _dot((z * z).astype(_bf16), bd_ref[:width, :width])
        zn = z * lax.rsqrt(ss * (1.0 / HEAD_DIM) + NORM_EPS) * gain
        outs = []
        for j in range(width // LANES):
            c = zn[:, j * LANES:(j + 1) * LANES]
            partner = jnp.where(first_half,
                                pltpu.roll(c, LANES - ROPE_PAIRS_PER_AXIS, 1),
                                pltpu.roll(c, ROPE_PAIRS_PER_AXIS, 1))
            outs.append(c * cos + partner * sin)
        return outs

    c0 = 0
    zq = _dot(h, w_ref[:, c0:c0 + ATTN_WIDTH])
    for j, o in enumerate(head_norm_rope(zq, qg_ref[...])):
        q_ref[:, j * LANES:(j + 1) * LANES] = o.astype(_bf16)
    c0 += ATTN_WIDTH
    zk = _dot(h, w_ref[:, c0:c0 + KV_WIDTH])
    (kr,) = head_norm_rope(zk, kg_ref[...])
    for j in range(N_KV_HEADS):
        k_ref[j] = kr[:, j * HEAD_DIM:(j + 1) * HEAD_DIM].astype(_bf16)
    c0 += KV_WIDTH
    zvt = lax.dot_general(wvt_ref[...], h, (((1,), (1,)), ((), ())), preferred_element_type=_f32)
    for j in range(N_KV_HEADS):
        vt_ref[j] = zvt[j * HEAD_DIM:(j + 1) * HEAD_DIM, :].astype(_bf16)
    c0 += KV_WIDTH
    u_ref[...] = _dot(h, w_ref[:, c0:c0 + POOL_WIDTH]).astype(_bf16)
    c0 += POOL_WIDTH
    sa_ref[...] = _sigmoid(_dot(h, w_ref[:, c0:c0 + D_MODEL])).astype(_bf16)
    c0 += D_MODEL
    sb_ref[...] = _sigmoid(_dot(h, w_ref[:, c0:c0 + D_MODEL])).astype(_bf16)


def _in_proj(x2, norm_g, w_in, wvt, qg, kg, cos_t, sin_t, bd, seq):
    tokens = x2.shape[0]
    tm = IN_PROJ_ROWS
    in_width = w_in.shape[1]
    tiles_per_seq = seq // tm
    const = lambda i: (0, 0)
    row = lambda i: (i, 0)
    return pl.pallas_call(
        _in_proj_body,
        grid=(tokens // tm,),
        in_specs=[
            pl.BlockSpec((tm, D_MODEL), row),
            pl.BlockSpec((1, D_MODEL), const),
            pl.BlockSpec((D_MODEL, in_width), const),
            pl.BlockSpec((KV_WIDTH, D_MODEL), const),
            pl.BlockSpec((1, ATTN_WIDTH), const),
            pl.BlockSpec((1, KV_WIDTH), const),
            pl.BlockSpec((tm, LANES), lambda i: (i % tiles_per_seq, 0)),
            pl.BlockSpec((tm, LANES), lambda i: (i % tiles_per_seq, 0)),
            pl.BlockSpec((ATTN_WIDTH, ATTN_WIDTH), const),
        ],
        out_specs=[
            pl.BlockSpec((tm, ATTN_WIDTH), row),
            pl.BlockSpec((N_KV_HEADS, tm, HEAD_DIM), lambda i: (0, i, 0)),
            pl.BlockSpec((N_KV_HEADS, HEAD_DIM, tm), lambda i: (0, 0, i)),
            pl.BlockSpec((tm, POOL_WIDTH), row),
            pl.BlockSpec((tm, D_MODEL), row),
            pl.BlockSpec((tm, D_MODEL), row),
        ],
        out_shape=[
            jax.ShapeDtypeStruct((tokens, ATTN_WIDTH), _bf16),
            jax.ShapeDtypeStruct((N_KV_HEADS, tokens, HEAD_DIM), _bf16),
            jax.ShapeDtypeStruct((N_KV_HEADS, HEAD_DIM, tokens), _bf16),
            jax.ShapeDtypeStruct((tokens, POOL_WIDTH), _bf16),
            jax.ShapeDtypeStruct((tokens, D_MODEL), _bf16),
            jax.ShapeDtypeStruct((tokens, D_MODEL), _bf16),
        ],
        compiler_params=pltpu.CompilerParams(
            dimension_semantics=("parallel",), vmem_limit_bytes=VMEM_LIMIT),
        name="in_proj",
    )(x2, norm_g, w_in, wvt, qg, kg, cos_t, sin_t, bd)


def _attention_step(q_ref, k_ref, vt_ref, o_ref, st_new, m_new, st_old, m_old):
    tq = q_ref.shape[0]
    q = q_ref[...]
    qs = jnp.concatenate([q[:, g * HEAD_DIM:(g + 1) * HEAD_DIM] for g in range(N_GROUPS_PER_KV)], axis=0)
    seq = k_ref.shape[0]
    m_prev = m_old[...]
    m_run = denom = ot = None
    for c0 in range(0, seq, ATTN_KEY_CHUNK):
        rows = slice(c0, c0 + ATTN_KEY_CHUNK)
        st = lax.dot_general(k_ref[rows, :], qs, (((1,), (1,)), ((), ())), preferred_element_type=_f32)
        st_new[rows, :] = st
        m_c = jnp.max(st, axis=0, keepdims=True)
        m_run = m_c if m_run is None else jnp.maximum(m_run, m_c)
        p = jnp.exp2(st_old[rows, :] - m_prev)
        l_c = jnp.sum(p, axis=0, keepdims=True)
        o_c = _dot(vt_ref[:, rows], p.astype(_bf16))
        denom = l_c if denom is None else denom + l_c
        ot = o_c if ot is None else ot + o_c
    m_new[...] = m_run
    ot = ot * (1.0 / denom)
    stacked = jnp.concatenate([ot[:, g * tq:(g + 1) * tq] for g in range(N_GROUPS_PER_KV)], axis=0)
    o_ref[...] = stacked.T.astype(_bf16)


def _attention_body(q_ref, k_ref, vt_ref, o_ref, st_a, m_a, st_b, m_b):
    n = pl.program_id(0)

    @pl.when(n == 0)
    def _():
        st_b[...] = jnp.zeros_like(st_b)
        m_b[...] = jnp.zeros_like(m_b)

    @pl.when(n % 2 == 0)
    def _():
        _attention_step(q_ref, k_ref, vt_ref, o_ref, st_a, m_a, st_b, m_b)

    @pl.when(n % 2 == 1)
    def _():
        _attention_step(q_ref, k_ref, vt_ref, o_ref, st_b, m_b, st_a, m_a)


def _attention(q, k2, vt, batch, seq):
    tq = ATTN_Q_ROWS
    nq = seq // tq
    n_blocks = batch * N_KV_HEADS * nq
    group_width = N_GROUPS_PER_KV * HEAD_DIM
    lanes = N_GROUPS_PER_KV * tq

    def decode(n):
        return n // (N_KV_HEADS * nq), (n // nq) % N_KV_HEADS, n % nq

    def scores_of(n):
        return decode(jnp.minimum(n, n_blocks - 1))

    def finish_of(n):
        return decode(jnp.maximum(n - 1, 0))

    def q_map(n):
        b, kh, i = scores_of(n)
        return b * nq + i, kh

    def k_map(n):
        b, kh, _ = scores_of(n)
        return kh, b, 0

    def vt_map(n):
        b, kh, _ = finish_of(n)
        return kh, 0, b

    def o_map(n):
        b, kh, i = finish_of(n)
        return b * nq + i, kh

    return pl.pallas_call(
        _attention_body,
        grid=(n_blocks + 1,),
        in_specs=[
            pl.BlockSpec((tq, group_width), q_map),
            pl.BlockSpec((None, seq, HEAD_DIM), k_map),
            pl.BlockSpec((None, HEAD_DIM, seq), vt_map),
        ],
        out_specs=pl.BlockSpec((tq, group_width), o_map),
        out_shape=jax.ShapeDtypeStruct((batch * seq, ATTN_WIDTH), _bf16),
        scratch_shapes=[
            pltpu.VMEM((seq, lanes), _f32), pltpu.VMEM((1, lanes), _f32),
            pltpu.VMEM((seq, lanes), _f32), pltpu.VMEM((1, lanes), _f32),
        ],
        compiler_params=pltpu.CompilerParams(
            dimension_semantics=("arbitrary",), vmem_limit_bytes=VMEM_LIMIT),
        name="attention",
    )(q, k2, vt)


def _mix_out_body(seq, x_ref, a_ref, up_ref, u_ref, un_ref, sa_ref, sb_ref,
                  wap_ref, pgw_ref, ps_ref, wpp_ref, wo_ref, g2_ref, wr_ref, br_ref, tri_ref,
                  x1_ref, h2_ref, idx_ref, gate_ref, rank_ref, cnt_ref,
                  ext_ref, lv_ref, carry_ref):
    i = pl.program_id(0)
    tm = x_ref.shape[0]

    @pl.when(i == 0)
    def _():
        carry_ref[...] = jnp.zeros_like(carry_ref)
        ext_ref[...] = jnp.zeros_like(ext_ref)
        lv_ref[...] = jnp.zeros_like(lv_ref)

    start = (i * tm) % seq
    has_prev = (start != 0).astype(_f32)
    has_next = (start + tm != seq).astype(_f32)
    first = 2 * POOL_HALO
    ext_ref[POOL_HALO:first, :] = up_ref[...].astype(_f32) * has_prev
    ext_ref[first:first + tm, :] = u_ref[...].astype(_f32)
    ext_ref[first + tm:first + tm + POOL_HALO, :] = un_ref[...].astype(_f32) * has_next
    y_attn = _dot(a_ref[...], wap_ref[...])
    lo, n = POOL_HALO, tm + 2 * POOL_HALO
    pos = start + lax.broadcasted_iota(jnp.int32, (tm, 1), 0)
    pooled = []
    for g, w in enumerate(POOL_WINDOWS):
        half = w // 2
        assert half == 1 << g
        wide = slice(g * POOL_GROUP_DIM, POOL_WIDTH)
        cols = slice(g * POOL_GROUP_DIM, (g + 1) * POOL_GROUP_DIM)
        if g == 0:
            lv_ref[0, lo:lo + n, wide] = ext_ref[pl.ds(lo - 1, n), wide] + ext_ref[pl.ds(lo, n), wide]
        elif g < N_POOL_GROUPS - 1:
            lv_ref[g, lo:lo + n, wide] = (lv_ref[g - 1, pl.ds(lo - half // 2, n), wide]
                                          + lv_ref[g - 1, pl.ds(lo + half // 2, n), wide])
        if g < N_POOL_GROUPS - 1:
            tot = lv_ref[g, first:first + tm, cols]
        else:
            tot = (lv_ref[g - 1, pl.ds(first - half // 2, tm), cols]
                   + lv_ref[g - 1, pl.ds(first + half // 2, tm), cols])
        cnt = (jnp.minimum(pos + half, seq) - jnp.maximum(pos - half, 0)).astype(_f32)
        diff = tot / cnt - ext_ref[first:first + tm, cols]
        pooled.append(_dot(diff.astype(_bf16), pgw_ref[g]))
    pooled = (jnp.concatenate(pooled, axis=1) * ps_ref[...]).astype(_bf16)
    y_pool = _dot(pooled, wpp_ref[...])
    merged = sa_ref[...].astype(_f32) * y_attn + sb_ref[...].astype(_f32) * y_pool
    x1 = x_ref[...] + _dot(merged.astype(_bf16), wo_ref[...])
    x1_ref[...] = x1

    ms = jnp.mean(x1 * x1, axis=-1, keepdims=True)
    h2 = x1 * lax.rsqrt(ms + NORM_EPS) * g2_ref[...]
    h2_ref[...] = _pack_bf16_pair(h2[:, :HALF_MODEL], h2[:, HALF_MODEL:])
    hi, lo = _split_bf16(h2)
    both = _dot(hi, wr_ref[...])
    logits = both[:, :LANES] + both[:, LANES:] + _dot(lo, wr_ref[:, :LANES])
    lt = logits.T[:N_EXPERTS, :] + br_ref[...]
    eid = lax.broadcasted_iota(jnp.int32, lt.shape, 0)
    vals, idxs = [], []
    multi_hot = jnp.zeros(lt.shape, _f32)
    for _ in range(TOP_K):
        m = jnp.max(lt, axis=0, keepdims=True)
        sel = jnp.min(jnp.where(lt == m, eid, N_EXPERTS), axis=0, keepdims=True)
        hit = eid == sel
        vals.append(m)
        idxs.append(sel)
        multi_hot = multi_hot + hit.astype(_f32)
        lt = jnp.where(hit, -jnp.inf, lt)
    es = [jnp.exp(v - vals[0]) for v in vals]
    inv = 1.0 / (es[0] + es[1] + es[2] + es[3])
    gate_ref[...] = jnp.concatenate([e * inv for e in es], axis=0)
    idx_ref[...] = jnp.concatenate(idxs, axis=0)
    before = _dot(multi_hot.astype(_bf16), tri_ref[...]) + carry_ref[:, 0:1]
    ranks = [jnp.sum(jnp.where(eid == sel, before, 0.0), axis=0, keepdims=True) for sel in idxs]
    rank_ref[...] = jnp.concatenate(ranks, axis=0).astype(jnp.int32)
    carry_ref[...] = carry_ref[...] + jnp.sum(multi_hot, axis=1, keepdims=True)
    cnt_ref[...] = carry_ref[...]


def _mix_out(x2, attn, u, sa, sb, wap, pgw, ps, wpp, wo, g2, wr, br, tri, seq):
    tokens = x2.shape[0]
    tm = MIX_ROWS
    n = tokens // tm
    halo_blocks = tm // POOL_HALO
    last_halo = tokens // POOL_HALO - 1
    row = lambda i: (i, 0)
    const2 = lambda i: (0, 0)
    const3 = lambda i: (0, 0, 0)
    col = lambda i: (0, i)
    return pl.pallas_call(
        functools.partial(_mix_out_body, seq),
        grid=(n,),
        in_specs=[
            pl.BlockSpec((tm, D_MODEL), row),
            pl.BlockSpec((tm, ATTN_WIDTH), row),
            pl.BlockSpec((POOL_HALO, POOL_WIDTH), lambda i: (jnp.maximum(i * halo_blocks - 1, 0), 0)),
            pl.BlockSpec((tm, POOL_WIDTH), row),
            pl.BlockSpec((POOL_HALO, POOL_WIDTH), lambda i: (jnp.minimum((i + 1) * halo_blocks, last_halo), 0)),
            pl.BlockSpec((tm, D_MODEL), row),
            pl.BlockSpec((tm, D_MODEL), row),
            pl.BlockSpec((ATTN_WIDTH, D_MODEL), const2),
            pl.BlockSpec((N_POOL_GROUPS, POOL_GROUP_DIM, POOL_GROUP_DIM), const3),
            pl.BlockSpec((1, POOL_WIDTH), const2),
            pl.BlockSpec((POOL_WIDTH, D_MODEL), const2),
            pl.BlockSpec((D_MODEL, D_MODEL), const2),
            pl.BlockSpec((1, D_MODEL), const2),
            pl.BlockSpec((D_MODEL, 2 * LANES), const2),
            pl.BlockSpec((N_EXPERTS, 1), const2),
            pl.BlockSpec((tm, tm), const2),
        ],
        out_specs=[
            pl.BlockSpec((tm, D_MODEL), row),
            pl.BlockSpec((tm, HALF_MODEL), row),
            pl.BlockSpec((TOP_K, tm), col),
            pl.BlockSpec((TOP_K, tm), col),
            pl.BlockSpec((TOP_K, tm), col),
            pl.BlockSpec((N_EXPERTS, LANES), const2),
        ],
        out_shape=[
            jax.ShapeDtypeStruct((tokens, D_MODEL), _f32),
            jax.ShapeDtypeStruct((tokens, HALF_MODEL), jnp.uint32),
            jax.ShapeDtypeStruct((TOP_K, tokens), jnp.int32),
            jax.ShapeDtypeStruct((TOP_K, tokens), _f32),
            jax.ShapeDtypeStruct((TOP_K, tokens), jnp.int32),
            jax.ShapeDtypeStruct((N_EXPERTS, LANES), _f32),
        ],
        scratch_shapes=[
            pltpu.VMEM((tm + 4 * POOL_HALO, POOL_WIDTH), _f32),
            pltpu.VMEM((N_POOL_GROUPS - 1, tm + 4 * POOL_HALO, POOL_WIDTH), _f32),
            pltpu.VMEM((N_EXPERTS, LANES), _f32),
        ],
        compiler_params=pltpu.CompilerParams(
            dimension_semantics=("arbitrary",), vmem_limit_bytes=VMEM_LIMIT),
        name="mix_out",
    )(x2, attn, u, u, u, sa, sb, wap, pgw, ps, wpp, wo, g2, wr, br, tri)


def _worker_ranges(group_tokens, per_worker, workers):
    first = [0]
    for t in group_tokens:
        assert t % per_worker == 0
        first.append(first[-1] + t // per_worker)
    assert first[-1] == workers
    return first


def _dispatch(dest4, pad3, hs, n_rows):
    workers, n_chunks = dest4.shape[0], dest4.shape[1]
    n_pad_chunks = pad3.shape[1]
    per_worker = n_chunks * SC_CHUNK
    first_worker = _worker_ranges([h.shape[0] for h in hs], per_worker, workers)
    width, dtype = hs[0].shape[1], hs[0].dtype
    assert n_chunks % 2 == 0
    mesh = plsc.VectorSubcoreMesh(core_axis_name="c", subcore_axis_name="s")

    @functools.partial(
        pl.kernel, mesh=mesh,
        out_type=jax.ShapeDtypeStruct((n_rows, width), dtype),
        scratch_types=[
            pltpu.VMEM((n_chunks, TOP_K, SC_CHUNK), jnp.int32),
            pltpu.VMEM((n_pad_chunks, SC_CHUNK), jnp.int32),
            pltpu.VMEM((SC_CHUNK, width), dtype),
            pltpu.VMEM((SC_CHUNK, width), dtype),
            pltpu.SemaphoreType.DMA,
            pltpu.SemaphoreType.DMA,
            pltpu.SemaphoreType.DMA,
        ],
        name="dispatch",
    )
    def body(dest_hbm, pad_hbm, *refs):
        h_hbms = refs[:len(hs)]
        out_hbm, idx_v, pad_v, rows0, rows1, load_sem0, load_sem1, store_sem = refs[len(hs):]
        wid = lax.axis_index("s") * SC_CORES + lax.axis_index("c")
        pltpu.sync_copy(dest_hbm.at[wid], idx_v)
        pltpu.sync_copy(pad_hbm.at[wid], pad_v)

        zero = jnp.zeros((SC_LANES,), dtype)

        @pl.loop(0, SC_CHUNK)
        def _(r):
            @pl.loop(0, width, step=SC_LANES)
            def _(c):
                rows0.at[r, pl.ds(c, SC_LANES)][...] = zero

        @pl.loop(0, n_pad_chunks)
        def _(c):
            pltpu.sync_copy(rows0, out_hbm.at[pad_v.at[c]])

        def scatter_tokens(src_hbm, base):
            bufs = (rows0, rows1)
            load_sems = (load_sem0, load_sem1)

            def load(c, b):
                return pltpu.make_async_copy(src_hbm.at[pl.ds(base + c * SC_CHUNK, SC_CHUNK)], bufs[b], load_sems[b])

            load(0, 0).start()

            @pl.loop(0, n_chunks, step=2)
            def _(c):
                for b in range(2):
                    cc = c + b
                    load(cc, b).wait()

                    @pl.when(cc + 1 < n_chunks)
                    def _():
                        load(cc + 1, 1 - b).start()

                    copies = [pltpu.make_async_copy(bufs[b], out_hbm.at[idx_v.at[cc, k]], store_sem)
                              for k in range(TOP_K)]
                    for cp in copies:
                        cp.start()
                    for cp in copies:
                        cp.wait()

        for g, h_hbm in enumerate(h_hbms):
            @pl.when(jnp.logical_and(wid >= first_worker[g], wid < first_worker[g + 1]))
            def _(g=g, h_hbm=h_hbm):
                scatter_tokens(h_hbm, (wid - first_worker[g]) * per_worker)

    return body(dest4, pad3, *hs)


def _experts_body(be_ref, nu_ref, first_ref, slot_ref, next_ref, half_ref, xs_ref, wgu_hbm, bgu_ref, wd_hbm, bd_ref,
                  y_ref, wgu_buf, wd_buf, sems):
    i = pl.program_id(0)
    used = i < nu_ref[0]
    slot = slot_ref[i]

    def weight_copies(e, s):
        return (pltpu.make_async_copy(wgu_hbm.at[e], wgu_buf.at[s], sems.at[s]),
                pltpu.make_async_copy(wd_hbm.at[e], wd_buf.at[s], sems.at[s]))

    @pl.when(i == 0)
    def _():
        for cp in weight_copies(be_ref[0], 0):
            cp.start()

    @pl.when(first_ref[i] == 1)
    def _():
        for cp in weight_copies(be_ref[i], slot):
            cp.wait()

        @pl.when(next_ref[i] >= 0)
        def _():
            for cp in weight_copies(next_ref[i], 1 - slot):
                cp.start()

    @pl.when(jnp.logical_not(used))
    def _():
        y_ref[...] = jnp.zeros_like(y_ref)

    def run_rows(n):
        x = jnp.concatenate(_unpack_bf16_pair(xs_ref[:n, :]), axis=1).astype(_bf16)
        gu = _dot(x, wgu_buf[slot].astype(_bf16)) + bgu_ref[...]
        gate = jnp.minimum(gu[:, :D_FF], SWIGLU_LIMIT)
        up = jnp.clip(gu[:, D_FF:], -SWIGLU_LIMIT, SWIGLU_LIMIT)
        act = (up + 1.0) * (gate * _sigmoid(SWIGLU_ALPHA * gate))
        y = _dot(act.astype(_bf16), wd_buf[slot].astype(_bf16)) + bd_ref[...]
        y_ref[:n, :] = _pack_bf16_pair(y[:, :HALF_MODEL], y[:, HALF_MODEL:])

    lower_only = half_ref[i] == 1

    @pl.when(jnp.logical_and(used, jnp.logical_not(lower_only)))
    def _():
        run_rows(MOE_ROWS)

    @pl.when(jnp.logical_and(used, lower_only))
    def _():
        run_rows(MOE_ROWS // 2)
        y_ref[MOE_ROWS // 2:, :] = jnp.zeros((MOE_ROWS // 2, HALF_MODEL), jnp.uint32)


def _experts(block_e, n_used, first, slot, next_e, half, xs, wgu, bgu, wd, bd):
    n_blocks = block_e.shape[0]
    n_rows = n_blocks * MOE_ROWS
    rows = lambda i, be, nu, *_: (jnp.minimum(i, nu[0] - 1), 0)
    per_e = lambda i, be, *_: (be[i], 0, 0)
    return pl.pallas_call(
        _experts_body,
        grid_spec=pltpu.PrefetchScalarGridSpec(
            num_scalar_prefetch=6,
            grid=(n_blocks,),
            in_specs=[
                pl.BlockSpec((MOE_ROWS, HALF_MODEL), rows),
                pl.BlockSpec(memory_space=pl.ANY),
                pl.BlockSpec((None, 1, 2 * D_FF), per_e),
                pl.BlockSpec(memory_space=pl.ANY),
                pl.BlockSpec((None, 1, D_MODEL), per_e),
            ],
            out_specs=pl.BlockSpec((MOE_ROWS, HALF_MODEL), lambda i, *_: (i, 0)),
            scratch_shapes=[
                pltpu.VMEM((2, D_MODEL, 2 * D_FF), _f32),
                pltpu.VMEM((2, D_FF, D_MODEL), _f32),
                pltpu.SemaphoreType.DMA((2,)),
            ],
        ),
        out_shape=jax.ShapeDtypeStruct((n_rows, HALF_MODEL), jnp.uint32),
        compiler_params=pltpu.CompilerParams(
            dimension_semantics=("arbitrary",), vmem_limit_bytes=VMEM_LIMIT),
        name="experts",
    )(block_e, n_used, first, slot, next_e, half, xs, wgu, bgu, wd, bd)


def _combine(dest_rows, gates, x1s, yb):
    workers = dest_rows.shape[0]
    per_worker = dest_rows.shape[1] * LANES // TOP_K
    n_chunks = per_worker // SC_COMBINE_CHUNK
    first_worker = _worker_ranges([x.shape[0] for x in x1s], per_worker, workers)
    n_groups = len(x1s)
    assert n_chunks % 2 == 0 and LANES % SC_COMBINE_CHUNK == 0
    mesh = plsc.VectorSubcoreMesh(core_axis_name="c", subcore_axis_name="s")
    row_buf = pltpu.VMEM((SC_COMBINE_CHUNK, D_MODEL), _f32)
    packed_buf = pltpu.VMEM((SC_COMBINE_CHUNK, HALF_MODEL), jnp.uint32)

    @functools.partial(
        pl.kernel, mesh=mesh,
        out_type=[jax.ShapeDtypeStruct(x.shape, _f32) for x in x1s],
        scratch_types=[
            pltpu.VMEM(dest_rows.shape[1:], jnp.int32),
            [[packed_buf] * TOP_K + [row_buf]] * 2,
            [pltpu.VMEM((SC_COMBINE_CHUNK, TOP_K * SC_LANES), _f32)] * 2,
            [pltpu.SemaphoreType.DMA] * 2, [pltpu.SemaphoreType.DMA] * 2, [pltpu.SemaphoreType.DMA] * 2,
        ],
        compiler_params=pltpu.CompilerParams(needs_layout_passes=False),
        name="combine",
    )
    def body(dest_hbm, *refs):
        g_hbms, x_hbms = refs[:n_groups], refs[n_groups:2 * n_groups]
        yb_hbm = refs[2 * n_groups]
        o_hbms = refs[2 * n_groups + 1:3 * n_groups + 1]
        idx_v, row_bufs, gate_bufs, gather_sems, load_sems, store_sems = refs[3 * n_groups + 1:]
        wid = lax.axis_index("s") * SC_CORES + lax.axis_index("c")
        pltpu.sync_copy(dest_hbm.at[wid], idx_v)

        def run(g_hbm, x_hbm, o_hbm, base):
            def loads(c, b):
                rows = pl.ds(base + c * SC_COMBINE_CHUNK, SC_COMBINE_CHUNK)
                cps = []
                for k in range(TOP_K):
                    pos = (c * TOP_K + k) * SC_COMBINE_CHUNK
                    ids = idx_v.at[pos // LANES, pl.ds(pos % LANES, SC_COMBINE_CHUNK)]
                    cps.append(pltpu.make_async_copy(yb_hbm.at[ids], row_bufs[b][k], gather_sems[b]))
                cps.append(pltpu.make_async_copy(x_hbm.at[rows], row_bufs[b][TOP_K], load_sems[b]))
                cps.append(pltpu.make_async_copy(g_hbm.at[rows], gate_bufs[b], load_sems[b]))
                return cps

            def store(c, b):
                rows = pl.ds(base + c * SC_COMBINE_CHUNK, SC_COMBINE_CHUNK)
                return pltpu.make_async_copy(row_bufs[b][TOP_K], o_hbm.at[rows], store_sems[b])

            for cp in loads(0, 0):
                cp.start()

            @pl.loop(0, n_chunks, step=2)
            def _(c):
                for b in range(2):
                    cc = c + b

                    @pl.when(cc + 1 < n_chunks)
                    def _():
                        @pl.when(cc >= 1)
                        def _():
                            store(cc - 1, 1 - b).wait()
                        for cp in loads(cc + 1, 1 - b):
                            cp.start()

                    for cp in loads(cc, b):
                        cp.wait()
                    acc_buf, gate_buf = row_bufs[b][TOP_K], gate_bufs[b]

                    @pl.loop(0, SC_COMBINE_CHUNK)
                    def _(r):
                        g = [gate_buf.at[r, pl.ds(k * SC_LANES, SC_LANES)][...] for k in range(TOP_K)]

                        @plsc.parallel_loop(0, HALF_MODEL, step=SC_LANES, unroll=4)
                        def _(col):
                            lo = acc_buf.at[r, pl.ds(col, SC_LANES)][...]
                            hi = acc_buf.at[r, pl.ds(HALF_MODEL + col, SC_LANES)][...]
                            for k in range(TOP_K):
                                words = row_bufs[b][k].at[r, pl.ds(col, SC_LANES)][...]
                                lo = lo + g[k] * plsc.bitcast(lax.shift_left(words, jnp.uint32(16)), _f32)
                                hi = hi + g[k] * plsc.bitcast(words & jnp.uint32(_HIGH_HALF), _f32)
                            acc_buf.at[r, pl.ds(col, SC_LANES)][...] = lo
                            acc_buf.at[r, pl.ds(HALF_MODEL + col, SC_LANES)][...] = hi

                    store(cc, b).start()

            store(n_chunks - 2, 0).wait()
            store(n_chunks - 1, 1).wait()

        for g in range(n_groups):
            @pl.when(jnp.logical_and(wid >= first_worker[g], wid < first_worker[g + 1]))
            def _(g=g):
                run(g_hbms[g], x_hbms[g], o_hbms[g], (wid - first_worker[g]) * per_worker)

    return body(dest_rows, *gates, *x1s, yb)


def _rope_tables(seq):
    pos = jnp.arange(seq)
    row_ids = (pos // GRID_W).astype(_f32)
    col_ids = (pos % GRID_W).astype(_f32)
    inv_freq = ROPE_THETA ** (-jnp.arange(ROPE_PAIRS_PER_AXIS, dtype=_f32) / ROPE_PAIRS_PER_AXIS)
    ang_r = row_ids[:, None] * inv_freq
    ang_c = col_ids[:, None] * inv_freq
    cos = jnp.concatenate([jnp.cos(ang_r)] * 2 + [jnp.cos(ang_c)] * 2, axis=1)
    sin = jnp.concatenate([-jnp.sin(ang_r), jnp.sin(ang_r), -jnp.sin(ang_c), jnp.sin(ang_c)], axis=1)
    reps = LANES // HEAD_DIM
    return jnp.tile(cos, (1, reps)), jnp.tile(sin, (1, reps))


def _mixer(x, p):
    batch, seq, _ = x.shape
    x2 = x.reshape(batch * seq, D_MODEL)
    cos_t, sin_t = _rope_tables(seq)
    q, k2, vt, u, sa, sb = _in_proj(x2, p["mix_g"], p["w_in"], p["wvt"], p["qg"], p["kg"], cos_t, sin_t, p["bd"], seq)
    attn = _attention(q, k2, vt, batch, seq)
    return _mix_out(x2, attn, u, sa, sb, p["wap"], p["pgw"], p["ps"], p["wpp"], p["wo"], p["ffn_g"],
                    p["wr"], p["br"], p["tri"], seq)


def _moe(mixed, w_gu, b_gu, w_down, b_down):
    counts = [m[5][:, 0].astype(jnp.int32) for m in mixed]
    total = sum(counts)
    padded = ((total + MOE_ROWS - 1) // MOE_ROWS) * MOE_ROWS
    padded_end = jnp.cumsum(padded)
    padded_start = padded_end - padded
    n_tok = sum(m[0].shape[0] for m in mixed)
    n_blocks = (n_tok * TOP_K + MOE_ROWS - 1) // MOE_ROWS + N_EXPERTS
    n_used = (padded_end[-1] // MOE_ROWS).astype(jnp.int32)
    blk = jnp.minimum(jnp.arange(n_blocks, dtype=jnp.int32), n_used - 1)
    block_e = jnp.sum((padded_end[None, :] <= (blk * MOE_ROWS)[:, None]).astype(jnp.int32), axis=1)
    block_e = jnp.minimum(block_e, N_EXPERTS - 1)
    expert_ids = jnp.arange(N_EXPERTS, dtype=jnp.int32)[:, None, None]
    dests = []
    seen = jnp.zeros((N_EXPERTS,), jnp.int32)
    for m, c in zip(mixed, counts):
        base = (padded_start + seen)[:, None, None]
        dests.append(jnp.sum(jnp.where(m[2][None] == expert_ids, base, 0), axis=0) + m[4])
        seen = seen + c
    workers = SC_CORES * SC_SUBCORES
    per_worker = n_tok // workers
    dest4 = jnp.concatenate(
        [d.reshape(TOP_K, -1, per_worker // SC_CHUNK, SC_CHUNK).transpose(1, 2, 0, 3) for d in dests], axis=0)
    n_rows = n_blocks * MOE_ROWS
    slot = jnp.arange(MOE_ROWS, dtype=jnp.int32)[None, :]
    pad_rows = jnp.where(slot < (padded - total)[:, None], (padded_start + total)[:, None] + slot,
                         n_rows + jnp.arange(N_EXPERTS, dtype=jnp.int32)[:, None] * MOE_ROWS + slot)
    pad3 = pad_rows.astype(jnp.int32).reshape(workers, -1, SC_CHUNK)
    xs_buf = _dispatch(dest4, pad3, [m[1] for m in mixed], n_rows + N_EXPERTS * MOE_ROWS)
    blk_id = jnp.arange(n_blocks, dtype=jnp.int32)
    first = jnp.logical_and(block_e != jnp.concatenate([jnp.full((1,), -1, jnp.int32), block_e[:-1]]),
                            blk_id < n_used).astype(jnp.int32)
    slot = (jnp.cumsum(first) - 1) % 2
    group_end = jnp.sum(jnp.where(block_e[:, None] == jnp.arange(N_EXPERTS, dtype=jnp.int32)[None, :],
                                  (padded_end // MOE_ROWS)[None, :], 0), axis=1)
    next_e = jnp.sum(jnp.where(group_end[:, None] == blk_id[None, :], block_e[None, :], 0), axis=1)
    next_e = jnp.where(group_end < n_used, next_e, -1)
    token_end = jnp.sum(jnp.where(block_e[:, None] == jnp.arange(N_EXPERTS, dtype=jnp.int32)[None, :],
                                  (padded_start + total)[None, :], 0), axis=1)
    half = (token_end - blk_id * MOE_ROWS <= MOE_ROWS // 2).astype(jnp.int32)
    yb = _experts(block_e, n_used.reshape(1), first, slot.astype(jnp.int32), next_e.astype(jnp.int32), half, xs_buf,
                  w_gu, b_gu.reshape(N_EXPERTS, 1, -1), w_down, b_down.reshape(N_EXPERTS, 1, -1))
    dest_rows = jnp.concatenate(
        [d.reshape(TOP_K, -1, per_worker // SC_COMBINE_CHUNK, SC_COMBINE_CHUNK).transpose(1, 2, 0, 3)
         .reshape(-1, per_worker * TOP_K // LANES, LANES) for d in dests], axis=0)
    gate_vecs = [jnp.repeat(m[3].T, SC_LANES, axis=1) for m in mixed]
    return _combine(dest_rows, gate_vecs, [m[0] for m in mixed], yb)


def kernel(x_prompt, x_sample, mix_norm_g, w_in, q_norm_g, k_norm_g, w_attn_proj, pool_group_w, pool_scale,
           w_pool_proj, w_out, ffn_norm_g, w_router, b_router, w_gu, b_gu, w_down, b_down):
    depth = w_in.shape[0]
    xs_all = [x_prompt, x_sample]
    head_id = jnp.arange(ATTN_WIDTH) // HEAD_DIM
    block_diag = (head_id[:, None] == head_id[None, :]).astype(_bf16)
    tri_id = jnp.arange(MIX_ROWS)
    tri = (tri_id[:, None] < tri_id[None, :]).astype(_bf16)
    for l in range(depth):
        wr = jnp.pad(w_router[l].astype(_f32), ((0, 0), (0, LANES - N_EXPERTS)))
        wrh = wr.astype(_bf16)
        p = dict(
            mix_g=mix_norm_g[l].reshape(1, D_MODEL),
            w_in=w_in[l].astype(_bf16),
            wvt=w_in[l][:, ATTN_WIDTH + KV_WIDTH:ATTN_WIDTH + 2 * KV_WIDTH].T.astype(_bf16),
            qg=jnp.tile(q_norm_g[l] * (HEAD_DIM ** -0.5 * LOG2_E), N_HEADS).reshape(1, ATTN_WIDTH),
            kg=jnp.tile(k_norm_g[l], N_KV_HEADS).reshape(1, KV_WIDTH),
            bd=block_diag,
            wap=w_attn_proj[l].astype(_bf16),
            pgw=pool_group_w[l].astype(_bf16),
            ps=pool_scale[l].reshape(1, POOL_WIDTH),
            wpp=w_pool_proj[l].astype(_bf16),
            wo=w_out[l].astype(_bf16),
            ffn_g=ffn_norm_g[l].reshape(1, D_MODEL),
            wr=jnp.concatenate([wrh, (wr - wrh.astype(_f32)).astype(_bf16)], axis=1),
            br=b_router[l].astype(_f32).reshape(N_EXPERTS, 1),
            tri=tri,
        )
        mixed = [_mixer(x, p) for x in xs_all]
        outs = [_moe([m], w_gu[l], b_gu[l], w_down[l], b_down[l])[0] for m in mixed]
        xs_all = [o.reshape(x.shape) for o, x in zip(outs, xs_all)]
    return tuple(xs_all)
```

```python
import functools

import jax
import jax.numpy as jnp
from jax import lax
from jax.experimental import pallas as pl
from jax.experimental.pallas import tpu as pltpu
from jax.experimental.pallas import tpu_sc as plsc

D_MODEL = 1024
HALF_MODEL = D_MODEL // 2
GRID_W = 64
N_HEADS = 8
N_KV_HEADS = 2
HEAD_DIM = 64
N_GROUPS_PER_KV = N_HEADS // N_KV_HEADS
ATTN_WIDTH = N_HEADS * HEAD_DIM
KV_WIDTH = N_KV_HEADS * HEAD_DIM
ROPE_THETA = 10000.0
ROPE_PAIRS_PER_AXIS = HEAD_DIM // 4
POOL_WINDOWS = (2, 4, 8, 16)
N_POOL_GROUPS = 4
POOL_WIDTH = 512
POOL_GROUP_DIM = POOL_WIDTH // N_POOL_GROUPS
N_EXPERTS = 32
TOP_K = 4
D_FF = 1024
SWIGLU_ALPHA = 1.702
SWIGLU_LIMIT = 7.0
NORM_EPS = 1e-6

LANES = 128
SUBLANES = 8
POOL_HALO = 8
LOG2_E = 1.4426950408889634

IN_PROJ_ROWS = 1024
ATTN_Q_ROWS = 256
ATTN_KEY_CHUNK = 256
MIX_ROWS = 1024
MOE_ROWS = 512
SC_CORES = 2
SC_SUBCORES = 16
SC_LANES = 16
SC_CHUNK = 32
SC_COMBINE_CHUNK = 16
V7X_VMEM_BYTES = 64 * 1024 * 1024
VMEM_LIMIT = V7X_VMEM_BYTES * 7 // 8

_bf16 = jnp.bfloat16
_f32 = jnp.float32


def _dot(a, b):
    return jnp.dot(a, b, preferred_element_type=_f32)


def _split_bf16(x):
    hi = x.astype(_bf16)
    lo = (x - hi.astype(_f32)).astype(_bf16)
    return hi, lo


def _sigmoid(x):
    return 1.0 / (1.0 + jnp.exp(-x))


_HIGH_HALF = 0xFFFF0000


def _pack_bf16_pair(lo, hi):
    lo_bits = lax.bitcast_convert_type(lo.astype(_bf16).astype(_f32), jnp.uint32)
    hi_bits = lax.bitcast_convert_type(hi.astype(_bf16).astype(_f32), jnp.uint32)
    return lax.shift_right_logical(lo_bits, jnp.uint32(16)) | (hi_bits & jnp.uint32(_HIGH_HALF))


def _unpack_bf16_pair(words):
    lo = lax.bitcast_convert_type(lax.shift_left(words, jnp.uint32(16)), _f32)
    hi = lax.bitcast_convert_type(words & jnp.uint32(_HIGH_HALF), _f32)
    return lo, hi


def _in_proj_body(x_ref, g_ref, w_ref, wvt_ref, qg_ref, kg_ref, cos_ref, sin_ref, bd_ref,
                  q_ref, k_ref, vt_ref, u_ref, sa_ref, sb_ref):
    x = x_ref[...]
    ms = jnp.mean(x * x, axis=-1, keepdims=True)
    h = (x * lax.rsqrt(ms + NORM_EPS) * g_ref[...]).astype(_bf16)
    cos = cos_ref[...]
    sin = sin_ref[...]
    lane = lax.broadcasted_iota(jnp.int32, cos.shape, 1)
    first_half = (lane % (2 * ROPE_PAIRS_PER_AXIS)) < ROPE_PAIRS_PER_AXIS

    def head_norm_rope(z, gain):
        width = z.shape[1]
        ss = _dot((z * z).astype(_bf16), bd_ref[:width, :width])
        zn = z * lax.rsqrt(ss * (1.0 / HEAD_DIM) + NORM_EPS) * gain
        outs = []
        for j in range(width // LANES):
            c = zn[:, j * LANES:(j + 1) * LANES]
            partner = jnp.where(first_half,
                                pltpu.roll(c, LANES - ROPE_PAIRS_PER_AXIS, 1),
                                pltpu.roll(c, ROPE_PAIRS_PER_AXIS, 1))
            outs.append(c * cos + partner * sin)
        return outs

    c0 = 0
    zq = _dot(h, w_ref[:, c0:c0 + ATTN_WIDTH])
    for j, o in enumerate(head_norm_rope(zq, qg_ref[...])):
        q_ref[:, j * LANES:(j + 1) * LANES] = o.astype(_bf16)
    c0 += ATTN_WIDTH
    zk = _dot(h, w_ref[:, c0:c0 + KV_WIDTH])
    (kr,) = head_norm_rope(zk, kg_ref[...])
    for j in range(N_KV_HEADS):
        k_ref[j] = kr[:, j * HEAD_DIM:(j + 1) * HEAD_DIM].astype(_bf16)
    c0 += KV_WIDTH
    zvt = lax.dot_general(wvt_ref[...], h, (((1,), (1,)), ((), ())), preferred_element_type=_f32)
    for j in range(N_KV_HEADS):
        vt_ref[j] = zvt[j * HEAD_DIM:(j + 1) * HEAD_DIM, :].astype(_bf16)
    c0 += KV_WIDTH
    u_ref[...] = _dot(h, w_ref[:, c0:c0 + POOL_WIDTH]).astype(_bf16)
    c0 += POOL_WIDTH
    sa_ref[...] = _sigmoid(_dot(h, w_ref[:, c0:c0 + D_MODEL])).astype(_bf16)
    c0 += D_MODEL
    sb_ref[...] = _sigmoid(_dot(h, w_ref[:, c0:c0 + D_MODEL])).astype(_bf16)


def _in_proj(x2, norm_g, w_in, wvt, qg, kg, cos_t, sin_t, bd, seq):
    tokens = x2.shape[0]
    tm = IN_PROJ_ROWS
    in_width = w_in.shape[1]
    tiles_per_seq = seq // tm
    const = lambda i: (0, 0)
    row = lambda i: (i, 0)
    return pl.pallas_call(
        _in_proj_body,
        grid=(tokens // tm,),
        in_specs=[
            pl.BlockSpec((tm, D_MODEL), row),
            pl.BlockSpec((1, D_MODEL), const),
            pl.BlockSpec((D_MODEL, in_width), const),
            pl.BlockSpec((KV_WIDTH, D_MODEL), const),
            pl.BlockSpec((1, ATTN_WIDTH), const),
            pl.BlockSpec((1, KV_WIDTH), const),
            pl.BlockSpec((tm, LANES), lambda i: (i % tiles_per_seq, 0)),
            pl.BlockSpec((tm, LANES), lambda i: (i % tiles_per_seq, 0)),
            pl.BlockSpec((ATTN_WIDTH, ATTN_WIDTH), const),
        ],
        out_specs=[
            pl.BlockSpec((tm, ATTN_WIDTH), row),
            pl.BlockSpec((N_KV_HEADS, tm, HEAD_DIM), lambda i: (0, i, 0)),
            pl.BlockSpec((N_KV_HEADS, HEAD_DIM, tm), lambda i: (0, 0, i)),
            pl.BlockSpec((tm, POOL_WIDTH), row),
            pl.BlockSpec((tm, D_MODEL), row),
            pl.BlockSpec((tm, D_MODEL), row),
        ],
        out_shape=[
            jax.ShapeDtypeStruct((tokens, ATTN_WIDTH), _bf16),
            jax.ShapeDtypeStruct((N_KV_HEADS, tokens, HEAD_DIM), _bf16),
            jax.ShapeDtypeStruct((N_KV_HEADS, HEAD_DIM, tokens), _bf16),
            jax.ShapeDtypeStruct((tokens, POOL_WIDTH), _bf16),
            jax.ShapeDtypeStruct((tokens, D_MODEL), _bf16),
            jax.ShapeDtypeStruct((tokens, D_MODEL), _bf16),
        ],
        compiler_params=pltpu.CompilerParams(
            dimension_semantics=("parallel",), vmem_limit_bytes=VMEM_LIMIT),
        name="in_proj",
    )(x2, norm_g, w_in, wvt, qg, kg, cos_t, sin_t, bd)


def _attention_step(q_ref, k_ref, vt_ref, o_ref, st_new, m_new, st_old, m_old):
    tq = q_ref.shape[0]
    q = q_ref[...]
    qs = jnp.concatenate([q[:, g * HEAD_DIM:(g + 1) * HEAD_DIM] for g in range(N_GROUPS_PER_KV)], axis=0)
    seq = k_ref.shape[0]
    m_prev = m_old[...]
    m_run = denom = ot = None
    for c0 in range(0, seq, ATTN_KEY_CHUNK):
        rows = slice(c0, c0 + ATTN_KEY_CHUNK)
        st = lax.dot_general(k_ref[rows, :], qs, (((1,), (1,)), ((), ())), preferred_element_type=_f32)
        st_new[rows, :] = st
        m_c = jnp.max(st, axis=0, keepdims=True)
        m_run = m_c if m_run is None else jnp.maximum(m_run, m_c)
        p = jnp.exp2(st_old[rows, :] - m_prev)
        l_c = jnp.sum(p, axis=0, keepdims=True)
        o_c = _dot(vt_ref[:, rows], p.astype(_bf16))
        denom = l_c if denom is None else denom + l_c
        ot = o_c if ot is None else ot + o_c
    m_new[...] = m_run
    ot = ot * (1.0 / denom)
    stacked = jnp.concatenate([ot[:, g * tq:(g + 1) * tq] for g in range(N_GROUPS_PER_KV)], axis=0)
    o_ref[...] = stacked.T.astype(_bf16)


def _attention_body(q_ref, k_ref, vt_ref, o_ref, st_a, m_a, st_b, m_b):
    n = pl.program_id(0)

    @pl.when(n == 0)
    def _():
        st_b[...] = jnp.zeros_like(st_b)
        m_b[...] = jnp.zeros_like(m_b)

    @pl.when(n % 2 == 0)
    def _():
        _attention_step(q_ref, k_ref, vt_ref, o_ref, st_a, m_a, st_b, m_b)

    @pl.when(n % 2 == 1)
    def _():
        _attention_step(q_ref, k_ref, vt_ref, o_ref, st_b, m_b, st_a, m_a)


def _attention(q, k2, vt, batch, seq):
    tq = ATTN_Q_ROWS
    nq = seq // tq
    n_blocks = batch * N_KV_HEADS * nq
    group_width = N_GROUPS_PER_KV * HEAD_DIM
    lanes = N_GROUPS_PER_KV * tq

    def decode(n):
        return n // (N_KV_HEADS * nq), (n // nq) % N_KV_HEADS, n % nq

    def scores_of(n):
        return decode(jnp.minimum(n, n_blocks - 1))

    def finish_of(n):
        return decode(jnp.maximum(n - 1, 0))

    def q_map(n):
        b, kh, i = scores_of(n)
        return b * nq + i, kh

    def k_map(n):
        b, kh, _ = scores_of(n)
        return kh, b, 0

    def vt_map(n):
        b, kh, _ = finish_of(n)
        return kh, 0, b

    def o_map(n):
        b, kh, i = finish_of(n)
        return b * nq + i, kh

    return pl.pallas_call(
        _attention_body,
        grid=(n_blocks + 1,),
        in_specs=[
            pl.BlockSpec((tq, group_width), q_map),
            pl.BlockSpec((None, seq, HEAD_DIM), k_map),
            pl.BlockSpec((None, HEAD_DIM, seq), vt_map),
        ],
        out_specs=pl.BlockSpec((tq, group_width), o_map),
        out_shape=jax.ShapeDtypeStruct((batch * seq, ATTN_WIDTH), _bf16),
        scratch_shapes=[
            pltpu.VMEM((seq, lanes), _f32), pltpu.VMEM((1, lanes), _f32),
            pltpu.VMEM((seq, lanes), _f32), pltpu.VMEM((1, lanes), _f32),
        ],
        compiler_params=pltpu.CompilerParams(
            dimension_semantics=("arbitrary",), vmem_limit_bytes=VMEM_LIMIT),
        name="attention",
    )(q, k2, vt)


def _mix_out_body(seq, x_ref, a_ref, up_ref, u_ref, un_ref, sa_ref, sb_ref,
                  wap_ref, pgw_ref, ps_ref, wpp_ref, wo_ref, g2_ref, wr_ref, br_ref, tri_ref, anchor_ref,
                  x1_ref, h2_ref, idx_ref, gate_ref, rank_ref, cnt_ref,
                  ext_ref, lv_ref, carry_ref):
    del anchor_ref
    i = pl.program_id(0)
    tm = x_ref.shape[0]

    @pl.when(i == 0)
    def _():
        carry_ref[...] = jnp.zeros_like(carry_ref)
        ext_ref[...] = jnp.zeros_like(ext_ref)
        lv_ref[...] = jnp.zeros_like(lv_ref)

    start = (i * tm) % seq
    has_prev = (start != 0).astype(_f32)
    has_next = (start + tm != seq).astype(_f32)
    first = 2 * POOL_HALO
    ext_ref[POOL_HALO:first, :] = up_ref[...].astype(_f32) * has_prev
    ext_ref[first:first + tm, :] = u_ref[...].astype(_f32)
    ext_ref[first + tm:first + tm + POOL_HALO, :] = un_ref[...].astype(_f32) * has_next
    y_attn = _dot(a_ref[...], wap_ref[...])
    lo, n = POOL_HALO, tm + 2 * POOL_HALO
    pos = start + lax.broadcasted_iota(jnp.int32, (tm, 1), 0)
    pooled = []
    for g, w in enumerate(POOL_WINDOWS):
        half = w // 2
        assert half == 1 << g
        wide = slice(g * POOL_GROUP_DIM, POOL_WIDTH)
        cols = slice(g * POOL_GROUP_DIM, (g + 1) * POOL_GROUP_DIM)
        if g == 0:
            lv_ref[0, lo:lo + n, wide] = ext_ref[pl.ds(lo - 1, n), wide] + ext_ref[pl.ds(lo, n), wide]
        elif g < N_POOL_GROUPS - 1:
            lv_ref[g, lo:lo + n, wide] = (lv_ref[g - 1, pl.ds(lo - half // 2, n), wide]
                                          + lv_ref[g - 1, pl.ds(lo + half // 2, n), wide])
        if g < N_POOL_GROUPS - 1:
            tot = lv_ref[g, first:first + tm, cols]
        else:
            tot = (lv_ref[g - 1, pl.ds(first - half // 2, tm), cols]
                   + lv_ref[g - 1, pl.ds(first + half // 2, tm), cols])
        cnt = (jnp.minimum(pos + half, seq) - jnp.maximum(pos - half, 0)).astype(_f32)
        diff = tot / cnt - ext_ref[first:first + tm, cols]
        pooled.append(_dot(diff.astype(_bf16), pgw_ref[g]))
    pooled = (jnp.concatenate(pooled, axis=1) * ps_ref[...]).astype(_bf16)
    y_pool = _dot(pooled, wpp_ref[...])
    merged = sa_ref[...].astype(_f32) * y_attn + sb_ref[...].astype(_f32) * y_pool
    x1 = x_ref[...] + _dot(merged.astype(_bf16), wo_ref[...])
    x1_ref[...] = x1

    ms = jnp.mean(x1 * x1, axis=-1, keepdims=True)
    h2 = x1 * lax.rsqrt(ms + NORM_EPS) * g2_ref[...]
    h2_ref[...] = _pack_bf16_pair(h2[:, :HALF_MODEL], h2[:, HALF_MODEL:])
    hi, lo = _split_bf16(h2)
    both = _dot(hi, wr_ref[...])
    logits = both[:, :LANES] + both[:, LANES:] + _dot(lo, wr_ref[:, :LANES])
    lt = logits.T[:N_EXPERTS, :] + br_ref[...]
    eid = lax.broadcasted_iota(jnp.int32, lt.shape, 0)
    vals, idxs = [], []
    multi_hot = jnp.zeros(lt.shape, _f32)
    for _ in range(TOP_K):
        m = jnp.max(lt, axis=0, keepdims=True)
        sel = jnp.min(jnp.where(lt == m, eid, N_EXPERTS), axis=0, keepdims=True)
        hit = eid == sel
        vals.append(m)
        idxs.append(sel)
        multi_hot = multi_hot + hit.astype(_f32)
        lt = jnp.where(hit, -jnp.inf, lt)
    es = [jnp.exp(v - vals[0]) for v in vals]
    inv = 1.0 / (es[0] + es[1] + es[2] + es[3])
    gate_ref[...] = jnp.concatenate([e * inv for e in es], axis=0)
    idx_ref[...] = jnp.concatenate(idxs, axis=0)
    before = _dot(multi_hot.astype(_bf16), tri_ref[...]) + carry_ref[:, 0:1]
    ranks = [jnp.sum(jnp.where(eid == sel, before, 0.0), axis=0, keepdims=True) for sel in idxs]
    rank_ref[...] = jnp.concatenate(ranks, axis=0).astype(jnp.int32)
    carry_ref[...] = carry_ref[...] + jnp.sum(multi_hot, axis=1, keepdims=True)
    cnt_ref[...] = carry_ref[...]


def _mix_out(x2, attn, u, sa, sb, wap, pgw, ps, wpp, wo, g2, wr, br, tri, seq, anchor):
    tokens = x2.shape[0]
    tm = MIX_ROWS
    n = tokens // tm
    halo_blocks = tm // POOL_HALO
    last_halo = tokens // POOL_HALO - 1
    row = lambda i: (i, 0)
    const2 = lambda i: (0, 0)
    const3 = lambda i: (0, 0, 0)
    col = lambda i: (0, i)
    return pl.pallas_call(
        functools.partial(_mix_out_body, seq),
        grid=(n,),
        in_specs=[
            pl.BlockSpec((tm, D_MODEL), row),
            pl.BlockSpec((tm, ATTN_WIDTH), row),
            pl.BlockSpec((POOL_HALO, POOL_WIDTH), lambda i: (jnp.maximum(i * halo_blocks - 1, 0), 0)),
            pl.BlockSpec((tm, POOL_WIDTH), row),
            pl.BlockSpec((POOL_HALO, POOL_WIDTH), lambda i: (jnp.minimum((i + 1) * halo_blocks, last_halo), 0)),
            pl.BlockSpec((tm, D_MODEL), row),
            pl.BlockSpec((tm, D_MODEL), row),
            pl.BlockSpec((ATTN_WIDTH, D_MODEL), const2),
            pl.BlockSpec((N_POOL_GROUPS, POOL_GROUP_DIM, POOL_GROUP_DIM), const3),
            pl.BlockSpec((1, POOL_WIDTH), const2),
            pl.BlockSpec((POOL_WIDTH, D_MODEL), const2),
            pl.BlockSpec((D_MODEL, D_MODEL), const2),
            pl.BlockSpec((1, D_MODEL), const2),
            pl.BlockSpec((D_MODEL, 2 * LANES), const2),
            pl.BlockSpec((N_EXPERTS, 1), const2),
            pl.BlockSpec((tm, tm), const2),
            pl.BlockSpec(memory_space=pl.ANY),
        ],
        out_specs=[
            pl.BlockSpec((tm, D_MODEL), row),
            pl.BlockSpec((tm, HALF_MODEL), row),
            pl.BlockSpec((TOP_K, tm), col),
            pl.BlockSpec((TOP_K, tm), col),
            pl.BlockSpec((TOP_K, tm), col),
            pl.BlockSpec((N_EXPERTS, LANES), const2),
        ],
        out_shape=[
            jax.ShapeDtypeStruct((tokens, D_MODEL), _f32),
            jax.ShapeDtypeStruct((tokens, HALF_MODEL), jnp.uint32),
            jax.ShapeDtypeStruct((TOP_K, tokens), jnp.int32),
            jax.ShapeDtypeStruct((TOP_K, tokens), _f32),
            jax.ShapeDtypeStruct((TOP_K, tokens), jnp.int32),
            jax.ShapeDtypeStruct((N_EXPERTS, LANES), _f32),
        ],
        scratch_shapes=[
            pltpu.VMEM((tm + 4 * POOL_HALO, POOL_WIDTH), _f32),
            pltpu.VMEM((N_POOL_GROUPS - 1, tm + 4 * POOL_HALO, POOL_WIDTH), _f32),
            pltpu.VMEM((N_EXPERTS, LANES), _f32),
        ],
        compiler_params=pltpu.CompilerParams(
            dimension_semantics=("arbitrary",), vmem_limit_bytes=VMEM_LIMIT),
        name="mix_out",
    )(x2, attn, u, u, u, sa, sb, wap, pgw, ps, wpp, wo, g2, wr, br, tri, anchor)


def _worker_ranges(group_tokens, per_worker, workers):
    first = [0]
    for t in group_tokens:
        assert t % per_worker == 0
        first.append(first[-1] + t // per_worker)
    assert first[-1] == workers
    return first


def _dispatch(dest4, pad3, hs, n_rows):
    workers, n_chunks = dest4.shape[0], dest4.shape[1]
    n_pad_chunks = pad3.shape[1]
    per_worker = n_chunks * SC_CHUNK
    first_worker = _worker_ranges([h.shape[0] for h in hs], per_worker, workers)
    width, dtype = hs[0].shape[1], hs[0].dtype
    assert n_chunks % 2 == 0
    mesh = plsc.VectorSubcoreMesh(core_axis_name="c", subcore_axis_name="s")

    @functools.partial(
        pl.kernel, mesh=mesh,
        out_type=jax.ShapeDtypeStruct((n_rows, width), dtype),
        scratch_types=[
            pltpu.VMEM((n_chunks, TOP_K, SC_CHUNK), jnp.int32),
            pltpu.VMEM((n_pad_chunks, SC_CHUNK), jnp.int32),
            pltpu.VMEM((SC_CHUNK, width), dtype),
            pltpu.VMEM((SC_CHUNK, width), dtype),
            pltpu.SemaphoreType.DMA,
            pltpu.SemaphoreType.DMA,
            pltpu.SemaphoreType.DMA,
        ],
        name="dispatch",
    )
    def body(dest_hbm, pad_hbm, *refs):
        h_hbms = refs[:len(hs)]
        out_hbm, idx_v, pad_v, rows0, rows1, load_sem0, load_sem1, store_sem = refs[len(hs):]
        wid = lax.axis_index("s") * SC_CORES + lax.axis_index("c")
        pltpu.sync_copy(dest_hbm.at[wid], idx_v)
        pltpu.sync_copy(pad_hbm.at[wid], pad_v)

        zero = jnp.zeros((SC_LANES,), dtype)

        @pl.loop(0, SC_CHUNK)
        def _(r):
            @pl.loop(0, width, step=SC_LANES)
            def _(c):
                rows0.at[r, pl.ds(c, SC_LANES)][...] = zero

        @pl.loop(0, n_pad_chunks)
        def _(c):
            pltpu.sync_copy(rows0, out_hbm.at[pad_v.at[c]])

        def scatter_tokens(src_hbm, base):
            bufs = (rows0, rows1)
            load_sems = (load_sem0, load_sem1)

            def load(c, b):
                return pltpu.make_async_copy(src_hbm.at[pl.ds(base + c * SC_CHUNK, SC_CHUNK)], bufs[b], load_sems[b])

            load(0, 0).start()

            @pl.loop(0, n_chunks, step=2)
            def _(c):
                for b in range(2):
                    cc = c + b
                    load(cc, b).wait()

                    @pl.when(cc + 1 < n_chunks)
                    def _():
                        load(cc + 1, 1 - b).start()

                    copies = [pltpu.make_async_copy(bufs[b], out_hbm.at[idx_v.at[cc, k]], store_sem)
                              for k in range(TOP_K)]
                    for cp in copies:
                        cp.start()
                    for cp in copies:
                        cp.wait()

        for g, h_hbm in enumerate(h_hbms):
            @pl.when(jnp.logical_and(wid >= first_worker[g], wid < first_worker[g + 1]))
            def _(g=g, h_hbm=h_hbm):
                scatter_tokens(h_hbm, (wid - first_worker[g]) * per_worker)

    return body(dest4, pad3, *hs)


def _experts_body(be_ref, nu_ref, first_ref, slot_ref, next_ref, half_ref, xs_ref, wgu_hbm, bgu_ref, wd_hbm, bd_ref,
                  y_ref, wgu_buf, wd_buf, sems):
    i = pl.program_id(0)
    used = i < nu_ref[0]
    slot = slot_ref[i]

    def weight_copies(e, s):
        return (pltpu.make_async_copy(wgu_hbm.at[e], wgu_buf.at[s], sems.at[s]),
                pltpu.make_async_copy(wd_hbm.at[e], wd_buf.at[s], sems.at[s]))

    @pl.when(i == 0)
    def _():
        for cp in weight_copies(be_ref[0], 0):
            cp.start()

    @pl.when(first_ref[i] == 1)
    def _():
        for cp in weight_copies(be_ref[i], slot):
            cp.wait()

        @pl.when(next_ref[i] >= 0)
        def _():
            for cp in weight_copies(next_ref[i], 1 - slot):
                cp.start()

    @pl.when(jnp.logical_not(used))
    def _():
        y_ref[...] = jnp.zeros_like(y_ref)

    def run_rows(n):
        x = jnp.concatenate(_unpack_bf16_pair(xs_ref[:n, :]), axis=1).astype(_bf16)
        gu = _dot(x, wgu_buf[slot].astype(_bf16)) + bgu_ref[...]
        gate = jnp.minimum(gu[:, :D_FF], SWIGLU_LIMIT)
        up = jnp.clip(gu[:, D_FF:], -SWIGLU_LIMIT, SWIGLU_LIMIT)
        act = (up + 1.0) * (gate * _sigmoid(SWIGLU_ALPHA * gate))
        y = _dot(act.astype(_bf16), wd_buf[slot].astype(_bf16)) + bd_ref[...]
        y_ref[:n, :] = _pack_bf16_pair(y[:, :HALF_MODEL], y[:, HALF_MODEL:])

    lower_only = half_ref[i] == 1

    @pl.when(jnp.logical_and(used, jnp.logical_not(lower_only)))
    def _():
        run_rows(MOE_ROWS)

    @pl.when(jnp.logical_and(used, lower_only))
    def _():
        run_rows(MOE_ROWS // 2)
        y_ref[MOE_ROWS // 2:, :] = jnp.zeros((MOE_ROWS // 2, HALF_MODEL), jnp.uint32)


def _experts(block_e, n_used, first, slot, next_e, half, xs, wgu, bgu, wd, bd):
    n_blocks = block_e.shape[0]
    n_rows = n_blocks * MOE_ROWS
    rows = lambda i, be, nu, *_: (jnp.minimum(i, nu[0] - 1), 0)
    per_e = lambda i, be, *_: (be[i], 0, 0)
    return pl.pallas_call(
        _experts_body,
        grid_spec=pltpu.PrefetchScalarGridSpec(
            num_scalar_prefetch=6,
            grid=(n_blocks,),
            in_specs=[
                pl.BlockSpec((MOE_ROWS, HALF_MODEL), rows),
                pl.BlockSpec(memory_space=pl.ANY),
                pl.BlockSpec((None, 1, 2 * D_FF), per_e),
                pl.BlockSpec(memory_space=pl.ANY),
                pl.BlockSpec((None, 1, D_MODEL), per_e),
            ],
            out_specs=pl.BlockSpec((MOE_ROWS, HALF_MODEL), lambda i, *_: (i, 0)),
            scratch_shapes=[
                pltpu.VMEM((2, D_MODEL, 2 * D_FF), _f32),
                pltpu.VMEM((2, D_FF, D_MODEL), _f32),
                pltpu.SemaphoreType.DMA((2,)),
            ],
        ),
        out_shape=jax.ShapeDtypeStruct((n_rows, HALF_MODEL), jnp.uint32),
        compiler_params=pltpu.CompilerParams(
            dimension_semantics=("arbitrary",), vmem_limit_bytes=VMEM_LIMIT),
        name="experts",
    )(block_e, n_used, first, slot, next_e, half, xs, wgu, bgu, wd, bd)


def _combine(dest_rows, gates, x1s, yb):
    workers = dest_rows.shape[0]
    per_worker = dest_rows.shape[1] * LANES // TOP_K
    n_chunks = per_worker // SC_COMBINE_CHUNK
    first_worker = _worker_ranges([x.shape[0] for x in x1s], per_worker, workers)
    n_groups = len(x1s)
    assert n_chunks % 2 == 0 and LANES % SC_COMBINE_CHUNK == 0
    mesh = plsc.VectorSubcoreMesh(core_axis_name="c", subcore_axis_name="s")
    row_buf = pltpu.VMEM((SC_COMBINE_CHUNK, D_MODEL), _f32)
    packed_buf = pltpu.VMEM((SC_COMBINE_CHUNK, HALF_MODEL), jnp.uint32)

    @functools.partial(
        pl.kernel, mesh=mesh,
        out_type=[jax.ShapeDtypeStruct(x.shape, _f32) for x in x1s],
        scratch_types=[
            pltpu.VMEM(dest_rows.shape[1:], jnp.int32),
            [[packed_buf] * TOP_K + [row_buf]] * 2,
            [pltpu.VMEM((SC_COMBINE_CHUNK, TOP_K * SC_LANES), _f32)] * 2,
            [pltpu.SemaphoreType.DMA] * 2, [pltpu.SemaphoreType.DMA] * 2, [pltpu.SemaphoreType.DMA] * 2,
        ],
        compiler_params=pltpu.CompilerParams(needs_layout_passes=False),
        name="combine",
    )
    def body(dest_hbm, *refs):
        g_hbms, x_hbms = refs[:n_groups], refs[n_groups:2 * n_groups]
        yb_hbm = refs[2 * n_groups]
        o_hbms = refs[2 * n_groups + 1:3 * n_groups + 1]
        idx_v, row_bufs, gate_bufs, gather_sems, load_sems, store_sems = refs[3 * n_groups + 1:]
        wid = lax.axis_index("s") * SC_CORES + lax.axis_index("c")
        pltpu.sync_copy(dest_hbm.at[wid], idx_v)

        def run(g_hbm, x_hbm, o_hbm, base):
            def loads(c, b):
                rows = pl.ds(base + c * SC_COMBINE_CHUNK, SC_COMBINE_CHUNK)
                cps = []
                for k in range(TOP_K):
                    pos = (c * TOP_K + k) * SC_COMBINE_CHUNK
                    ids = idx_v.at[pos // LANES, pl.ds(pos % LANES, SC_COMBINE_CHUNK)]
                    cps.append(pltpu.make_async_copy(yb_hbm.at[ids], row_bufs[b][k], gather_sems[b]))
                cps.append(pltpu.make_async_copy(x_hbm.at[rows], row_bufs[b][TOP_K], load_sems[b]))
                cps.append(pltpu.make_async_copy(g_hbm.at[rows], gate_bufs[b], load_sems[b]))
                return cps

            def store(c, b):
                rows = pl.ds(base + c * SC_COMBINE_CHUNK, SC_COMBINE_CHUNK)
                return pltpu.make_async_copy(row_bufs[b][TOP_K], o_hbm.at[rows], store_sems[b])

            for cp in loads(0, 0):
                cp.start()

            @pl.loop(0, n_chunks, step=2)
            def _(c):
                for b in range(2):
                    cc = c + b

                    @pl.when(cc + 1 < n_chunks)
                    def _():
                        @pl.when(cc >= 1)
                        def _():
                            store(cc - 1, 1 - b).wait()
                        for cp in loads(cc + 1, 1 - b):
                            cp.start()

                    for cp in loads(cc, b):
                        cp.wait()
                    acc_buf, gate_buf = row_bufs[b][TOP_K], gate_bufs[b]

                    @pl.loop(0, SC_COMBINE_CHUNK)
                    def _(r):
                        g = [gate_buf.at[r, pl.ds(k * SC_LANES, SC_LANES)][...] for k in range(TOP_K)]

                        @plsc.parallel_loop(0, HALF_MODEL, step=SC_LANES, unroll=4)
                        def _(col):
                            lo = acc_buf.at[r, pl.ds(col, SC_LANES)][...]
                            hi = acc_buf.at[r, pl.ds(HALF_MODEL + col, SC_LANES)][...]
                            for k in range(TOP_K):
                                words = row_bufs[b][k].at[r, pl.ds(col, SC_LANES)][...]
                                lo = lo + g[k] * plsc.bitcast(lax.shift_left(words, jnp.uint32(16)), _f32)
                                hi = hi + g[k] * plsc.bitcast(words & jnp.uint32(_HIGH_HALF), _f32)
                            acc_buf.at[r, pl.ds(col, SC_LANES)][...] = lo
                            acc_buf.at[r, pl.ds(HALF_MODEL + col, SC_LANES)][...] = hi

                    store(cc, b).start()

            store(n_chunks - 2, 0).wait()
            store(n_chunks - 1, 1).wait()

        for g in range(n_groups):
            @pl.when(jnp.logical_and(wid >= first_worker[g], wid < first_worker[g + 1]))
            def _(g=g):
                run(g_hbms[g], x_hbms[g], o_hbms[g], (wid - first_worker[g]) * per_worker)

    return body(dest_rows, *gates, *x1s, yb)


def _rope_tables(seq):
    pos = jnp.arange(seq)
    row_ids = (pos // GRID_W).astype(_f32)
    col_ids = (pos % GRID_W).astype(_f32)
    inv_freq = ROPE_THETA ** (-jnp.arange(ROPE_PAIRS_PER_AXIS, dtype=_f32) / ROPE_PAIRS_PER_AXIS)
    ang_r = row_ids[:, None] * inv_freq
    ang_c = col_ids[:, None] * inv_freq
    cos = jnp.concatenate([jnp.cos(ang_r)] * 2 + [jnp.cos(ang_c)] * 2, axis=1)
    sin = jnp.concatenate([-jnp.sin(ang_r), jnp.sin(ang_r), -jnp.sin(ang_c), jnp.sin(ang_c)], axis=1)
    reps = LANES // HEAD_DIM
    return jnp.tile(cos, (1, reps)), jnp.tile(sin, (1, reps))


def _mixer(x, p, anchor):
    batch, seq, _ = x.shape
    x2 = x.reshape(batch * seq, D_MODEL)
    cos_t, sin_t = _rope_tables(seq)
    q, k2, vt, u, sa, sb = _in_proj(x2, p["mix_g"], p["w_in"], p["wvt"], p["qg"], p["kg"], cos_t, sin_t, p["bd"], seq)
    attn = _attention(q, k2, vt, batch, seq)
    return _mix_out(x2, attn, u, sa, sb, p["wap"], p["pgw"], p["ps"], p["wpp"], p["wo"], p["ffn_g"],
                    p["wr"], p["br"], p["tri"], seq, anchor)


def _moe_dispatch(mixed):
    counts = [m[5][:, 0].astype(jnp.int32) for m in mixed]
    total = sum(counts)
    padded = ((total + MOE_ROWS - 1) // MOE_ROWS) * MOE_ROWS
    padded_end = jnp.cumsum(padded)
    padded_start = padded_end - padded
    n_tok = sum(m[0].shape[0] for m in mixed)
    n_blocks = (n_tok * TOP_K + MOE_ROWS - 1) // MOE_ROWS + N_EXPERTS
    n_used = (padded_end[-1] // MOE_ROWS).astype(jnp.int32)
    blk = jnp.minimum(jnp.arange(n_blocks, dtype=jnp.int32), n_used - 1)
    block_e = jnp.sum((padded_end[None, :] <= (blk * MOE_ROWS)[:, None]).astype(jnp.int32), axis=1)
    block_e = jnp.minimum(block_e, N_EXPERTS - 1)
    expert_ids = jnp.arange(N_EXPERTS, dtype=jnp.int32)[:, None, None]
    dests = []
    seen = jnp.zeros((N_EXPERTS,), jnp.int32)
    for m, c in zip(mixed, counts):
        base = (padded_start + seen)[:, None, None]
        dests.append(jnp.sum(jnp.where(m[2][None] == expert_ids, base, 0), axis=0) + m[4])
        seen = seen + c
    workers = SC_CORES * SC_SUBCORES
    per_worker = n_tok // workers
    dest4 = jnp.concatenate(
        [d.reshape(TOP_K, -1, per_worker // SC_CHUNK, SC_CHUNK).transpose(1, 2, 0, 3) for d in dests], axis=0)
    n_rows = n_blocks * MOE_ROWS
    slot = jnp.arange(MOE_ROWS, dtype=jnp.int32)[None, :]
    pad_rows = jnp.where(slot < (padded - total)[:, None], (padded_start + total)[:, None] + slot,
                         n_rows + jnp.arange(N_EXPERTS, dtype=jnp.int32)[:, None] * MOE_ROWS + slot)
    pad3 = pad_rows.astype(jnp.int32).reshape(workers, -1, SC_CHUNK)
    xs_buf = _dispatch(dest4, pad3, [m[1] for m in mixed], n_rows + N_EXPERTS * MOE_ROWS)
    return xs_buf, dict(mixed=mixed, dests=dests, per_worker=per_worker, n_blocks=n_blocks, n_used=n_used, block_e=block_e,
                        padded_end=padded_end, padded_start=padded_start, total=total)


def _moe_finish(xs_buf, r, w_gu, b_gu, w_down, b_down):
    mixed, dests, per_worker, n_blocks, n_used = r["mixed"], r["dests"], r["per_worker"], r["n_blocks"], r["n_used"]
    block_e, padded_end, padded_start, total = r["block_e"], r["padded_end"], r["padded_start"], r["total"]
    blk_id = jnp.arange(n_blocks, dtype=jnp.int32)
    first = jnp.logical_and(block_e != jnp.concatenate([jnp.full((1,), -1, jnp.int32), block_e[:-1]]),
                            blk_id < n_used).astype(jnp.int32)
    slot = (jnp.cumsum(first) - 1) % 2
    group_end = jnp.sum(jnp.where(block_e[:, None] == jnp.arange(N_EXPERTS, dtype=jnp.int32)[None, :],
                                  (padded_end // MOE_ROWS)[None, :], 0), axis=1)
    next_e = jnp.sum(jnp.where(group_end[:, None] == blk_id[None, :], block_e[None, :], 0), axis=1)
    next_e = jnp.where(group_end < n_used, next_e, -1)
    token_end = jnp.sum(jnp.where(block_e[:, None] == jnp.arange(N_EXPERTS, dtype=jnp.int32)[None, :],
                                  (padded_start + total)[None, :], 0), axis=1)
    half = (token_end - blk_id * MOE_ROWS <= MOE_ROWS // 2).astype(jnp.int32)
    yb = _experts(block_e, n_used.reshape(1), first, slot.astype(jnp.int32), next_e.astype(jnp.int32), half, xs_buf,
                  w_gu, b_gu.reshape(N_EXPERTS, 1, -1), w_down, b_down.reshape(N_EXPERTS, 1, -1))
    dest_rows = jnp.concatenate(
        [d.reshape(TOP_K, -1, per_worker // SC_COMBINE_CHUNK, SC_COMBINE_CHUNK).transpose(1, 2, 0, 3)
         .reshape(-1, per_worker * TOP_K // LANES, LANES) for d in dests], axis=0)
    gate_vecs = [jnp.repeat(m[3].T, SC_LANES, axis=1) for m in mixed]
    return _combine(dest_rows, gate_vecs, [m[0] for m in mixed], yb)


def kernel(x_prompt, x_sample, mix_norm_g, w_in, q_norm_g, k_norm_g, w_attn_proj, pool_group_w, pool_scale,
           w_pool_proj, w_out, ffn_norm_g, w_router, b_router, w_gu, b_gu, w_down, b_down):
    depth = w_in.shape[0]
    xs_all = [x_prompt, x_sample]
    head_id = jnp.arange(ATTN_WIDTH) // HEAD_DIM
    block_diag = (head_id[:, None] == head_id[None, :]).astype(_bf16)
    tri_id = jnp.arange(MIX_ROWS)
    tri = (tri_id[:, None] < tri_id[None, :]).astype(_bf16)
    for l in range(depth):
        wr = jnp.pad(w_router[l].astype(_f32), ((0, 0), (0, LANES - N_EXPERTS)))
        wrh = wr.astype(_bf16)
        w_in_bf = w_in[l].astype(_bf16)
        p = dict(
            mix_g=mix_norm_g[l].reshape(1, D_MODEL),
            w_in=w_in_bf,
            wvt=w_in_bf[:, ATTN_WIDTH + KV_WIDTH:ATTN_WIDTH + 2 * KV_WIDTH].T,
            qg=jnp.tile(q_norm_g[l] * (HEAD_DIM ** -0.5 * LOG2_E), N_HEADS).reshape(1, ATTN_WIDTH),
            kg=jnp.tile(k_norm_g[l], N_KV_HEADS).reshape(1, KV_WIDTH),
            bd=block_diag,
            wap=w_attn_proj[l].astype(_bf16),
            pgw=pool_group_w[l].astype(_bf16),
            ps=pool_scale[l].reshape(1, POOL_WIDTH),
            wpp=w_pool_proj[l].astype(_bf16),
            wo=w_out[l].astype(_bf16),
            ffn_g=ffn_norm_g[l].reshape(1, D_MODEL),
            wr=jnp.concatenate([wrh, (wr - wrh.astype(_f32)).astype(_bf16)], axis=1),
            br=b_router[l].astype(_f32).reshape(N_EXPERTS, 1),
            tri=tri,
        )
        routed = []
        anchor = jnp.zeros((SUBLANES, LANES), _f32)
        for x in xs_all:
            xs_buf, r = _moe_dispatch([_mixer(x, p, anchor)])
            routed.append((xs_buf, r))
            anchor = xs_buf
        outs = [_moe_finish(xs_buf, r, w_gu[l], b_gu[l], w_down[l], b_down[l])[0] for xs_buf, r in routed]
        xs_all = [o.reshape(x.shape) for o, x in zip(outs, xs_all)]
    return tuple(xs_all)
```

```python
import functools

import jax
import jax.numpy as jnp
import numpy as np
from jax import lax
from jax.experimental import pallas as pl
from jax.experimental.pallas import tpu as pltpu
from jax.experimental.pallas import tpu_sc as plsc

D_MODEL = 1024
HALF_MODEL = D_MODEL // 2
GRID_W = 64
N_HEADS = 8
N_KV_HEADS = 2
HEAD_DIM = 64
N_GROUPS_PER_KV = N_HEADS // N_KV_HEADS
ATTN_WIDTH = N_HEADS * HEAD_DIM
KV_WIDTH = N_KV_HEADS * HEAD_DIM
ROPE_THETA = 10000.0
ROPE_PAIRS_PER_AXIS = HEAD_DIM // 4
POOL_WINDOWS = (2, 4, 8, 16)
N_POOL_GROUPS = 4
POOL_WIDTH = 512
POOL_GROUP_DIM = POOL_WIDTH // N_POOL_GROUPS
N_EXPERTS = 32
TOP_K = 4
D_FF = 1024
SWIGLU_ALPHA = 1.702
SWIGLU_LIMIT = 7.0
NORM_EPS = 1e-6

LANES = 128
SUBLANES = 8
POOL_HALO = 8
LOG2_E = 1.4426950408889634

IN_PROJ_ROWS = 1024
ATTN_Q_ROWS = 256
ATTN_KEY_CHUNK = 256
MIX_ROWS = 1024
MOE_ROWS = 512
SC_CORES = 2
SC_SUBCORES = 16
SC_LANES = 16
SC_CHUNK = 32
SC_COMBINE_CHUNK = 16
V7X_VMEM_BYTES = 64 * 1024 * 1024
VMEM_LIMIT = V7X_VMEM_BYTES * 7 // 8

_bf16 = jnp.bfloat16
_f32 = jnp.float32


def _dot(a, b):
    return jnp.dot(a, b, preferred_element_type=_f32)


def _split_bf16(x):
    hi = x.astype(_bf16)
    lo = (x - hi.astype(_f32)).astype(_bf16)
    return hi, lo


def _sigmoid(x):
    return 1.0 / (1.0 + jnp.exp(-x))


_HIGH_HALF = 0xFFFF0000


def _pack_bf16_pair(lo, hi):
    lo_bits = lax.bitcast_convert_type(lo.astype(_bf16).astype(_f32), jnp.uint32)
    hi_bits = lax.bitcast_convert_type(hi.astype(_bf16).astype(_f32), jnp.uint32)
    return lax.shift_right_logical(lo_bits, jnp.uint32(16)) | (hi_bits & jnp.uint32(_HIGH_HALF))


def _unpack_bf16_pair(words):
    lo = lax.bitcast_convert_type(lax.shift_left(words, jnp.uint32(16)), _f32)
    hi = lax.bitcast_convert_type(words & jnp.uint32(_HIGH_HALF), _f32)
    return lo, hi


def _in_proj_body(x_ref, g_ref, w_ref, wvt_ref, qg_ref, kg_ref, cos_ref, sin_ref, bd_ref,
                  q_ref, k_ref, vt_ref, u_ref, sa_ref, sb_ref):
    x = x_ref[...]
    ms = jnp.mean(x * x, axis=-1, keepdims=True)
    h = (x * lax.rsqrt(ms + NORM_EPS) * g_ref[...]).astype(_bf16)
    cos = cos_ref[...]
    sin = sin_ref[...]
    lane = lax.broadcasted_iota(jnp.int32, cos.shape, 1)
    first_half = (lane % (2 * ROPE_PAIRS_PER_AXIS)) < ROPE_PAIRS_PER_AXIS

    def head_norm_rope(z, gain):
        width = z.shape[1]
        ss = _dot((z * z).astype(_bf16), bd_ref[:width, :width])
        zn = z * lax.rsqrt(ss * (1.0 / HEAD_DIM) + NORM_EPS) * gain
        outs = []
        for j in range(width // LANES):
            c = zn[:, j * LANES:(j + 1) * LANES]
            partner = jnp.where(first_half,
                                pltpu.roll(c, LANES - ROPE_PAIRS_PER_AXIS, 1),
                                pltpu.roll(c, ROPE_PAIRS_PER_AXIS, 1))
            outs.append(c * cos + partner * sin)
        return outs

    c0 = 0
    zq = _dot(h, w_ref[:, c0:c0 + ATTN_WIDTH])
    for j, o in enumerate(head_norm_rope(zq, qg_ref[...])):
        q_ref[:, j * LANES:(j + 1) * LANES] = o.astype(_bf16)
    c0 += ATTN_WIDTH
    zk = _dot(h, w_ref[:, c0:c0 + KV_WIDTH])
    (kr,) = head_norm_rope(zk, kg_ref[...])
    for j in range(N_KV_HEADS):
        k_ref[j] = kr[:, j * HEAD_DIM:(j + 1) * HEAD_DIM].astype(_bf16)
    c0 += KV_WIDTH
    zvt = lax.dot_general(wvt_ref[...], h, (((1,), (1,)), ((), ())), preferred_element_type=_f32)
    for j in range(N_KV_HEADS):
        vt_ref[j] = zvt[j * HEAD_DIM:(j + 1) * HEAD_DIM, :].astype(_bf16)
    c0 += KV_WIDTH
    u_ref[...] = _dot(h, w_ref[:, c0:c0 + POOL_WIDTH]).astype(_bf16)
    c0 += POOL_WIDTH
    sa_ref[...] = _sigmoid(_dot(h, w_ref[:, c0:c0 + D_MODEL])).astype(_bf16)
    c0 += D_MODEL
    sb_ref[...] = _sigmoid(_dot(h, w_ref[:, c0:c0 + D_MODEL])).astype(_bf16)


def _in_proj(x2, norm_g, w_in, wvt, qg, kg, cos_t, sin_t, bd, seq):
    tokens = x2.shape[0]
    tm = IN_PROJ_ROWS
    in_width = w_in.shape[1]
    tiles_per_seq = seq // tm
    const = lambda i: (0, 0)
    row = lambda i: (i, 0)
    return pl.pallas_call(
        _in_proj_body,
        grid=(tokens // tm,),
        in_specs=[
            pl.BlockSpec((tm, D_MODEL), row),
            pl.BlockSpec((1, D_MODEL), const),
            pl.BlockSpec((D_MODEL, in_width), const),
            pl.BlockSpec((KV_WIDTH, D_MODEL), const),
            pl.BlockSpec((1, ATTN_WIDTH), const),
            pl.BlockSpec((1, KV_WIDTH), const),
            pl.BlockSpec((tm, LANES), lambda i: (i % tiles_per_seq, 0)),
            pl.BlockSpec((tm, LANES), lambda i: (i % tiles_per_seq, 0)),
            pl.BlockSpec((ATTN_WIDTH, ATTN_WIDTH), const),
        ],
        out_specs=[
            pl.BlockSpec((tm, ATTN_WIDTH), row),
            pl.BlockSpec((N_KV_HEADS, tm, HEAD_DIM), lambda i: (0, i, 0)),
            pl.BlockSpec((N_KV_HEADS, HEAD_DIM, tm), lambda i: (0, 0, i)),
            pl.BlockSpec((tm, POOL_WIDTH), row),
            pl.BlockSpec((tm, D_MODEL), row),
            pl.BlockSpec((tm, D_MODEL), row),
        ],
        out_shape=[
            jax.ShapeDtypeStruct((tokens, ATTN_WIDTH), _bf16),
            jax.ShapeDtypeStruct((N_KV_HEADS, tokens, HEAD_DIM), _bf16),
            jax.ShapeDtypeStruct((N_KV_HEADS, HEAD_DIM, tokens), _bf16),
            jax.ShapeDtypeStruct((tokens, POOL_WIDTH), _bf16),
            jax.ShapeDtypeStruct((tokens, D_MODEL), _bf16),
            jax.ShapeDtypeStruct((tokens, D_MODEL), _bf16),
        ],
        compiler_params=pltpu.CompilerParams(
            dimension_semantics=("parallel",), vmem_limit_bytes=VMEM_LIMIT),
        name="in_proj",
    )(x2, norm_g, w_in, wvt, qg, kg, cos_t, sin_t, bd)


def _attention_step(q_ref, k_ref, vt_ref, o_ref, st_new, m_new, st_old, m_old):
    tq = q_ref.shape[0]
    q = q_ref[...]
    qs = jnp.concatenate([q[:, g * HEAD_DIM:(g + 1) * HEAD_DIM] for g in range(N_GROUPS_PER_KV)], axis=0)
    seq = k_ref.shape[0]
    m_prev = m_old[...]
    m_run = denom = ot = None
    for c0 in range(0, seq, ATTN_KEY_CHUNK):
        rows = slice(c0, c0 + ATTN_KEY_CHUNK)
        st = lax.dot_general(k_ref[rows, :], qs, (((1,), (1,)), ((), ())), preferred_element_type=_f32)
        st_new[rows, :] = st
        m_c = jnp.max(st, axis=0, keepdims=True)
        m_run = m_c if m_run is None else jnp.maximum(m_run, m_c)
        p = jnp.exp2(st_old[rows, :] - m_prev)
        l_c = jnp.sum(p, axis=0, keepdims=True)
        o_c = _dot(vt_ref[:, rows], p.astype(_bf16))
        denom = l_c if denom is None else denom + l_c
        ot = o_c if ot is None else ot + o_c
    m_new[...] = m_run
    ot = ot * (1.0 / denom)
    stacked = jnp.concatenate([ot[:, g * tq:(g + 1) * tq] for g in range(N_GROUPS_PER_KV)], axis=0)
    o_ref[...] = stacked.T.astype(_bf16)


def _attention_body(q_ref, k_ref, vt_ref, o_ref, st_a, m_a, st_b, m_b):
    n = pl.program_id(0)

    @pl.when(n == 0)
    def _():
        st_b[...] = jnp.zeros_like(st_b)
        m_b[...] = jnp.zeros_like(m_b)

    @pl.when(n % 2 == 0)
    def _():
        _attention_step(q_ref, k_ref, vt_ref, o_ref, st_a, m_a, st_b, m_b)

    @pl.when(n % 2 == 1)
    def _():
        _attention_step(q_ref, k_ref, vt_ref, o_ref, st_b, m_b, st_a, m_a)


def _attention(q, k2, vt, batch, seq):
    tq = ATTN_Q_ROWS
    nq = seq // tq
    n_blocks = batch * N_KV_HEADS * nq
    group_width = N_GROUPS_PER_KV * HEAD_DIM
    lanes = N_GROUPS_PER_KV * tq

    def decode(n):
        return n // (N_KV_HEADS * nq), (n // nq) % N_KV_HEADS, n % nq

    def scores_of(n):
        return decode(jnp.minimum(n, n_blocks - 1))

    def finish_of(n):
        return decode(jnp.maximum(n - 1, 0))

    def q_map(n):
        b, kh, i = scores_of(n)
        return b * nq + i, kh

    def k_map(n):
        b, kh, _ = scores_of(n)
        return kh, b, 0

    def vt_map(n):
        b, kh, _ = finish_of(n)
        return kh, 0, b

    def o_map(n):
        b, kh, i = finish_of(n)
        return b * nq + i, kh

    return pl.pallas_call(
        _attention_body,
        grid=(n_blocks + 1,),
        in_specs=[
            pl.BlockSpec((tq, group_width), q_map),
            pl.BlockSpec((None, seq, HEAD_DIM), k_map),
            pl.BlockSpec((None, HEAD_DIM, seq), vt_map),
        ],
        out_specs=pl.BlockSpec((tq, group_width), o_map),
        out_shape=jax.ShapeDtypeStruct((batch * seq, ATTN_WIDTH), _bf16),
        scratch_shapes=[
            pltpu.VMEM((seq, lanes), _f32), pltpu.VMEM((1, lanes), _f32),
            pltpu.VMEM((seq, lanes), _f32), pltpu.VMEM((1, lanes), _f32),
        ],
        compiler_params=pltpu.CompilerParams(
            dimension_semantics=("arbitrary",), vmem_limit_bytes=VMEM_LIMIT),
        name="attention",
    )(q, k2, vt)


def _mix_out_body(seq, x_ref, a_ref, up_ref, u_ref, un_ref, sa_ref, sb_ref,
                  wap_ref, pgw_ref, ps_ref, wpp_ref, wo_ref, g2_ref, wr_ref, br_ref, tri_ref, anchor_ref,
                  x1_ref, h2_ref, idx_ref, gate_ref, rank_ref, cnt_ref,
                  ext_ref, lv_ref, carry_ref):
    del anchor_ref
    i = pl.program_id(0)
    tm = x_ref.shape[0]

    @pl.when(i == 0)
    def _():
        carry_ref[...] = jnp.zeros_like(carry_ref)
        ext_ref[...] = jnp.zeros_like(ext_ref)
        lv_ref[...] = jnp.zeros_like(lv_ref)

    start = (i * tm) % seq
    has_prev = (start != 0).astype(_f32)
    has_next = (start + tm != seq).astype(_f32)
    first = 2 * POOL_HALO
    ext_ref[POOL_HALO:first, :] = up_ref[...].astype(_f32) * has_prev
    ext_ref[first:first + tm, :] = u_ref[...].astype(_f32)
    ext_ref[first + tm:first + tm + POOL_HALO, :] = un_ref[...].astype(_f32) * has_next
    y_attn = _dot(a_ref[...], wap_ref[...])
    lo, n = POOL_HALO, tm + 2 * POOL_HALO
    pos = start + lax.broadcasted_iota(jnp.int32, (tm, 1), 0)
    pooled = []
    for g, w in enumerate(POOL_WINDOWS):
        half = w // 2
        assert half == 1 << g
        wide = slice(g * POOL_GROUP_DIM, POOL_WIDTH)
        cols = slice(g * POOL_GROUP_DIM, (g + 1) * POOL_GROUP_DIM)
        if g == 0:
            lv_ref[0, lo:lo + n, wide] = ext_ref[pl.ds(lo - 1, n), wide] + ext_ref[pl.ds(lo, n), wide]
        elif g < N_POOL_GROUPS - 1:
            lv_ref[g, lo:lo + n, wide] = (lv_ref[g - 1, pl.ds(lo - half // 2, n), wide]
                                          + lv_ref[g - 1, pl.ds(lo + half // 2, n), wide])
        if g < N_POOL_GROUPS - 1:
            tot = lv_ref[g, first:first + tm, cols]
        else:
            tot = (lv_ref[g - 1, pl.ds(first - half // 2, tm), cols]
                   + lv_ref[g - 1, pl.ds(first + half // 2, tm), cols])
        cnt = (jnp.minimum(pos + half, seq) - jnp.maximum(pos - half, 0)).astype(_f32)
        diff = tot / cnt - ext_ref[first:first + tm, cols]
        pooled.append(_dot(diff.astype(_bf16), pgw_ref[g]))
    pooled = (jnp.concatenate(pooled, axis=1) * ps_ref[...]).astype(_bf16)
    y_pool = _dot(pooled, wpp_ref[...])
    merged = sa_ref[...].astype(_f32) * y_attn + sb_ref[...].astype(_f32) * y_pool
    x1 = x_ref[...] + _dot(merged.astype(_bf16), wo_ref[...])
    x1_ref[...] = x1

    ms = jnp.mean(x1 * x1, axis=-1, keepdims=True)
    h2 = x1 * lax.rsqrt(ms + NORM_EPS) * g2_ref[...]
    h2_ref[...] = _pack_bf16_pair(h2[:, :HALF_MODEL], h2[:, HALF_MODEL:])
    hi, lo = _split_bf16(h2)
    both = _dot(hi, wr_ref[...])
    logits = both[:, :LANES] + both[:, LANES:] + _dot(lo, wr_ref[:, :LANES])
    lt = logits.T[:N_EXPERTS, :] + br_ref[...]
    eid = lax.broadcasted_iota(jnp.int32, lt.shape, 0)
    vals, idxs = [], []
    multi_hot = jnp.zeros(lt.shape, _f32)
    for _ in range(TOP_K):
        m = jnp.max(lt, axis=0, keepdims=True)
        sel = jnp.min(jnp.where(lt == m, eid, N_EXPERTS), axis=0, keepdims=True)
        hit = eid == sel
        vals.append(m)
        idxs.append(sel)
        multi_hot = multi_hot + hit.astype(_f32)
        lt = jnp.where(hit, -jnp.inf, lt)
    es = [jnp.exp(v - vals[0]) for v in vals]
    inv = 1.0 / (es[0] + es[1] + es[2] + es[3])
    gate_ref[...] = jnp.concatenate([e * inv for e in es], axis=0)
    idx_ref[...] = jnp.concatenate(idxs, axis=0)
    before = _dot(multi_hot.astype(_bf16), tri_ref[...]) + carry_ref[:, 0:1]
    ranks = [jnp.sum(jnp.where(eid == sel, before, 0.0), axis=0, keepdims=True) for sel in idxs]
    rank_ref[...] = jnp.concatenate(ranks, axis=0).astype(jnp.int32)
    carry_ref[...] = carry_ref[...] + jnp.sum(multi_hot, axis=1, keepdims=True)
    cnt_ref[...] = carry_ref[...]


def _mix_out(x2, attn, u, sa, sb, wap, pgw, ps, wpp, wo, g2, wr, br, tri, seq, anchor):
    tokens = x2.shape[0]
    tm = MIX_ROWS
    n = tokens // tm
    halo_blocks = tm // POOL_HALO
    last_halo = tokens // POOL_HALO - 1
    row = lambda i: (i, 0)
    const2 = lambda i: (0, 0)
    const3 = lambda i: (0, 0, 0)
    col = lambda i: (0, i)
    return pl.pallas_call(
        functools.partial(_mix_out_body, seq),
        grid=(n,),
        in_specs=[
            pl.BlockSpec((tm, D_MODEL), row),
            pl.BlockSpec((tm, ATTN_WIDTH), row),
            pl.BlockSpec((POOL_HALO, POOL_WIDTH), lambda i: (jnp.maximum(i * halo_blocks - 1, 0), 0)),
            pl.BlockSpec((tm, POOL_WIDTH), row),
            pl.BlockSpec((POOL_HALO, POOL_WIDTH), lambda i: (jnp.minimum((i + 1) * halo_blocks, last_halo), 0)),
            pl.BlockSpec((tm, D_MODEL), row),
            pl.BlockSpec((tm, D_MODEL), row),
            pl.BlockSpec((ATTN_WIDTH, D_MODEL), const2),
            pl.BlockSpec((N_POOL_GROUPS, POOL_GROUP_DIM, POOL_GROUP_DIM), const3),
            pl.BlockSpec((1, POOL_WIDTH), const2),
            pl.BlockSpec((POOL_WIDTH, D_MODEL), const2),
            pl.BlockSpec((D_MODEL, D_MODEL), const2),
            pl.BlockSpec((1, D_MODEL), const2),
            pl.BlockSpec((D_MODEL, 2 * LANES), const2),
            pl.BlockSpec((N_EXPERTS, 1), const2),
            pl.BlockSpec((tm, tm), const2),
            pl.BlockSpec(memory_space=pl.ANY),
        ],
        out_specs=[
            pl.BlockSpec((tm, D_MODEL), row),
            pl.BlockSpec((tm, HALF_MODEL), row),
            pl.BlockSpec((TOP_K, tm), col),
            pl.BlockSpec((TOP_K, tm), col),
            pl.BlockSpec((TOP_K, tm), col),
            pl.BlockSpec((N_EXPERTS, LANES), const2),
        ],
        out_shape=[
            jax.ShapeDtypeStruct((tokens, D_MODEL), _f32),
            jax.ShapeDtypeStruct((tokens, HALF_MODEL), jnp.uint32),
            jax.ShapeDtypeStruct((TOP_K, tokens), jnp.int32),
            jax.ShapeDtypeStruct((TOP_K, tokens), _f32),
            jax.ShapeDtypeStruct((TOP_K, tokens), jnp.int32),
            jax.ShapeDtypeStruct((N_EXPERTS, LANES), _f32),
        ],
        scratch_shapes=[
            pltpu.VMEM((tm + 4 * POOL_HALO, POOL_WIDTH), _f32),
            pltpu.VMEM((N_POOL_GROUPS - 1, tm + 4 * POOL_HALO, POOL_WIDTH), _f32),
            pltpu.VMEM((N_EXPERTS, LANES), _f32),
        ],
        compiler_params=pltpu.CompilerParams(
            dimension_semantics=("arbitrary",), vmem_limit_bytes=VMEM_LIMIT),
        name="mix_out",
    )(x2, attn, u, u, u, sa, sb, wap, pgw, ps, wpp, wo, g2, wr, br, tri, anchor)


def _worker_ranges(group_tokens, per_worker, workers):
    first = [0]
    for t in group_tokens:
        assert t % per_worker == 0
        first.append(first[-1] + t // per_worker)
    assert first[-1] == workers
    return first


def _dispatch(dest4, pad3, hs, n_rows):
    workers, n_chunks = dest4.shape[0], dest4.shape[1]
    n_pad_chunks = pad3.shape[1]
    per_worker = n_chunks * SC_CHUNK
    first_worker = _worker_ranges([h.shape[0] for h in hs], per_worker, workers)
    width, dtype = hs[0].shape[1], hs[0].dtype
    assert n_chunks % 2 == 0
    mesh = plsc.VectorSubcoreMesh(core_axis_name="c", subcore_axis_name="s")

    @functools.partial(
        pl.kernel, mesh=mesh,
        out_type=jax.ShapeDtypeStruct((n_rows, width), dtype),
        scratch_types=[
            pltpu.VMEM((n_chunks, TOP_K, SC_CHUNK), jnp.int32),
            pltpu.VMEM((n_pad_chunks, SC_CHUNK), jnp.int32),
            pltpu.VMEM((SC_CHUNK, width), dtype),
            pltpu.VMEM((SC_CHUNK, width), dtype),
            pltpu.SemaphoreType.DMA,
            pltpu.SemaphoreType.DMA,
            pltpu.SemaphoreType.DMA,
        ],
        name="dispatch",
    )
    def body(dest_hbm, pad_hbm, *refs):
        h_hbms = refs[:len(hs)]
        out_hbm, idx_v, pad_v, rows0, rows1, load_sem0, load_sem1, store_sem = refs[len(hs):]
        wid = lax.axis_index("s") * SC_CORES + lax.axis_index("c")
        pltpu.sync_copy(dest_hbm.at[wid], idx_v)
        pltpu.sync_copy(pad_hbm.at[wid], pad_v)

        zero = jnp.zeros((SC_LANES,), dtype)

        @pl.loop(0, SC_CHUNK)
        def _(r):
            @pl.loop(0, width, step=SC_LANES)
            def _(c):
                rows0.at[r, pl.ds(c, SC_LANES)][...] = zero

        @pl.loop(0, n_pad_chunks)
        def _(c):
            pltpu.sync_copy(rows0, out_hbm.at[pad_v.at[c]])

        def scatter_tokens(src_hbm, base):
            bufs = (rows0, rows1)
            load_sems = (load_sem0, load_sem1)

            def load(c, b):
                return pltpu.make_async_copy(src_hbm.at[pl.ds(base + c * SC_CHUNK, SC_CHUNK)], bufs[b], load_sems[b])

            load(0, 0).start()

            @pl.loop(0, n_chunks, step=2)
            def _(c):
                for b in range(2):
                    cc = c + b
                    load(cc, b).wait()

                    @pl.when(cc + 1 < n_chunks)
                    def _():
                        load(cc + 1, 1 - b).start()

                    copies = [pltpu.make_async_copy(bufs[b], out_hbm.at[idx_v.at[cc, k]], store_sem)
                              for k in range(TOP_K)]
                    for cp in copies:
                        cp.start()
                    for cp in copies:
                        cp.wait()

        for g, h_hbm in enumerate(h_hbms):
            @pl.when(jnp.logical_and(wid >= first_worker[g], wid < first_worker[g + 1]))
            def _(g=g, h_hbm=h_hbm):
                scatter_tokens(h_hbm, (wid - first_worker[g]) * per_worker)

    return body(dest4, pad3, *hs)


def _experts_body(be_ref, nu_ref, first_ref, slot_ref, next_ref, half_ref, xs_ref, wgu_hbm, bgu_ref, wd_hbm, bd_ref,
                  y_ref, wgu_buf, wd_buf, sems):
    i = pl.program_id(0)
    used = i < nu_ref[0]
    slot = slot_ref[i]

    def weight_copies(e, s):
        return (pltpu.make_async_copy(wgu_hbm.at[e], wgu_buf.at[s], sems.at[s]),
                pltpu.make_async_copy(wd_hbm.at[e], wd_buf.at[s], sems.at[s]))

    @pl.when(i == 0)
    def _():
        for cp in weight_copies(be_ref[0], 0):
            cp.start()

    @pl.when(first_ref[i] == 1)
    def _():
        for cp in weight_copies(be_ref[i], slot):
            cp.wait()

        @pl.when(next_ref[i] >= 0)
        def _():
            for cp in weight_copies(next_ref[i], 1 - slot):
                cp.start()

    @pl.when(jnp.logical_not(used))
    def _():
        y_ref[...] = jnp.zeros_like(y_ref)

    def run_rows(n):
        x = jnp.concatenate(_unpack_bf16_pair(xs_ref[:n, :]), axis=1).astype(_bf16)
        gu = _dot(x, wgu_buf[slot].astype(_bf16)) + bgu_ref[...]
        gate = jnp.minimum(gu[:, :D_FF], SWIGLU_LIMIT)
        up = jnp.clip(gu[:, D_FF:], -SWIGLU_LIMIT, SWIGLU_LIMIT)
        act = (up + 1.0) * (gate * _sigmoid(SWIGLU_ALPHA * gate))
        y = _dot(act.astype(_bf16), wd_buf[slot].astype(_bf16)) + bd_ref[...]
        y_ref[:n, :] = _pack_bf16_pair(y[:, :HALF_MODEL], y[:, HALF_MODEL:])

    lower_only = half_ref[i] == 1

    @pl.when(jnp.logical_and(used, jnp.logical_not(lower_only)))
    def _():
        run_rows(MOE_ROWS)

    @pl.when(jnp.logical_and(used, lower_only))
    def _():
        run_rows(MOE_ROWS // 2)
        y_ref[MOE_ROWS // 2:, :] = jnp.zeros((MOE_ROWS // 2, HALF_MODEL), jnp.uint32)


def _experts(block_e, n_used, first, slot, next_e, half, xs, wgu, bgu, wd, bd):
    n_blocks = block_e.shape[0]
    n_rows = n_blocks * MOE_ROWS
    rows = lambda i, be, nu, *_: (jnp.minimum(i, nu[0] - 1), 0)
    per_e = lambda i, be, *_: (be[i], 0, 0)
    return pl.pallas_call(
        _experts_body,
        grid_spec=pltpu.PrefetchScalarGridSpec(
            num_scalar_prefetch=6,
            grid=(n_blocks,),
            in_specs=[
                pl.BlockSpec((MOE_ROWS, HALF_MODEL), rows),
                pl.BlockSpec(memory_space=pl.ANY),
                pl.BlockSpec((None, 1, 2 * D_FF), per_e),
                pl.BlockSpec(memory_space=pl.ANY),
                pl.BlockSpec((None, 1, D_MODEL), per_e),
            ],
            out_specs=pl.BlockSpec((MOE_ROWS, HALF_MODEL), lambda i, *_: (i, 0)),
            scratch_shapes=[
                pltpu.VMEM((2, D_MODEL, 2 * D_FF), _f32),
                pltpu.VMEM((2, D_FF, D_MODEL), _f32),
                pltpu.SemaphoreType.DMA((2,)),
            ],
        ),
        out_shape=jax.ShapeDtypeStruct((n_rows, HALF_MODEL), jnp.uint32),
        compiler_params=pltpu.CompilerParams(
            dimension_semantics=("arbitrary",), vmem_limit_bytes=VMEM_LIMIT),
        name="experts",
    )(block_e, n_used, first, slot, next_e, half, xs, wgu, bgu, wd, bd)


def _combine(dest_rows, gates, x1s, yb):
    workers = dest_rows.shape[0]
    per_worker = dest_rows.shape[1] * LANES // TOP_K
    n_chunks = per_worker // SC_COMBINE_CHUNK
    first_worker = _worker_ranges([x.shape[0] for x in x1s], per_worker, workers)
    n_groups = len(x1s)
    assert n_chunks % 2 == 0 and LANES % SC_COMBINE_CHUNK == 0
    mesh = plsc.VectorSubcoreMesh(core_axis_name="c", subcore_axis_name="s")
    row_buf = pltpu.VMEM((SC_COMBINE_CHUNK, D_MODEL), _f32)
    packed_buf = pltpu.VMEM((SC_COMBINE_CHUNK, HALF_MODEL), jnp.uint32)

    @functools.partial(
        pl.kernel, mesh=mesh,
        out_type=[jax.ShapeDtypeStruct(x.shape, _f32) for x in x1s],
        scratch_types=[
            pltpu.VMEM(dest_rows.shape[1:], jnp.int32),
            [[packed_buf] * TOP_K + [row_buf]] * 2,
            [pltpu.VMEM((SC_COMBINE_CHUNK, TOP_K * SC_LANES), _f32)] * 2,
            [pltpu.SemaphoreType.DMA] * 2, [pltpu.SemaphoreType.DMA] * 2, [pltpu.SemaphoreType.DMA] * 2,
        ],
        compiler_params=pltpu.CompilerParams(needs_layout_passes=False),
        name="combine",
    )
    def body(dest_hbm, *refs):
        g_hbms, x_hbms = refs[:n_groups], refs[n_groups:2 * n_groups]
        yb_hbm = refs[2 * n_groups]
        o_hbms = refs[2 * n_groups + 1:3 * n_groups + 1]
        idx_v, row_bufs, gate_bufs, gather_sems, load_sems, store_sems = refs[3 * n_groups + 1:]
        wid = lax.axis_index("s") * SC_CORES + lax.axis_index("c")
        pltpu.sync_copy(dest_hbm.at[wid], idx_v)

        def run(g_hbm, x_hbm, o_hbm, base):
            def loads(c, b):
                rows = pl.ds(base + c * SC_COMBINE_CHUNK, SC_COMBINE_CHUNK)
                cps = []
                for k in range(TOP_K):
                    pos = (c * TOP_K + k) * SC_COMBINE_CHUNK
                    ids = idx_v.at[pos // LANES, pl.ds(pos % LANES, SC_COMBINE_CHUNK)]
                    cps.append(pltpu.make_async_copy(yb_hbm.at[ids], row_bufs[b][k], gather_sems[b]))
                cps.append(pltpu.make_async_copy(x_hbm.at[rows], row_bufs[b][TOP_K], load_sems[b]))
                cps.append(pltpu.make_async_copy(g_hbm.at[rows], gate_bufs[b], load_sems[b]))
                return cps

            def store(c, b):
                rows = pl.ds(base + c * SC_COMBINE_CHUNK, SC_COMBINE_CHUNK)
                return pltpu.make_async_copy(row_bufs[b][TOP_K], o_hbm.at[rows], store_sems[b])

            for cp in loads(0, 0):
                cp.start()

            @pl.loop(0, n_chunks, step=2)
            def _(c):
                for b in range(2):
                    cc = c + b

                    @pl.when(cc + 1 < n_chunks)
                    def _():
                        @pl.when(cc >= 1)
                        def _():
                            store(cc - 1, 1 - b).wait()
                        for cp in loads(cc + 1, 1 - b):
                            cp.start()

                    for cp in loads(cc, b):
                        cp.wait()
                    acc_buf, gate_buf = row_bufs[b][TOP_K], gate_bufs[b]

                    @pl.loop(0, SC_COMBINE_CHUNK)
                    def _(r):
                        g = [gate_buf.at[r, pl.ds(k * SC_LANES, SC_LANES)][...] for k in range(TOP_K)]

                        @plsc.parallel_loop(0, HALF_MODEL, step=SC_LANES, unroll=4)
                        def _(col):
                            lo = acc_buf.at[r, pl.ds(col, SC_LANES)][...]
                            hi = acc_buf.at[r, pl.ds(HALF_MODEL + col, SC_LANES)][...]
                            for k in range(TOP_K):
                                words = row_bufs[b][k].at[r, pl.ds(col, SC_LANES)][...]
                                lo = lo + g[k] * plsc.bitcast(lax.shift_left(words, jnp.uint32(16)), _f32)
                                hi = hi + g[k] * plsc.bitcast(words & jnp.uint32(_HIGH_HALF), _f32)
                            acc_buf.at[r, pl.ds(col, SC_LANES)][...] = lo
                            acc_buf.at[r, pl.ds(HALF_MODEL + col, SC_LANES)][...] = hi

                    store(cc, b).start()

            store(n_chunks - 2, 0).wait()
            store(n_chunks - 1, 1).wait()

        for g in range(n_groups):
            @pl.when(jnp.logical_and(wid >= first_worker[g], wid < first_worker[g + 1]))
            def _(g=g):
                run(g_hbms[g], x_hbms[g], o_hbms[g], (wid - first_worker[g]) * per_worker)

    return body(dest_rows, *gates, *x1s, yb)


def _rope_tables(seq):
    pos = np.arange(seq)
    row_ids = (pos // GRID_W).astype(np.float32)
    col_ids = (pos % GRID_W).astype(np.float32)
    inv_freq = np.float32(ROPE_THETA) ** (-np.arange(ROPE_PAIRS_PER_AXIS, dtype=np.float32) / ROPE_PAIRS_PER_AXIS)
    ang_r = row_ids[:, None] * inv_freq
    ang_c = col_ids[:, None] * inv_freq
    cos = np.concatenate([np.cos(ang_r)] * 2 + [np.cos(ang_c)] * 2, axis=1)
    sin = np.concatenate([-np.sin(ang_r), np.sin(ang_r), -np.sin(ang_c), np.sin(ang_c)], axis=1)
    reps = LANES // HEAD_DIM
    return (jnp.asarray(np.tile(cos, (1, reps)).astype(np.float32)),
            jnp.asarray(np.tile(sin, (1, reps)).astype(np.float32)))


def _mixer(x, p, anchor):
    batch, seq, _ = x.shape
    x2 = x.reshape(batch * seq, D_MODEL)
    cos_t, sin_t = _rope_tables(seq)
    q, k2, vt, u, sa, sb = _in_proj(x2, p["mix_g"], p["w_in"], p["wvt"], p["qg"], p["kg"], cos_t, sin_t, p["bd"], seq)
    attn = _attention(q, k2, vt, batch, seq)
    return _mix_out(x2, attn, u, sa, sb, p["wap"], p["pgw"], p["ps"], p["wpp"], p["wo"], p["ffn_g"],
                    p["wr"], p["br"], p["tri"], seq, anchor)


def _moe_dispatch(mixed):
    counts = [m[5][:, 0].astype(jnp.int32) for m in mixed]
    total = sum(counts)
    padded = ((total + MOE_ROWS - 1) // MOE_ROWS) * MOE_ROWS
    padded_end = jnp.cumsum(padded)
    padded_start = padded_end - padded
    n_tok = sum(m[0].shape[0] for m in mixed)
    n_blocks = (n_tok * TOP_K + MOE_ROWS - 1) // MOE_ROWS + N_EXPERTS
    n_used = (padded_end[-1] // MOE_ROWS).astype(jnp.int32)
    blk = jnp.minimum(jnp.arange(n_blocks, dtype=jnp.int32), n_used - 1)
    block_e = jnp.sum((padded_end[None, :] <= (blk * MOE_ROWS)[:, None]).astype(jnp.int32), axis=1)
    block_e = jnp.minimum(block_e, N_EXPERTS - 1)
    expert_ids = jnp.arange(N_EXPERTS, dtype=jnp.int32)[:, None, None]
    dests = []
    seen = jnp.zeros((N_EXPERTS,), jnp.int32)
    for m, c in zip(mixed, counts):
        base = (padded_start + seen)[:, None, None]
        dests.append(jnp.sum(jnp.where(m[2][None] == expert_ids, base, 0), axis=0) + m[4])
        seen = seen + c
    workers = SC_CORES * SC_SUBCORES
    per_worker = n_tok // workers
    dest4 = jnp.concatenate(
        [d.reshape(TOP_K, -1, per_worker // SC_CHUNK, SC_CHUNK).transpose(1, 2, 0, 3) for d in dests], axis=0)
    n_rows = n_blocks * MOE_ROWS
    slot = jnp.arange(MOE_ROWS, dtype=jnp.int32)[None, :]
    pad_rows = jnp.where(slot < (padded - total)[:, None], (padded_start + total)[:, None] + slot,
                         n_rows + jnp.arange(N_EXPERTS, dtype=jnp.int32)[:, None] * MOE_ROWS + slot)
    pad3 = pad_rows.astype(jnp.int32).reshape(workers, -1, SC_CHUNK)
    xs_buf = _dispatch(dest4, pad3, [m[1] for m in mixed], n_rows + N_EXPERTS * MOE_ROWS)
    return xs_buf, dict(mixed=mixed, dests=dests, per_worker=per_worker, n_blocks=n_blocks, n_used=n_used, block_e=block_e,
                        padded_end=padded_end, padded_start=padded_start, total=total)


def _moe_finish(xs_buf, r, w_gu, b_gu, w_down, b_down):
    mixed, dests, per_worker, n_blocks, n_used = r["mixed"], r["dests"], r["per_worker"], r["n_blocks"], r["n_used"]
    block_e, padded_end, padded_start, total = r["block_e"], r["padded_end"], r["padded_start"], r["total"]
    blk_id = jnp.arange(n_blocks, dtype=jnp.int32)
    first = jnp.logical_and(block_e != jnp.concatenate([jnp.full((1,), -1, jnp.int32), block_e[:-1]]),
                            blk_id < n_used).astype(jnp.int32)
    slot = (jnp.cumsum(first) - 1) % 2
    group_end = jnp.sum(jnp.where(block_e[:, None] == jnp.arange(N_EXPERTS, dtype=jnp.int32)[None, :],
                                  (padded_end // MOE_ROWS)[None, :], 0), axis=1)
    next_e = jnp.sum(jnp.where(group_end[:, None] == blk_id[None, :], block_e[None, :], 0), axis=1)
    next_e = jnp.where(group_end < n_used, next_e, -1)
    token_end = jnp.sum(jnp.where(block_e[:, None] == jnp.arange(N_EXPERTS, dtype=jnp.int32)[None, :],
                                  (padded_start + total)[None, :], 0), axis=1)
    half = (token_end - blk_id * MOE_ROWS <= MOE_ROWS // 2).astype(jnp.int32)
    yb = _experts(block_e, n_used.reshape(1), first, slot.astype(jnp.int32), next_e.astype(jnp.int32), half, xs_buf,
                  w_gu, b_gu.reshape(N_EXPERTS, 1, -1), w_down, b_down.reshape(N_EXPERTS, 1, -1))
    dest_rows = jnp.concatenate(
        [d.reshape(TOP_K, -1, per_worker // SC_COMBINE_CHUNK, SC_COMBINE_CHUNK).transpose(1, 2, 0, 3)
         .reshape(-1, per_worker * TOP_K // LANES, LANES) for d in dests], axis=0)
    gate_vecs = [jnp.repeat(m[3].T, SC_LANES, axis=1) for m in mixed]
    return _combine(dest_rows, gate_vecs, [m[0] for m in mixed], yb)


def kernel(x_prompt, x_sample, mix_norm_g, w_in, q_norm_g, k_norm_g, w_attn_proj, pool_group_w, pool_scale,
           w_pool_proj, w_out, ffn_norm_g, w_router, b_router, w_gu, b_gu, w_down, b_down):
    depth = w_in.shape[0]
    xs_all = [x_prompt, x_sample]
    head_id = np.arange(ATTN_WIDTH) // HEAD_DIM
    block_diag = jnp.asarray(head_id[:, None] == head_id[None, :], _bf16)
    tri_id = np.arange(MIX_ROWS)
    tri = jnp.asarray(tri_id[:, None] < tri_id[None, :], _bf16)
    for l in range(depth):
        wr = jnp.pad(w_router[l].astype(_f32), ((0, 0), (0, LANES - N_EXPERTS)))
        wrh = wr.astype(_bf16)
        w_in_bf = w_in[l].astype(_bf16)
        p = dict(
            mix_g=mix_norm_g[l].reshape(1, D_MODEL),
            w_in=w_in_bf,
            wvt=w_in_bf[:, ATTN_WIDTH + KV_WIDTH:ATTN_WIDTH + 2 * KV_WIDTH].T,
            qg=jnp.tile(q_norm_g[l] * (HEAD_DIM ** -0.5 * LOG2_E), N_HEADS).reshape(1, ATTN_WIDTH),
            kg=jnp.tile(k_norm_g[l], N_KV_HEADS).reshape(1, KV_WIDTH),
            bd=block_diag,
            wap=w_attn_proj[l].astype(_bf16),
            pgw=pool_group_w[l].astype(_bf16),
            ps=pool_scale[l].reshape(1, POOL_WIDTH),
            wpp=w_pool_proj[l].astype(_bf16),
            wo=w_out[l].astype(_bf16),
            ffn_g=ffn_norm_g[l].reshape(1, D_MODEL),
            wr=jnp.concatenate([wrh, (wr - wrh.astype(_f32)).astype(_bf16)], axis=1),
            br=b_router[l].astype(_f32).reshape(N_EXPERTS, 1),
            tri=tri,
        )
        routed = []
        anchor = jnp.zeros((SUBLANES, LANES), _f32)
        for x in xs_all:
            xs_buf, r = _moe_dispatch([_mixer(x, p, anchor)])
            routed.append((xs_buf, r))
            anchor = xs_buf
        outs = [_moe_finish(xs_buf, r, w_gu[l], b_gu[l], w_down[l], b_down[l])[0] for xs_buf, r in routed]
        xs_all = [o.reshape(x.shape) for o, x in zip(outs, xs_all)]
    return tuple(xs_all)
```

```python
import functools

import jax
import jax.numpy as jnp
import numpy as np
from jax import lax
from jax.experimental import pallas as pl
from jax.experimental.pallas import tpu as pltpu
from jax.experimental.pallas import tpu_sc as plsc

D_MODEL = 1024
HALF_MODEL = D_MODEL // 2
GRID_W = 64
N_HEADS = 8
N_KV_HEADS = 2
HEAD_DIM = 64
N_GROUPS_PER_KV = N_HEADS // N_KV_HEADS
ATTN_WIDTH = N_HEADS * HEAD_DIM
KV_WIDTH = N_KV_HEADS * HEAD_DIM
ROPE_THETA = 10000.0
ROPE_PAIRS_PER_AXIS = HEAD_DIM // 4
POOL_WINDOWS = (2, 4, 8, 16)
N_POOL_GROUPS = 4
POOL_WIDTH = 512
POOL_GROUP_DIM = POOL_WIDTH // N_POOL_GROUPS
N_EXPERTS = 32
TOP_K = 4
D_FF = 1024
SWIGLU_ALPHA = 1.702
SWIGLU_LIMIT = 7.0
NORM_EPS = 1e-6

LANES = 128
SUBLANES = 8
POOL_HALO = 8
LOG2_E = 1.4426950408889634

IN_PROJ_ROWS = 1024
ATTN_Q_ROWS = 256
ATTN_KEY_CHUNK = 256
MIX_ROWS = 1024
MOE_ROWS = 512
MOE_PARTS = 4
SC_CORES = 2
SC_SUBCORES = 16
SC_LANES = 16
SC_CHUNK = 32
SC_COMBINE_CHUNK = 16
V7X_VMEM_BYTES = 64 * 1024 * 1024
VMEM_LIMIT = V7X_VMEM_BYTES * 7 // 8

_bf16 = jnp.bfloat16
_f32 = jnp.float32


def _dot(a, b):
    return jnp.dot(a, b, preferred_element_type=_f32)


def _split_bf16(x):
    hi = x.astype(_bf16)
    lo = (x - hi.astype(_f32)).astype(_bf16)
    return hi, lo


def _sigmoid(x):
    return 1.0 / (1.0 + jnp.exp(-x))


_HIGH_HALF = 0xFFFF0000


def _pack_bf16_pair(lo, hi):
    lo_bits = lax.bitcast_convert_type(lo.astype(_bf16).astype(_f32), jnp.uint32)
    hi_bits = lax.bitcast_convert_type(hi.astype(_bf16).astype(_f32), jnp.uint32)
    return lax.shift_right_logical(lo_bits, jnp.uint32(16)) | (hi_bits & jnp.uint32(_HIGH_HALF))


def _unpack_bf16_pair(words):
    lo = lax.bitcast_convert_type(lax.shift_left(words, jnp.uint32(16)), _f32)
    hi = lax.bitcast_convert_type(words & jnp.uint32(_HIGH_HALF), _f32)
    return lo, hi


def _in_proj_body(x_ref, g_ref, w_ref, wvt_ref, qg_ref, kg_ref, cos_ref, sin_ref, bd_ref,
                  q_ref, k_ref, vt_ref, u_ref, sa_ref, sb_ref):
    x = x_ref[...]
    ms = jnp.mean(x * x, axis=-1, keepdims=True)
    h = (x * lax.rsqrt(ms + NORM_EPS) * g_ref[...]).astype(_bf16)
    cos = cos_ref[...]
    sin = sin_ref[...]
    lane = lax.broadcasted_iota(jnp.int32, cos.shape, 1)
    first_half = (lane % (2 * ROPE_PAIRS_PER_AXIS)) < ROPE_PAIRS_PER_AXIS

    def head_norm_rope(z, gain):
        width = z.shape[1]
        ss = _dot((z * z).astype(_bf16), bd_ref[:width, :width])
        zn = z * lax.rsqrt(ss * (1.0 / HEAD_DIM) + NORM_EPS) * gain
        outs = []
        for j in range(width // LANES):
            c = zn[:, j * LANES:(j + 1) * LANES]
            partner = jnp.where(first_half,
                                pltpu.roll(c, LANES - ROPE_PAIRS_PER_AXIS, 1),
                                pltpu.roll(c, ROPE_PAIRS_PER_AXIS, 1))
            outs.append(c * cos + partner * sin)
        return outs

    c0 = 0
    zq = _dot(h, w_ref[:, c0:c0 + ATTN_WIDTH])
    for j, o in enumerate(head_norm_rope(zq, qg_ref[...])):
        q_ref[:, j * LANES:(j + 1) * LANES] = o.astype(_bf16)
    c0 += ATTN_WIDTH
    zk = _dot(h, w_ref[:, c0:c0 + KV_WIDTH])
    (kr,) = head_norm_rope(zk, kg_ref[...])
    for j in range(N_KV_HEADS):
        k_ref[j] = kr[:, j * HEAD_DIM:(j + 1) * HEAD_DIM].astype(_bf16)
    c0 += KV_WIDTH
    zvt = lax.dot_general(wvt_ref[...], h, (((1,), (1,)), ((), ())), preferred_element_type=_f32)
    for j in range(N_KV_HEADS):
        vt_ref[j] = zvt[j * HEAD_DIM:(j + 1) * HEAD_DIM, :].astype(_bf16)
    c0 += KV_WIDTH
    u_ref[...] = _dot(h, w_ref[:, c0:c0 + POOL_WIDTH]).astype(_bf16)
    c0 += POOL_WIDTH
    sa_ref[...] = _sigmoid(_dot(h, w_ref[:, c0:c0 + D_MODEL])).astype(_bf16)
    c0 += D_MODEL
    sb_ref[...] = _sigmoid(_dot(h, w_ref[:, c0:c0 + D_MODEL])).astype(_bf16)


def _in_proj(x2, norm_g, w_in, wvt, qg, kg, cos_t, sin_t, bd, seq):
    tokens = x2.shape[0]
    tm = IN_PROJ_ROWS
    in_width = w_in.shape[1]
    tiles_per_seq = seq // tm
    const = lambda i: (0, 0)
    row = lambda i: (i, 0)
    return pl.pallas_call(
        _in_proj_body,
        grid=(tokens // tm,),
        in_specs=[
            pl.BlockSpec((tm, D_MODEL), row),
            pl.BlockSpec((1, D_MODEL), const),
            pl.BlockSpec((D_MODEL, in_width), const),
            pl.BlockSpec((KV_WIDTH, D_MODEL), const),
            pl.BlockSpec((1, ATTN_WIDTH), const),
            pl.BlockSpec((1, KV_WIDTH), const),
            pl.BlockSpec((tm, LANES), lambda i: (i % tiles_per_seq, 0)),
            pl.BlockSpec((tm, LANES), lambda i: (i % tiles_per_seq, 0)),
            pl.BlockSpec((ATTN_WIDTH, ATTN_WIDTH), const),
        ],
        out_specs=[
            pl.BlockSpec((tm, ATTN_WIDTH), row),
            pl.BlockSpec((N_KV_HEADS, tm, HEAD_DIM), lambda i: (0, i, 0)),
            pl.BlockSpec((N_KV_HEADS, HEAD_DIM, tm), lambda i: (0, 0, i)),
            pl.BlockSpec((tm, POOL_WIDTH), row),
            pl.BlockSpec((tm, D_MODEL), row),
            pl.BlockSpec((tm, D_MODEL), row),
        ],
        out_shape=[
            jax.ShapeDtypeStruct((tokens, ATTN_WIDTH), _bf16),
            jax.ShapeDtypeStruct((N_KV_HEADS, tokens, HEAD_DIM), _bf16),
            jax.ShapeDtypeStruct((N_KV_HEADS, HEAD_DIM, tokens), _bf16),
            jax.ShapeDtypeStruct((tokens, POOL_WIDTH), _bf16),
            jax.ShapeDtypeStruct((tokens, D_MODEL), _bf16),
            jax.ShapeDtypeStruct((tokens, D_MODEL), _bf16),
        ],
        compiler_params=pltpu.CompilerParams(
            dimension_semantics=("parallel",), vmem_limit_bytes=VMEM_LIMIT),
        name="in_proj",
    )(x2, norm_g, w_in, wvt, qg, kg, cos_t, sin_t, bd)


def _attention_step(q_ref, k_ref, vt_ref, o_ref, st_new, m_new, st_old, m_old):
    tq = q_ref.shape[0]
    q = q_ref[...]
    qs = jnp.concatenate([q[:, g * HEAD_DIM:(g + 1) * HEAD_DIM] for g in range(N_GROUPS_PER_KV)], axis=0)
    seq = k_ref.shape[0]
    m_prev = m_old[...]
    m_run = denom = ot = None
    for c0 in range(0, seq, ATTN_KEY_CHUNK):
        rows = slice(c0, c0 + ATTN_KEY_CHUNK)
        st = lax.dot_general(k_ref[rows, :], qs, (((1,), (1,)), ((), ())), preferred_element_type=_f32)
        st_new[rows, :] = st
        m_c = jnp.max(st, axis=0, keepdims=True)
        m_run = m_c if m_run is None else jnp.maximum(m_run, m_c)
        p = jnp.exp2(st_old[rows, :] - m_prev)
        l_c = jnp.sum(p, axis=0, keepdims=True)
        o_c = _dot(vt_ref[:, rows], p.astype(_bf16))
        denom = l_c if denom is None else denom + l_c
        ot = o_c if ot is None else ot + o_c
    m_new[...] = m_run
    ot = ot * (1.0 / denom)
    stacked = jnp.concatenate([ot[:, g * tq:(g + 1) * tq] for g in range(N_GROUPS_PER_KV)], axis=0)
    o_ref[...] = stacked.T.astype(_bf16)


def _attention_body(q_ref, k_ref, vt_ref, o_ref, st_a, m_a, st_b, m_b):
    n = pl.program_id(0)

    @pl.when(n == 0)
    def _():
        st_b[...] = jnp.zeros_like(st_b)
        m_b[...] = jnp.zeros_like(m_b)

    @pl.when(n % 2 == 0)
    def _():
        _attention_step(q_ref, k_ref, vt_ref, o_ref, st_a, m_a, st_b, m_b)

    @pl.when(n % 2 == 1)
    def _():
        _attention_step(q_ref, k_ref, vt_ref, o_ref, st_b, m_b, st_a, m_a)


def _attention(q, k2, vt, batch, seq):
    tq = ATTN_Q_ROWS
    nq = seq // tq
    n_blocks = batch * N_KV_HEADS * nq
    group_width = N_GROUPS_PER_KV * HEAD_DIM
    lanes = N_GROUPS_PER_KV * tq

    def decode(n):
        return n // (N_KV_HEADS * nq), (n // nq) % N_KV_HEADS, n % nq

    def scores_of(n):
        return decode(jnp.minimum(n, n_blocks - 1))

    def finish_of(n):
        return decode(jnp.maximum(n - 1, 0))

    def q_map(n):
        b, kh, i = scores_of(n)
        return b * nq + i, kh

    def k_map(n):
        b, kh, _ = scores_of(n)
        return kh, b, 0

    def vt_map(n):
        b, kh, _ = finish_of(n)
        return kh, 0, b

    def o_map(n):
        b, kh, i = finish_of(n)
        return b * nq + i, kh

    return pl.pallas_call(
        _attention_body,
        grid=(n_blocks + 1,),
        in_specs=[
            pl.BlockSpec((tq, group_width), q_map),
            pl.BlockSpec((None, seq, HEAD_DIM), k_map),
            pl.BlockSpec((None, HEAD_DIM, seq), vt_map),
        ],
        out_specs=pl.BlockSpec((tq, group_width), o_map),
        out_shape=jax.ShapeDtypeStruct((batch * seq, ATTN_WIDTH), _bf16),
        scratch_shapes=[
            pltpu.VMEM((seq, lanes), _f32), pltpu.VMEM((1, lanes), _f32),
            pltpu.VMEM((seq, lanes), _f32), pltpu.VMEM((1, lanes), _f32),
        ],
        compiler_params=pltpu.CompilerParams(
            dimension_semantics=("arbitrary",), vmem_limit_bytes=VMEM_LIMIT),
        name="attention",
    )(q, k2, vt)


def _mix_out_body(seq, x_ref, a_ref, up_ref, u_ref, un_ref, sa_ref, sb_ref,
                  wap_ref, pgw_ref, ps_ref, wpp_ref, wo_ref, g2_ref, wr_ref, br_ref, tri_ref, anchor_ref,
                  x1_ref, h2_ref, idx_ref, gate_ref, rank_ref, cnt_ref,
                  ext_ref, lv_ref, carry_ref):
    del anchor_ref
    i = pl.program_id(0)
    tm = x_ref.shape[0]

    @pl.when(i == 0)
    def _():
        carry_ref[...] = jnp.zeros_like(carry_ref)
        ext_ref[...] = jnp.zeros_like(ext_ref)
        lv_ref[...] = jnp.zeros_like(lv_ref)

    start = (i * tm) % seq
    has_prev = (start != 0).astype(_f32)
    has_next = (start + tm != seq).astype(_f32)
    first = 2 * POOL_HALO
    ext_ref[POOL_HALO:first, :] = up_ref[...].astype(_f32) * has_prev
    ext_ref[first:first + tm, :] = u_ref[...].astype(_f32)
    ext_ref[first + tm:first + tm + POOL_HALO, :] = un_ref[...].astype(_f32) * has_next
    y_attn = _dot(a_ref[...], wap_ref[...])
    lo, n = POOL_HALO, tm + 2 * POOL_HALO
    pos = start + lax.broadcasted_iota(jnp.int32, (tm, 1), 0)
    pooled = []
    for g, w in enumerate(POOL_WINDOWS):
        half = w // 2
        assert half == 1 << g
        wide = slice(g * POOL_GROUP_DIM, POOL_WIDTH)
        cols = slice(g * POOL_GROUP_DIM, (g + 1) * POOL_GROUP_DIM)
        if g == 0:
            lv_ref[0, lo:lo + n, wide] = ext_ref[pl.ds(lo - 1, n), wide] + ext_ref[pl.ds(lo, n), wide]
        elif g < N_POOL_GROUPS - 1:
            lv_ref[g, lo:lo + n, wide] = (lv_ref[g - 1, pl.ds(lo - half // 2, n), wide]
                                          + lv_ref[g - 1, pl.ds(lo + half // 2, n), wide])
        if g < N_POOL_GROUPS - 1:
            tot = lv_ref[g, first:first + tm, cols]
        else:
            tot = (lv_ref[g - 1, pl.ds(first - half // 2, tm), cols]
                   + lv_ref[g - 1, pl.ds(first + half // 2, tm), cols])
        cnt = (jnp.minimum(pos + half, seq) - jnp.maximum(pos - half, 0)).astype(_f32)
        diff = tot / cnt - ext_ref[first:first + tm, cols]
        pooled.append(_dot(diff.astype(_bf16), pgw_ref[g]))
    pooled = (jnp.concatenate(pooled, axis=1) * ps_ref[...]).astype(_bf16)
    y_pool = _dot(pooled, wpp_ref[...])
    merged = sa_ref[...].astype(_f32) * y_attn + sb_ref[...].astype(_f32) * y_pool
    x1 = x_ref[...] + _dot(merged.astype(_bf16), wo_ref[...])
    x1_ref[...] = x1

    ms = jnp.mean(x1 * x1, axis=-1, keepdims=True)
    h2 = x1 * lax.rsqrt(ms + NORM_EPS) * g2_ref[...]
    h2_ref[...] = _pack_bf16_pair(h2[:, :HALF_MODEL], h2[:, HALF_MODEL:])
    hi, lo = _split_bf16(h2)
    both = _dot(hi, wr_ref[...])
    logits = both[:, :LANES] + both[:, LANES:] + _dot(lo, wr_ref[:, :LANES])
    lt = logits.T[:N_EXPERTS, :] + br_ref[...]
    eid = lax.broadcasted_iota(jnp.int32, lt.shape, 0)
    vals, idxs = [], []
    multi_hot = jnp.zeros(lt.shape, _f32)
    for _ in range(TOP_K):
        m = jnp.max(lt, axis=0, keepdims=True)
        sel = jnp.min(jnp.where(lt == m, eid, N_EXPERTS), axis=0, keepdims=True)
        hit = eid == sel
        vals.append(m)
        idxs.append(sel)
        multi_hot = multi_hot + hit.astype(_f32)
        lt = jnp.where(hit, -jnp.inf, lt)
    es = [jnp.exp(v - vals[0]) for v in vals]
    inv = 1.0 / (es[0] + es[1] + es[2] + es[3])
    gate_ref[...] = jnp.concatenate([e * inv for e in es], axis=0)
    idx_ref[...] = jnp.concatenate(idxs, axis=0)
    before = _dot(multi_hot.astype(_bf16), tri_ref[...]) + carry_ref[:, 0:1]
    ranks = [jnp.sum(jnp.where(eid == sel, before, 0.0), axis=0, keepdims=True) for sel in idxs]
    rank_ref[...] = jnp.concatenate(ranks, axis=0).astype(jnp.int32)
    carry_ref[...] = carry_ref[...] + jnp.sum(multi_hot, axis=1, keepdims=True)
    cnt_ref[...] = carry_ref[...]


def _mix_out(x2, attn, u, sa, sb, wap, pgw, ps, wpp, wo, g2, wr, br, tri, seq, anchor):
    tokens = x2.shape[0]
    tm = MIX_ROWS
    n = tokens // tm
    halo_blocks = tm // POOL_HALO
    last_halo = tokens // POOL_HALO - 1
    row = lambda i: (i, 0)
    const2 = lambda i: (0, 0)
    const3 = lambda i: (0, 0, 0)
    col = lambda i: (0, i)
    return pl.pallas_call(
        functools.partial(_mix_out_body, seq),
        grid=(n,),
        in_specs=[
            pl.BlockSpec((tm, D_MODEL), row),
            pl.BlockSpec((tm, ATTN_WIDTH), row),
            pl.BlockSpec((POOL_HALO, POOL_WIDTH), lambda i: (jnp.maximum(i * halo_blocks - 1, 0), 0)),
            pl.BlockSpec((tm, POOL_WIDTH), row),
            pl.BlockSpec((POOL_HALO, POOL_WIDTH), lambda i: (jnp.minimum((i + 1) * halo_blocks, last_halo), 0)),
            pl.BlockSpec((tm, D_MODEL), row),
            pl.BlockSpec((tm, D_MODEL), row),
            pl.BlockSpec((ATTN_WIDTH, D_MODEL), const2),
            pl.BlockSpec((N_POOL_GROUPS, POOL_GROUP_DIM, POOL_GROUP_DIM), const3),
            pl.BlockSpec((1, POOL_WIDTH), const2),
            pl.BlockSpec((POOL_WIDTH, D_MODEL), const2),
            pl.BlockSpec((D_MODEL, D_MODEL), const2),
            pl.BlockSpec((1, D_MODEL), const2),
            pl.BlockSpec((D_MODEL, 2 * LANES), const2),
            pl.BlockSpec((N_EXPERTS, 1), const2),
            pl.BlockSpec((tm, tm), const2),
            pl.BlockSpec(memory_space=pl.ANY),
        ],
        out_specs=[
            pl.BlockSpec((tm, D_MODEL), row),
            pl.BlockSpec((tm, HALF_MODEL), row),
            pl.BlockSpec((TOP_K, tm), col),
            pl.BlockSpec((TOP_K, tm), col),
            pl.BlockSpec((TOP_K, tm), col),
            pl.BlockSpec((N_EXPERTS, LANES), const2),
        ],
        out_shape=[
            jax.ShapeDtypeStruct((tokens, D_MODEL), _f32),
            jax.ShapeDtypeStruct((tokens, HALF_MODEL), jnp.uint32),
            jax.ShapeDtypeStruct((TOP_K, tokens), jnp.int32),
            jax.ShapeDtypeStruct((TOP_K, tokens), _f32),
            jax.ShapeDtypeStruct((TOP_K, tokens), jnp.int32),
            jax.ShapeDtypeStruct((N_EXPERTS, LANES), _f32),
        ],
        scratch_shapes=[
            pltpu.VMEM((tm + 4 * POOL_HALO, POOL_WIDTH), _f32),
            pltpu.VMEM((N_POOL_GROUPS - 1, tm + 4 * POOL_HALO, POOL_WIDTH), _f32),
            pltpu.VMEM((N_EXPERTS, LANES), _f32),
        ],
        compiler_params=pltpu.CompilerParams(
            dimension_semantics=("arbitrary",), vmem_limit_bytes=VMEM_LIMIT),
        name="mix_out",
    )(x2, attn, u, u, u, sa, sb, wap, pgw, ps, wpp, wo, g2, wr, br, tri, anchor)


def _worker_ranges(group_tokens, per_worker, workers):
    first = [0]
    for t in group_tokens:
        assert t % per_worker == 0
        first.append(first[-1] + t // per_worker)
    assert first[-1] == workers
    return first


def _dispatch(dest4, pad3, hs, n_rows):
    workers, n_chunks = dest4.shape[0], dest4.shape[1]
    n_pad_chunks = pad3.shape[1]
    per_worker = n_chunks * SC_CHUNK
    first_worker = _worker_ranges([h.shape[0] for h in hs], per_worker, workers)
    width, dtype = hs[0].shape[1], hs[0].dtype
    assert n_chunks % 2 == 0
    mesh = plsc.VectorSubcoreMesh(core_axis_name="c", subcore_axis_name="s")

    @functools.partial(
        pl.kernel, mesh=mesh,
        out_type=jax.ShapeDtypeStruct((n_rows, width), dtype),
        scratch_types=[
            pltpu.VMEM((n_chunks, TOP_K, SC_CHUNK), jnp.int32),
            pltpu.VMEM((n_pad_chunks, SC_CHUNK), jnp.int32),
            pltpu.VMEM((SC_CHUNK, width), dtype),
            pltpu.VMEM((SC_CHUNK, width), dtype),
            pltpu.SemaphoreType.DMA,
            pltpu.SemaphoreType.DMA,
            pltpu.SemaphoreType.DMA,
        ],
        name="dispatch",
    )
    def body(dest_hbm, pad_hbm, *refs):
        h_hbms = refs[:len(hs)]
        out_hbm, idx_v, pad_v, rows0, rows1, load_sem0, load_sem1, store_sem = refs[len(hs):]
        wid = lax.axis_index("s") * SC_CORES + lax.axis_index("c")
        pltpu.sync_copy(dest_hbm.at[wid], idx_v)
        pltpu.sync_copy(pad_hbm.at[wid], pad_v)

        zero = jnp.zeros((SC_LANES,), dtype)

        @pl.loop(0, SC_CHUNK)
        def _(r):
            @pl.loop(0, width, step=SC_LANES)
            def _(c):
                rows0.at[r, pl.ds(c, SC_LANES)][...] = zero

        @pl.loop(0, n_pad_chunks)
        def _(c):
            pltpu.sync_copy(rows0, out_hbm.at[pad_v.at[c]])

        def scatter_tokens(src_hbm, base):
            bufs = (rows0, rows1)
            load_sems = (load_sem0, load_sem1)

            def load(c, b):
                return pltpu.make_async_copy(src_hbm.at[pl.ds(base + c * SC_CHUNK, SC_CHUNK)], bufs[b], load_sems[b])

            load(0, 0).start()

            @pl.loop(0, n_chunks, step=2)
            def _(c):
                for b in range(2):
                    cc = c + b
                    load(cc, b).wait()

                    @pl.when(cc + 1 < n_chunks)
                    def _():
                        load(cc + 1, 1 - b).start()

                    copies = [pltpu.make_async_copy(bufs[b], out_hbm.at[idx_v.at[cc, k]], store_sem)
                              for k in range(TOP_K)]
                    for cp in copies:
                        cp.start()
                    for cp in copies:
                        cp.wait()

        for g, h_hbm in enumerate(h_hbms):
            @pl.when(jnp.logical_and(wid >= first_worker[g], wid < first_worker[g + 1]))
            def _(g=g, h_hbm=h_hbm):
                scatter_tokens(h_hbm, (wid - first_worker[g]) * per_worker)

    return body(dest4, pad3, *hs)


def _experts_body(be_ref, nu_ref, first_ref, slot_ref, next_ref, half_ref, xs_ref, wgu_hbm, bgu_ref, wd_hbm, bd_ref,
                  y_ref, wgu_buf, wd_buf, sems):
    i = pl.program_id(0)
    used = i < nu_ref[0]
    slot = slot_ref[i]

    def weight_copies(e, s):
        return (pltpu.make_async_copy(wgu_hbm.at[e], wgu_buf.at[s], sems.at[s]),
                pltpu.make_async_copy(wd_hbm.at[e], wd_buf.at[s], sems.at[s]))

    @pl.when(i == 0)
    def _():
        for cp in weight_copies(be_ref[0], 0):
            cp.start()

    @pl.when(first_ref[i] == 1)
    def _():
        for cp in weight_copies(be_ref[i], slot):
            cp.wait()

        @pl.when(next_ref[i] >= 0)
        def _():
            for cp in weight_copies(next_ref[i], 1 - slot):
                cp.start()

    @pl.when(jnp.logical_not(used))
    def _():
        y_ref[...] = jnp.zeros_like(y_ref)

    def run_rows(n):
        x = jnp.concatenate(_unpack_bf16_pair(xs_ref[:n, :]), axis=1).astype(_bf16)
        gu = _dot(x, wgu_buf[slot].astype(_bf16)) + bgu_ref[...]
        gate = jnp.minimum(gu[:, :D_FF], SWIGLU_LIMIT)
        up = jnp.clip(gu[:, D_FF:], -SWIGLU_LIMIT, SWIGLU_LIMIT)
        act = (up + 1.0) * (gate * _sigmoid(SWIGLU_ALPHA * gate))
        y = _dot(act.astype(_bf16), wd_buf[slot].astype(_bf16)) + bd_ref[...]
        y_ref[:n, :] = _pack_bf16_pair(y[:, :HALF_MODEL], y[:, HALF_MODEL:])

    for parts in range(1, MOE_PARTS + 1):
        @pl.when(jnp.logical_and(used, half_ref[i] == parts))
        def _(parts=parts):
            n = parts * (MOE_ROWS // MOE_PARTS)
            run_rows(n)
            if n < MOE_ROWS:
                y_ref[n:, :] = jnp.zeros((MOE_ROWS - n, HALF_MODEL), jnp.uint32)


def _experts(block_e, n_used, first, slot, next_e, half, xs, wgu, bgu, wd, bd):
    n_blocks = block_e.shape[0]
    n_rows = n_blocks * MOE_ROWS
    rows = lambda i, be, nu, *_: (jnp.minimum(i, nu[0] - 1), 0)
    per_e = lambda i, be, *_: (be[i], 0, 0)
    return pl.pallas_call(
        _experts_body,
        grid_spec=pltpu.PrefetchScalarGridSpec(
            num_scalar_prefetch=6,
            grid=(n_blocks,),
            in_specs=[
                pl.BlockSpec((MOE_ROWS, HALF_MODEL), rows),
                pl.BlockSpec(memory_space=pl.ANY),
                pl.BlockSpec((None, 1, 2 * D_FF), per_e),
                pl.BlockSpec(memory_space=pl.ANY),
                pl.BlockSpec((None, 1, D_MODEL), per_e),
            ],
            out_specs=pl.BlockSpec((MOE_ROWS, HALF_MODEL), lambda i, *_: (i, 0)),
            scratch_shapes=[
                pltpu.VMEM((2, D_MODEL, 2 * D_FF), _f32),
                pltpu.VMEM((2, D_FF, D_MODEL), _f32),
                pltpu.SemaphoreType.DMA((2,)),
            ],
        ),
        out_shape=jax.ShapeDtypeStruct((n_rows, HALF_MODEL), jnp.uint32),
        compiler_params=pltpu.CompilerParams(
            dimension_semantics=("arbitrary",), vmem_limit_bytes=VMEM_LIMIT),
        name="experts",
    )(block_e, n_used, first, slot, next_e, half, xs, wgu, bgu, wd, bd)


def _combine(dest_rows, gates, x1s, yb):
    workers = dest_rows.shape[0]
    per_worker = dest_rows.shape[1] * LANES // TOP_K
    n_chunks = per_worker // SC_COMBINE_CHUNK
    first_worker = _worker_ranges([x.shape[0] for x in x1s], per_worker, workers)
    n_groups = len(x1s)
    assert n_chunks % 2 == 0 and LANES % SC_COMBINE_CHUNK == 0
    mesh = plsc.VectorSubcoreMesh(core_axis_name="c", subcore_axis_name="s")
    row_buf = pltpu.VMEM((SC_COMBINE_CHUNK, D_MODEL), _f32)
    packed_buf = pltpu.VMEM((SC_COMBINE_CHUNK, HALF_MODEL), jnp.uint32)

    @functools.partial(
        pl.kernel, mesh=mesh,
        out_type=[jax.ShapeDtypeStruct(x.shape, _f32) for x in x1s],
        scratch_types=[
            pltpu.VMEM(dest_rows.shape[1:], jnp.int32),
            [[packed_buf] * TOP_K + [row_buf]] * 2,
            [pltpu.VMEM((SC_COMBINE_CHUNK, TOP_K * SC_LANES), _f32)] * 2,
            [pltpu.SemaphoreType.DMA] * 2, [pltpu.SemaphoreType.DMA] * 2, [pltpu.SemaphoreType.DMA] * 2,
        ],
        compiler_params=pltpu.CompilerParams(needs_layout_passes=False),
        name="combine",
    )
    def body(dest_hbm, *refs):
        g_hbms, x_hbms = refs[:n_groups], refs[n_groups:2 * n_groups]
        yb_hbm = refs[2 * n_groups]
        o_hbms = refs[2 * n_groups + 1:3 * n_groups + 1]
        idx_v, row_bufs, gate_bufs, gather_sems, load_sems, store_sems = refs[3 * n_groups + 1:]
        wid = lax.axis_index("s") * SC_CORES + lax.axis_index("c")
        pltpu.sync_copy(dest_hbm.at[wid], idx_v)

        def run(g_hbm, x_hbm, o_hbm, base):
            def loads(c, b):
                rows = pl.ds(base + c * SC_COMBINE_CHUNK, SC_COMBINE_CHUNK)
                cps = []
                for k in range(TOP_K):
                    pos = (c * TOP_K + k) * SC_COMBINE_CHUNK
                    ids = idx_v.at[pos // LANES, pl.ds(pos % LANES, SC_COMBINE_CHUNK)]
                    cps.append(pltpu.make_async_copy(yb_hbm.at[ids], row_bufs[b][k], gather_sems[b]))
                cps.append(pltpu.make_async_copy(x_hbm.at[rows], row_bufs[b][TOP_K], load_sems[b]))
                cps.append(pltpu.make_async_copy(g_hbm.at[rows], gate_bufs[b], load_sems[b]))
                return cps

            def store(c, b):
                rows = pl.ds(base + c * SC_COMBINE_CHUNK, SC_COMBINE_CHUNK)
                return pltpu.make_async_copy(row_bufs[b][TOP_K], o_hbm.at[rows], store_sems[b])

            for cp in loads(0, 0):
                cp.start()

            @pl.loop(0, n_chunks, step=2)
            def _(c):
                for b in range(2):
                    cc = c + b

                    @pl.when(cc + 1 < n_chunks)
                    def _():
                        @pl.when(cc >= 1)
                        def _():
                            store(cc - 1, 1 - b).wait()
                        for cp in loads(cc + 1, 1 - b):
                            cp.start()

                    for cp in loads(cc, b):
                        cp.wait()
                    acc_buf, gate_buf = row_bufs[b][TOP_K], gate_bufs[b]

                    @pl.loop(0, SC_COMBINE_CHUNK)
                    def _(r):
                        g = [gate_buf.at[r, pl.ds(k * SC_LANES, SC_LANES)][...] for k in range(TOP_K)]

                        @plsc.parallel_loop(0, HALF_MODEL, step=SC_LANES, unroll=4)
                        def _(col):
                            lo = acc_buf.at[r, pl.ds(col, SC_LANES)][...]
                            hi = acc_buf.at[r, pl.ds(HALF_MODEL + col, SC_LANES)][...]
                            for k in range(TOP_K):
                                words = row_bufs[b][k].at[r, pl.ds(col, SC_LANES)][...]
                                lo = lo + g[k] * plsc.bitcast(lax.shift_left(words, jnp.uint32(16)), _f32)
                                hi = hi + g[k] * plsc.bitcast(words & jnp.uint32(_HIGH_HALF), _f32)
                            acc_buf.at[r, pl.ds(col, SC_LANES)][...] = lo
                            acc_buf.at[r, pl.ds(HALF_MODEL + col, SC_LANES)][...] = hi

                    store(cc, b).start()

            store(n_chunks - 2, 0).wait()
            store(n_chunks - 1, 1).wait()

        for g in range(n_groups):
            @pl.when(jnp.logical_and(wid >= first_worker[g], wid < first_worker[g + 1]))
            def _(g=g):
                run(g_hbms[g], x_hbms[g], o_hbms[g], (wid - first_worker[g]) * per_worker)

    return body(dest_rows, *gates, *x1s, yb)


def _rope_tables(seq):
    pos = np.arange(seq)
    row_ids = (pos // GRID_W).astype(np.float32)
    col_ids = (pos % GRID_W).astype(np.float32)
    inv_freq = np.float32(ROPE_THETA) ** (-np.arange(ROPE_PAIRS_PER_AXIS, dtype=np.float32) / ROPE_PAIRS_PER_AXIS)
    ang_r = row_ids[:, None] * inv_freq
    ang_c = col_ids[:, None] * inv_freq
    cos = np.concatenate([np.cos(ang_r)] * 2 + [np.cos(ang_c)] * 2, axis=1)
    sin = np.concatenate([-np.sin(ang_r), np.sin(ang_r), -np.sin(ang_c), np.sin(ang_c)], axis=1)
    reps = LANES // HEAD_DIM
    return (jnp.asarray(np.tile(cos, (1, reps)).astype(np.float32)),
            jnp.asarray(np.tile(sin, (1, reps)).astype(np.float32)))


def _mixer(x, p, anchor):
    batch, seq, _ = x.shape
    x2 = x.reshape(batch * seq, D_MODEL)
    cos_t, sin_t = _rope_tables(seq)
    q, k2, vt, u, sa, sb = _in_proj(x2, p["mix_g"], p["w_in"], p["wvt"], p["qg"], p["kg"], cos_t, sin_t, p["bd"], seq)
    attn = _attention(q, k2, vt, batch, seq)
    return _mix_out(x2, attn, u, sa, sb, p["wap"], p["pgw"], p["ps"], p["wpp"], p["wo"], p["ffn_g"],
                    p["wr"], p["br"], p["tri"], seq, anchor)


def _moe_dispatch(mixed):
    counts = [m[5][:, 0].astype(jnp.int32) for m in mixed]
    total = sum(counts)
    padded = ((total + MOE_ROWS - 1) // MOE_ROWS) * MOE_ROWS
    padded_end = jnp.cumsum(padded)
    padded_start = padded_end - padded
    n_tok = sum(m[0].shape[0] for m in mixed)
    n_blocks = (n_tok * TOP_K + MOE_ROWS - 1) // MOE_ROWS + N_EXPERTS
    n_used = (padded_end[-1] // MOE_ROWS).astype(jnp.int32)
    blk = jnp.minimum(jnp.arange(n_blocks, dtype=jnp.int32), n_used - 1)
    block_e = jnp.sum((padded_end[None, :] <= (blk * MOE_ROWS)[:, None]).astype(jnp.int32), axis=1)
    block_e = jnp.minimum(block_e, N_EXPERTS - 1)
    expert_ids = jnp.arange(N_EXPERTS, dtype=jnp.int32)[:, None, None]
    dests = []
    seen = jnp.zeros((N_EXPERTS,), jnp.int32)
    for m, c in zip(mixed, counts):
        base = (padded_start + seen)[:, None, None]
        dests.append(jnp.sum(jnp.where(m[2][None] == expert_ids, base, 0), axis=0) + m[4])
        seen = seen + c
    workers = SC_CORES * SC_SUBCORES
    per_worker = n_tok // workers
    dest4 = jnp.concatenate(
        [d.reshape(TOP_K, -1, per_worker // SC_CHUNK, SC_CHUNK).transpose(1, 2, 0, 3) for d in dests], axis=0)
    n_rows = n_blocks * MOE_ROWS
    slot = jnp.arange(MOE_ROWS, dtype=jnp.int32)[None, :]
    pad_rows = jnp.where(slot < (padded - total)[:, None], (padded_start + total)[:, None] + slot,
                         n_rows + jnp.arange(N_EXPERTS, dtype=jnp.int32)[:, None] * MOE_ROWS + slot)
    pad3 = pad_rows.astype(jnp.int32).reshape(workers, -1, SC_CHUNK)
    xs_buf = _dispatch(dest4, pad3, [m[1] for m in mixed], n_rows + N_EXPERTS * MOE_ROWS)
    return xs_buf, dict(mixed=mixed, dests=dests, per_worker=per_worker, n_blocks=n_blocks, n_used=n_used, block_e=block_e,
                        padded_end=padded_end, padded_start=padded_start, total=total)


def _moe_finish(xs_buf, r, w_gu, b_gu, w_down, b_down):
    mixed, dests, per_worker, n_blocks, n_used = r["mixed"], r["dests"], r["per_worker"], r["n_blocks"], r["n_used"]
    block_e, padded_end, padded_start, total = r["block_e"], r["padded_end"], r["padded_start"], r["total"]
    blk_id = jnp.arange(n_blocks, dtype=jnp.int32)
    first = jnp.logical_and(block_e != jnp.concatenate([jnp.full((1,), -1, jnp.int32), block_e[:-1]]),
                            blk_id < n_used).astype(jnp.int32)
    slot = (jnp.cumsum(first) - 1) % 2
    group_end = jnp.sum(jnp.where(block_e[:, None] == jnp.arange(N_EXPERTS, dtype=jnp.int32)[None, :],
                                  (padded_end // MOE_ROWS)[None, :], 0), axis=1)
    next_e = jnp.sum(jnp.where(group_end[:, None] == blk_id[None, :], block_e[None, :], 0), axis=1)
    next_e = jnp.where(group_end < n_used, next_e, -1)
    token_end = jnp.sum(jnp.where(block_e[:, None] == jnp.arange(N_EXPERTS, dtype=jnp.int32)[None, :],
                                  (padded_start + total)[None, :], 0), axis=1)
    part_rows = MOE_ROWS // MOE_PARTS
    half = jnp.clip((token_end - blk_id * MOE_ROWS + part_rows - 1) // part_rows, 1, MOE_PARTS).astype(jnp.int32)
    yb = _experts(block_e, n_used.reshape(1), first, slot.astype(jnp.int32), next_e.astype(jnp.int32), half, xs_buf,
                  w_gu, b_gu.reshape(N_EXPERTS, 1, -1), w_down, b_down.reshape(N_EXPERTS, 1, -1))
    dest_rows = jnp.concatenate(
        [d.reshape(TOP_K, -1, per_worker // SC_COMBINE_CHUNK, SC_COMBINE_CHUNK).transpose(1, 2, 0, 3)
         .reshape(-1, per_worker * TOP_K // LANES, LANES) for d in dests], axis=0)
    gate_vecs = [jnp.repeat(m[3].T, SC_LANES, axis=1) for m in mixed]
    return _combine(dest_rows, gate_vecs, [m[0] for m in mixed], yb)


def kernel(x_prompt, x_sample, mix_norm_g, w_in, q_norm_g, k_norm_g, w_attn_proj, pool_group_w, pool_scale,
           w_pool_proj, w_out, ffn_norm_g, w_router, b_router, w_gu, b_gu, w_down, b_down):
    depth = w_in.shape[0]
    xs_all = [x_prompt, x_sample]
    head_id = np.arange(ATTN_WIDTH) // HEAD_DIM
    block_diag = jnp.asarray(head_id[:, None] == head_id[None, :], _bf16)
    tri_id = np.arange(MIX_ROWS)
    tri = jnp.asarray(tri_id[:, None] < tri_id[None, :], _bf16)
    for l in range(depth):
        wr = jnp.pad(w_router[l].astype(_f32), ((0, 0), (0, LANES - N_EXPERTS)))
        wrh = wr.astype(_bf16)
        w_in_bf = w_in[l].astype(_bf16)
        p = dict(
            mix_g=mix_norm_g[l].reshape(1, D_MODEL),
            w_in=w_in_bf,
            wvt=w_in_bf[:, ATTN_WIDTH + KV_WIDTH:ATTN_WIDTH + 2 * KV_WIDTH].T,
            qg=jnp.tile(q_norm_g[l] * (HEAD_DIM ** -0.5 * LOG2_E), N_HEADS).reshape(1, ATTN_WIDTH),
            kg=jnp.tile(k_norm_g[l], N_KV_HEADS).reshape(1, KV_WIDTH),
            bd=block_diag,
            wap=w_attn_proj[l].astype(_bf16),
            pgw=pool_group_w[l].astype(_bf16),
            ps=pool_scale[l].reshape(1, POOL_WIDTH),
            wpp=w_pool_proj[l].astype(_bf16),
            wo=w_out[l].astype(_bf16),
            ffn_g=ffn_norm_g[l].reshape(1, D_MODEL),
            wr=jnp.concatenate([wrh, (wr - wrh.astype(_f32)).astype(_bf16)], axis=1),
            br=b_router[l].astype(_f32).reshape(N_EXPERTS, 1),
            tri=tri,
        )
        routed = []
        anchor = jnp.zeros((SUBLANES, LANES), _f32)
        for x in xs_all:
            xs_buf, r = _moe_dispatch([_mixer(x, p, anchor)])
            routed.append((xs_buf, r))
            anchor = xs_buf
        outs = [_moe_finish(xs_buf, r, w_gu[l], b_gu[l], w_down[l], b_down[l])[0] for xs_buf, r in routed]
        xs_all = [o.reshape(x.shape) for o, x in zip(outs, xs_all)]
    return tuple(xs_all)
```

```python
import functools

import jax
import jax.numpy as jnp
import numpy as np
from jax import lax
from jax.experimental import pallas as pl
from jax.experimental.pallas import tpu as pltpu
from jax.experimental.pallas import tpu_sc as plsc

D_MODEL = 1024
HALF_MODEL = D_MODEL // 2
GRID_W = 64
N_HEADS = 8
N_KV_HEADS = 2
HEAD_DIM = 64
N_GROUPS_PER_KV = N_HEADS // N_KV_HEADS
ATTN_WIDTH = N_HEADS * HEAD_DIM
KV_WIDTH = N_KV_HEADS * HEAD_DIM
ROPE_THETA = 10000.0
ROPE_PAIRS_PER_AXIS = HEAD_DIM // 4
POOL_WINDOWS = (2, 4, 8, 16)
N_POOL_GROUPS = 4
POOL_WIDTH = 512
POOL_GROUP_DIM = POOL_WIDTH // N_POOL_GROUPS
N_EXPERTS = 32
TOP_K = 4
D_FF = 1024
SWIGLU_ALPHA = 1.702
SWIGLU_LIMIT = 7.0
NORM_EPS = 1e-6

LANES = 128
SUBLANES = 8
POOL_HALO = 8
LOG2_E = 1.4426950408889634

IN_PROJ_ROWS = 1024
ATTN_Q_ROWS = 256
ATTN_KEY_CHUNK = 256
MIX_ROWS = 1024
MOE_ROWS = 512
MOE_PARTS = 4
SC_CORES = 2
SC_SUBCORES = 16
SC_LANES = 16
SC_CHUNK = 32
SC_COMBINE_CHUNK = 16
V7X_VMEM_BYTES = 64 * 1024 * 1024
VMEM_LIMIT = V7X_VMEM_BYTES * 7 // 8

_bf16 = jnp.bfloat16
_f32 = jnp.float32


def _dot(a, b):
    return jnp.dot(a, b, preferred_element_type=_f32)


def _split_bf16(x):
    hi = x.astype(_bf16)
    lo = (x - hi.astype(_f32)).astype(_bf16)
    return hi, lo


def _sigmoid(x):
    return 1.0 / (1.0 + jnp.exp(-x))


_HIGH_HALF = 0xFFFF0000


def _pack_bf16_pair(lo, hi):
    lo_bits = lax.bitcast_convert_type(lo.astype(_bf16).astype(_f32), jnp.uint32)
    hi_bits = lax.bitcast_convert_type(hi.astype(_bf16).astype(_f32), jnp.uint32)
    return lax.shift_right_logical(lo_bits, jnp.uint32(16)) | (hi_bits & jnp.uint32(_HIGH_HALF))


def _unpack_bf16_pair(words):
    lo = lax.bitcast_convert_type(lax.shift_left(words, jnp.uint32(16)), _f32)
    hi = lax.bitcast_convert_type(words & jnp.uint32(_HIGH_HALF), _f32)
    return lo, hi


def _in_proj_body(x_ref, g_ref, w_ref, wvt_ref, qg_ref, kg_ref, cos_ref, sin_ref, bd_ref,
                  q_ref, k_ref, vt_ref, u_ref, sa_ref, sb_ref):
    x = x_ref[...]
    ms = jnp.mean(x * x, axis=-1, keepdims=True)
    h = (x * lax.rsqrt(ms + NORM_EPS) * g_ref[...]).astype(_bf16)
    cos = cos_ref[...]
    sin = sin_ref[...]
    lane = lax.broadcasted_iota(jnp.int32, cos.shape, 1)
    first_half = (lane % (2 * ROPE_PAIRS_PER_AXIS)) < ROPE_PAIRS_PER_AXIS

    def head_norm_rope(z, gain):
        width = z.shape[1]
        ss = _dot((z * z).astype(_bf16), bd_ref[:width, :width])
        zn = z * lax.rsqrt(ss * (1.0 / HEAD_DIM) + NORM_EPS) * gain
        outs = []
        for j in range(width // LANES):
            c = zn[:, j * LANES:(j + 1) * LANES]
            partner = jnp.where(first_half,
                                pltpu.roll(c, LANES - ROPE_PAIRS_PER_AXIS, 1),
                                pltpu.roll(c, ROPE_PAIRS_PER_AXIS, 1))
            outs.append(c * cos + partner * sin)
        return outs

    c0 = 0
    zq = _dot(h, w_ref[:, c0:c0 + ATTN_WIDTH])
    for j, o in enumerate(head_norm_rope(zq, qg_ref[...])):
        q_ref[:, j * LANES:(j + 1) * LANES] = o.astype(_bf16)
    c0 += ATTN_WIDTH
    zk = _dot(h, w_ref[:, c0:c0 + KV_WIDTH])
    (kr,) = head_norm_rope(zk, kg_ref[...])
    for j in range(N_KV_HEADS):
        k_ref[j] = kr[:, j * HEAD_DIM:(j + 1) * HEAD_DIM].astype(_bf16)
    c0 += KV_WIDTH
    zvt = lax.dot_general(wvt_ref[...], h, (((1,), (1,)), ((), ())), preferred_element_type=_f32)
    for j in range(N_KV_HEADS):
        vt_ref[j] = zvt[j * HEAD_DIM:(j + 1) * HEAD_DIM, :].astype(_bf16)
    c0 += KV_WIDTH
    u_ref[...] = _dot(h, w_ref[:, c0:c0 + POOL_WIDTH]).astype(_bf16)
    c0 += POOL_WIDTH
    sa_ref[...] = _sigmoid(_dot(h, w_ref[:, c0:c0 + D_MODEL])).astype(_bf16)
    c0 += D_MODEL
    sb_ref[...] = _sigmoid(_dot(h, w_ref[:, c0:c0 + D_MODEL])).astype(_bf16)


def _in_proj(x2, norm_g, w_in, wvt, qg, kg, cos_t, sin_t, bd, seq):
    tokens = x2.shape[0]
    tm = IN_PROJ_ROWS
    in_width = w_in.shape[1]
    tiles_per_seq = seq // tm
    const = lambda i: (0, 0)
    row = lambda i: (i, 0)
    return pl.pallas_call(
        _in_proj_body,
        grid=(tokens // tm,),
        in_specs=[
            pl.BlockSpec((tm, D_MODEL), row),
            pl.BlockSpec((1, D_MODEL), const),
            pl.BlockSpec((D_MODEL, in_width), const),
            pl.BlockSpec((KV_WIDTH, D_MODEL), const),
            pl.BlockSpec((1, ATTN_WIDTH), const),
            pl.BlockSpec((1, KV_WIDTH), const),
            pl.BlockSpec((tm, LANES), lambda i: (i % tiles_per_seq, 0)),
            pl.BlockSpec((tm, LANES), lambda i: (i % tiles_per_seq, 0)),
            pl.BlockSpec((ATTN_WIDTH, ATTN_WIDTH), const),
        ],
        out_specs=[
            pl.BlockSpec((tm, ATTN_WIDTH), row),
            pl.BlockSpec((N_KV_HEADS, tm, HEAD_DIM), lambda i: (0, i, 0)),
            pl.BlockSpec((N_KV_HEADS, HEAD_DIM, tm), lambda i: (0, 0, i)),
            pl.BlockSpec((tm, POOL_WIDTH), row),
            pl.BlockSpec((tm, D_MODEL), row),
            pl.BlockSpec((tm, D_MODEL), row),
        ],
        out_shape=[
            jax.ShapeDtypeStruct((tokens, ATTN_WIDTH), _bf16),
            jax.ShapeDtypeStruct((N_KV_HEADS, tokens, HEAD_DIM), _bf16),
            jax.ShapeDtypeStruct((N_KV_HEADS, HEAD_DIM, tokens), _bf16),
            jax.ShapeDtypeStruct((tokens, POOL_WIDTH), _bf16),
            jax.ShapeDtypeStruct((tokens, D_MODEL), _bf16),
            jax.ShapeDtypeStruct((tokens, D_MODEL), _bf16),
        ],
        compiler_params=pltpu.CompilerParams(
            dimension_semantics=("parallel",), vmem_limit_bytes=VMEM_LIMIT),
        name="in_proj",
    )(x2, norm_g, w_in, wvt, qg, kg, cos_t, sin_t, bd)


def _attention_step(q_ref, k_ref, vt_ref, o_ref, st_new, m_new, st_old, m_old):
    tq = q_ref.shape[0]
    q = q_ref[...]
    qs = jnp.concatenate([q[:, g * HEAD_DIM:(g + 1) * HEAD_DIM] for g in range(N_GROUPS_PER_KV)], axis=0)
    seq = k_ref.shape[0]
    m_prev = m_old[...]
    m_run = denom = ot = None
    for c0 in range(0, seq, ATTN_KEY_CHUNK):
        rows = slice(c0, c0 + ATTN_KEY_CHUNK)
        st = lax.dot_general(k_ref[rows, :], qs, (((1,), (1,)), ((), ())), preferred_element_type=_f32)
        st_new[rows, :] = st
        m_c = jnp.max(st, axis=0, keepdims=True)
        m_run = m_c if m_run is None else jnp.maximum(m_run, m_c)
        p = jnp.exp2(st_old[rows, :] - m_prev)
        l_c = jnp.sum(p, axis=0, keepdims=True)
        o_c = _dot(vt_ref[:, rows], p.astype(_bf16))
        denom = l_c if denom is None else denom + l_c
        ot = o_c if ot is None else ot + o_c
    m_new[...] = m_run
    ot = ot * (1.0 / denom)
    stacked = jnp.concatenate([ot[:, g * tq:(g + 1) * tq] for g in range(N_GROUPS_PER_KV)], axis=0)
    o_ref[...] = stacked.T.astype(_bf16)


def _attention_body(q_ref, k_ref, vt_ref, o_ref, st_a, m_a, st_b, m_b):
    n = pl.program_id(0)

    @pl.when(n == 0)
    def _():
        st_b[...] = jnp.zeros_like(st_b)
        m_b[...] = jnp.zeros_like(m_b)

    @pl.when(n % 2 == 0)
    def _():
        _attention_step(q_ref, k_ref, vt_ref, o_ref, st_a, m_a, st_b, m_b)

    @pl.when(n % 2 == 1)
    def _():
        _attention_step(q_ref, k_ref, vt_ref, o_ref, st_b, m_b, st_a, m_a)


def _attention(q, k2, vt, batch, seq):
    tq = ATTN_Q_ROWS
    nq = seq // tq
    n_blocks = batch * N_KV_HEADS * nq
    group_width = N_GROUPS_PER_KV * HEAD_DIM
    lanes = N_GROUPS_PER_KV * tq

    def decode(n):
        return n // (N_KV_HEADS * nq), (n // nq) % N_KV_HEADS, n % nq

    def scores_of(n):
        return decode(jnp.minimum(n, n_blocks - 1))

    def finish_of(n):
        return decode(jnp.maximum(n - 1, 0))

    def q_map(n):
        b, kh, i = scores_of(n)
        return b * nq + i, kh

    def k_map(n):
        b, kh, _ = scores_of(n)
        return kh, b, 0

    def vt_map(n):
        b, kh, _ = finish_of(n)
        return kh, 0, b

    def o_map(n):
        b, kh, i = finish_of(n)
        return b * nq + i, kh

    return pl.pallas_call(
        _attention_body,
        grid=(n_blocks + 1,),
        in_specs=[
            pl.BlockSpec((tq, group_width), q_map),
            pl.BlockSpec((None, seq, HEAD_DIM), k_map),
            pl.BlockSpec((None, HEAD_DIM, seq), vt_map),
        ],
        out_specs=pl.BlockSpec((tq, group_width), o_map),
        out_shape=jax.ShapeDtypeStruct((batch * seq, ATTN_WIDTH), _bf16),
        scratch_shapes=[
            pltpu.VMEM((seq, lanes), _f32), pltpu.VMEM((1, lanes), _f32),
            pltpu.VMEM((seq, lanes), _f32), pltpu.VMEM((1, lanes), _f32),
        ],
        compiler_params=pltpu.CompilerParams(
            dimension_semantics=("arbitrary",), vmem_limit_bytes=VMEM_LIMIT),
        name="attention",
    )(q, k2, vt)


def _mix_out_body(seq, x_ref, a_ref, up_ref, u_ref, un_ref, sa_ref, sb_ref,
                  wap_ref, pgw_ref, ps_ref, wpp_ref, wo_ref, g2_ref, wr_ref, br_ref, tri_ref, anchor_ref,
                  x1_ref, h2_ref, idx_ref, gate_ref, rank_ref, cnt_ref,
                  ext_ref, lv_ref, carry_ref):
    del anchor_ref
    i = pl.program_id(0)
    tm = x_ref.shape[0]

    @pl.when(i == 0)
    def _():
        carry_ref[...] = jnp.zeros_like(carry_ref)
        ext_ref[...] = jnp.zeros_like(ext_ref)
        lv_ref[...] = jnp.zeros_like(lv_ref)

    start = (i * tm) % seq
    has_prev = (start != 0).astype(_f32)
    has_next = (start + tm != seq).astype(_f32)
    first = 2 * POOL_HALO
    ext_ref[POOL_HALO:first, :] = up_ref[...].astype(_f32) * has_prev
    ext_ref[first:first + tm, :] = u_ref[...].astype(_f32)
    ext_ref[first + tm:first + tm + POOL_HALO, :] = un_ref[...].astype(_f32) * has_next
    y_attn = _dot(a_ref[...], wap_ref[...])
    lo, n = POOL_HALO, tm + 2 * POOL_HALO
    pos = start + lax.broadcasted_iota(jnp.int32, (tm, 1), 0)
    pooled = []
    for g, w in enumerate(POOL_WINDOWS):
        half = w // 2
        assert half == 1 << g
        wide = slice(g * POOL_GROUP_DIM, POOL_WIDTH)
        cols = slice(g * POOL_GROUP_DIM, (g + 1) * POOL_GROUP_DIM)
        if g == 0:
            lv_ref[0, lo:lo + n, wide] = ext_ref[pl.ds(lo - 1, n), wide] + ext_ref[pl.ds(lo, n), wide]
        elif g < N_POOL_GROUPS - 1:
            lv_ref[g, lo:lo + n, wide] = (lv_ref[g - 1, pl.ds(lo - half // 2, n), wide]
                                          + lv_ref[g - 1, pl.ds(lo + half // 2, n), wide])
        if g < N_POOL_GROUPS - 1:
            tot = lv_ref[g, first:first + tm, cols]
        else:
            tot = (lv_ref[g - 1, pl.ds(first - half // 2, tm), cols]
                   + lv_ref[g - 1, pl.ds(first + half // 2, tm), cols])
        cnt = (jnp.minimum(pos + half, seq) - jnp.maximum(pos - half, 0)).astype(_f32)
        diff = tot / cnt - ext_ref[first:first + tm, cols]
        pooled.append(_dot(diff.astype(_bf16), pgw_ref[g]))
    pooled = (jnp.concatenate(pooled, axis=1) * ps_ref[...]).astype(_bf16)
    y_pool = _dot(pooled, wpp_ref[...])
    merged = sa_ref[...].astype(_f32) * y_attn + sb_ref[...].astype(_f32) * y_pool
    x1 = x_ref[...] + _dot(merged.astype(_bf16), wo_ref[...])
    x1_ref[...] = x1

    ms = jnp.mean(x1 * x1, axis=-1, keepdims=True)
    h2 = x1 * lax.rsqrt(ms + NORM_EPS) * g2_ref[...]
    h2_ref[...] = _pack_bf16_pair(h2[:, :HALF_MODEL], h2[:, HALF_MODEL:])
    hi, lo = _split_bf16(h2)
    both = _dot(hi, wr_ref[...])
    logits = both[:, :LANES] + both[:, LANES:] + _dot(lo, wr_ref[:, :LANES])
    lt = logits.T[:N_EXPERTS, :] + br_ref[...]
    eid = lax.broadcasted_iota(jnp.int32, lt.shape, 0)
    vals, idxs = [], []
    multi_hot = jnp.zeros(lt.shape, _f32)
    for _ in range(TOP_K):
        m = jnp.max(lt, axis=0, keepdims=True)
        sel = jnp.min(jnp.where(lt == m, eid, N_EXPERTS), axis=0, keepdims=True)
        hit = eid == sel
        vals.append(m)
        idxs.append(sel)
        multi_hot = multi_hot + hit.astype(_f32)
        lt = jnp.where(hit, -jnp.inf, lt)
    es = [jnp.exp(v - vals[0]) for v in vals]
    inv = 1.0 / (es[0] + es[1] + es[2] + es[3])
    gate_ref[...] = jnp.concatenate([e * inv for e in es], axis=0)
    idx_ref[...] = jnp.concatenate(idxs, axis=0)
    before = _dot(multi_hot.astype(_bf16), tri_ref[...]) + carry_ref[:, 0:1]
    ranks = [jnp.sum(jnp.where(eid == sel, before, 0.0), axis=0, keepdims=True) for sel in idxs]
    rank_ref[...] = jnp.concatenate(ranks, axis=0).astype(jnp.int32)
    carry_ref[...] = carry_ref[...] + jnp.sum(multi_hot, axis=1, keepdims=True)
    cnt_ref[...] = carry_ref[...]


def _mix_out(x2, attn, u, sa, sb, wap, pgw, ps, wpp, wo, g2, wr, br, tri, seq, anchor):
    tokens = x2.shape[0]
    tm = MIX_ROWS
    n = tokens // tm
    halo_blocks = tm // POOL_HALO
    last_halo = tokens // POOL_HALO - 1
    row = lambda i: (i, 0)
    const2 = lambda i: (0, 0)
    const3 = lambda i: (0, 0, 0)
    col = lambda i: (0, i)
    return pl.pallas_call(
        functools.partial(_mix_out_body, seq),
        grid=(n,),
        in_specs=[
            pl.BlockSpec((tm, D_MODEL), row),
            pl.BlockSpec((tm, ATTN_WIDTH), row),
            pl.BlockSpec((POOL_HALO, POOL_WIDTH), lambda i: (jnp.maximum(i * halo_blocks - 1, 0), 0)),
            pl.BlockSpec((tm, POOL_WIDTH), row),
            pl.BlockSpec((POOL_HALO, POOL_WIDTH), lambda i: (jnp.minimum((i + 1) * halo_blocks, last_halo), 0)),
            pl.BlockSpec((tm, D_MODEL), row),
            pl.BlockSpec((tm, D_MODEL), row),
            pl.BlockSpec((ATTN_WIDTH, D_MODEL), const2),
            pl.BlockSpec((N_POOL_GROUPS, POOL_GROUP_DIM, POOL_GROUP_DIM), const3),
            pl.BlockSpec((1, POOL_WIDTH), const2),
            pl.BlockSpec((POOL_WIDTH, D_MODEL), const2),
            pl.BlockSpec((D_MODEL, D_MODEL), const2),
            pl.BlockSpec((1, D_MODEL), const2),
            pl.BlockSpec((D_MODEL, 2 * LANES), const2),
            pl.BlockSpec((N_EXPERTS, 1), const2),
            pl.BlockSpec((tm, tm), const2),
            pl.BlockSpec(memory_space=pl.ANY),
        ],
        out_specs=[
            pl.BlockSpec((tm, D_MODEL), row),
            pl.BlockSpec((tm, HALF_MODEL), row),
            pl.BlockSpec((TOP_K, tm), col),
            pl.BlockSpec((TOP_K, tm), col),
            pl.BlockSpec((TOP_K, tm), col),
            pl.BlockSpec((N_EXPERTS, LANES), const2),
        ],
        out_shape=[
            jax.ShapeDtypeStruct((tokens, D_MODEL), _f32),
            jax.ShapeDtypeStruct((tokens, HALF_MODEL), jnp.uint32),
            jax.ShapeDtypeStruct((TOP_K, tokens), jnp.int32),
            jax.ShapeDtypeStruct((TOP_K, tokens), _f32),
            jax.ShapeDtypeStruct((TOP_K, tokens), jnp.int32),
            jax.ShapeDtypeStruct((N_EXPERTS, LANES), _f32),
        ],
        scratch_shapes=[
            pltpu.VMEM((tm + 4 * POOL_HALO, POOL_WIDTH), _f32),
            pltpu.VMEM((N_POOL_GROUPS - 1, tm + 4 * POOL_HALO, POOL_WIDTH), _f32),
            pltpu.VMEM((N_EXPERTS, LANES), _f32),
        ],
        compiler_params=pltpu.CompilerParams(
            dimension_semantics=("arbitrary",), vmem_limit_bytes=VMEM_LIMIT),
        name="mix_out",
    )(x2, attn, u, u, u, sa, sb, wap, pgw, ps, wpp, wo, g2, wr, br, tri, anchor)


def _worker_ranges(group_tokens, per_worker, workers):
    first = [0]
    for t in group_tokens:
        assert t % per_worker == 0
        first.append(first[-1] + t // per_worker)
    assert first[-1] == workers
    return first


def _dispatch(dest4, pad3, hs, n_rows):
    workers, n_chunks = dest4.shape[0], dest4.shape[1]
    n_pad_chunks = pad3.shape[1]
    per_worker = n_chunks * SC_CHUNK
    first_worker = _worker_ranges([h.shape[0] for h in hs], per_worker, workers)
    width, dtype = hs[0].shape[1], hs[0].dtype
    assert n_chunks % 2 == 0
    mesh = plsc.VectorSubcoreMesh(core_axis_name="c", subcore_axis_name="s")

    @functools.partial(
        pl.kernel, mesh=mesh,
        out_type=jax.ShapeDtypeStruct((n_rows, width), dtype),
        scratch_types=[
            pltpu.VMEM((n_chunks, TOP_K, SC_CHUNK), jnp.int32),
            pltpu.VMEM((n_pad_chunks, SC_CHUNK), jnp.int32),
            pltpu.VMEM((SC_CHUNK, width), dtype),
            pltpu.VMEM((SC_CHUNK, width), dtype),
            pltpu.SemaphoreType.DMA,
            pltpu.SemaphoreType.DMA,
            pltpu.SemaphoreType.DMA,
        ],
        name="dispatch",
    )
    def body(dest_hbm, pad_hbm, *refs):
        h_hbms = refs[:len(hs)]
        out_hbm, idx_v, pad_v, rows0, rows1, load_sem0, load_sem1, store_sem = refs[len(hs):]
        wid = lax.axis_index("s") * SC_CORES + lax.axis_index("c")
        pltpu.sync_copy(dest_hbm.at[wid], idx_v)
        pltpu.sync_copy(pad_hbm.at[wid], pad_v)

        zero = jnp.zeros((SC_LANES,), dtype)

        @pl.loop(0, SC_CHUNK)
        def _(r):
            @pl.loop(0, width, step=SC_LANES)
            def _(c):
                rows0.at[r, pl.ds(c, SC_LANES)][...] = zero

        @pl.loop(0, n_pad_chunks)
        def _(c):
            pltpu.sync_copy(rows0, out_hbm.at[pad_v.at[c]])

        def scatter_tokens(src_hbm, base):
            bufs = (rows0, rows1)
            load_sems = (load_sem0, load_sem1)

            def load(c, b):
                return pltpu.make_async_copy(src_hbm.at[pl.ds(base + c * SC_CHUNK, SC_CHUNK)], bufs[b], load_sems[b])

            load(0, 0).start()

            @pl.loop(0, n_chunks, step=2)
            def _(c):
                for b in range(2):
                    cc = c + b
                    load(cc, b).wait()

                    @pl.when(cc + 1 < n_chunks)
                    def _():
                        load(cc + 1, 1 - b).start()

                    copies = [pltpu.make_async_copy(bufs[b], out_hbm.at[idx_v.at[cc, k]], store_sem)
                              for k in range(TOP_K)]
                    for cp in copies:
                        cp.start()
                    for cp in copies:
                        cp.wait()

        for g, h_hbm in enumerate(h_hbms):
            @pl.when(jnp.logical_and(wid >= first_worker[g], wid < first_worker[g + 1]))
            def _(g=g, h_hbm=h_hbm):
                scatter_tokens(h_hbm, (wid - first_worker[g]) * per_worker)

    return body(dest4, pad3, *hs)


def _experts_body(be_ref, nu_ref, first_ref, slot_ref, next_ref, half_ref, xs_hbm, wgu_hbm, bgu_ref, wd_hbm, bd_ref,
                  y_hbm, xs_buf, y_buf, wgu_buf, wd_buf, in_sems, out_sems, w_sems):
    n_used = nu_ref[0]

    def weight_copies(e, s):
        return (pltpu.make_async_copy(wgu_hbm.at[e], wgu_buf.at[s], w_sems.at[s]),
                pltpu.make_async_copy(wd_hbm.at[e], wd_buf.at[s], w_sems.at[s]))

    def rows_in(i, b):
        return pltpu.make_async_copy(xs_hbm.at[pl.ds(i * MOE_ROWS, MOE_ROWS)], xs_buf.at[b], in_sems.at[b])

    def rows_out(i, b):
        return pltpu.make_async_copy(y_buf.at[b], y_hbm.at[pl.ds(i * MOE_ROWS, MOE_ROWS)], out_sems.at[b])

    rows_in(0, 0).start()
    for cp in weight_copies(be_ref[0], 0):
        cp.start()

    def step(i, carry):
        b = i % 2
        slot = slot_ref[i]
        e = be_ref[i]
        rows_in(i, b).wait()

        @pl.when(i + 1 < n_used)
        def _():
            rows_in(i + 1, 1 - b).start()

        @pl.when(first_ref[i] == 1)
        def _():
            for cp in weight_copies(e, slot):
                cp.wait()

            @pl.when(next_ref[i] >= 0)
            def _():
                for cp in weight_copies(next_ref[i], 1 - slot):
                    cp.start()

        @pl.when(i >= 2)
        def _():
            rows_out(i - 2, b).wait()

        def run_rows(n):
            x = jnp.concatenate(_unpack_bf16_pair(xs_buf[b, :n, :]), axis=1).astype(_bf16)
            gu = _dot(x, wgu_buf[slot].astype(_bf16)) + bgu_ref[e]
            gate = jnp.minimum(gu[:, :D_FF], SWIGLU_LIMIT)
            up = jnp.clip(gu[:, D_FF:], -SWIGLU_LIMIT, SWIGLU_LIMIT)
            act = (up + 1.0) * (gate * _sigmoid(SWIGLU_ALPHA * gate))
            y = _dot(act.astype(_bf16), wd_buf[slot].astype(_bf16)) + bd_ref[e]
            y_buf[b, :n, :] = _pack_bf16_pair(y[:, :HALF_MODEL], y[:, HALF_MODEL:])

        for parts in range(1, MOE_PARTS + 1):
            @pl.when(half_ref[i] == parts)
            def _(parts=parts):
                n = parts * (MOE_ROWS // MOE_PARTS)
                run_rows(n)
                if n < MOE_ROWS:
                    y_buf[b, n:, :] = jnp.zeros((MOE_ROWS - n, HALF_MODEL), jnp.uint32)

        rows_out(i, b).start()
        return carry

    lax.fori_loop(0, n_used, step, 0)

    @pl.when(n_used >= 2)
    def _():
        rows_out(n_used - 2, n_used % 2).wait()

    rows_out(n_used - 1, (n_used - 1) % 2).wait()


def _experts(block_e, n_used, first, slot, next_e, half, xs, wgu, bgu, wd, bd):
    n_blocks = block_e.shape[0]
    n_rows = n_blocks * MOE_ROWS
    whole = lambda i, *_: (0, 0, 0)
    return pl.pallas_call(
        _experts_body,
        grid_spec=pltpu.PrefetchScalarGridSpec(
            num_scalar_prefetch=6,
            grid=(1,),
            in_specs=[
                pl.BlockSpec(memory_space=pl.ANY),
                pl.BlockSpec(memory_space=pl.ANY),
                pl.BlockSpec((N_EXPERTS, 1, 2 * D_FF), whole),
                pl.BlockSpec(memory_space=pl.ANY),
                pl.BlockSpec((N_EXPERTS, 1, D_MODEL), whole),
            ],
            out_specs=pl.BlockSpec(memory_space=pl.ANY),
            scratch_shapes=[
                pltpu.VMEM((2, MOE_ROWS, HALF_MODEL), jnp.uint32),
                pltpu.VMEM((2, MOE_ROWS, HALF_MODEL), jnp.uint32),
                pltpu.VMEM((2, D_MODEL, 2 * D_FF), _f32),
                pltpu.VMEM((2, D_FF, D_MODEL), _f32),
                pltpu.SemaphoreType.DMA((2,)),
                pltpu.SemaphoreType.DMA((2,)),
                pltpu.SemaphoreType.DMA((2,)),
            ],
        ),
        out_shape=jax.ShapeDtypeStruct((n_rows, HALF_MODEL), jnp.uint32),
        compiler_params=pltpu.CompilerParams(
            dimension_semantics=("arbitrary",), vmem_limit_bytes=VMEM_LIMIT),
        name="experts",
    )(block_e, n_used, first, slot, next_e, half, xs, wgu, bgu, wd, bd)


def _combine(dest_rows, gates, x1s, yb):
    workers = dest_rows.shape[0]
    per_worker = dest_rows.shape[1] * LANES // TOP_K
    n_chunks = per_worker // SC_COMBINE_CHUNK
    first_worker = _worker_ranges([x.shape[0] for x in x1s], per_worker, workers)
    n_groups = len(x1s)
    assert n_chunks % 2 == 0 and LANES % SC_COMBINE_CHUNK == 0
    mesh = plsc.VectorSubcoreMesh(core_axis_name="c", subcore_axis_name="s")
    row_buf = pltpu.VMEM((SC_COMBINE_CHUNK, D_MODEL), _f32)
    packed_buf = pltpu.VMEM((SC_COMBINE_CHUNK, HALF_MODEL), jnp.uint32)

    @functools.partial(
        pl.kernel, mesh=mesh,
        out_type=[jax.ShapeDtypeStruct(x.shape, _f32) for x in x1s],
        scratch_types=[
            pltpu.VMEM(dest_rows.shape[1:], jnp.int32),
            [[packed_buf] * TOP_K + [row_buf]] * 2,
            [pltpu.VMEM((SC_COMBINE_CHUNK, TOP_K * SC_LANES), _f32)] * 2,
            [pltpu.SemaphoreType.DMA] * 2, [pltpu.SemaphoreType.DMA] * 2, [pltpu.SemaphoreType.DMA] * 2,
        ],
        compiler_params=pltpu.CompilerParams(needs_layout_passes=False),
        name="combine",
    )
    def body(dest_hbm, *refs):
        g_hbms, x_hbms = refs[:n_groups], refs[n_groups:2 * n_groups]
        yb_hbm = refs[2 * n_groups]
        o_hbms = refs[2 * n_groups + 1:3 * n_groups + 1]
        idx_v, row_bufs, gate_bufs, gather_sems, load_sems, store_sems = refs[3 * n_groups + 1:]
        wid = lax.axis_index("s") * SC_CORES + lax.axis_index("c")
        pltpu.sync_copy(dest_hbm.at[wid], idx_v)

        def run(g_hbm, x_hbm, o_hbm, base):
            def loads(c, b):
                rows = pl.ds(base + c * SC_COMBINE_CHUNK, SC_COMBINE_CHUNK)
                cps = []
                for k in range(TOP_K):
                    pos = (c * TOP_K + k) * SC_COMBINE_CHUNK
                    ids = idx_v.at[pos // LANES, pl.ds(pos % LANES, SC_COMBINE_CHUNK)]
                    cps.append(pltpu.make_async_copy(yb_hbm.at[ids], row_bufs[b][k], gather_sems[b]))
                cps.append(pltpu.make_async_copy(x_hbm.at[rows], row_bufs[b][TOP_K], load_sems[b]))
                cps.append(pltpu.make_async_copy(g_hbm.at[rows], gate_bufs[b], load_sems[b]))
                return cps

            def store(c, b):
                rows = pl.ds(base + c * SC_COMBINE_CHUNK, SC_COMBINE_CHUNK)
                return pltpu.make_async_copy(row_bufs[b][TOP_K], o_hbm.at[rows], store_sems[b])

            for cp in loads(0, 0):
                cp.start()

            @pl.loop(0, n_chunks, step=2)
            def _(c):
                for b in range(2):
                    cc = c + b

                    @pl.when(cc + 1 < n_chunks)
                    def _():
                        @pl.when(cc >= 1)
                        def _():
                            store(cc - 1, 1 - b).wait()
                        for cp in loads(cc + 1, 1 - b):
                            cp.start()

                    for cp in loads(cc, b):
                        cp.wait()
                    acc_buf, gate_buf = row_bufs[b][TOP_K], gate_bufs[b]

                    @pl.loop(0, SC_COMBINE_CHUNK)
                    def _(r):
                        g = [gate_buf.at[r, pl.ds(k * SC_LANES, SC_LANES)][...] for k in range(TOP_K)]

                        @plsc.parallel_loop(0, HALF_MODEL, step=SC_LANES, unroll=4)
                        def _(col):
                            lo = acc_buf.at[r, pl.ds(col, SC_LANES)][...]
                            hi = acc_buf.at[r, pl.ds(HALF_MODEL + col, SC_LANES)][...]
                            for k in range(TOP_K):
                                words = row_bufs[b][k].at[r, pl.ds(col, SC_LANES)][...]
                                lo = lo + g[k] * plsc.bitcast(lax.shift_left(words, jnp.uint32(16)), _f32)
                                hi = hi + g[k] * plsc.bitcast(words & jnp.uint32(_HIGH_HALF), _f32)
                            acc_buf.at[r, pl.ds(col, SC_LANES)][...] = lo
                            acc_buf.at[r, pl.ds(HALF_MODEL + col, SC_LANES)][...] = hi

                    store(cc, b).start()

            store(n_chunks - 2, 0).wait()
            store(n_chunks - 1, 1).wait()

        for g in range(n_groups):
            @pl.when(jnp.logical_and(wid >= first_worker[g], wid < first_worker[g + 1]))
            def _(g=g):
                run(g_hbms[g], x_hbms[g], o_hbms[g], (wid - first_worker[g]) * per_worker)

    return body(dest_rows, *gates, *x1s, yb)


def _rope_tables(seq):
    pos = np.arange(seq)
    row_ids = (pos // GRID_W).astype(np.float32)
    col_ids = (pos % GRID_W).astype(np.float32)
    inv_freq = np.float32(ROPE_THETA) ** (-np.arange(ROPE_PAIRS_PER_AXIS, dtype=np.float32) / ROPE_PAIRS_PER_AXIS)
    ang_r = row_ids[:, None] * inv_freq
    ang_c = col_ids[:, None] * inv_freq
    cos = np.concatenate([np.cos(ang_r)] * 2 + [np.cos(ang_c)] * 2, axis=1)
    sin = np.concatenate([-np.sin(ang_r), np.sin(ang_r), -np.sin(ang_c), np.sin(ang_c)], axis=1)
    reps = LANES // HEAD_DIM
    return (jnp.asarray(np.tile(cos, (1, reps)).astype(np.float32)),
            jnp.asarray(np.tile(sin, (1, reps)).astype(np.float32)))


def _mixer(x, p, anchor):
    batch, seq, _ = x.shape
    x2 = x.reshape(batch * seq, D_MODEL)
    cos_t, sin_t = _rope_tables(seq)
    q, k2, vt, u, sa, sb = _in_proj(x2, p["mix_g"], p["w_in"], p["wvt"], p["qg"], p["kg"], cos_t, sin_t, p["bd"], seq)
    attn = _attention(q, k2, vt, batch, seq)
    return _mix_out(x2, attn, u, sa, sb, p["wap"], p["pgw"], p["ps"], p["wpp"], p["wo"], p["ffn_g"],
                    p["wr"], p["br"], p["tri"], seq, anchor)


def _moe_dispatch(mixed):
    counts = [m[5][:, 0].astype(jnp.int32) for m in mixed]
    total = sum(counts)
    padded = ((total + MOE_ROWS - 1) // MOE_ROWS) * MOE_ROWS
    padded_end = jnp.cumsum(padded)
    padded_start = padded_end - padded
    n_tok = sum(m[0].shape[0] for m in mixed)
    n_blocks = (n_tok * TOP_K + MOE_ROWS - 1) // MOE_ROWS + N_EXPERTS
    n_used = (padded_end[-1] // MOE_ROWS).astype(jnp.int32)
    blk = jnp.minimum(jnp.arange(n_blocks, dtype=jnp.int32), n_used - 1)
    block_e = jnp.sum((padded_end[None, :] <= (blk * MOE_ROWS)[:, None]).astype(jnp.int32), axis=1)
    block_e = jnp.minimum(block_e, N_EXPERTS - 1)
    expert_ids = jnp.arange(N_EXPERTS, dtype=jnp.int32)[:, None, None]
    dests = []
    seen = jnp.zeros((N_EXPERTS,), jnp.int32)
    for m, c in zip(mixed, counts):
        base = (padded_start + seen)[:, None, None]
        dests.append(jnp.sum(jnp.where(m[2][None] == expert_ids, base, 0), axis=0) + m[4])
        seen = seen + c
    workers = SC_CORES * SC_SUBCORES
    per_worker = n_tok // workers
    dest4 = jnp.concatenate(
        [d.reshape(TOP_K, -1, per_worker // SC_CHUNK, SC_CHUNK).transpose(1, 2, 0, 3) for d in dests], axis=0)
    n_rows = n_blocks * MOE_ROWS
    slot = jnp.arange(MOE_ROWS, dtype=jnp.int32)[None, :]
    pad_rows = jnp.where(slot < (padded - total)[:, None], (padded_start + total)[:, None] + slot,
                         n_rows + jnp.arange(N_EXPERTS, dtype=jnp.int32)[:, None] * MOE_ROWS + slot)
    pad3 = pad_rows.astype(jnp.int32).reshape(workers, -1, SC_CHUNK)
    xs_buf = _dispatch(dest4, pad3, [m[1] for m in mixed], n_rows + N_EXPERTS * MOE_ROWS)
    return xs_buf, dict(mixed=mixed, dests=dests, per_worker=per_worker, n_blocks=n_blocks, n_used=n_used, block_e=block_e,
                        padded_end=padded_end, padded_start=padded_start, total=total)


def _moe_finish(xs_buf, r, w_gu, b_gu, w_down, b_down):
    mixed, dests, per_worker, n_blocks, n_used = r["mixed"], r["dests"], r["per_worker"], r["n_blocks"], r["n_used"]
    block_e, padded_end, padded_start, total = r["block_e"], r["padded_end"], r["padded_start"], r["total"]
    blk_id = jnp.arange(n_blocks, dtype=jnp.int32)
    first = jnp.logical_and(block_e != jnp.concatenate([jnp.full((1,), -1, jnp.int32), block_e[:-1]]),
                            blk_id < n_used).astype(jnp.int32)
    slot = (jnp.cumsum(first) - 1) % 2
    group_end = jnp.sum(jnp.where(block_e[:, None] == jnp.arange(N_EXPERTS, dtype=jnp.int32)[None, :],
                                  (padded_end // MOE_ROWS)[None, :], 0), axis=1)
    next_e = jnp.sum(jnp.where(group_end[:, None] == blk_id[None, :], block_e[None, :], 0), axis=1)
    next_e = jnp.where(group_end < n_used, next_e, -1)
    token_end = jnp.sum(jnp.where(block_e[:, None] == jnp.arange(N_EXPERTS, dtype=jnp.int32)[None, :],
                                  (padded_start + total)[None, :], 0), axis=1)
    part_rows = MOE_ROWS // MOE_PARTS
    half = jnp.clip((token_end - blk_id * MOE_ROWS + part_rows - 1) // part_rows, 1, MOE_PARTS).astype(jnp.int32)
    yb = _experts(block_e, n_used.reshape(1), first, slot.astype(jnp.int32), next_e.astype(jnp.int32), half, xs_buf,
                  w_gu, b_gu.reshape(N_EXPERTS, 1, -1), w_down, b_down.reshape(N_EXPERTS, 1, -1))
    dest_rows = jnp.concatenate(
        [d.reshape(TOP_K, -1, per_worker // SC_COMBINE_CHUNK, SC_COMBINE_CHUNK).transpose(1, 2, 0, 3)
         .reshape(-1, per_worker * TOP_K // LANES, LANES) for d in dests], axis=0)
    gate_vecs = [jnp.repeat(m[3].T, SC_LANES, axis=1) for m in mixed]
    return _combine(dest_rows, gate_vecs, [m[0] for m in mixed], yb)


def kernel(x_prompt, x_sample, mix_norm_g, w_in, q_norm_g, k_norm_g, w_attn_proj, pool_group_w, pool_scale,
           w_pool_proj, w_out, ffn_norm_g, w_router, b_router, w_gu, b_gu, w_down, b_down):
    depth = w_in.shape[0]
    xs_all = [x_prompt, x_sample]
    head_id = np.arange(ATTN_WIDTH) // HEAD_DIM
    block_diag = jnp.asarray(head_id[:, None] == head_id[None, :], _bf16)
    tri_id = np.arange(MIX_ROWS)
    tri = jnp.asarray(tri_id[:, None] < tri_id[None, :], _bf16)
    for l in range(depth):
        wr = jnp.pad(w_router[l].astype(_f32), ((0, 0), (0, LANES - N_EXPERTS)))
        wrh = wr.astype(_bf16)
        w_in_bf = w_in[l].astype(_bf16)
        p = dict(
            mix_g=mix_norm_g[l].reshape(1, D_MODEL),
            w_in=w_in_bf,
            wvt=w_in_bf[:, ATTN_WIDTH + KV_WIDTH:ATTN_WIDTH + 2 * KV_WIDTH].T,
            qg=jnp.tile(q_norm_g[l] * (HEAD_DIM ** -0.5 * LOG2_E), N_HEADS).reshape(1, ATTN_WIDTH),
            kg=jnp.tile(k_norm_g[l], N_KV_HEADS).reshape(1, KV_WIDTH),
            bd=block_diag,
            wap=w_attn_proj[l].astype(_bf16),
            pgw=pool_group_w[l].astype(_bf16),
            ps=pool_scale[l].reshape(1, POOL_WIDTH),
            wpp=w_pool_proj[l].astype(_bf16),
            wo=w_out[l].astype(_bf16),
            ffn_g=ffn_norm_g[l].reshape(1, D_MODEL),
            wr=jnp.concatenate([wrh, (wr - wrh.astype(_f32)).astype(_bf16)], axis=1),
            br=b_router[l].astype(_f32).reshape(N_EXPERTS, 1),
            tri=tri,
        )
        routed = []
        anchor = jnp.zeros((SUBLANES, LANES), _f32)
        for x in xs_all:
            xs_buf, r = _moe_dispatch([_mixer(x, p, anchor)])
            routed.append((xs_buf, r))
            anchor = xs_buf
        outs = [_moe_finish(xs_buf, r, w_gu[l], b_gu[l], w_down[l], b_down[l])[0] for xs_buf, r in routed]
        xs_all = [o.reshape(x.shape) for o, x in zip(outs, xs_all)]
    return tuple(xs_all)
```

```python
import functools

import jax
import jax.numpy as jnp
import numpy as np
from jax import lax
from jax.experimental import pallas as pl
from jax.experimental.pallas import tpu as pltpu
from jax.experimental.pallas import tpu_sc as plsc

D_MODEL = 1024
HALF_MODEL = D_MODEL // 2
GRID_W = 64
N_HEADS = 8
N_KV_HEADS = 2
HEAD_DIM = 64
N_GROUPS_PER_KV = N_HEADS // N_KV_HEADS
ATTN_WIDTH = N_HEADS * HEAD_DIM
KV_WIDTH = N_KV_HEADS * HEAD_DIM
ROPE_THETA = 10000.0
ROPE_PAIRS_PER_AXIS = HEAD_DIM // 4
POOL_WINDOWS = (2, 4, 8, 16)
N_POOL_GROUPS = 4
POOL_WIDTH = 512
POOL_GROUP_DIM = POOL_WIDTH // N_POOL_GROUPS
N_EXPERTS = 32
TOP_K = 4
D_FF = 1024
SWIGLU_ALPHA = 1.702
SWIGLU_LIMIT = 7.0
NORM_EPS = 1e-6

LANES = 128
SUBLANES = 8
POOL_HALO = 8
LOG2_E = 1.4426950408889634

IN_PROJ_ROWS = 1024
ATTN_Q_ROWS = 256
ATTN_KEY_CHUNK = 256
MIX_ROWS = 1024
MOE_ROWS = 512
MOE_PARTS = 4
SC_CORES = 2
SC_SUBCORES = 16
SC_LANES = 16
SC_CHUNK = 32
SC_COMBINE_CHUNK = 16
V7X_VMEM_BYTES = 64 * 1024 * 1024
VMEM_LIMIT = V7X_VMEM_BYTES * 7 // 8

_bf16 = jnp.bfloat16
_f32 = jnp.float32


def _dot(a, b):
    return jnp.dot(a, b, preferred_element_type=_f32)


def _split_bf16(x):
    hi = x.astype(_bf16)
    lo = (x - hi.astype(_f32)).astype(_bf16)
    return hi, lo


def _sigmoid(x):
    return 1.0 / (1.0 + jnp.exp(-x))


_HIGH_HALF = 0xFFFF0000


def _pack_bf16_pair(lo, hi):
    lo_bits = lax.bitcast_convert_type(lo.astype(_bf16).astype(_f32), jnp.uint32)
    hi_bits = lax.bitcast_convert_type(hi.astype(_bf16).astype(_f32), jnp.uint32)
    return lax.shift_right_logical(lo_bits, jnp.uint32(16)) | (hi_bits & jnp.uint32(_HIGH_HALF))


def _unpack_bf16_pair(words):
    lo = lax.bitcast_convert_type(lax.shift_left(words, jnp.uint32(16)), _f32)
    hi = lax.bitcast_convert_type(words & jnp.uint32(_HIGH_HALF), _f32)
    return lo, hi


def _in_proj_body(x_ref, g_ref, w_ref, wvt_ref, qg_ref, kg_ref, cos_ref, sin_ref, bd_ref,
                  q_ref, k_ref, vt_ref, u_ref, sa_ref, sb_ref):
    x = x_ref[...]
    ms = jnp.mean(x * x, axis=-1, keepdims=True)
    h = (x * lax.rsqrt(ms + NORM_EPS) * g_ref[...]).astype(_bf16)
    cos = cos_ref[...]
    sin = sin_ref[...]
    lane = lax.broadcasted_iota(jnp.int32, cos.shape, 1)
    first_half = (lane % (2 * ROPE_PAIRS_PER_AXIS)) < ROPE_PAIRS_PER_AXIS

    def head_norm_rope(z, gain):
        width = z.shape[1]
        ss = _dot((z * z).astype(_bf16), bd_ref[:width, :width])
        zn = z * lax.rsqrt(ss * (1.0 / HEAD_DIM) + NORM_EPS) * gain
        outs = []
        for j in range(width // LANES):
            c = zn[:, j * LANES:(j + 1) * LANES]
            partner = jnp.where(first_half,
                                pltpu.roll(c, LANES - ROPE_PAIRS_PER_AXIS, 1),
                                pltpu.roll(c, ROPE_PAIRS_PER_AXIS, 1))
            outs.append(c * cos + partner * sin)
        return outs

    c0 = 0
    zq = _dot(h, w_ref[:, c0:c0 + ATTN_WIDTH])
    for j, o in enumerate(head_norm_rope(zq, qg_ref[...])):
        q_ref[:, j * LANES:(j + 1) * LANES] = o.astype(_bf16)
    c0 += ATTN_WIDTH
    zk = _dot(h, w_ref[:, c0:c0 + KV_WIDTH])
    (kr,) = head_norm_rope(zk, kg_ref[...])
    for j in range(N_KV_HEADS):
        k_ref[j] = kr[:, j * HEAD_DIM:(j + 1) * HEAD_DIM].astype(_bf16)
    c0 += KV_WIDTH
    zvt = lax.dot_general(wvt_ref[...], h, (((1,), (1,)), ((), ())), preferred_element_type=_f32)
    for j in range(N_KV_HEADS):
        vt_ref[j] = zvt[j * HEAD_DIM:(j + 1) * HEAD_DIM, :].astype(_bf16)
    c0 += KV_WIDTH
    u_ref[...] = _dot(h, w_ref[:, c0:c0 + POOL_WIDTH]).astype(_bf16)
    c0 += POOL_WIDTH
    sa_ref[...] = _sigmoid(_dot(h, w_ref[:, c0:c0 + D_MODEL])).astype(_bf16)
    c0 += D_MODEL
    sb_ref[...] = _sigmoid(_dot(h, w_ref[:, c0:c0 + D_MODEL])).astype(_bf16)


def _in_proj(x2, norm_g, w_in, wvt, qg, kg, cos_t, sin_t, bd, seq):
    tokens = x2.shape[0]
    tm = IN_PROJ_ROWS
    in_width = w_in.shape[1]
    tiles_per_seq = seq // tm
    const = lambda i: (0, 0)
    row = lambda i: (i, 0)
    return pl.pallas_call(
        _in_proj_body,
        grid=(tokens // tm,),
        in_specs=[
            pl.BlockSpec((tm, D_MODEL), row),
            pl.BlockSpec((1, D_MODEL), const),
            pl.BlockSpec((D_MODEL, in_width), const),
            pl.BlockSpec((KV_WIDTH, D_MODEL), const),
            pl.BlockSpec((1, ATTN_WIDTH), const),
            pl.BlockSpec((1, KV_WIDTH), const),
            pl.BlockSpec((tm, LANES), lambda i: (i % tiles_per_seq, 0)),
            pl.BlockSpec((tm, LANES), lambda i: (i % tiles_per_seq, 0)),
            pl.BlockSpec((ATTN_WIDTH, ATTN_WIDTH), const),
        ],
        out_specs=[
            pl.BlockSpec((tm, ATTN_WIDTH), row),
            pl.BlockSpec((N_KV_HEADS, tm, HEAD_DIM), lambda i: (0, i, 0)),
            pl.BlockSpec((N_KV_HEADS, HEAD_DIM, tm), lambda i: (0, 0, i)),
            pl.BlockSpec((tm, POOL_WIDTH), row),
            pl.BlockSpec((tm, D_MODEL), row),
            pl.BlockSpec((tm, D_MODEL), row),
        ],
        out_shape=[
            jax.ShapeDtypeStruct((tokens, ATTN_WIDTH), _bf16),
            jax.ShapeDtypeStruct((N_KV_HEADS, tokens, HEAD_DIM), _bf16),
            jax.ShapeDtypeStruct((N_KV_HEADS, HEAD_DIM, tokens), _bf16),
            jax.ShapeDtypeStruct((tokens, POOL_WIDTH), _bf16),
            jax.ShapeDtypeStruct((tokens, D_MODEL), _bf16),
            jax.ShapeDtypeStruct((tokens, D_MODEL), _bf16),
        ],
        compiler_params=pltpu.CompilerParams(
            dimension_semantics=("parallel",), vmem_limit_bytes=VMEM_LIMIT),
        name="in_proj",
    )(x2, norm_g, w_in, wvt, qg, kg, cos_t, sin_t, bd)


def _attention_step(q_ref, k_ref, vt_ref, o_ref, st_new, m_new, st_old, m_old):
    tq = q_ref.shape[0]
    q = q_ref[...]
    qs = jnp.concatenate([q[:, g * HEAD_DIM:(g + 1) * HEAD_DIM] for g in range(N_GROUPS_PER_KV)], axis=0)
    seq = k_ref.shape[0]
    m_prev = m_old[...]
    m_run = denom = ot = None
    for c0 in range(0, seq, ATTN_KEY_CHUNK):
        rows = slice(c0, c0 + ATTN_KEY_CHUNK)
        st = lax.dot_general(k_ref[rows, :], qs, (((1,), (1,)), ((), ())), preferred_element_type=_f32)
        st_new[rows, :] = st
        m_c = jnp.max(st, axis=0, keepdims=True)
        m_run = m_c if m_run is None else jnp.maximum(m_run, m_c)
        p = jnp.exp2(st_old[rows, :] - m_prev)
        l_c = jnp.sum(p, axis=0, keepdims=True)
        o_c = _dot(vt_ref[:, rows], p.astype(_bf16))
        denom = l_c if denom is None else denom + l_c
        ot = o_c if ot is None else ot + o_c
    m_new[...] = m_run
    ot = ot * (1.0 / denom)
    stacked = jnp.concatenate([ot[:, g * tq:(g + 1) * tq] for g in range(N_GROUPS_PER_KV)], axis=0)
    o_ref[...] = stacked.T.astype(_bf16)


def _attention_body(q_ref, k_ref, vt_ref, o_ref, st_a, m_a, st_b, m_b):
    n = pl.program_id(0)

    @pl.when(n == 0)
    def _():
        st_b[...] = jnp.zeros_like(st_b)
        m_b[...] = jnp.zeros_like(m_b)

    @pl.when(n % 2 == 0)
    def _():
        _attention_step(q_ref, k_ref, vt_ref, o_ref, st_a, m_a, st_b, m_b)

    @pl.when(n % 2 == 1)
    def _():
        _attention_step(q_ref, k_ref, vt_ref, o_ref, st_b, m_b, st_a, m_a)


def _attention(q, k2, vt, batch, seq):
    tq = ATTN_Q_ROWS
    nq = seq // tq
    n_blocks = batch * N_KV_HEADS * nq
    group_width = N_GROUPS_PER_KV * HEAD_DIM
    lanes = N_GROUPS_PER_KV * tq

    def decode(n):
        return n // (N_KV_HEADS * nq), (n // nq) % N_KV_HEADS, n % nq

    def scores_of(n):
        return decode(jnp.minimum(n, n_blocks - 1))

    def finish_of(n):
        return decode(jnp.maximum(n - 1, 0))

    def q_map(n):
        b, kh, i = scores_of(n)
        return b * nq + i, kh

    def k_map(n):
        b, kh, _ = scores_of(n)
        return kh, b, 0

    def vt_map(n):
        b, kh, _ = finish_of(n)
        return kh, 0, b

    def o_map(n):
        b, kh, i = finish_of(n)
        return b * nq + i, kh

    return pl.pallas_call(
        _attention_body,
        grid=(n_blocks + 1,),
        in_specs=[
            pl.BlockSpec((tq, group_width), q_map),
            pl.BlockSpec((None, seq, HEAD_DIM), k_map),
            pl.BlockSpec((None, HEAD_DIM, seq), vt_map),
        ],
        out_specs=pl.BlockSpec((tq, group_width), o_map),
        out_shape=jax.ShapeDtypeStruct((batch * seq, ATTN_WIDTH), _bf16),
        scratch_shapes=[
            pltpu.VMEM((seq, lanes), _f32), pltpu.VMEM((1, lanes), _f32),
            pltpu.VMEM((seq, lanes), _f32), pltpu.VMEM((1, lanes), _f32),
        ],
        compiler_params=pltpu.CompilerParams(
            dimension_semantics=("arbitrary",), vmem_limit_bytes=VMEM_LIMIT),
        name="attention",
    )(q, k2, vt)


def _mix_out_body(seq, x_ref, a_ref, up_ref, u_ref, un_ref, sa_ref, sb_ref,
                  wap_ref, pgw_ref, ps_ref, wpp_ref, wo_ref, g2_ref, wr_ref, br_ref, tri_ref, anchor_ref,
                  x1_ref, h2_ref, idx_ref, gate_ref, rank_ref, cnt_ref,
                  ext_ref, lv_ref, carry_ref):
    del anchor_ref
    i = pl.program_id(0)
    tm = x_ref.shape[0]

    @pl.when(i == 0)
    def _():
        carry_ref[...] = jnp.zeros_like(carry_ref)
        ext_ref[...] = jnp.zeros_like(ext_ref)
        lv_ref[...] = jnp.zeros_like(lv_ref)

    start = (i * tm) % seq
    has_prev = (start != 0).astype(_f32)
    has_next = (start + tm != seq).astype(_f32)
    first = 2 * POOL_HALO
    ext_ref[POOL_HALO:first, :] = up_ref[...].astype(_f32) * has_prev
    ext_ref[first:first + tm, :] = u_ref[...].astype(_f32)
    ext_ref[first + tm:first + tm + POOL_HALO, :] = un_ref[...].astype(_f32) * has_next
    y_attn = _dot(a_ref[...], wap_ref[...])
    lo, n = POOL_HALO, tm + 2 * POOL_HALO
    pos = start + lax.broadcasted_iota(jnp.int32, (tm, 1), 0)
    pooled = []
    for g, w in enumerate(POOL_WINDOWS):
        half = w // 2
        assert half == 1 << g
        wide = slice(g * POOL_GROUP_DIM, POOL_WIDTH)
        cols = slice(g * POOL_GROUP_DIM, (g + 1) * POOL_GROUP_DIM)
        if g == 0:
            lv_ref[0, lo:lo + n, wide] = ext_ref[pl.ds(lo - 1, n), wide] + ext_ref[pl.ds(lo, n), wide]
        elif g < N_POOL_GROUPS - 1:
            lv_ref[g, lo:lo + n, wide] = (lv_ref[g - 1, pl.ds(lo - half // 2, n), wide]
                                          + lv_ref[g - 1, pl.ds(lo + half // 2, n), wide])
        if g < N_POOL_GROUPS - 1:
            tot = lv_ref[g, first:first + tm, cols]
        else:
            tot = (lv_ref[g - 1, pl.ds(first - half // 2, tm), cols]
                   + lv_ref[g - 1, pl.ds(first + half // 2, tm), cols])
        cnt = (jnp.minimum(pos + half, seq) - jnp.maximum(pos - half, 0)).astype(_f32)
        diff = tot / cnt - ext_ref[first:first + tm, cols]
        pooled.append(_dot(diff.astype(_bf16), pgw_ref[g]))
    pooled = (jnp.concatenate(pooled, axis=1) * ps_ref[...]).astype(_bf16)
    y_pool = _dot(pooled, wpp_ref[...])
    merged = sa_ref[...].astype(_f32) * y_attn + sb_ref[...].astype(_f32) * y_pool
    x1 = x_ref[...] + _dot(merged.astype(_bf16), wo_ref[...])
    x1_ref[...] = x1

    ms = jnp.mean(x1 * x1, axis=-1, keepdims=True)
    h2 = x1 * lax.rsqrt(ms + NORM_EPS) * g2_ref[...]
    h2_ref[...] = _pack_bf16_pair(h2[:, :HALF_MODEL], h2[:, HALF_MODEL:])
    hi, lo = _split_bf16(h2)
    both = _dot(hi, wr_ref[...])
    logits = both[:, :LANES] + both[:, LANES:] + _dot(lo, wr_ref[:, :LANES])
    lt = logits.T[:N_EXPERTS, :] + br_ref[...]
    eid = lax.broadcasted_iota(jnp.int32, lt.shape, 0)
    vals, idxs = [], []
    multi_hot = jnp.zeros(lt.shape, _f32)
    for _ in range(TOP_K):
        m = jnp.max(lt, axis=0, keepdims=True)
        sel = jnp.min(jnp.where(lt == m, eid, N_EXPERTS), axis=0, keepdims=True)
        hit = eid == sel
        vals.append(m)
        idxs.append(sel)
        multi_hot = multi_hot + hit.astype(_f32)
        lt = jnp.where(hit, -jnp.inf, lt)
    es = [jnp.exp(v - vals[0]) for v in vals]
    inv = 1.0 / (es[0] + es[1] + es[2] + es[3])
    gate_ref[...] = jnp.concatenate([e * inv for e in es], axis=0)
    idx_ref[...] = jnp.concatenate(idxs, axis=0)
    before = _dot(multi_hot.astype(_bf16), tri_ref[...]) + carry_ref[:, 0:1]
    ranks = [jnp.sum(jnp.where(eid == sel, before, 0.0), axis=0, keepdims=True) for sel in idxs]
    rank_ref[...] = jnp.concatenate(ranks, axis=0).astype(jnp.int32)
    carry_ref[...] = carry_ref[...] + jnp.sum(multi_hot, axis=1, keepdims=True)
    cnt_ref[...] = carry_ref[...]


def _mix_out(x2, attn, u, sa, sb, wap, pgw, ps, wpp, wo, g2, wr, br, tri, seq, anchor):
    tokens = x2.shape[0]
    tm = MIX_ROWS
    n = tokens // tm
    halo_blocks = tm // POOL_HALO
    last_halo = tokens // POOL_HALO - 1
    row = lambda i: (i, 0)
    const2 = lambda i: (0, 0)
    const3 = lambda i: (0, 0, 0)
    col = lambda i: (0, i)
    return pl.pallas_call(
        functools.partial(_mix_out_body, seq),
        grid=(n,),
        in_specs=[
            pl.BlockSpec((tm, D_MODEL), row),
            pl.BlockSpec((tm, ATTN_WIDTH), row),
            pl.BlockSpec((POOL_HALO, POOL_WIDTH), lambda i: (jnp.maximum(i * halo_blocks - 1, 0), 0)),
            pl.BlockSpec((tm, POOL_WIDTH), row),
            pl.BlockSpec((POOL_HALO, POOL_WIDTH), lambda i: (jnp.minimum((i + 1) * halo_blocks, last_halo), 0)),
            pl.BlockSpec((tm, D_MODEL), row),
            pl.BlockSpec((tm, D_MODEL), row),
            pl.BlockSpec((ATTN_WIDTH, D_MODEL), const2),
            pl.BlockSpec((N_POOL_GROUPS, POOL_GROUP_DIM, POOL_GROUP_DIM), const3),
            pl.BlockSpec((1, POOL_WIDTH), const2),
            pl.BlockSpec((POOL_WIDTH, D_MODEL), const2),
            pl.BlockSpec((D_MODEL, D_MODEL), const2),
            pl.BlockSpec((1, D_MODEL), const2),
            pl.BlockSpec((D_MODEL, 2 * LANES), const2),
            pl.BlockSpec((N_EXPERTS, 1), const2),
            pl.BlockSpec((tm, tm), const2),
            pl.BlockSpec(memory_space=pl.ANY),
        ],
        out_specs=[
            pl.BlockSpec((tm, D_MODEL), row),
            pl.BlockSpec((tm, HALF_MODEL), row),
            pl.BlockSpec((TOP_K, tm), col),
            pl.BlockSpec((TOP_K, tm), col),
            pl.BlockSpec((TOP_K, tm), col),
            pl.BlockSpec((N_EXPERTS, LANES), const2),
        ],
        out_shape=[
            jax.ShapeDtypeStruct((tokens, D_MODEL), _f32),
            jax.ShapeDtypeStruct((tokens, HALF_MODEL), jnp.uint32),
            jax.ShapeDtypeStruct((TOP_K, tokens), jnp.int32),
            jax.ShapeDtypeStruct((TOP_K, tokens), _f32),
            jax.ShapeDtypeStruct((TOP_K, tokens), jnp.int32),
            jax.ShapeDtypeStruct((N_EXPERTS, LANES), _f32),
        ],
        scratch_shapes=[
            pltpu.VMEM((tm + 4 * POOL_HALO, POOL_WIDTH), _f32),
            pltpu.VMEM((N_POOL_GROUPS - 1, tm + 4 * POOL_HALO, POOL_WIDTH), _f32),
            pltpu.VMEM((N_EXPERTS, LANES), _f32),
        ],
        compiler_params=pltpu.CompilerParams(
            dimension_semantics=("arbitrary",), vmem_limit_bytes=VMEM_LIMIT),
        name="mix_out",
    )(x2, attn, u, u, u, sa, sb, wap, pgw, ps, wpp, wo, g2, wr, br, tri, anchor)


def _worker_ranges(group_tokens, per_worker, workers):
    first = [0]
    for t in group_tokens:
        assert t % per_worker == 0
        first.append(first[-1] + t // per_worker)
    assert first[-1] == workers
    return first


def _dispatch(dest4, pad3, hs, n_rows):
    workers, n_chunks = dest4.shape[0], dest4.shape[1]
    n_pad_chunks = pad3.shape[1]
    per_worker = n_chunks * SC_CHUNK
    first_worker = _worker_ranges([h.shape[0] for h in hs], per_worker, workers)
    width, dtype = hs[0].shape[1], hs[0].dtype
    assert n_chunks % 2 == 0
    mesh = plsc.VectorSubcoreMesh(core_axis_name="c", subcore_axis_name="s")

    @functools.partial(
        pl.kernel, mesh=mesh,
        out_type=jax.ShapeDtypeStruct((n_rows, width), dtype),
        scratch_types=[
            pltpu.VMEM((n_chunks, TOP_K, SC_CHUNK), jnp.int32),
            pltpu.VMEM((n_pad_chunks, SC_CHUNK), jnp.int32),
            pltpu.VMEM((SC_CHUNK, width), dtype),
            pltpu.VMEM((SC_CHUNK, width), dtype),
            pltpu.SemaphoreType.DMA,
            pltpu.SemaphoreType.DMA,
            pltpu.SemaphoreType.DMA,
        ],
        name="dispatch",
    )
    def body(dest_hbm, pad_hbm, *refs):
        h_hbms = refs[:len(hs)]
        out_hbm, idx_v, pad_v, rows0, rows1, load_sem0, load_sem1, store_sem = refs[len(hs):]
        wid = lax.axis_index("s") * SC_CORES + lax.axis_index("c")
        pltpu.sync_copy(dest_hbm.at[wid], idx_v)
        pltpu.sync_copy(pad_hbm.at[wid], pad_v)

        zero = jnp.zeros((SC_LANES,), dtype)

        @pl.loop(0, SC_CHUNK)
        def _(r):
            @pl.loop(0, width, step=SC_LANES)
            def _(c):
                rows0.at[r, pl.ds(c, SC_LANES)][...] = zero

        @pl.loop(0, n_pad_chunks)
        def _(c):
            pltpu.sync_copy(rows0, out_hbm.at[pad_v.at[c]])

        def scatter_tokens(src_hbm, base):
            bufs = (rows0, rows1)
            load_sems = (load_sem0, load_sem1)

            def load(c, b):
                return pltpu.make_async_copy(src_hbm.at[pl.ds(base + c * SC_CHUNK, SC_CHUNK)], bufs[b], load_sems[b])

            load(0, 0).start()

            @pl.loop(0, n_chunks, step=2)
            def _(c):
                for b in range(2):
                    cc = c + b
                    load(cc, b).wait()

                    @pl.when(cc + 1 < n_chunks)
                    def _():
                        load(cc + 1, 1 - b).start()

                    copies = [pltpu.make_async_copy(bufs[b], out_hbm.at[idx_v.at[cc, k]], store_sem)
                              for k in range(TOP_K)]
                    for cp in copies:
                        cp.start()
                    for cp in copies:
                        cp.wait()

        for g, h_hbm in enumerate(h_hbms):
            @pl.when(jnp.logical_and(wid >= first_worker[g], wid < first_worker[g + 1]))
            def _(g=g, h_hbm=h_hbm):
                scatter_tokens(h_hbm, (wid - first_worker[g]) * per_worker)

    return body(dest4, pad3, *hs)


def _experts_body(be_ref, nu_ref, first_ref, slot_ref, next_ref, half_ref, xs_hbm, wgu_hbm, bgu_ref, wd_hbm, bd_ref,
                  y_hbm, xs_buf, y_buf, wgu_buf, wd_buf, in_sems, out_sems, w_sems):
    n_used = nu_ref[0]

    def weight_copies(e, s):
        return (pltpu.make_async_copy(wgu_hbm.at[e], wgu_buf.at[s], w_sems.at[s]),
                pltpu.make_async_copy(wd_hbm.at[e], wd_buf.at[s], w_sems.at[s]))

    def rows_in(i, b):
        return pltpu.make_async_copy(xs_hbm.at[pl.ds(i * MOE_ROWS, MOE_ROWS)], xs_buf.at[b], in_sems.at[b])

    def rows_out(i, b):
        return pltpu.make_async_copy(y_buf.at[b], y_hbm.at[pl.ds(i * MOE_ROWS, MOE_ROWS)], out_sems.at[b])

    rows_in(0, 0).start()
    for cp in weight_copies(be_ref[0], 0):
        cp.start()

    def step(i, carry):
        b = i % 2
        slot = slot_ref[i]
        e = be_ref[i]
        rows_in(i, b).wait()

        @pl.when(i + 1 < n_used)
        def _():
            rows_in(i + 1, 1 - b).start()

        @pl.when(first_ref[i] == 1)
        def _():
            for cp in weight_copies(e, slot):
                cp.wait()

            @pl.when(next_ref[i] >= 0)
            def _():
                for cp in weight_copies(next_ref[i], 1 - slot):
                    cp.start(priority=1)

        @pl.when(i >= 2)
        def _():
            rows_out(i - 2, b).wait()

        def run_rows(n):
            x = jnp.concatenate(_unpack_bf16_pair(xs_buf[b, :n, :]), axis=1).astype(_bf16)
            gu = _dot(x, wgu_buf[slot].astype(_bf16)) + bgu_ref[e]
            gate = jnp.minimum(gu[:, :D_FF], SWIGLU_LIMIT)
            up = jnp.clip(gu[:, D_FF:], -SWIGLU_LIMIT, SWIGLU_LIMIT)
            act = (up + 1.0) * (gate * _sigmoid(SWIGLU_ALPHA * gate))
            y = _dot(act.astype(_bf16), wd_buf[slot].astype(_bf16)) + bd_ref[e]
            y_buf[b, :n, :] = _pack_bf16_pair(y[:, :HALF_MODEL], y[:, HALF_MODEL:])

        for parts in range(1, MOE_PARTS + 1):
            @pl.when(half_ref[i] == parts)
            def _(parts=parts):
                n = parts * (MOE_ROWS // MOE_PARTS)
                run_rows(n)
                if n < MOE_ROWS:
                    y_buf[b, n:, :] = jnp.zeros((MOE_ROWS - n, HALF_MODEL), jnp.uint32)

        rows_out(i, b).start()
        return carry

    lax.fori_loop(0, n_used, step, 0)

    @pl.when(n_used >= 2)
    def _():
        rows_out(n_used - 2, n_used % 2).wait()

    rows_out(n_used - 1, (n_used - 1) % 2).wait()


def _experts(block_e, n_used, first, slot, next_e, half, xs, wgu, bgu, wd, bd):
    n_blocks = block_e.shape[0]
    n_rows = n_blocks * MOE_ROWS
    whole = lambda i, *_: (0, 0, 0)
    return pl.pallas_call(
        _experts_body,
        grid_spec=pltpu.PrefetchScalarGridSpec(
            num_scalar_prefetch=6,
            grid=(1,),
            in_specs=[
                pl.BlockSpec(memory_space=pl.ANY),
                pl.BlockSpec(memory_space=pl.ANY),
                pl.BlockSpec((N_EXPERTS, 1, 2 * D_FF), whole),
                pl.BlockSpec(memory_space=pl.ANY),
                pl.BlockSpec((N_EXPERTS, 1, D_MODEL), whole),
            ],
            out_specs=pl.BlockSpec(memory_space=pl.ANY),
            scratch_shapes=[
                pltpu.VMEM((2, MOE_ROWS, HALF_MODEL), jnp.uint32),
                pltpu.VMEM((2, MOE_ROWS, HALF_MODEL), jnp.uint32),
                pltpu.VMEM((2, D_MODEL, 2 * D_FF), _f32),
                pltpu.VMEM((2, D_FF, D_MODEL), _f32),
                pltpu.SemaphoreType.DMA((2,)),
                pltpu.SemaphoreType.DMA((2,)),
                pltpu.SemaphoreType.DMA((2,)),
            ],
        ),
        out_shape=jax.ShapeDtypeStruct((n_rows, HALF_MODEL), jnp.uint32),
        compiler_params=pltpu.CompilerParams(
            dimension_semantics=("arbitrary",), vmem_limit_bytes=VMEM_LIMIT),
        name="experts",
    )(block_e, n_used, first, slot, next_e, half, xs, wgu, bgu, wd, bd)


def _combine(dest_rows, gates, x1s, yb):
    workers = dest_rows.shape[0]
    per_worker = dest_rows.shape[1] * LANES // TOP_K
    n_chunks = per_worker // SC_COMBINE_CHUNK
    first_worker = _worker_ranges([x.shape[0] for x in x1s], per_worker, workers)
    n_groups = len(x1s)
    assert n_chunks % 2 == 0 and LANES % SC_COMBINE_CHUNK == 0
    mesh = plsc.VectorSubcoreMesh(core_axis_name="c", subcore_axis_name="s")
    row_buf = pltpu.VMEM((SC_COMBINE_CHUNK, D_MODEL), _f32)
    packed_buf = pltpu.VMEM((SC_COMBINE_CHUNK, HALF_MODEL), jnp.uint32)

    @functools.partial(
        pl.kernel, mesh=mesh,
        out_type=[jax.ShapeDtypeStruct(x.shape, _f32) for x in x1s],
        scratch_types=[
            pltpu.VMEM(dest_rows.shape[1:], jnp.int32),
            [[packed_buf] * TOP_K + [row_buf]] * 2,
            [pltpu.VMEM((SC_COMBINE_CHUNK, TOP_K * SC_LANES), _f32)] * 2,
            [pltpu.SemaphoreType.DMA] * 2, [pltpu.SemaphoreType.DMA] * 2, [pltpu.SemaphoreType.DMA] * 2,
        ],
        compiler_params=pltpu.CompilerParams(needs_layout_passes=False),
        name="combine",
    )
    def body(dest_hbm, *refs):
        g_hbms, x_hbms = refs[:n_groups], refs[n_groups:2 * n_groups]
        yb_hbm = refs[2 * n_groups]
        o_hbms = refs[2 * n_groups + 1:3 * n_groups + 1]
        idx_v, row_bufs, gate_bufs, gather_sems, load_sems, store_sems = refs[3 * n_groups + 1:]
        wid = lax.axis_index("s") * SC_CORES + lax.axis_index("c")
        pltpu.sync_copy(dest_hbm.at[wid], idx_v)

        def run(g_hbm, x_hbm, o_hbm, base):
            def loads(c, b):
                rows = pl.ds(base + c * SC_COMBINE_CHUNK, SC_COMBINE_CHUNK)
                cps = []
                for k in range(TOP_K):
                    pos = (c * TOP_K + k) * SC_COMBINE_CHUNK
                    ids = idx_v.at[pos // LANES, pl.ds(pos % LANES, SC_COMBINE_CHUNK)]
                    cps.append(pltpu.make_async_copy(yb_hbm.at[ids], row_bufs[b][k], gather_sems[b]))
                cps.append(pltpu.make_async_copy(x_hbm.at[rows], row_bufs[b][TOP_K], load_sems[b]))
                cps.append(pltpu.make_async_copy(g_hbm.at[rows], gate_bufs[b], load_sems[b]))
                return cps

            def store(c, b):
                rows = pl.ds(base + c * SC_COMBINE_CHUNK, SC_COMBINE_CHUNK)
                return pltpu.make_async_copy(row_bufs[b][TOP_K], o_hbm.at[rows], store_sems[b])

            for cp in loads(0, 0):
                cp.start()

            @pl.loop(0, n_chunks, step=2)
            def _(c):
                for b in range(2):
                    cc = c + b

                    @pl.when(cc + 1 < n_chunks)
                    def _():
                        @pl.when(cc >= 1)
                        def _():
                            store(cc - 1, 1 - b).wait()
                        for cp in loads(cc + 1, 1 - b):
                            cp.start()

                    for cp in loads(cc, b):
                        cp.wait()
                    acc_buf, gate_buf = row_bufs[b][TOP_K], gate_bufs[b]

                    @pl.loop(0, SC_COMBINE_CHUNK)
                    def _(r):
                        g = [gate_buf.at[r, pl.ds(k * SC_LANES, SC_LANES)][...] for k in range(TOP_K)]

                        @plsc.parallel_loop(0, HALF_MODEL, step=SC_LANES, unroll=4)
                        def _(col):
                            lo = acc_buf.at[r, pl.ds(col, SC_LANES)][...]
                            hi = acc_buf.at[r, pl.ds(HALF_MODEL + col, SC_LANES)][...]
                            for k in range(TOP_K):
                                words = row_bufs[b][k].at[r, pl.ds(col, SC_LANES)][...]
                                lo = lo + g[k] * plsc.bitcast(lax.shift_left(words, jnp.uint32(16)), _f32)
                                hi = hi + g[k] * plsc.bitcast(words & jnp.uint32(_HIGH_HALF), _f32)
                            acc_buf.at[r, pl.ds(col, SC_LANES)][...] = lo
                            acc_buf.at[r, pl.ds(HALF_MODEL + col, SC_LANES)][...] = hi

                    store(cc, b).start()

            store(n_chunks - 2, 0).wait()
            store(n_chunks - 1, 1).wait()

        for g in range(n_groups):
            @pl.when(jnp.logical_and(wid >= first_worker[g], wid < first_worker[g + 1]))
            def _(g=g):
                run(g_hbms[g], x_hbms[g], o_hbms[g], (wid - first_worker[g]) * per_worker)

    return body(dest_rows, *gates, *x1s, yb)


def _rope_tables(seq):
    pos = np.arange(seq)
    row_ids = (pos // GRID_W).astype(np.float32)
    col_ids = (pos % GRID_W).astype(np.float32)
    inv_freq = np.float32(ROPE_THETA) ** (-np.arange(ROPE_PAIRS_PER_AXIS, dtype=np.float32) / ROPE_PAIRS_PER_AXIS)
    ang_r = row_ids[:, None] * inv_freq
    ang_c = col_ids[:, None] * inv_freq
    cos = np.concatenate([np.cos(ang_r)] * 2 + [np.cos(ang_c)] * 2, axis=1)
    sin = np.concatenate([-np.sin(ang_r), np.sin(ang_r), -np.sin(ang_c), np.sin(ang_c)], axis=1)
    reps = LANES // HEAD_DIM
    return (jnp.asarray(np.tile(cos, (1, reps)).astype(np.float32)),
            jnp.asarray(np.tile(sin, (1, reps)).astype(np.float32)))


def _mixer(x, p, anchor):
    batch, seq, _ = x.shape
    x2 = x.reshape(batch * seq, D_MODEL)
    cos_t, sin_t = _rope_tables(seq)
    q, k2, vt, u, sa, sb = _in_proj(x2, p["mix_g"], p["w_in"], p["wvt"], p["qg"], p["kg"], cos_t, sin_t, p["bd"], seq)
    attn = _attention(q, k2, vt, batch, seq)
    return _mix_out(x2, attn, u, sa, sb, p["wap"], p["pgw"], p["ps"], p["wpp"], p["wo"], p["ffn_g"],
                    p["wr"], p["br"], p["tri"], seq, anchor)


def _moe_dispatch(mixed):
    counts = [m[5][:, 0].astype(jnp.int32) for m in mixed]
    total = sum(counts)
    padded = ((total + MOE_ROWS - 1) // MOE_ROWS) * MOE_ROWS
    padded_end = jnp.cumsum(padded)
    padded_start = padded_end - padded
    n_tok = sum(m[0].shape[0] for m in mixed)
    n_blocks = (n_tok * TOP_K + MOE_ROWS - 1) // MOE_ROWS + N_EXPERTS
    n_used = (padded_end[-1] // MOE_ROWS).astype(jnp.int32)
    blk = jnp.minimum(jnp.arange(n_blocks, dtype=jnp.int32), n_used - 1)
    block_e = jnp.sum((padded_end[None, :] <= (blk * MOE_ROWS)[:, None]).astype(jnp.int32), axis=1)
    block_e = jnp.minimum(block_e, N_EXPERTS - 1)
    expert_ids = jnp.arange(N_EXPERTS, dtype=jnp.int32)[:, None, None]
    dests = []
    seen = jnp.zeros((N_EXPERTS,), jnp.int32)
    for m, c in zip(mixed, counts):
        base = (padded_start + seen)[:, None, None]
        dests.append(jnp.sum(jnp.where(m[2][None] == expert_ids, base, 0), axis=0) + m[4])
        seen = seen + c
    workers = SC_CORES * SC_SUBCORES
    per_worker = n_tok // workers
    dest4 = jnp.concatenate(
        [d.reshape(TOP_K, -1, per_worker // SC_CHUNK, SC_CHUNK).transpose(1, 2, 0, 3) for d in dests], axis=0)
    n_rows = n_blocks * MOE_ROWS
    slot = jnp.arange(MOE_ROWS, dtype=jnp.int32)[None, :]
    pad_rows = jnp.where(slot < (padded - total)[:, None], (padded_start + total)[:, None] + slot,
                         n_rows + jnp.arange(N_EXPERTS, dtype=jnp.int32)[:, None] * MOE_ROWS + slot)
    pad3 = pad_rows.astype(jnp.int32).reshape(workers, -1, SC_CHUNK)
    xs_buf = _dispatch(dest4, pad3, [m[1] for m in mixed], n_rows + N_EXPERTS * MOE_ROWS)
    return xs_buf, dict(mixed=mixed, dests=dests, per_worker=per_worker, n_blocks=n_blocks, n_used=n_used, block_e=block_e,
                        padded_end=padded_end, padded_start=padded_start, total=total)


def _moe_finish(xs_buf, r, w_gu, b_gu, w_down, b_down):
    mixed, dests, per_worker, n_blocks, n_used = r["mixed"], r["dests"], r["per_worker"], r["n_blocks"], r["n_used"]
    block_e, padded_end, padded_start, total = r["block_e"], r["padded_end"], r["padded_start"], r["total"]
    blk_id = jnp.arange(n_blocks, dtype=jnp.int32)
    first = jnp.logical_and(block_e != jnp.concatenate([jnp.full((1,), -1, jnp.int32), block_e[:-1]]),
                            blk_id < n_used).astype(jnp.int32)
    slot = (jnp.cumsum(first) - 1) % 2
    group_end = jnp.sum(jnp.where(block_e[:, None] == jnp.arange(N_EXPERTS, dtype=jnp.int32)[None, :],
                                  (padded_end // MOE_ROWS)[None, :], 0), axis=1)
    next_e = jnp.sum(jnp.where(group_end[:, None] == blk_id[None, :], block_e[None, :], 0), axis=1)
    next_e = jnp.where(group_end < n_used, next_e, -1)
    token_end = jnp.sum(jnp.where(block_e[:, None] == jnp.arange(N_EXPERTS, dtype=jnp.int32)[None, :],
                                  (padded_start + total)[None, :], 0), axis=1)
    part_rows = MOE_ROWS // MOE_PARTS
    half = jnp.clip((token_end - blk_id * MOE_ROWS + part_rows - 1) // part_rows, 1, MOE_PARTS).astype(jnp.int32)
    yb = _experts(block_e, n_used.reshape(1), first, slot.astype(jnp.int32), next_e.astype(jnp.int32), half, xs_buf,
                  w_gu, b_gu.reshape(N_EXPERTS, 1, -1), w_down, b_down.reshape(N_EXPERTS, 1, -1))
    dest_rows = jnp.concatenate(
        [d.reshape(TOP_K, -1, per_worker // SC_COMBINE_CHUNK, SC_COMBINE_CHUNK).transpose(1, 2, 0, 3)
         .reshape(-1, per_worker * TOP_K // LANES, LANES) for d in dests], axis=0)
    gate_vecs = [jnp.repeat(m[3].T, SC_LANES, axis=1) for m in mixed]
    return _combine(dest_rows, gate_vecs, [m[0] for m in mixed], yb)


def kernel(x_prompt, x_sample, mix_norm_g, w_in, q_norm_g, k_norm_g, w_attn_proj, pool_group_w, pool_scale,
           w_pool_proj, w_out, ffn_norm_g, w_router, b_router, w_gu, b_gu, w_down, b_down):
    depth = w_in.shape[0]
    xs_all = [x_prompt, x_sample]
    head_id = np.arange(ATTN_WIDTH) // HEAD_DIM
    block_diag = jnp.asarray(head_id[:, None] == head_id[None, :], _bf16)
    tri_id = np.arange(MIX_ROWS)
    tri = jnp.asarray(tri_id[:, None] < tri_id[None, :], _bf16)
    for l in range(depth):
        wr = jnp.pad(w_router[l].astype(_f32), ((0, 0), (0, LANES - N_EXPERTS)))
        wrh = wr.astype(_bf16)
        w_in_bf = w_in[l].astype(_bf16)
        p = dict(
            mix_g=mix_norm_g[l].reshape(1, D_MODEL),
            w_in=w_in_bf,
            wvt=w_in_bf[:, ATTN_WIDTH + KV_WIDTH:ATTN_WIDTH + 2 * KV_WIDTH].T,
            qg=jnp.tile(q_norm_g[l] * (HEAD_DIM ** -0.5 * LOG2_E), N_HEADS).reshape(1, ATTN_WIDTH),
            kg=jnp.tile(k_norm_g[l], N_KV_HEADS).reshape(1, KV_WIDTH),
            bd=block_diag,
            wap=w_attn_proj[l].astype(_bf16),
            pgw=pool_group_w[l].astype(_bf16),
            ps=pool_scale[l].reshape(1, POOL_WIDTH),
            wpp=w_pool_proj[l].astype(_bf16),
            wo=w_out[l].astype(_bf16),
            ffn_g=ffn_norm_g[l].reshape(1, D_MODEL),
            wr=jnp.concatenate([wrh, (wr - wrh.astype(_f32)).astype(_bf16)], axis=1),
            br=b_router[l].astype(_f32).reshape(N_EXPERTS, 1),
            tri=tri,
        )
        routed = []
        anchor = jnp.zeros((SUBLANES, LANES), _f32)
        for x in xs_all:
            xs_buf, r = _moe_dispatch([_mixer(x, p, anchor)])
            routed.append((xs_buf, r))
            anchor = xs_buf
        outs = [_moe_finish(xs_buf, r, w_gu[l], b_gu[l], w_down[l], b_down[l])[0] for xs_buf, r in routed]
        xs_all = [o.reshape(x.shape) for o, x in zip(outs, xs_all)]
    return tuple(xs_all)
```
